```python
import math
import jax
import jax.numpy as jnp
from jax import lax
import numpy as np

D_MODEL = 1024
BATCH = 8
SEQ = 2048
DEPTH = 2
DEC_BATCH = 128
DEC_SEQ = 8
PAST_LEN = 16384
PAGE_SIZE = 128

RET_HEADS = 4
RET_DK = 64
RET_DV = 128
S5_GROUPS = 16
S5_GROUP_CH = 16
S5_CH = S5_GROUPS * S5_GROUP_CH
S5_STATE = 64
GDN_HEADS = 4
GDN_DK = 64
GDN_DV = 64
GDN_CONV = 4
GDN_QKV = GDN_HEADS * (2 * GDN_DK + GDN_DV)
RET_IN = RET_HEADS * (2 * RET_DK + 2 * RET_DV)
S5_IN = S5_CH
GDN_IN = GDN_QKV + GDN_HEADS * GDN_DV + 2 * GDN_HEADS
D_IN = RET_IN + S5_IN + GDN_IN
D_MIX = RET_HEADS * RET_DV + S5_CH + GDN_HEADS * GDN_DV
CHUNK = 64
MOE_GROUPS = 4
EXPERTS_PER_GROUP = 8
N_EXPERTS = MOE_GROUPS * EXPERTS_PER_GROUP
D_EXPERT = 512
TOP_K = 2
MOE_BLOCK = 128
ROPE_BASE = 10000.0
EPS = 1e-6

kernel_name = 'hybrid_retention_s5_gdn_hmoe_step'


def rmsnorm(x, w=None):
    xf = x.astype(jnp.float32)
    y = xf * lax.rsqrt(jnp.mean(xf * xf, axis=-1, keepdims=True) + EPS)
    if w is not None:
        y = y * w.astype(jnp.float32)
    return y.astype(x.dtype)


def l2norm(x):
    xf = x.astype(jnp.float32)
    return xf * lax.rsqrt(jnp.sum(xf * xf, axis=-1, keepdims=True) + EPS)


def rotary(x, pos):
    half = x.shape[-1] // 2
    inv = ROPE_BASE ** (-jnp.arange(half, dtype=jnp.float32) / half)
    ang = pos[:, None] * inv[None, :]
    cos = jnp.cos(ang)[None, :, None, :]
    sin = jnp.sin(ang)[None, :, None, :]
    x1 = x[..., :half].astype(jnp.float32)
    x2 = x[..., half:].astype(jnp.float32)
    return jnp.concatenate([x1 * cos - x2 * sin, x1 * sin + x2 * cos], axis=-1).astype(x.dtype)


def to_chunks(x, c):
    L = x.shape[1]
    lp = -(-L // c) * c
    pad = [(0, 0)] * x.ndim
    pad[1] = (0, lp - L)
    x = jnp.pad(x, pad)
    return x.reshape((x.shape[0], lp // c, c) + x.shape[2:])


def from_chunks(x, L):
    return x.reshape((x.shape[0], -1) + x.shape[3:])[:, :L]


def decay_mask(G):
    c = G.shape[2]
    gh = jnp.swapaxes(G, 2, 3)
    diff = gh[..., :, None] - gh[..., None, :]
    causal = jnp.tril(jnp.ones((c, c), dtype=bool))
    return jnp.where(causal, jnp.exp(jnp.where(causal, diff, 0.0)), 0.0)


def retention_chunked(q, k, v, g, s0):
    L = q.shape[1]
    c = min(CHUNK, L)
    qc, kc, vc, gc = (to_chunks(t, c) for t in (q, k, v, g))
    G = jnp.cumsum(gc.astype(jnp.float32), axis=2)
    g_last = G[:, :, -1]
    scores = jnp.einsum('bnihd,bnjhd->bnhij', qc, kc) * decay_mask(G)
    o_intra = jnp.einsum('bnhij,bnjhe->bnihe', scores, vc)
    q_dec = qc * jnp.exp(G)[..., None]
    k_dec = kc * jnp.exp(g_last[:, :, None] - G)[..., None]

    def step(S, inp):
        q_n, k_n, v_n, gl = inp
        o = jnp.einsum('bihd,bhde->bihe', q_n, S)
        S = S * jnp.exp(gl)[..., None, None] + jnp.einsum('bjhd,bjhe->bhde', k_n, v_n)
        return S, o

    xs = tuple(jnp.moveaxis(t, 1, 0) for t in (q_dec, k_dec, vc, g_last))
    s_fin, o_inter = lax.scan(step, s0.astype(jnp.float32), xs)
    o = o_intra + jnp.moveaxis(o_inter, 0, 1)
    return from_chunks(o, L), s_fin


def gated_delta_chunked(q, k, v, g, beta, s0):
    L = q.shape[1]
    c = min(CHUNK, L)
    qc, kc, vc, gc, bc = (to_chunks(t, c) for t in (q, k, v, g, beta))
    G = jnp.cumsum(gc.astype(jnp.float32), axis=2)
    g_last = G[:, :, -1]
    mask = decay_mask(G)
    kb = kc * bc[..., None]
    strict = jnp.tril(jnp.ones((c, c), dtype=bool), -1)
    lmat = jnp.where(strict, jnp.einsum('bnihd,bnjhd->bnhij', kb, kc) * mask, 0.0)
    rhs = jnp.concatenate([vc * bc[..., None], kb * jnp.exp(G)[..., None]], axis=-1)
    rhs = jnp.swapaxes(rhs, 2, 3).astype(jnp.float32)
    sol = lax.linalg.triangular_solve(lmat.astype(jnp.float32), rhs, left_side=True,
                                      lower=True, unit_diagonal=True)
    u = sol[..., :GDN_DV]
    w = sol[..., GDN_DV:]
    attn = jnp.einsum('bnihd,bnjhd->bnhij', qc, kc) * mask
    q_dec = jnp.swapaxes(qc * jnp.exp(G)[..., None], 2, 3)
    k_dec = jnp.swapaxes(kc * jnp.exp(g_last[:, :, None] - G)[..., None], 2, 3)

    def step(S, inp):
        u_n, w_n, a_n, q_n, k_n, gl = inp
        v_new = u_n - jnp.einsum('bhcd,bhde->bhce', w_n, S)
        o = jnp.einsum('bhcd,bhde->bhce', q_n, S) + jnp.einsum('bhij,bhje->bhie', a_n, v_new)
        S = S * jnp.exp(gl)[..., None, None] + jnp.einsum('bhcd,bhce->bhde', k_n, v_new)
        return S, o

    xs = tuple(jnp.moveaxis(t, 1, 0) for t in (u, w, attn, q_dec, k_dec, g_last))
    s_fin, o = lax.scan(step, s0.astype(jnp.float32), xs)
    o = jnp.transpose(o, (1, 0, 3, 2, 4))
    return from_chunks(o, L), s_fin


def s5_mixer(u, h0_re, h0_im, a_re, a_im, log_dt, b_re, b_im, c_re, c_im, d_skip, glu_w, glu_b):
    bsz, L, _ = u.shape
    uf = u.astype(jnp.float32).reshape(bsz, L, S5_GROUPS, S5_GROUP_CH)
    dt = jnp.exp(log_dt.astype(jnp.float32))[:, None]
    lam_re = a_re.astype(jnp.float32)
    lam_im = a_im.astype(jnp.float32)
    mag = jnp.exp(lam_re * dt)
    ab_re = mag * jnp.cos(lam_im * dt)
    ab_im = mag * jnp.sin(lam_im * dt)
    den = lam_re * lam_re + lam_im * lam_im
    f_re = ((ab_re - 1.0) * lam_re + ab_im * lam_im) / den
    f_im = (ab_im * lam_re - (ab_re - 1.0) * lam_im) / den
    bb_re = f_re[..., None] * b_re - f_im[..., None] * b_im
    bb_im = f_re[..., None] * b_im + f_im[..., None] * b_re
    bu_re = jnp.einsum('blgc,gpc->blgp', uf, bb_re)
    bu_im = jnp.einsum('blgc,gpc->blgp', uf, bb_im)
    h0r = h0_re.astype(jnp.float32)
    h0i = h0_im.astype(jnp.float32)
    bu_re = bu_re.at[:, 0].add(ab_re * h0r - ab_im * h0i)
    bu_im = bu_im.at[:, 0].add(ab_re * h0i + ab_im * h0r)
    a_r = jnp.broadcast_to(ab_re, bu_re.shape)
    a_i = jnp.broadcast_to(ab_im, bu_im.shape)

    def combine(e1, e2):
        a1r, a1i, b1r, b1i = e1
        a2r, a2i, b2r, b2i = e2
        return (a2r * a1r - a2i * a1i, a2r * a1i + a2i * a1r,
                a2r * b1r - a2i * b1i + b2r, a2r * b1i + a2i * b1r + b2i)

    _, _, h_re, h_im = lax.associative_scan(combine, (a_r, a_i, bu_re, bu_im), axis=1)
    y = jnp.einsum('blgp,gcp->blgc', h_re, c_re) - jnp.einsum('blgp,gcp->blgc', h_im, c_im)
    y = y.reshape(bsz, L, S5_CH) + d_skip * u
    y = jax.nn.gelu(y)
    y = y * jax.nn.sigmoid(y @ glu_w + glu_b)
    return y, h_re[:, -1], h_im[:, -1]


def gdn_mixer(z, buf0, s0, conv_w, a_log, dt_bias, norm_w):
    bsz, L, _ = z.shape
    hd = GDN_HEADS * GDN_DK
    qkv_pre = z[..., :GDN_QKV]
    gate = z[..., GDN_QKV:GDN_QKV + GDN_HEADS * GDN_DV].reshape(bsz, L, GDN_HEADS, GDN_DV)
    a_in = z[..., GDN_IN - 2 * GDN_HEADS:GDN_IN - GDN_HEADS]
    b_in = z[..., GDN_IN - GDN_HEADS:]
    ext = jnp.concatenate([buf0.astype(qkv_pre.dtype), qkv_pre], axis=1)
    conv = ext[:, 0:L] * conv_w[0]
    for i in range(1, GDN_CONV):
        conv = conv + ext[:, i:i + L] * conv_w[i]
    qkv = jax.nn.silu(conv)
    new_buf = ext[:, -(GDN_CONV - 1):]
    q = l2norm(qkv[..., :hd].reshape(bsz, L, GDN_HEADS, GDN_DK)) * (GDN_DK ** -0.5)
    k = l2norm(qkv[..., hd:2 * hd].reshape(bsz, L, GDN_HEADS, GDN_DK))
    v = qkv[..., 2 * hd:].reshape(bsz, L, GDN_HEADS, GDN_DV)
    beta = jax.nn.sigmoid(b_in.astype(jnp.float32))
    g = -jnp.exp(a_log.astype(jnp.float32)) * jax.nn.softplus(a_in.astype(jnp.float32) + dt_bias)
    o, s_fin = gated_delta_chunked(q, k, v, g, beta, s0)
    o = rmsnorm(o, norm_w) * jax.nn.silu(gate)
    return o.reshape(bsz, L, GDN_HEADS * GDN_DV), new_buf, s_fin


def mixer_block(h, pos, ret_s0, s5_h0_re, s5_h0_im, gdn_s0, gdn_buf0, lp):
    (w_in, s5_a_re, s5_a_im, s5_log_dt, s5_b_re, s5_b_im, s5_c_re, s5_c_im, s5_d,
     s5_glu_w, s5_glu_b, gdn_conv_w, gdn_a_log, gdn_dt_bias, gdn_norm_w, w_out) = lp
    bsz, L, _ = h.shape
    z = h @ w_in
    zr = z[..., :RET_IN]
    zs = z[..., RET_IN:RET_IN + S5_IN]
    zg = z[..., RET_IN + S5_IN:]
    qd = RET_HEADS * RET_DK
    vd = RET_HEADS * RET_DV
    rq = rotary(zr[..., :qd].reshape(bsz, L, RET_HEADS, RET_DK), pos)
    rk = rotary(zr[..., qd:2 * qd].reshape(bsz, L, RET_HEADS, RET_DK), pos) * (RET_DK ** -0.5)
    rv = zr[..., 2 * qd:2 * qd + vd].reshape(bsz, L, RET_HEADS, RET_DV)
    rg = zr[..., 2 * qd + vd:].reshape(bsz, L, RET_HEADS, RET_DV)
    log_gamma = jnp.log(1.0 - 2.0 ** (-5.0 - jnp.arange(RET_HEADS, dtype=jnp.float32)))
    rdec = jnp.broadcast_to(log_gamma, (bsz, L, RET_HEADS))
    o_ret, ret_s = retention_chunked(rq, rk, rv, rdec, ret_s0)
    o_ret = (rmsnorm(o_ret) * jax.nn.silu(rg)).reshape(bsz, L, vd)
    o_s5, s5_re, s5_im = s5_mixer(zs, s5_h0_re, s5_h0_im, s5_a_re, s5_a_im, s5_log_dt, s5_b_re,
                                  s5_b_im, s5_c_re, s5_c_im, s5_d, s5_glu_w, s5_glu_b)
    o_gdn, gdn_buf, gdn_s = gdn_mixer(zg, gdn_buf0, gdn_s0, gdn_conv_w, gdn_a_log, gdn_dt_bias,
                                      gdn_norm_w)
    mix = jnp.concatenate([o_ret.astype(h.dtype), o_s5.astype(h.dtype), o_gdn.astype(h.dtype)], axis=-1)
    return mix @ w_out, (ret_s, s5_re, s5_im, gdn_s, gdn_buf)


def hier_moe(h, w_rg, b_rg, w_re, b_re, w1, w3, w2):
    bsz, L, d = h.shape
    t = bsz * L
    xt = h.reshape(t, d)
    tok = jnp.arange(t)
    g_logits = (xt @ w_rg + b_rg).astype(jnp.float32)
    p_group = jax.nn.softmax(g_logits, axis=-1)
    grp = jnp.argmax(g_logits, axis=-1)
    e_logits = (xt @ w_re + b_re).astype(jnp.float32).reshape(t, MOE_GROUPS, EXPERTS_PER_GROUP)
    e_in = e_logits[tok, grp]
    top_val, top_idx = lax.top_k(e_in, TOP_K)
    gate = jax.nn.softmax(top_val, axis=-1) * p_group[tok, grp][:, None]
    expert = grp[:, None] * EXPERTS_PER_GROUP + top_idx
    npair = t * TOP_K
    flat_e = expert.reshape(npair).astype(jnp.int32)
    flat_g = gate.reshape(npair)
    order = jnp.argsort(flat_e)
    sorted_e = flat_e[order]
    counts = jnp.zeros((N_EXPERTS,), jnp.int32).at[flat_e].add(1)
    padded = (counts + MOE_BLOCK - 1) // MOE_BLOCK * MOE_BLOCK
    pad_end = jnp.cumsum(padded)
    pad_start = pad_end - padded
    start = jnp.cumsum(counts) - counts
    dest = pad_start[sorted_e] + (jnp.arange(npair, dtype=jnp.int32) - start[sorted_e])
    nb = (npair + N_EXPERTS * (MOE_BLOCK - 1)) // MOE_BLOCK
    rows = nb * MOE_BLOCK
    row_tok = jnp.full((rows,), t, jnp.int32).at[dest].set((order // TOP_K).astype(jnp.int32))
    row_gate = jnp.zeros((rows,), jnp.float32).at[dest].set(flat_g[order])
    block_e = jnp.minimum(jnp.searchsorted(pad_end, jnp.arange(nb, dtype=jnp.int32) * MOE_BLOCK,
                                           side='right'), N_EXPERTS - 1)
    x_pad = jnp.concatenate([xt, jnp.zeros((1, d), xt.dtype)], axis=0)
    xb = x_pad[row_tok].reshape(nb, MOE_BLOCK, d)

    def run_block(args):
        xblk, e = args
        hid = jax.nn.silu(xblk @ w1[e]) * (xblk @ w3[e])
        return hid @ w2[e]

    yb = lax.map(run_block, (xb, block_e)).reshape(rows, d)
    y = jax.ops.segment_sum(yb * row_gate[:, None], row_tok, num_segments=t + 1)[:t]
    return y.reshape(bsz, L, d).astype(h.dtype)


def trunk(x, pos, states, weights):
    ret_s, s5_re, s5_im, gdn_s, gdn_buf = states
    (norm_mix, w_in, s5_a_re, s5_a_im, s5_log_dt, s5_b_re, s5_b_im, s5_c_re, s5_c_im, s5_d,
     s5_glu_w, s5_glu_b, gdn_conv_w, gdn_a_log, gdn_dt_bias, gdn_norm_w, w_out, norm_ffn,
     router_group_w, router_group_b, router_expert_w, router_expert_b, expert_w1, expert_w3,
     expert_w2, norm_final) = weights
    outs = [[], [], [], [], []]
    for l in range(DEPTH):
        lp = (w_in[l], s5_a_re[l], s5_a_im[l], s5_log_dt[l], s5_b_re[l], s5_b_im[l], s5_c_re[l],
              s5_c_im[l], s5_d[l], s5_glu_w[l], s5_glu_b[l], gdn_conv_w[l], gdn_a_log[l],
              gdn_dt_bias[l], gdn_norm_w[l], w_out[l])
        mix, new = mixer_block(rmsnorm(x, norm_mix[l]), pos, ret_s[l], s5_re[l], s5_im[l],
                               gdn_s[l], gdn_buf[l], lp)
        x = x + mix.astype(x.dtype)
        x = x + hier_moe(rmsnorm(x, norm_ffn[l]), router_group_w[l], router_group_b[l],
                         router_expert_w[l], router_expert_b[l], expert_w1[l], expert_w3[l],
                         expert_w2[l]).astype(x.dtype)
        for lst, s in zip(outs, new):
            lst.append(s)
    return rmsnorm(x, norm_final), tuple(jnp.stack(o) for o in outs)


def setup_inputs(seed: int = 0) -> dict:
    key = jax.random.key(seed)
    ks = jax.random.split(key, 40)
    f32 = jnp.float32
    D = D_MODEL

    def nrm(k, shape, s):
        return s * jax.random.normal(k, shape, f32)

    dt_gdn = jnp.exp(jax.random.uniform(ks[21], (DEPTH, GDN_HEADS), f32, math.log(1e-3), math.log(1e-1)))
    return {
        'x_prompt': nrm(ks[0], (BATCH, SEQ, D), 1.0),
        'x_sample': nrm(ks[1], (DEC_BATCH, DEC_SEQ, D), 1.0),
        'state_ret': nrm(ks[2], (DEPTH, DEC_BATCH, RET_HEADS, RET_DK, RET_DV), 0.3),
        'state_s5_re': nrm(ks[3], (DEPTH, DEC_BATCH, S5_GROUPS, S5_STATE), 0.5),
        'state_s5_im': nrm(ks[4], (DEPTH, DEC_BATCH, S5_GROUPS, S5_STATE), 0.5),
        'state_gdn': nrm(ks[5], (DEPTH, DEC_BATCH, GDN_HEADS, GDN_DK, GDN_DV), 0.1),
        'state_gdn_conv': nrm(ks[6], (DEPTH, DEC_BATCH, GDN_CONV - 1, GDN_QKV), 1.0),
        'norm_mix': 1.0 + nrm(ks[7], (DEPTH, D), 0.02),
        'w_in': nrm(ks[8], (DEPTH, D, D_IN), D ** -0.5),
        's5_a_re': -0.5 + nrm(ks[9], (DEPTH, S5_GROUPS, S5_STATE), 0.01),
        's5_a_im': math.pi * jnp.arange(S5_STATE, dtype=f32) + nrm(ks[10], (DEPTH, S5_GROUPS, S5_STATE), 0.01),
        's5_log_dt': jax.random.uniform(ks[11], (DEPTH, S5_GROUPS), f32, math.log(1e-3), math.log(1e-1)),
        's5_b_re': nrm(ks[12], (DEPTH, S5_GROUPS, S5_STATE, S5_GROUP_CH), (2 * S5_GROUP_CH) ** -0.5),
        's5_b_im': nrm(ks[13], (DEPTH, S5_GROUPS, S5_STATE, S5_GROUP_CH), (2 * S5_GROUP_CH) ** -0.5),
        's5_c_re': nrm(ks[14], (DEPTH, S5_GROUPS, S5_GROUP_CH, S5_STATE), (2 * S5_STATE) ** -0.5),
        's5_c_im': nrm(ks[15], (DEPTH, S5_GROUPS, S5_GROUP_CH, S5_STATE), (2 * S5_STATE) ** -0.5),
        's5_d': nrm(ks[16], (DEPTH, S5_CH), 1.0),
        's5_glu_w': nrm(ks[17], (DEPTH, S5_CH, S5_CH), S5_CH ** -0.5),
        's5_glu_b': nrm(ks[18], (DEPTH, S5_CH), 0.01),
        'gdn_conv_w': nrm(ks[19], (DEPTH, GDN_CONV, GDN_QKV), GDN_CONV ** -0.5),
        'gdn_a_log': jnp.log(jax.random.uniform(ks[20], (DEPTH, GDN_HEADS), f32, 1.0, 16.0)),
        'gdn_dt_bias': dt_gdn + jnp.log(-jnp.expm1(-dt_gdn)),
        'gdn_norm_w': 1.0 + nrm(ks[22], (DEPTH, GDN_DV), 0.02),
        'w_out': nrm(ks[23], (DEPTH, D_MIX, D), D_MIX ** -0.5),
        'norm_ffn': 1.0 + nrm(ks[24], (DEPTH, D), 0.02),
        'router_group_w': nrm(ks[25], (DEPTH, D, MOE_GROUPS), D ** -0.5),
        'router_group_b': nrm(ks[26], (DEPTH, MOE_GROUPS), 0.01),
        'router_expert_w': nrm(ks[27], (DEPTH, D, N_EXPERTS), D ** -0.5),
        'router_expert_b': nrm(ks[28], (DEPTH, N_EXPERTS), 0.01),
        'expert_w1': nrm(ks[29], (DEPTH, N_EXPERTS, D, D_EXPERT), D ** -0.5),
        'expert_w3': nrm(ks[30], (DEPTH, N_EXPERTS, D, D_EXPERT), D ** -0.5),
        'expert_w2': nrm(ks[31], (DEPTH, N_EXPERTS, D_EXPERT, D), D_EXPERT ** -0.5),
        'norm_final': 1.0 + nrm(ks[32], (D,), 0.02),
    }


def reference(x_prompt, x_sample, state_ret, state_s5_re, state_s5_im, state_gdn, state_gdn_conv,
              norm_mix, w_in, s5_a_re, s5_a_im, s5_log_dt, s5_b_re, s5_b_im, s5_c_re, s5_c_im,
              s5_d, s5_glu_w, s5_glu_b, gdn_conv_w, gdn_a_log, gdn_dt_bias, gdn_norm_w, w_out,
              norm_ffn, router_group_w, router_group_b, router_expert_w, router_expert_b,
              expert_w1, expert_w3, expert_w2, norm_final):
    weights = (norm_mix, w_in, s5_a_re, s5_a_im, s5_log_dt, s5_b_re, s5_b_im, s5_c_re, s5_c_im,
               s5_d, s5_glu_w, s5_glu_b, gdn_conv_w, gdn_a_log, gdn_dt_bias, gdn_norm_w, w_out,
               norm_ffn, router_group_w, router_group_b, router_expert_w, router_expert_b,
               expert_w1, expert_w3, expert_w2, norm_final)
    bp, lp_len, _ = x_prompt.shape
    zero_states = (
        jnp.zeros((DEPTH, bp, RET_HEADS, RET_DK, RET_DV), jnp.float32),
        jnp.zeros((DEPTH, bp, S5_GROUPS, S5_STATE), jnp.float32),
        jnp.zeros((DEPTH, bp, S5_GROUPS, S5_STATE), jnp.float32),
        jnp.zeros((DEPTH, bp, GDN_HEADS, GDN_DK, GDN_DV), jnp.float32),
        jnp.zeros((DEPTH, bp, GDN_CONV - 1, GDN_QKV), x_prompt.dtype),
    )
    pos_prompt = jnp.arange(lp_len, dtype=jnp.float32)
    y_prompt, p_states = trunk(x_prompt, pos_prompt, zero_states, weights)
    p_ret, p_s5_re, p_s5_im, p_gdn, p_gdn_conv = p_states
    pos_sample = PAST_LEN + jnp.arange(x_sample.shape[1], dtype=jnp.float32)
    y_sample, s_states = trunk(x_sample, pos_sample,
                               (state_ret, state_s5_re, state_s5_im, state_gdn, state_gdn_conv),
                               weights)
    s_ret, s_s5_re, s_s5_im, s_gdn, s_gdn_conv = s_states
    return (y_prompt, y_sample, p_ret, p_s5_re, p_s5_im, p_gdn, p_gdn_conv,
            s_ret, s_s5_re, s_s5_im, s_gdn, s_gdn_conv)
```

```python
import functools
import math

import jax
import jax.numpy as jnp
from jax import lax
from jax.experimental import pallas as pl
from jax.experimental.pallas import tpu as pltpu

F32 = jnp.float32
BF16 = jnp.bfloat16
I32 = jnp.int32

D_MODEL = 1024
DEPTH = 2
RET_HEADS, RET_DK, RET_DV = 4, 64, 128
S5_GROUPS, S5_GROUP_CH, S5_STATE = 16, 16, 64
S5_CH = S5_GROUPS * S5_GROUP_CH
S5_LANES = S5_GROUPS * S5_STATE
GDN_HEADS, GDN_DK, GDN_DV, GDN_CONV = 4, 64, 64, 4
GDN_HD = GDN_HEADS * GDN_DK
GDN_QKV = 3 * GDN_HD
RET_QD = RET_HEADS * RET_DK
RET_VD = RET_HEADS * RET_DV
RET_IN = 2 * RET_QD + 2 * RET_VD
S5_IN = S5_CH
GDN_IN = GDN_QKV + GDN_HD + 2 * GDN_HEADS
GDN_CHUNK = 64
RET_CHUNK = 64
MOE_GROUPS, EXPERTS_PER_GROUP = 4, 8
N_EXPERTS = MOE_GROUPS * EXPERTS_PER_GROUP
D_EXPERT = 512
ROPE_BASE = 10000.0
EPS = 1e-6
PAST_LEN = 16384

LANE = 128
SUBLANE = 8
ROW_TILE = 512
MOE_BLK = 256
GATHER_TILE = 256
DMA_LAG = 16
VMEM_LIMIT = 56 * 1024 * 1024


def _params(*sem):
    return pltpu.CompilerParams(dimension_semantics=sem, vmem_limit_bytes=VMEM_LIMIT)


def _dot(a, b):
    return jnp.dot(a.astype(BF16), b.astype(BF16), preferred_element_type=F32)


def _dot_nt(a, b):
    return lax.dot_general(a.astype(BF16), b.astype(BF16), (((1,), (1,)), ((), ())),
                           preferred_element_type=F32)


def _dot_tn(a, b):
    return lax.dot_general(a.astype(BF16), b.astype(BF16), (((0,), (0,)), ((), ())),
                           preferred_element_type=F32)


def _dot_hi(a, b):
    a1 = a.astype(BF16)
    a2 = (a - a1.astype(F32)).astype(BF16)
    b1 = b.astype(BF16)
    b2 = (b - b1.astype(F32)).astype(BF16)
    d = lambda x, y: jnp.dot(x, y, preferred_element_type=F32)
    return d(a1, b1) + (d(a1, b2) + d(a2, b1))


def _split3(x):
    p1 = x.astype(BF16)
    r1 = x - p1.astype(F32)
    p2 = r1.astype(BF16)
    p3 = (r1 - p2.astype(F32)).astype(BF16)
    return p1, p2, p3


def _dot_sel_l(sel, x):
    p1, p2, p3 = _split3(x)
    d = lambda p: jnp.dot(sel, p, preferred_element_type=F32)
    return d(p1) + d(p2) + d(p3)


def _dot_sel_r(x, sel):
    p1, p2, p3 = _split3(x)
    d = lambda p: jnp.dot(p, sel, preferred_element_type=F32)
    return d(p1) + d(p2) + d(p3)


def _rms(x):
    return x * lax.rsqrt(jnp.mean(x * x, axis=-1, keepdims=True) + EPS)


def _silu(x):
    return x * jax.nn.sigmoid(x)


def _inproj_body(x_ref, nw_ref, wr_ref, wq_ref, ws_ref, wg_ref, wab_ref,
                 zr_ref, zq_ref, zs_ref, zg_ref, zab_ref):
    h = (_rms(x_ref[...]) * nw_ref[...]).astype(BF16)
    for w_ref, z_ref in ((wr_ref, zr_ref), (wq_ref, zq_ref), (ws_ref, zs_ref),
                         (wg_ref, zg_ref), (wab_ref, zab_ref)):
        z_ref[...] = jnp.dot(h, w_ref[...], preferred_element_type=F32)


def _inproj(x, nw, ws):
    t = x.shape[0]
    tm = min(ROW_TILE, t)
    widths = [w.shape[1] for w in ws]
    row = lambda n: pl.BlockSpec((tm, n), lambda i: (i, 0))
    full = lambda a: pl.BlockSpec(a.shape, lambda i: (0, 0))
    return pl.pallas_call(
        _inproj_body,
        grid=(t // tm,),
        in_specs=[row(D_MODEL), full(nw)] + [full(w) for w in ws],
        out_specs=[row(n) for n in widths],
        out_shape=[jax.ShapeDtypeStruct((t, n), F32) for n in widths],
        compiler_params=_params("parallel"),
        name="inproj",
    )(x, nw, *ws)


def _ret_log_gamma(h):
    return math.log(1.0 - 2.0 ** (-5.0 - h))


def _ret_body(zr_ref, cos_ref, sin_ref, s0_ref, o_ref, sfin_ref, s_scr, *, rows, chunk):
    j = pl.program_id(1)
    stack = RET_HEADS * chunk

    @pl.when(j == 0)
    def _():
        s_scr[...] = s0_ref[...]

    lane = lax.broadcasted_iota(I32, (rows, RET_QD), 1)
    first_half = (lane % RET_DK) < (RET_DK // 2)

    def rotary(x):
        swapped = jnp.where(first_half, pltpu.roll(x, RET_QD - RET_DK // 2, 1),
                            pltpu.roll(x, RET_DK // 2, 1))
        return x * cos_ref[...] + swapped * sin_ref[...]

    q = rotary(zr_ref[:, 0:RET_QD])
    k = rotary(zr_ref[:, RET_QD:2 * RET_QD]) * (RET_DK ** -0.5)

    def by_head(idx, fn):
        out = jnp.zeros(idx.shape, F32)
        for h in range(RET_HEADS):
            out = jnp.where(idx == h, fn(h), out)
        return out

    lg_lane = by_head(lax.broadcasted_iota(I32, (chunk, RET_QD), 1) // RET_DK, _ret_log_gamma)
    pos = lax.broadcasted_iota(I32, (chunk, RET_QD), 0).astype(F32)
    q_scale = jnp.exp((pos + 1.0) * lg_lane)
    k_scale = jnp.exp((chunk - 1.0 - pos) * lg_lane)
    st_row = lax.broadcasted_iota(I32, (stack, stack), 0)
    st_col = lax.broadcasted_iota(I32, (stack, stack), 1)
    causal = ((st_row // chunk) == (st_col // chunk)) & (st_row >= st_col)
    lg_stack = by_head(st_row // chunk, _ret_log_gamma)
    decay = jnp.where(causal, jnp.exp(jnp.where(causal, (st_row - st_col).astype(F32), 0.0) * lg_stack), 0.0)
    q_rows = (lax.broadcasted_iota(I32, (stack, RET_QD), 0) // chunk) == \
             (lax.broadcasted_iota(I32, (stack, RET_QD), 1) // RET_DK)
    v_rows = (lax.broadcasted_iota(I32, (stack, RET_VD), 0) // chunk) == \
             (lax.broadcasted_iota(I32, (stack, RET_VD), 1) // RET_DV)
    s_row_head = lax.broadcasted_iota(I32, (RET_QD, RET_DV), 0) // RET_DK
    s_decay = by_head(s_row_head, lambda h: math.exp(chunk * _ret_log_gamma(h)))
    kv_diag = (lax.broadcasted_iota(I32, (RET_QD, RET_VD), 0) // RET_DK) == \
              (lax.broadcasted_iota(I32, (RET_QD, RET_VD), 1) // RET_DV)

    def tile4(a):
        return jnp.concatenate([a] * RET_HEADS, axis=0)

    def collapse(a):
        out = a[0:chunk]
        for h in range(1, RET_HEADS):
            out = out + a[h * chunk:(h + 1) * chunk]
        return out

    s = s_scr[...]
    for ci in range(rows // chunk):
        r = slice(ci * chunk, (ci + 1) * chunk)
        q_c, k_c = q[r], k[r]
        v_c = zr_ref[r, 2 * RET_QD:2 * RET_QD + RET_VD]
        gate = zr_ref[r, 2 * RET_QD + RET_VD:2 * RET_QD + 2 * RET_VD]
        scores = _dot_nt(jnp.where(q_rows, tile4(q_c), 0.0), tile4(k_c)) * decay
        o_intra = collapse(_dot(scores, jnp.where(v_rows, tile4(v_c), 0.0)))
        s_full = jnp.concatenate([jnp.where(s_row_head == h, s, 0.0) for h in range(RET_HEADS)], axis=1)
        o_c = o_intra + _dot(q_c * q_scale, s_full)
        for h in range(RET_HEADS):
            sl = slice(h * RET_DV, (h + 1) * RET_DV)
            o_ref[r, sl] = _rms(o_c[:, sl]) * _silu(gate[:, sl])
        kv = jnp.where(kv_diag, _dot_tn(k_c * k_scale, v_c), 0.0)
        kv_own = kv[:, 0:RET_DV]
        for h in range(1, RET_HEADS):
            kv_own = kv_own + kv[:, h * RET_DV:(h + 1) * RET_DV]
        s = s * s_decay + kv_own
    s_scr[...] = s

    @pl.when(j == pl.num_programs(1) - 1)
    def _():
        sfin_ref[...] = s


def _retention(zr, cos, sin, s0, bsz, seq):
    chunk = min(RET_CHUNK, seq)
    rows = min(4 * chunk, seq)
    nj = seq // rows
    return pl.pallas_call(
        functools.partial(_ret_body, rows=rows, chunk=chunk),
        grid=(bsz, nj),
        in_specs=[pl.BlockSpec((rows, RET_IN), lambda b, j: (b * nj + j, 0)),
                  pl.BlockSpec((rows, RET_QD), lambda b, j: (j, 0)),
                  pl.BlockSpec((rows, RET_QD), lambda b, j: (j, 0)),
                  pl.BlockSpec((None, RET_QD, RET_DV), lambda b, j: (b, 0, 0))],
        out_specs=[pl.BlockSpec((rows, RET_VD), lambda b, j: (b * nj + j, 0)),
                   pl.BlockSpec((None, RET_QD, RET_DV), lambda b, j: (b, 0, 0))],
        out_shape=[jax.ShapeDtypeStruct((bsz * seq, RET_VD), F32),
                   jax.ShapeDtypeStruct((bsz, RET_QD, RET_DV), F32)],
        scratch_shapes=[pltpu.VMEM((RET_QD, RET_DV), F32)],
        compiler_params=_params("parallel", "arbitrary"),
        name="retention",
    )(zr, cos, sin, s0)


def _s5_prep_body(are_ref, aim_ref, ldt_ref, bre_ref, bim_ref, abre_ref, abim_ref, bbre_ref, bbim_ref):
    lam_re, lam_im = are_ref[...], aim_ref[...]
    dt = jnp.exp(ldt_ref[...])
    mag = jnp.exp(lam_re * dt)
    ab_re = mag * jnp.cos(lam_im * dt)
    ab_im = mag * jnp.sin(lam_im * dt)
    den = lam_re * lam_re + lam_im * lam_im
    f_re = ((ab_re - 1.0) * lam_re + ab_im * lam_im) / den
    f_im = (ab_im * lam_re - (ab_re - 1.0) * lam_im) / den
    abre_ref[...] = ab_re
    abim_ref[...] = ab_im
    bbre_ref[...] = f_re * bre_ref[...] - f_im * bim_ref[...]
    bbim_ref[...] = f_re * bim_ref[...] + f_im * bre_ref[...]


def _s5_prep(a_re, a_im, log_dt, b_re, b_im):
    n = a_re.shape[0]
    col = jax.ShapeDtypeStruct((n, 1), F32)
    mat = jax.ShapeDtypeStruct((n, S5_GROUP_CH), F32)
    return pl.pallas_call(_s5_prep_body, out_shape=[col, col, mat, mat], name="s5_prep")(
        a_re, a_im, log_dt, b_re, b_im)


def _gelu_tanh(x):
    return x * (0.5 * (1.0 + jnp.tanh(math.sqrt(2.0 / math.pi) * (x + 0.044715 * (x * x * x)))))


def _s5_body(u_ref, h0re_ref, h0im_ref, ab_ref, wb_ref, wcre_ref, wcim_ref, d_ref, gw_ref, gb_ref,
             o_ref, hre_ref, him_ref, bu_scr, st_scr, *, steps):
    j = pl.program_id(1)

    @pl.when(j == 0)
    def _():
        st_scr[0] = h0re_ref[...]
        st_scr[1] = h0im_ref[...]

    rows = steps * SUBLANE
    u = u_ref[...].reshape(rows, S5_CH)
    bu_scr[...] = _dot(u, wb_ref[...])
    a_re = jnp.broadcast_to(ab_ref[0:1, :], (SUBLANE, S5_LANES))
    a_im = jnp.broadcast_to(ab_ref[1:2, :], (SUBLANE, S5_LANES))

    def step(t, carry):
        h_re, h_im = carry
        r = pl.ds(pl.multiple_of(t * SUBLANE, SUBLANE), SUBLANE)
        n_re = a_re * h_re - a_im * h_im + bu_scr[r, 0:S5_LANES]
        n_im = a_re * h_im + a_im * h_re + bu_scr[r, S5_LANES:2 * S5_LANES]
        bu_scr[r, 0:S5_LANES] = n_re
        bu_scr[r, S5_LANES:2 * S5_LANES] = n_im
        return n_re, n_im

    h_re, h_im = lax.fori_loop(0, steps, step, (st_scr[0], st_scr[1]))
    st_scr[0] = h_re
    st_scr[1] = h_im
    hre_ref[...] = h_re
    him_ref[...] = h_im

    y = _dot(bu_scr[:, 0:S5_LANES], wcre_ref[...]) - _dot(bu_scr[:, S5_LANES:2 * S5_LANES], wcim_ref[...])
    y = _gelu_tanh(y + d_ref[...] * u)
    y = y * jax.nn.sigmoid(_dot(y, gw_ref[...]) + gb_ref[...])
    o_ref[...] = y.reshape(steps, SUBLANE, S5_CH)


def _s5(u_tm, h0_re, h0_im, ab, wb, wc_re, wc_im, d_skip, glu_w, glu_b):
    seq, bsz, _ = u_tm.shape
    steps = min(64, seq)
    full = lambda a: pl.BlockSpec(a.shape, lambda g, j: (0,) * a.ndim)
    st = pl.BlockSpec((SUBLANE, S5_LANES), lambda g, j: (g, 0))
    return pl.pallas_call(
        functools.partial(_s5_body, steps=steps),
        grid=(bsz // SUBLANE, seq // steps),
        in_specs=[pl.BlockSpec((steps, SUBLANE, S5_CH), lambda g, j: (j, g, 0)), st, st,
                  full(ab), full(wb), full(wc_re), full(wc_im), full(d_skip), full(glu_w), full(glu_b)],
        out_specs=[pl.BlockSpec((steps, SUBLANE, S5_CH), lambda g, j: (j, g, 0)), st, st],
        out_shape=[jax.ShapeDtypeStruct((seq, bsz, S5_CH), F32),
                   jax.ShapeDtypeStruct((bsz, S5_LANES), F32),
                   jax.ShapeDtypeStruct((bsz, S5_LANES), F32)],
        scratch_shapes=[pltpu.VMEM((steps * SUBLANE, 2 * S5_LANES), F32),
                        pltpu.VMEM((2, SUBLANE, S5_LANES), F32)],
        compiler_params=_params("parallel", "arbitrary"),
        name="s5",
    )(u_tm, h0_re, h0_im, ab, wb, wc_re, wc_im, d_skip, glu_w, glu_b)


def _gdn_body(zq_ref, zg_ref, zab_ref, cw_ref, alog_ref, dtb_ref, nw_ref, buf0_ref, s0_ref,
              ones_bd_ref, tri_ref, ea_ref, eb_ref, ec_ref, lbd_ref,
              o_ref, nbuf_ref, sfin_ref,
              ext_scr, q_scr, k_scr, kb_scr, vb_scr, g_scr, g4_scr, s_scr, *, rows, chunk):
    j = pl.program_id(1)
    stack = GDN_HEADS * chunk

    @pl.when(j == 0)
    def _():
        ext_scr[0:SUBLANE, :] = buf0_ref[...]
        s_scr[...] = s0_ref[...]

    ext_scr[SUBLANE:SUBLANE + rows, :] = zq_ref[...]
    conv = ext_scr[SUBLANE - 3:SUBLANE - 3 + rows, :] * cw_ref[0:1, :]
    for i in range(1, GDN_CONV):
        conv = conv + ext_scr[SUBLANE - 3 + i:SUBLANE - 3 + i + rows, :] * cw_ref[i:i + 1, :]
    tail = ext_scr[rows:rows + SUBLANE, :]
    nbuf_ref[...] = tail
    ext_scr[0:SUBLANE, :] = tail
    qkv = _silu(conv)

    ones_bd = ones_bd_ref[...]
    q_raw = qkv[:, 0:GDN_HD]
    k_raw = qkv[:, GDN_HD:2 * GDN_HD]
    q_scr[...] = q_raw * lax.rsqrt(_dot_sel_r(q_raw * q_raw, ones_bd) + EPS) * (GDN_DK ** -0.5)
    k_n = k_raw * lax.rsqrt(_dot_sel_r(k_raw * k_raw, ones_bd) + EPS)
    k_scr[...] = k_n

    ab = zab_ref[...]
    x = ab + dtb_ref[...]
    softplus = jnp.maximum(x, 0.0) + jnp.log1p(jnp.exp(-jnp.abs(x)))
    g_pad = -jnp.exp(alog_ref[...]) * softplus
    beta = _dot_sel_r(jax.nn.sigmoid(ab), eb_ref[...])
    g_cum = _dot_sel_l(tri_ref[...], g_pad)
    g_scr[...] = _dot_sel_r(g_cum, ea_ref[...])
    g4_scr[...] = _dot_sel_r(g_pad, ec_ref[...])
    kb_scr[...] = k_n * beta
    vb_scr[...] = qkv[:, 2 * GDN_HD:3 * GDN_HD] * beta

    st_row = lax.broadcasted_iota(I32, (stack, stack), 0)
    st_col = lax.broadcasted_iota(I32, (stack, stack), 1)
    same_head = (st_row // chunk) == (st_col // chunk)
    strict = same_head & (st_row > st_col)
    causal = same_head & (st_row >= st_col)
    head_rows = (lax.broadcasted_iota(I32, (stack, GDN_HD), 0) // chunk) == \
                (lax.broadcasted_iota(I32, (stack, GDN_HD), 1) // GDN_DK)
    head_rows2 = jnp.concatenate([head_rows, head_rows], axis=1)
    bd_state = (lax.broadcasted_iota(I32, (GDN_HD, GDN_HD), 0) // GDN_DK) == \
               (lax.broadcasted_iota(I32, (GDN_HD, GDN_HD), 1) // GDN_DV)
    lbd = lbd_ref[...]

    def tile4(a):
        return jnp.concatenate([a] * GDN_HEADS, axis=0)

    def collapse(a):
        out = a[0:chunk]
        for h in range(1, GDN_HEADS):
            out = out + a[h * chunk:(h + 1) * chunk]
        return out

    def chunk_step(ci, carry):
        r = pl.ds(pl.multiple_of(ci * chunk, chunk), chunk)
        q_c, k_c, kb_c, vb_c, g_c = q_scr[r, :], k_scr[r, :], kb_scr[r, :], vb_scr[r, :], g_scr[r, :]
        exp_g = jnp.exp(g_c)
        g_last = g_c[chunk - 1:chunk, :]
        g_diff = _dot_sel_l(lbd, jnp.where(strict, tile4(g4_scr[r, :]), 0.0))
        decay = jnp.exp(jnp.where(causal, g_diff, 0.0))
        k_rows = tile4(k_c)
        lmat = jnp.where(strict, _dot_nt(jnp.where(head_rows, tile4(kb_c), 0.0), k_rows) * decay, 0.0)
        attn = jnp.where(causal, _dot_nt(jnp.where(head_rows, tile4(q_c), 0.0), k_rows) * decay, 0.0)
        t_acc = -lmat
        power = lmat
        span = 2
        while span < chunk + 1:
            power = _dot_hi(power, power)
            t_acc = t_acc + power + _dot_hi(t_acc, power)
            span *= 2
        rhs = jnp.where(head_rows2, tile4(jnp.concatenate([vb_c, kb_c * exp_g], axis=1)), 0.0)
        sol = rhs + _dot_hi(t_acc, rhs)
        u_c = collapse(sol[:, 0:GDN_HD])
        w_c = collapse(sol[:, GDN_HD:2 * GDN_HD])
        s_old = s_scr[...]
        v_new = u_c - _dot(w_c, s_old)
        o_c = _dot(q_c * exp_g, s_old) + collapse(_dot(attn, jnp.where(head_rows, tile4(v_new), 0.0)))
        k_dec = k_c * jnp.exp(g_last - g_c)
        s_scr[...] = s_old * jnp.exp(g_last) + jnp.where(bd_state, _dot_tn(k_dec, v_new), 0.0)
        ms = _dot_sel_r(o_c * o_c, ones_bd) * (1.0 / GDN_DV)
        o_ref[r, :] = (o_c * lax.rsqrt(ms + EPS) * nw_ref[...] * _silu(zg_ref[r, :])).astype(o_ref.dtype)
        return carry

    lax.fori_loop(0, rows // chunk, chunk_step, 0)

    @pl.when(j == pl.num_programs(1) - 1)
    def _():
        sfin_ref[...] = s_scr[...]


def _gdn_consts(rows, chunk):
    stack = GDN_HEADS * chunk
    blk = lambda n, c: (jnp.arange(n)[:, None] // c) == (jnp.arange(n)[None, :] // c)
    ones_bd = blk(GDN_HD, GDN_DK).astype(BF16)
    tri = (blk(rows, chunk) & (jnp.arange(rows)[:, None] >= jnp.arange(rows)[None, :])).astype(BF16)
    lbd = (blk(stack, chunk) & (jnp.arange(stack)[:, None] >= jnp.arange(stack)[None, :])).astype(BF16)
    src = jnp.arange(LANE)[:, None]
    ea = (src == jnp.arange(GDN_HD)[None, :] // GDN_DK).astype(BF16)
    eb = (src == GDN_HEADS + jnp.arange(GDN_HD)[None, :] // GDN_DK).astype(BF16)
    ec = (src == jnp.arange(stack)[None, :] // chunk).astype(BF16)
    return ones_bd, tri, ea, eb, ec, lbd


def _gdn(zq, zg, zab, conv_w, alog_pad, dtb_pad, nw, buf0, s0, bsz, seq):
    chunk = min(GDN_CHUNK, seq)
    rows = min(4 * chunk, seq)
    nj = seq // rows
    stack = GDN_HEADS * chunk
    consts = _gdn_consts(rows, chunk)
    row = lambda n: pl.BlockSpec((rows, n), lambda b, j: (b * nj + j, 0))
    full = lambda a: pl.BlockSpec(a.shape, lambda b, j: (0,) * a.ndim)
    per_b = lambda *s: pl.BlockSpec((None,) + s, lambda b, j: (b, 0, 0))
    return pl.pallas_call(
        functools.partial(_gdn_body, rows=rows, chunk=chunk),
        grid=(bsz, nj),
        in_specs=[row(GDN_QKV), row(GDN_HD), row(LANE), full(conv_w), full(alog_pad), full(dtb_pad), full(nw),
                  per_b(SUBLANE, GDN_QKV), per_b(GDN_HD, GDN_HD)] + [full(c) for c in consts],
        out_specs=[row(GDN_HD), per_b(SUBLANE, GDN_QKV), per_b(GDN_HD, GDN_HD)],
        out_shape=[jax.ShapeDtypeStruct((bsz * seq, GDN_HD), F32),
                   jax.ShapeDtypeStruct((bsz, SUBLANE, GDN_QKV), F32),
                   jax.ShapeDtypeStruct((bsz, GDN_HD, GDN_HD), F32)],
        scratch_shapes=[pltpu.VMEM((rows + SUBLANE, GDN_QKV), F32)] +
                       [pltpu.VMEM((rows, GDN_HD), F32)] * 5 +
                       [pltpu.VMEM((rows, stack), F32), pltpu.VMEM((GDN_HD, GDN_HD), F32)],
        compiler_params=_params("parallel", "arbitrary"),
        name="gdn",
    )(zq, zg, zab, conv_w, alog_pad, dtb_pad, nw, buf0, s0, *consts)


def _outproj_body(x_ref, oret_ref, os5_ref, ogdn_ref, w1_ref, w2_ref, w3_ref, nw_ref, wr_ref, br_ref, tri_ref,
                  x1_ref, h2_ref, meta_ref, cnt_ref, carry_scr, *, tm):
    i = pl.program_id(0)

    @pl.when(i == 0)
    def _():
        carry_scr[...] = jnp.zeros_like(carry_scr)

    mix = _dot(oret_ref[...], w1_ref[...]) + _dot(os5_ref[...], w2_ref[...]) + _dot(ogdn_ref[...], w3_ref[...])
    x1 = x_ref[...] + mix
    x1_ref[...] = x1
    h2 = _rms(x1) * nw_ref[...]
    for s in range(D_MODEL // LANE):
        h2_ref[pl.ds(s, tm, stride=SUBLANE), :] = h2[:, s * LANE:(s + 1) * LANE]

    logits = _dot(h2, wr_ref[...]) + br_ref[...]
    lane_i = lax.broadcasted_iota(I32, (tm, LANE), 1)
    lane = lane_i.astype(F32)
    neg = -jnp.inf
    big = float(LANE)
    g_log = jnp.where((lane_i >= N_EXPERTS) & (lane_i < N_EXPERTS + MOE_GROUPS), logits, neg)
    g_max = jnp.max(g_log, axis=-1, keepdims=True)
    grp = jnp.min(jnp.where(g_log == g_max, lane - N_EXPERTS, big), axis=-1, keepdims=True)
    p_grp = 1.0 / jnp.sum(jnp.exp(g_log - g_max), axis=-1, keepdims=True)
    in_grp = (lane >= grp * EXPERTS_PER_GROUP) & (lane < (grp + 1.0) * EXPERTS_PER_GROUP)
    e_log = jnp.where(in_grp, logits, neg)
    v1 = jnp.max(e_log, axis=-1, keepdims=True)
    i1 = jnp.min(jnp.where(e_log == v1, lane, big), axis=-1, keepdims=True)
    e_log2 = jnp.where(lane == i1, neg, e_log)
    v2 = jnp.max(e_log2, axis=-1, keepdims=True)
    i2 = jnp.min(jnp.where(e_log2 == v2, lane, big), axis=-1, keepdims=True)
    e2 = jnp.exp(v2 - v1)
    gate1 = p_grp / (1.0 + e2)
    gate2 = p_grp * e2 / (1.0 + e2)

    oh1 = lane == i1
    oh2 = lane == i2
    picked = jnp.where(oh1 | oh2, 1.0, 0.0)
    before = jnp.dot(tri_ref[...], picked.astype(BF16), preferred_element_type=F32) + carry_scr[0:1, :]
    rank1 = jnp.sum(jnp.where(oh1, before, 0.0), axis=-1, keepdims=True)
    rank2 = jnp.sum(jnp.where(oh2, before, 0.0), axis=-1, keepdims=True)
    total = carry_scr[0:1, :] + jnp.sum(picked, axis=0, keepdims=True)
    carry_scr[...] = jnp.broadcast_to(total, carry_scr.shape)
    cnt_ref[...] = jnp.broadcast_to(total, cnt_ref.shape)

    meta = jnp.where(lane_i == 0, i1, 0.0)
    meta = jnp.where(lane_i == 1, i2, meta)
    meta = jnp.where(lane_i == 2, gate1, meta)
    meta = jnp.where(lane_i == 3, gate2, meta)
    meta = jnp.where(lane_i == 4, rank1, meta)
    meta = jnp.where(lane_i == 5, rank2, meta)
    meta_ref[...] = meta


def _outproj(x, o_ret, o_s5, o_gdn, w1, w2, w3, nw, wr, br):
    t = x.shape[0]
    tm = min(ROW_TILE, t)
    tri = (jnp.arange(tm)[:, None] > jnp.arange(tm)[None, :]).astype(BF16)
    row = lambda n: pl.BlockSpec((tm, n), lambda i: (i, 0))
    full = lambda a: pl.BlockSpec(a.shape, lambda i: (0, 0))
    return pl.pallas_call(
        functools.partial(_outproj_body, tm=tm),
        grid=(t // tm,),
        in_specs=[row(D_MODEL), row(RET_VD), row(S5_CH), row(GDN_HD),
                  full(w1), full(w2), full(w3), full(nw), full(wr), full(br), full(tri)],
        out_specs=[row(D_MODEL), pl.BlockSpec((tm * SUBLANE, LANE), lambda i: (i, 0)), row(LANE),
                   pl.BlockSpec((SUBLANE, LANE), lambda i: (0, 0))],
        out_shape=[jax.ShapeDtypeStruct((t, D_MODEL), F32),
                   jax.ShapeDtypeStruct((t * SUBLANE, LANE), F32),
                   jax.ShapeDtypeStruct((t, LANE), F32),
                   jax.ShapeDtypeStruct((SUBLANE, LANE), F32)],
        scratch_shapes=[pltpu.VMEM((SUBLANE, LANE), F32)],
        compiler_params=_params("arbitrary"),
        name="outproj_router",
    )(x, o_ret, o_s5, o_gdn, w1, w2, w3, nw, wr, br, tri)


def _token_rows(ref, idx):
    return ref.at[pl.ds(pl.multiple_of(idx * SUBLANE, SUBLANE), SUBLANE)]


def _scatter_body(d1_ref, d2_ref, src_ref, xin_ref, xb_ref, sem, *, tm):
    del xin_ref
    i = pl.program_id(0)
    base = i * tm

    def copy(t, d):
        return pltpu.make_async_copy(_token_rows(src_ref, t), _token_rows(xb_ref, d), sem)

    def wait_token():
        copy(0, 0).wait()
        copy(0, 0).wait()

    def body(t, carry):
        g = base + t
        copy(g, d1_ref[g]).start()
        copy(g, d2_ref[g]).start()
        pl.when(g >= DMA_LAG)(wait_token)
        return carry

    lax.fori_loop(0, tm, body, 0)

    @pl.when(i == pl.num_programs(0) - 1)
    def _():
        lax.fori_loop(0, DMA_LAG, lambda t, c: (wait_token(), c)[1], 0)


def _scatter(dest1, dest2, h2_tiles, xb):
    tokens = dest1.shape[0]
    tm = min(GATHER_TILE, tokens)
    assert tokens % tm == 0 and tokens >= DMA_LAG
    any_spec = pl.BlockSpec(memory_space=pl.ANY)
    return pl.pallas_call(
        functools.partial(_scatter_body, tm=tm),
        grid_spec=pltpu.PrefetchScalarGridSpec(
            num_scalar_prefetch=2, grid=(tokens // tm,),
            in_specs=[any_spec, any_spec], out_specs=any_spec,
            scratch_shapes=[pltpu.SemaphoreType.DMA(())]),
        out_shape=jax.ShapeDtypeStruct(xb.shape, xb.dtype),
        input_output_aliases={3: 0},
        compiler_params=_params("arbitrary"),
        name="moe_scatter",
    )(dest1, dest2, h2_tiles, xb)


def _experts_body(be_ref, nb_ref, xb_ref, w1_ref, w3_ref, w2_ref, yb_ref, *, blk):
    i = pl.program_id(0)

    @pl.when(i < nb_ref[0])
    def _():
        x = jnp.concatenate([xb_ref[pl.ds(s, blk, stride=SUBLANE), :] for s in range(D_MODEL // LANE)],
                            axis=1).astype(BF16)
        hid = _silu(jnp.dot(x, w1_ref[...], preferred_element_type=F32)) * \
            jnp.dot(x, w3_ref[...], preferred_element_type=F32)
        y = _dot(hid, w2_ref[...])
        for s in range(D_MODEL // LANE):
            yb_ref[pl.ds(s, blk, stride=SUBLANE), :] = y[:, s * LANE:(s + 1) * LANE]

    @pl.when(i >= nb_ref[0])
    def _():
        yb_ref[...] = jnp.zeros_like(yb_ref)


def _experts(block_e, nb_used, xb, w1, w3, w2, n_blocks):
    blk = MOE_BLK
    live = lambda i, be, nb: jnp.minimum(i, nb[0] - 1)
    tile_in = pl.BlockSpec((blk * SUBLANE, LANE), lambda i, be, nb: (live(i, be, nb), 0))
    tile_out = pl.BlockSpec((blk * SUBLANE, LANE), lambda i, be, nb: (i, 0))
    wspec = lambda a: pl.BlockSpec((None,) + a.shape[1:], lambda i, be, nb: (be[live(i, be, nb)], 0, 0))
    return pl.pallas_call(
        functools.partial(_experts_body, blk=blk),
        grid_spec=pltpu.PrefetchScalarGridSpec(
            num_scalar_prefetch=2, grid=(n_blocks,),
            in_specs=[tile_in, wspec(w1), wspec(w3), wspec(w2)], out_specs=tile_out),
        out_shape=jax.ShapeDtypeStruct(xb.shape, F32),
        compiler_params=_params("arbitrary"),
        name="moe_experts",
    )(block_e, nb_used, xb, w1, w3, w2)


def _combine_body(d1_ref, d2_ref, x1_ref, meta_ref, yb_ref, out_ref, buf_ref, sem, *, tm):
    base = pl.program_id(0) * tm

    def copy(slot, t, d):
        return pltpu.make_async_copy(_token_rows(yb_ref, d), _token_rows(buf_ref.at[slot], t), sem)

    def issue(t, carry):
        copy(0, t, d1_ref[base + t]).start()
        copy(1, t, d2_ref[base + t]).start()
        return carry

    lax.fori_loop(0, tm, issue, 0)

    def drain(t, carry):
        copy(0, 0, 0).wait()
        copy(1, 0, 0).wait()
        return carry

    lax.fori_loop(0, tm, drain, 0)

    def rows_of(slot):
        return jnp.concatenate([buf_ref[slot, pl.ds(s, tm, stride=SUBLANE), :] for s in range(D_MODEL // LANE)],
                               axis=1)

    meta = meta_ref[...]
    out_ref[...] = x1_ref[...] + (meta[:, 2:3] * rows_of(0) + meta[:, 3:4] * rows_of(1))


def _combine(dest1, dest2, x1, meta, yb):
    t = x1.shape[0]
    tm = min(GATHER_TILE, t)
    return pl.pallas_call(
        functools.partial(_combine_body, tm=tm),
        grid_spec=pltpu.PrefetchScalarGridSpec(
            num_scalar_prefetch=2, grid=(t // tm,),
            in_specs=[pl.BlockSpec((tm, D_MODEL), lambda i, a, b: (i, 0)),
                      pl.BlockSpec((tm, LANE), lambda i, a, b: (i, 0)),
                      pl.BlockSpec(memory_space=pl.ANY)],
            out_specs=pl.BlockSpec((tm, D_MODEL), lambda i, a, b: (i, 0)),
            scratch_shapes=[pltpu.VMEM((2, tm * SUBLANE, LANE), F32), pltpu.SemaphoreType.DMA(())]),
        out_shape=jax.ShapeDtypeStruct((t, D_MODEL), F32),
        compiler_params=_params("arbitrary"),
        name="moe_combine",
    )(dest1, dest2, x1, meta, yb)


def _moe(x1, h2_tiles, meta, counts, w1, w3, w2):
    t = x1.shape[0]
    n_blocks = (2 * t + N_EXPERTS * (MOE_BLK - 1)) // MOE_BLK
    e1, e2 = meta[:, 0].astype(I32), meta[:, 1].astype(I32)
    r1, r2 = meta[:, 4].astype(I32), meta[:, 5].astype(I32)
    cnt = counts[0, :N_EXPERTS].astype(I32)
    padded = (cnt + MOE_BLK - 1) // MOE_BLK * MOE_BLK
    pad_end = jnp.cumsum(padded)
    pad_start = pad_end - padded
    dest1 = pad_start[e1] + r1
    dest2 = pad_start[e2] + r2
    nb_used = (pad_end[-1:] // MOE_BLK).astype(I32)
    block_e = jnp.minimum(jnp.searchsorted(pad_end, jnp.arange(n_blocks, dtype=I32) * MOE_BLK, side='right'),
                          N_EXPERTS - 1).astype(I32)
    xb = _scatter(dest1, dest2, h2_tiles, jnp.zeros((n_blocks * MOE_BLK * SUBLANE, LANE), F32))
    yb = _experts(block_e, nb_used, xb, w1, w3, w2, n_blocks)
    return _combine(dest1, dest2, x1, meta, yb)


def _final_body(x_ref, nw_ref, o_ref):
    o_ref[...] = _rms(x_ref[...]) * nw_ref[...]


def _final_norm(x, nw):
    t = x.shape[0]
    tm = min(ROW_TILE, t)
    return pl.pallas_call(
        _final_body,
        grid=(t // tm,),
        in_specs=[pl.BlockSpec((tm, D_MODEL), lambda i: (i, 0)), pl.BlockSpec((1, D_MODEL), lambda i: (0, 0))],
        out_specs=pl.BlockSpec((tm, D_MODEL), lambda i: (i, 0)),
        out_shape=jax.ShapeDtypeStruct((t, D_MODEL), F32),
        compiler_params=_params("parallel"),
        name="final_norm",
    )(x, nw)


def _rope_tables(pos):
    half = RET_DK // 2
    inv = ROPE_BASE ** (-jnp.arange(half, dtype=F32) / half)
    ang = pos[:, None] * inv[None, :]
    cos, sin = jnp.cos(ang), jnp.sin(ang)
    cos_t = jnp.tile(jnp.concatenate([cos, cos], axis=1), (1, RET_HEADS))
    sin_t = jnp.tile(jnp.concatenate([-sin, sin], axis=1), (1, RET_HEADS))
    return cos_t, sin_t


def _block_diag(blocks):
    g, r, c = blocks.shape
    eye = jnp.eye(g, dtype=bool)
    return jnp.where(eye[:, None, :, None], blocks[:, :, None, :], 0).reshape(g * r, g * c)


def _layer_weights(l, w_in, s5, s5_b_bar, s5_c_re, s5_c_im, s5_d, s5_glu_w, s5_glu_b, gdn_conv_w, gdn_a_log,
                   gdn_dt_bias, gdn_norm_w, w_out, router_group_w, router_group_b, router_expert_w,
                   router_expert_b):
    wi = w_in[l]
    g0 = RET_IN + S5_IN
    w_ab = jnp.pad(wi[:, g0 + GDN_QKV + GDN_HD:], ((0, 0), (0, LANE - 2 * GDN_HEADS)))
    in_ws = [wi[:, :RET_IN], wi[:, g0:g0 + GDN_QKV], wi[:, RET_IN:g0], wi[:, g0 + GDN_QKV:g0 + GDN_QKV + GDN_HD],
             w_ab]
    in_ws = [w.astype(BF16) for w in in_ws]
    ab_re, ab_im, bb_re, bb_im = s5
    n = S5_LANES
    sl = slice(l * n, (l + 1) * n)
    ab = jnp.concatenate([ab_re[sl].reshape(1, n), ab_im[sl].reshape(1, n)], axis=0)
    to_bd = lambda m: _block_diag(jnp.swapaxes(m[sl].reshape(S5_GROUPS, S5_STATE, S5_GROUP_CH), 1, 2))
    wb = jnp.concatenate([to_bd(bb_re), to_bd(bb_im)], axis=1).astype(BF16)
    wc_re = _block_diag(jnp.swapaxes(s5_c_re[l], 1, 2)).astype(BF16)
    wc_im = _block_diag(jnp.swapaxes(s5_c_im[l], 1, 2)).astype(BF16)
    s5_ws = (ab, wb, wc_re, wc_im, s5_d[l].reshape(1, S5_CH), s5_glu_w[l].astype(BF16),
             s5_glu_b[l].reshape(1, S5_CH))
    pad4 = lambda v: jnp.pad(v.reshape(1, GDN_HEADS), ((0, 0), (0, LANE - GDN_HEADS)))
    gdn_ws = (gdn_conv_w[l], pad4(gdn_a_log[l]), pad4(gdn_dt_bias[l]),
              jnp.tile(gdn_norm_w[l], GDN_HEADS).reshape(1, GDN_HD))
    wo = w_out[l].astype(BF16)
    out_ws = (wo[:RET_VD], wo[RET_VD:RET_VD + S5_CH], wo[RET_VD + S5_CH:])
    wr = jnp.pad(jnp.concatenate([router_expert_w[l], router_group_w[l]], axis=1),
                 ((0, 0), (0, LANE - N_EXPERTS - MOE_GROUPS))).astype(BF16)
    br = jnp.pad(jnp.concatenate([router_expert_b[l], router_group_b[l]]),
                 (0, LANE - N_EXPERTS - MOE_GROUPS)).reshape(1, LANE)
    return in_ws, s5_ws, gdn_ws, out_ws, (wr, br)


def _trunk(x, pos, states, layer_ws, norm_mix, norm_ffn, experts, norm_final):
    bsz, seq, _ = x.shape
    t = bsz * seq
    ret_s, s5_re, s5_im, gdn_s, gdn_buf = states
    cos_t, sin_t = _rope_tables(pos)
    x = x.reshape(t, D_MODEL)
    outs = [[], [], [], [], []]
    for l in range(DEPTH):
        in_ws, s5_ws, gdn_ws, out_ws, (wr, br) = layer_ws[l]
        zr, zq, zs, zg, zab = _inproj(x, norm_mix[l].reshape(1, D_MODEL), in_ws)
        o_ret, ret_fin = _retention(zr, cos_t, sin_t, ret_s[l].reshape(bsz, RET_QD, RET_DV), bsz, seq)
        u_tm = jnp.swapaxes(zs.reshape(bsz, seq, S5_CH), 0, 1)
        o_s5_tm, re_fin, im_fin = _s5(u_tm, s5_re[l].reshape(bsz, S5_LANES), s5_im[l].reshape(bsz, S5_LANES),
                                      *s5_ws)
        o_s5 = jnp.swapaxes(o_s5_tm, 0, 1).reshape(t, S5_CH)
        buf0 = jnp.pad(gdn_buf[l], ((0, 0), (SUBLANE - (GDN_CONV - 1), 0), (0, 0)))
        s0 = jax.vmap(_block_diag)(gdn_s[l])
        o_gdn, nbuf, gdn_fin = _gdn(zq, zg, zab, *gdn_ws, buf0, s0, bsz, seq)
        x1, h2_tiles, meta, counts = _outproj(x, o_ret, o_s5, o_gdn, *out_ws, norm_ffn[l].reshape(1, D_MODEL),
                                              wr, br)
        x = _moe(x1, h2_tiles, meta, counts, *experts[l])
        gdn_fin = gdn_fin.reshape(bsz, GDN_HEADS, GDN_DK, GDN_HEADS, GDN_DV)
        gdn_fin = jnp.stack([gdn_fin[:, h, :, h, :] for h in range(GDN_HEADS)], axis=1)
        for lst, s in zip(outs, (ret_fin.reshape(bsz, RET_HEADS, RET_DK, RET_DV),
                                 re_fin.reshape(bsz, S5_GROUPS, S5_STATE),
                                 im_fin.reshape(bsz, S5_GROUPS, S5_STATE),
                                 gdn_fin, nbuf[:, SUBLANE - (GDN_CONV - 1):, :])):
            lst.append(s)
    y = _final_norm(x, norm_final.reshape(1, D_MODEL)).reshape(bsz, seq, D_MODEL)
    return y, tuple(jnp.stack(o) for o in outs)


def kernel(x_prompt, x_sample, state_ret, state_s5_re, state_s5_im, state_gdn, state_gdn_conv, norm_mix, w_in, s5_a_re, s5_a_im, s5_log_dt, s5_b_re, s5_b_im, s5_c_re, s5_c_im, s5_d, s5_glu_w, s5_glu_b, gdn_conv_w, gdn_a_log, gdn_dt_bias, gdn_norm_w, w_out, norm_ffn, router_group_w, router_group_b, router_expert_w, router_expert_b, expert_w1, expert_w3, expert_w2, norm_final):
    n = DEPTH * S5_LANES
    col = lambda a: a.reshape(n, 1)
    log_dt = jnp.broadcast_to(s5_log_dt[:, :, None], (DEPTH, S5_GROUPS, S5_STATE))
    s5 = _s5_prep(col(s5_a_re), col(s5_a_im), col(log_dt), s5_b_re.reshape(n, S5_GROUP_CH),
                  s5_b_im.reshape(n, S5_GROUP_CH))
    layer_ws = [_layer_weights(l, w_in, s5, None, s5_c_re, s5_c_im, s5_d, s5_glu_w, s5_glu_b, gdn_conv_w,
                               gdn_a_log, gdn_dt_bias, gdn_norm_w, w_out, router_group_w, router_group_b,
                               router_expert_w, router_expert_b) for l in range(DEPTH)]
    experts = [(expert_w1[l].astype(BF16), expert_w3[l].astype(BF16), expert_w2[l].astype(BF16))
               for l in range(DEPTH)]
    bp, lp, _ = x_prompt.shape
    zero_states = (jnp.zeros((DEPTH, bp, RET_HEADS, RET_DK, RET_DV), F32),
                   jnp.zeros((DEPTH, bp, S5_GROUPS, S5_STATE), F32),
                   jnp.zeros((DEPTH, bp, S5_GROUPS, S5_STATE), F32),
                   jnp.zeros((DEPTH, bp, GDN_HEADS, GDN_DK, GDN_DV), F32),
                   jnp.zeros((DEPTH, bp, GDN_CONV - 1, GDN_QKV), F32))
    y_p, p_states = _trunk(x_prompt, jnp.arange(lp, dtype=F32), zero_states, layer_ws, norm_mix, norm_ffn,
                           experts, norm_final)
    y_s, s_states = _trunk(x_sample, PAST_LEN + jnp.arange(x_sample.shape[1], dtype=F32),
                           (state_ret, state_s5_re, state_s5_im, state_gdn, state_gdn_conv), layer_ws, norm_mix,
                           norm_ffn, experts, norm_final)
    return (y_p, y_s) + p_states + s_states
```

```python
import functools
import math

import jax
import jax.numpy as jnp
from jax import lax
from jax.experimental import pallas as pl
from jax.experimental.pallas import tpu as pltpu

F32 = jnp.float32
BF16 = jnp.bfloat16
I32 = jnp.int32

D_MODEL = 1024
DEPTH = 2
RET_HEADS, RET_DK, RET_DV = 4, 64, 128
S5_GROUPS, S5_GROUP_CH, S5_STATE = 16, 16, 64
S5_CH = S5_GROUPS * S5_GROUP_CH
S5_LANES = S5_GROUPS * S5_STATE
GDN_HEADS, GDN_DK, GDN_DV, GDN_CONV = 4, 64, 64, 4
GDN_HD = GDN_HEADS * GDN_DK
GDN_QKV = 3 * GDN_HD
RET_QD = RET_HEADS * RET_DK
RET_VD = RET_HEADS * RET_DV
RET_IN = 2 * RET_QD + 2 * RET_VD
S5_IN = S5_CH
GDN_IN = GDN_QKV + GDN_HD + 2 * GDN_HEADS
GDN_CHUNK = 64
RET_CHUNK = 64
MOE_GROUPS, EXPERTS_PER_GROUP = 4, 8
N_EXPERTS = MOE_GROUPS * EXPERTS_PER_GROUP
D_EXPERT = 512
ROPE_BASE = 10000.0
EPS = 1e-6
PAST_LEN = 16384

LANE = 128
SUBLANE = 8
ROW_TILE = 512
MOE_BLK = 256
GATHER_TILE = 256
VMEM_LIMIT = 56 * 1024 * 1024


def _params(*sem):
    return pltpu.CompilerParams(dimension_semantics=sem, vmem_limit_bytes=VMEM_LIMIT)


def _dot(a, b):
    return jnp.dot(a.astype(BF16), b.astype(BF16), preferred_element_type=F32)


def _dot_nt(a, b):
    return lax.dot_general(a.astype(BF16), b.astype(BF16), (((1,), (1,)), ((), ())),
                           preferred_element_type=F32)


def _dot_tn(a, b):
    return lax.dot_general(a.astype(BF16), b.astype(BF16), (((0,), (0,)), ((), ())),
                           preferred_element_type=F32)


def _dot_hi(a, b):
    a1 = a.astype(BF16)
    a2 = (a - a1.astype(F32)).astype(BF16)
    b1 = b.astype(BF16)
    b2 = (b - b1.astype(F32)).astype(BF16)
    d = lambda x, y: jnp.dot(x, y, preferred_element_type=F32)
    return d(a1, b1) + (d(a1, b2) + d(a2, b1))


def _split3(x):
    p1 = x.astype(BF16)
    r1 = x - p1.astype(F32)
    p2 = r1.astype(BF16)
    p3 = (r1 - p2.astype(F32)).astype(BF16)
    return p1, p2, p3


def _dot_sel_l(sel, x):
    p1, p2, p3 = _split3(x)
    d = lambda p: jnp.dot(sel, p, preferred_element_type=F32)
    return d(p1) + d(p2) + d(p3)


def _dot_sel_r(x, sel):
    p1, p2, p3 = _split3(x)
    d = lambda p: jnp.dot(p, sel, preferred_element_type=F32)
    return d(p1) + d(p2) + d(p3)


def _rms(x):
    return x * lax.rsqrt(jnp.mean(x * x, axis=-1, keepdims=True) + EPS)


def _silu(x):
    return x * jax.nn.sigmoid(x)


def _inproj_body(x_ref, nw_ref, wr_ref, wq_ref, ws_ref, wg_ref, wab_ref,
                 zr_ref, zq_ref, zs_ref, zg_ref, zab_ref):
    h = (_rms(x_ref[...]) * nw_ref[...]).astype(BF16)
    for w_ref, z_ref in ((wr_ref, zr_ref), (wq_ref, zq_ref), (ws_ref, zs_ref),
                         (wg_ref, zg_ref), (wab_ref, zab_ref)):
        z_ref[...] = jnp.dot(h, w_ref[...], preferred_element_type=F32)


def _inproj(x, nw, ws):
    t = x.shape[0]
    tm = min(ROW_TILE, t)
    widths = [w.shape[1] for w in ws]
    row = lambda n: pl.BlockSpec((tm, n), lambda i: (i, 0))
    full = lambda a: pl.BlockSpec(a.shape, lambda i: (0, 0))
    return pl.pallas_call(
        _inproj_body,
        grid=(t // tm,),
        in_specs=[row(D_MODEL), full(nw)] + [full(w) for w in ws],
        out_specs=[row(n) for n in widths],
        out_shape=[jax.ShapeDtypeStruct((t, n), F32) for n in widths],
        compiler_params=_params("parallel"),
        name="inproj",
    )(x, nw, *ws)


def _ret_log_gamma(h):
    return math.log(1.0 - 2.0 ** (-5.0 - h))


def _ret_body(zr_ref, cos_ref, sin_ref, s0_ref, o_ref, sfin_ref, s_scr, *, rows, chunk):
    j = pl.program_id(1)
    stack = RET_HEADS * chunk

    @pl.when(j == 0)
    def _():
        s_scr[...] = s0_ref[...]

    lane = lax.broadcasted_iota(I32, (rows, RET_QD), 1)
    first_half = (lane % RET_DK) < (RET_DK // 2)

    def rotary(x):
        swapped = jnp.where(first_half, pltpu.roll(x, RET_QD - RET_DK // 2, 1),
                            pltpu.roll(x, RET_DK // 2, 1))
        return x * cos_ref[...] + swapped * sin_ref[...]

    q = rotary(zr_ref[:, 0:RET_QD])
    k = rotary(zr_ref[:, RET_QD:2 * RET_QD]) * (RET_DK ** -0.5)

    def by_head(idx, fn):
        out = jnp.zeros(idx.shape, F32)
        for h in range(RET_HEADS):
            out = jnp.where(idx == h, fn(h), out)
        return out

    lg_lane = by_head(lax.broadcasted_iota(I32, (chunk, RET_QD), 1) // RET_DK, _ret_log_gamma)
    pos = lax.broadcasted_iota(I32, (chunk, RET_QD), 0).astype(F32)
    q_scale = jnp.exp((pos + 1.0) * lg_lane)
    k_scale = jnp.exp((chunk - 1.0 - pos) * lg_lane)
    st_row = lax.broadcasted_iota(I32, (stack, stack), 0)
    st_col = lax.broadcasted_iota(I32, (stack, stack), 1)
    causal = ((st_row // chunk) == (st_col // chunk)) & (st_row >= st_col)
    lg_stack = by_head(st_row // chunk, _ret_log_gamma)
    decay = jnp.where(causal, jnp.exp(jnp.where(causal, (st_row - st_col).astype(F32), 0.0) * lg_stack), 0.0)
    q_rows = (lax.broadcasted_iota(I32, (stack, RET_QD), 0) // chunk) == \
             (lax.broadcasted_iota(I32, (stack, RET_QD), 1) // RET_DK)
    v_rows = (lax.broadcasted_iota(I32, (stack, RET_VD), 0) // chunk) == \
             (lax.broadcasted_iota(I32, (stack, RET_VD), 1) // RET_DV)
    s_row_head = lax.broadcasted_iota(I32, (RET_QD, RET_DV), 0) // RET_DK
    s_decay = by_head(s_row_head, lambda h: math.exp(chunk * _ret_log_gamma(h)))
    kv_diag = (lax.broadcasted_iota(I32, (RET_QD, RET_VD), 0) // RET_DK) == \
              (lax.broadcasted_iota(I32, (RET_QD, RET_VD), 1) // RET_DV)

    def tile4(a):
        return jnp.concatenate([a] * RET_HEADS, axis=0)

    def collapse(a):
        out = a[0:chunk]
        for h in range(1, RET_HEADS):
            out = out + a[h * chunk:(h + 1) * chunk]
        return out

    s = s_scr[...]
    for ci in range(rows // chunk):
        r = slice(ci * chunk, (ci + 1) * chunk)
        q_c, k_c = q[r], k[r]
        v_c = zr_ref[r, 2 * RET_QD:2 * RET_QD + RET_VD]
        gate = zr_ref[r, 2 * RET_QD + RET_VD:2 * RET_QD + 2 * RET_VD]
        scores = _dot_nt(jnp.where(q_rows, tile4(q_c), 0.0), tile4(k_c)) * decay
        o_intra = collapse(_dot(scores, jnp.where(v_rows, tile4(v_c), 0.0)))
        s_full = jnp.concatenate([jnp.where(s_row_head == h, s, 0.0) for h in range(RET_HEADS)], axis=1)
        o_c = o_intra + _dot(q_c * q_scale, s_full)
        for h in range(RET_HEADS):
            sl = slice(h * RET_DV, (h + 1) * RET_DV)
            o_ref[r, sl] = _rms(o_c[:, sl]) * _silu(gate[:, sl])
        kv = jnp.where(kv_diag, _dot_tn(k_c * k_scale, v_c), 0.0)
        kv_own = kv[:, 0:RET_DV]
        for h in range(1, RET_HEADS):
            kv_own = kv_own + kv[:, h * RET_DV:(h + 1) * RET_DV]
        s = s * s_decay + kv_own
    s_scr[...] = s

    @pl.when(j == pl.num_programs(1) - 1)
    def _():
        sfin_ref[...] = s


def _retention(zr, cos, sin, s0, bsz, seq):
    chunk = min(RET_CHUNK, seq)
    rows = min(4 * chunk, seq)
    nj = seq // rows
    return pl.pallas_call(
        functools.partial(_ret_body, rows=rows, chunk=chunk),
        grid=(bsz, nj),
        in_specs=[pl.BlockSpec((rows, RET_IN), lambda b, j: (b * nj + j, 0)),
                  pl.BlockSpec((rows, RET_QD), lambda b, j: (j, 0)),
                  pl.BlockSpec((rows, RET_QD), lambda b, j: (j, 0)),
                  pl.BlockSpec((None, RET_QD, RET_DV), lambda b, j: (b, 0, 0))],
        out_specs=[pl.BlockSpec((rows, RET_VD), lambda b, j: (b * nj + j, 0)),
                   pl.BlockSpec((None, RET_QD, RET_DV), lambda b, j: (b, 0, 0))],
        out_shape=[jax.ShapeDtypeStruct((bsz * seq, RET_VD), F32),
                   jax.ShapeDtypeStruct((bsz, RET_QD, RET_DV), F32)],
        scratch_shapes=[pltpu.VMEM((RET_QD, RET_DV), F32)],
        compiler_params=_params("parallel", "arbitrary"),
        name="retention",
    )(zr, cos, sin, s0)


def _s5_prep_body(are_ref, aim_ref, ldt_ref, bre_ref, bim_ref, abre_ref, abim_ref, bbre_ref, bbim_ref):
    lam_re, lam_im = are_ref[...], aim_ref[...]
    dt = jnp.exp(ldt_ref[...])
    mag = jnp.exp(lam_re * dt)
    ab_re = mag * jnp.cos(lam_im * dt)
    ab_im = mag * jnp.sin(lam_im * dt)
    den = lam_re * lam_re + lam_im * lam_im
    f_re = ((ab_re - 1.0) * lam_re + ab_im * lam_im) / den
    f_im = (ab_im * lam_re - (ab_re - 1.0) * lam_im) / den
    abre_ref[...] = ab_re
    abim_ref[...] = ab_im
    bbre_ref[...] = f_re * bre_ref[...] - f_im * bim_ref[...]
    bbim_ref[...] = f_re * bim_ref[...] + f_im * bre_ref[...]


def _s5_prep(a_re, a_im, log_dt, b_re, b_im):
    n = a_re.shape[0]
    col = jax.ShapeDtypeStruct((n, 1), F32)
    mat = jax.ShapeDtypeStruct((n, S5_GROUP_CH), F32)
    return pl.pallas_call(_s5_prep_body, out_shape=[col, col, mat, mat], name="s5_prep")(
        a_re, a_im, log_dt, b_re, b_im)


def _gelu_tanh(x):
    return x * (0.5 * (1.0 + jnp.tanh(math.sqrt(2.0 / math.pi) * (x + 0.044715 * (x * x * x)))))


def _s5_body(u_ref, h0re_ref, h0im_ref, ab_ref, wb_ref, wcre_ref, wcim_ref, d_ref, gw_ref, gb_ref,
             o_ref, hre_ref, him_ref, bu_scr, st_scr, *, steps):
    j = pl.program_id(1)

    @pl.when(j == 0)
    def _():
        st_scr[0] = h0re_ref[...]
        st_scr[1] = h0im_ref[...]

    rows = steps * SUBLANE
    u = u_ref[...].reshape(rows, S5_CH)
    bu_scr[...] = _dot(u, wb_ref[...])
    a_re = jnp.broadcast_to(ab_ref[0:1, :], (SUBLANE, S5_LANES))
    a_im = jnp.broadcast_to(ab_ref[1:2, :], (SUBLANE, S5_LANES))

    def step(t, carry):
        h_re, h_im = carry
        r = pl.ds(pl.multiple_of(t * SUBLANE, SUBLANE), SUBLANE)
        n_re = a_re * h_re - a_im * h_im + bu_scr[r, 0:S5_LANES]
        n_im = a_re * h_im + a_im * h_re + bu_scr[r, S5_LANES:2 * S5_LANES]
        bu_scr[r, 0:S5_LANES] = n_re
        bu_scr[r, S5_LANES:2 * S5_LANES] = n_im
        return n_re, n_im

    h_re, h_im = lax.fori_loop(0, steps, step, (st_scr[0], st_scr[1]))
    st_scr[0] = h_re
    st_scr[1] = h_im
    hre_ref[...] = h_re
    him_ref[...] = h_im

    y = _dot(bu_scr[:, 0:S5_LANES], wcre_ref[...]) - _dot(bu_scr[:, S5_LANES:2 * S5_LANES], wcim_ref[...])
    y = _gelu_tanh(y + d_ref[...] * u)
    y = y * jax.nn.sigmoid(_dot(y, gw_ref[...]) + gb_ref[...])
    o_ref[...] = y.reshape(steps, SUBLANE, S5_CH)


def _s5(u_tm, h0_re, h0_im, ab, wb, wc_re, wc_im, d_skip, glu_w, glu_b):
    seq, bsz, _ = u_tm.shape
    steps = min(64, seq)
    full = lambda a: pl.BlockSpec(a.shape, lambda g, j: (0,) * a.ndim)
    st = pl.BlockSpec((SUBLANE, S5_LANES), lambda g, j: (g, 0))
    return pl.pallas_call(
        functools.partial(_s5_body, steps=steps),
        grid=(bsz // SUBLANE, seq // steps),
        in_specs=[pl.BlockSpec((steps, SUBLANE, S5_CH), lambda g, j: (j, g, 0)), st, st,
                  full(ab), full(wb), full(wc_re), full(wc_im), full(d_skip), full(glu_w), full(glu_b)],
        out_specs=[pl.BlockSpec((steps, SUBLANE, S5_CH), lambda g, j: (j, g, 0)), st, st],
        out_shape=[jax.ShapeDtypeStruct((seq, bsz, S5_CH), F32),
                   jax.ShapeDtypeStruct((bsz, S5_LANES), F32),
                   jax.ShapeDtypeStruct((bsz, S5_LANES), F32)],
        scratch_shapes=[pltpu.VMEM((steps * SUBLANE, 2 * S5_LANES), F32),
                        pltpu.VMEM((2, SUBLANE, S5_LANES), F32)],
        compiler_params=_params("parallel", "arbitrary"),
        name="s5",
    )(u_tm, h0_re, h0_im, ab, wb, wc_re, wc_im, d_skip, glu_w, glu_b)


def _gdn_body(zq_ref, zg_ref, zab_ref, cw_ref, alog_ref, dtb_ref, nw_ref, buf0_ref, s0_ref,
              ones_bd_ref, tri_ref, ea_ref, eb_ref, ec_ref, lbd_ref, spread_ref, gather_ref,
              o_ref, nbuf_ref, sfin_ref,
              ext_scr, q_scr, k_scr, kb_scr, vb_scr, g_scr, g4_scr, s_scr, *, rows, chunk):
    j = pl.program_id(1)
    stack = GDN_HEADS * chunk
    bd_state = (lax.broadcasted_iota(I32, (GDN_HD, GDN_HD), 0) // GDN_DK) == \
               (lax.broadcasted_iota(I32, (GDN_HD, GDN_HD), 1) // GDN_DV)

    @pl.when(j == 0)
    def _():
        ext_scr[0:SUBLANE, :] = buf0_ref[...]
        s_scr[...] = jnp.where(bd_state, _dot_sel_r(s0_ref[...], spread_ref[...]), 0.0)

    ext_scr[SUBLANE:SUBLANE + rows, :] = zq_ref[...]
    conv = ext_scr[SUBLANE - 3:SUBLANE - 3 + rows, :] * cw_ref[0:1, :]
    for i in range(1, GDN_CONV):
        conv = conv + ext_scr[SUBLANE - 3 + i:SUBLANE - 3 + i + rows, :] * cw_ref[i:i + 1, :]
    tail = ext_scr[rows:rows + SUBLANE, :]
    nbuf_ref[...] = tail
    ext_scr[0:SUBLANE, :] = tail
    qkv = _silu(conv)

    ones_bd = ones_bd_ref[...]
    q_raw = qkv[:, 0:GDN_HD]
    k_raw = qkv[:, GDN_HD:2 * GDN_HD]
    q_scr[...] = q_raw * lax.rsqrt(_dot_sel_r(q_raw * q_raw, ones_bd) + EPS) * (GDN_DK ** -0.5)
    k_n = k_raw * lax.rsqrt(_dot_sel_r(k_raw * k_raw, ones_bd) + EPS)
    k_scr[...] = k_n

    ab = zab_ref[...]
    x = ab + dtb_ref[...]
    softplus = jnp.maximum(x, 0.0) + jnp.log1p(jnp.exp(-jnp.abs(x)))
    g_pad = -jnp.exp(alog_ref[...]) * softplus
    beta = _dot_sel_r(jax.nn.sigmoid(ab), eb_ref[...])
    g_cum = _dot_sel_l(tri_ref[...], g_pad)
    g_scr[...] = _dot_sel_r(g_cum, ea_ref[...])
    g4_scr[...] = _dot_sel_r(g_pad, ec_ref[...])
    kb_scr[...] = k_n * beta
    vb_scr[...] = qkv[:, 2 * GDN_HD:3 * GDN_HD] * beta

    st_row = lax.broadcasted_iota(I32, (stack, stack), 0)
    st_col = lax.broadcasted_iota(I32, (stack, stack), 1)
    same_head = (st_row // chunk) == (st_col // chunk)
    strict = same_head & (st_row > st_col)
    causal = same_head & (st_row >= st_col)
    head_rows = (lax.broadcasted_iota(I32, (stack, GDN_HD), 0) // chunk) == \
                (lax.broadcasted_iota(I32, (stack, GDN_HD), 1) // GDN_DK)
    head_rows2 = jnp.concatenate([head_rows, head_rows], axis=1)
    lbd = lbd_ref[...]

    def tile4(a):
        return jnp.concatenate([a] * GDN_HEADS, axis=0)

    def collapse(a):
        out = a[0:chunk]
        for h in range(1, GDN_HEADS):
            out = out + a[h * chunk:(h + 1) * chunk]
        return out

    def chunk_step(ci, carry):
        r = pl.ds(pl.multiple_of(ci * chunk, chunk), chunk)
        q_c, k_c, kb_c, vb_c, g_c = q_scr[r, :], k_scr[r, :], kb_scr[r, :], vb_scr[r, :], g_scr[r, :]
        exp_g = jnp.exp(g_c)
        g_last = g_c[chunk - 1:chunk, :]
        g_diff = _dot_sel_l(lbd, jnp.where(strict, tile4(g4_scr[r, :]), 0.0))
        decay = jnp.exp(jnp.where(causal, g_diff, 0.0))
        k_rows = tile4(k_c)
        lmat = jnp.where(strict, _dot_nt(jnp.where(head_rows, tile4(kb_c), 0.0), k_rows) * decay, 0.0)
        attn = jnp.where(causal, _dot_nt(jnp.where(head_rows, tile4(q_c), 0.0), k_rows) * decay, 0.0)
        t_acc = -lmat
        power = lmat
        span = 2
        while span < chunk + 1:
            power = _dot_hi(power, power)
            t_acc = t_acc + power + _dot_hi(t_acc, power)
            span *= 2
        rhs = jnp.where(head_rows2, tile4(jnp.concatenate([vb_c, kb_c * exp_g], axis=1)), 0.0)
        sol = rhs + _dot_hi(t_acc, rhs)
        u_c = collapse(sol[:, 0:GDN_HD])
        w_c = collapse(sol[:, GDN_HD:2 * GDN_HD])
        s_old = s_scr[...]
        v_new = u_c - _dot(w_c, s_old)
        o_c = _dot(q_c * exp_g, s_old) + collapse(_dot(attn, jnp.where(head_rows, tile4(v_new), 0.0)))
        k_dec = k_c * jnp.exp(g_last - g_c)
        s_scr[...] = s_old * jnp.exp(g_last) + jnp.where(bd_state, _dot_tn(k_dec, v_new), 0.0)
        ms = _dot_sel_r(o_c * o_c, ones_bd) * (1.0 / GDN_DV)
        o_ref[r, :] = (o_c * lax.rsqrt(ms + EPS) * nw_ref[...] * _silu(zg_ref[r, :])).astype(o_ref.dtype)
        return carry

    lax.fori_loop(0, rows // chunk, chunk_step, 0)

    @pl.when(j == pl.num_programs(1) - 1)
    def _():
        sfin_ref[...] = _dot_sel_r(s_scr[...], gather_ref[...])


def _gdn_consts(rows, chunk):
    stack = GDN_HEADS * chunk
    blk = lambda n, c: (jnp.arange(n)[:, None] // c) == (jnp.arange(n)[None, :] // c)
    ones_bd = blk(GDN_HD, GDN_DK).astype(BF16)
    tri = (blk(rows, chunk) & (jnp.arange(rows)[:, None] >= jnp.arange(rows)[None, :])).astype(BF16)
    lbd = (blk(stack, chunk) & (jnp.arange(stack)[:, None] >= jnp.arange(stack)[None, :])).astype(BF16)
    src = jnp.arange(LANE)[:, None]
    ea = (src == jnp.arange(GDN_HD)[None, :] // GDN_DK).astype(BF16)
    eb = (src == GDN_HEADS + jnp.arange(GDN_HD)[None, :] // GDN_DK).astype(BF16)
    ec = (src == jnp.arange(stack)[None, :] // chunk).astype(BF16)
    spread = (jnp.arange(GDN_DV)[:, None] == jnp.arange(GDN_HD)[None, :] % GDN_DV).astype(BF16)
    return ones_bd, tri, ea, eb, ec, lbd, spread, spread.T


def _gdn(zq, zg, zab, conv_w, alog_pad, dtb_pad, nw, buf0, s0, bsz, seq):
    chunk = min(GDN_CHUNK, seq)
    rows = min(4 * chunk, seq)
    nj = seq // rows
    stack = GDN_HEADS * chunk
    consts = _gdn_consts(rows, chunk)
    row = lambda n: pl.BlockSpec((rows, n), lambda b, j: (b * nj + j, 0))
    full = lambda a: pl.BlockSpec(a.shape, lambda b, j: (0,) * a.ndim)
    per_b = lambda *s: pl.BlockSpec((None,) + s, lambda b, j: (b, 0, 0))
    return pl.pallas_call(
        functools.partial(_gdn_body, rows=rows, chunk=chunk),
        grid=(bsz, nj),
        in_specs=[row(GDN_QKV), row(GDN_HD), row(LANE), full(conv_w), full(alog_pad), full(dtb_pad), full(nw),
                  per_b(SUBLANE, GDN_QKV), per_b(GDN_HD, GDN_DV)] + [full(c) for c in consts],
        out_specs=[row(GDN_HD), per_b(SUBLANE, GDN_QKV), per_b(GDN_HD, GDN_DV)],
        out_shape=[jax.ShapeDtypeStruct((bsz * seq, GDN_HD), F32),
                   jax.ShapeDtypeStruct((bsz, SUBLANE, GDN_QKV), F32),
                   jax.ShapeDtypeStruct((bsz, GDN_HD, GDN_DV), F32)],
        scratch_shapes=[pltpu.VMEM((rows + SUBLANE, GDN_QKV), F32)] +
                       [pltpu.VMEM((rows, GDN_HD), F32)] * 5 +
                       [pltpu.VMEM((rows, stack), F32), pltpu.VMEM((GDN_HD, GDN_HD), F32)],
        compiler_params=_params("parallel", "arbitrary"),
        name="gdn",
    )(zq, zg, zab, conv_w, alog_pad, dtb_pad, nw, buf0, s0, *consts)


def _outproj_body(x_ref, oret_ref, os5_ref, ogdn_ref, w1_ref, w2_ref, w3_ref, nw_ref, wr_ref, br_ref, tri_ref,
                  x1_ref, h2_ref, meta_ref, cnt_ref, carry_scr, *, tm):
    i = pl.program_id(0)

    @pl.when(i == 0)
    def _():
        carry_scr[...] = jnp.zeros_like(carry_scr)

    mix = _dot(oret_ref[...], w1_ref[...]) + _dot(os5_ref[...], w2_ref[...]) + _dot(ogdn_ref[...], w3_ref[...])
    x1 = x_ref[...] + mix
    x1_ref[...] = x1
    h2 = _rms(x1) * nw_ref[...]
    for s in range(D_MODEL // LANE):
        h2_ref[pl.ds(s, tm, stride=SUBLANE), :] = h2[:, s * LANE:(s + 1) * LANE]

    logits = _dot(h2, wr_ref[...]) + br_ref[...]
    lane_i = lax.broadcasted_iota(I32, (tm, LANE), 1)
    lane = lane_i.astype(F32)
    neg = -jnp.inf
    big = float(LANE)
    g_log = jnp.where((lane_i >= N_EXPERTS) & (lane_i < N_EXPERTS + MOE_GROUPS), logits, neg)
    g_max = jnp.max(g_log, axis=-1, keepdims=True)
    grp = jnp.min(jnp.where(g_log == g_max, lane - N_EXPERTS, big), axis=-1, keepdims=True)
    p_grp = 1.0 / jnp.sum(jnp.exp(g_log - g_max), axis=-1, keepdims=True)
    in_grp = (lane >= grp * EXPERTS_PER_GROUP) & (lane < (grp + 1.0) * EXPERTS_PER_GROUP)
    e_log = jnp.where(in_grp, logits, neg)
    v1 = jnp.max(e_log, axis=-1, keepdims=True)
    i1 = jnp.min(jnp.where(e_log == v1, lane, big), axis=-1, keepdims=True)
    e_log2 = jnp.where(lane == i1, neg, e_log)
    v2 = jnp.max(e_log2, axis=-1, keepdims=True)
    i2 = jnp.min(jnp.where(e_log2 == v2, lane, big), axis=-1, keepdims=True)
    e2 = jnp.exp(v2 - v1)
    gate1 = p_grp / (1.0 + e2)
    gate2 = p_grp * e2 / (1.0 + e2)

    oh1 = lane == i1
    oh2 = lane == i2
    picked = jnp.where(oh1 | oh2, 1.0, 0.0)
    before = jnp.dot(tri_ref[...], picked.astype(BF16), preferred_element_type=F32) + carry_scr[0:1, :]
    rank1 = jnp.sum(jnp.where(oh1, before, 0.0), axis=-1, keepdims=True)
    rank2 = jnp.sum(jnp.where(oh2, before, 0.0), axis=-1, keepdims=True)
    total = carry_scr[0:1, :] + jnp.sum(picked, axis=0, keepdims=True)
    carry_scr[...] = jnp.broadcast_to(total, carry_scr.shape)
    cnt_ref[...] = jnp.broadcast_to(total, cnt_ref.shape)

    meta = jnp.where(lane_i == 0, i1, 0.0)
    meta = jnp.where(lane_i == 1, i2, meta)
    meta = jnp.where(lane_i == 2, gate1, meta)
    meta = jnp.where(lane_i == 3, gate2, meta)
    meta = jnp.where(lane_i == 4, rank1, meta)
    meta = jnp.where(lane_i == 5, rank2, meta)
    meta_ref[...] = meta


def _outproj(x, o_ret, o_s5, o_gdn, w1, w2, w3, nw, wr, br):
    t = x.shape[0]
    tm = min(ROW_TILE, t)
    tri = (jnp.arange(tm)[:, None] > jnp.arange(tm)[None, :]).astype(BF16)
    row = lambda n: pl.BlockSpec((tm, n), lambda i: (i, 0))
    full = lambda a: pl.BlockSpec(a.shape, lambda i: (0, 0))
    return pl.pallas_call(
        functools.partial(_outproj_body, tm=tm),
        grid=(t // tm,),
        in_specs=[row(D_MODEL), row(RET_VD), row(S5_CH), row(GDN_HD),
                  full(w1), full(w2), full(w3), full(nw), full(wr), full(br), full(tri)],
        out_specs=[row(D_MODEL), pl.BlockSpec((tm * SUBLANE, LANE), lambda i: (i, 0)), row(LANE),
                   pl.BlockSpec((SUBLANE, LANE), lambda i: (0, 0))],
        out_shape=[jax.ShapeDtypeStruct((t, D_MODEL), F32),
                   jax.ShapeDtypeStruct((t * SUBLANE, LANE), F32),
                   jax.ShapeDtypeStruct((t, LANE), F32),
                   jax.ShapeDtypeStruct((SUBLANE, LANE), F32)],
        scratch_shapes=[pltpu.VMEM((SUBLANE, LANE), F32)],
        compiler_params=_params("arbitrary"),
        name="outproj_router",
    )(x, o_ret, o_s5, o_gdn, w1, w2, w3, nw, wr, br, tri)


def _token_rows(ref, idx):
    return ref.at[pl.ds(pl.multiple_of(idx * SUBLANE, SUBLANE), SUBLANE)]


def _slot(e_ref, r_ref, ps_ref, g):
    return ps_ref[e_ref[g]] + r_ref[g]


ZERO_ROWS = 128


def _zero_segment(zero_scr, xb_ref, sem, start, length, wait):
    def piece(off, n):
        cp = pltpu.make_async_copy(zero_scr.at[pl.ds(0, n * SUBLANE)],
                                   xb_ref.at[pl.ds(pl.multiple_of(off * SUBLANE, SUBLANE), n * SUBLANE)], sem)
        cp.wait() if wait else cp.start()

    n_big = length // ZERO_ROWS

    def big(i, carry):
        piece(start + i * ZERO_ROWS, ZERO_ROWS)
        return carry

    lax.fori_loop(0, n_big, big, 0)
    off = start + n_big * ZERO_ROWS
    rem = length - n_big * ZERO_ROWS
    bit = ZERO_ROWS // 2
    while bit >= 1:
        has = (rem & bit) != 0
        pl.when(has)(functools.partial(piece, off, bit))
        off = off + jnp.where(has, bit, 0)
        bit //= 2


def _scatter_body(e1_ref, e2_ref, r1_ref, r2_ref, ps_ref, zs_ref, zl_ref, src_ref, xb_ref, zero_scr, sem, zsem,
                  *, tm):
    base = pl.program_id(0) * tm

    @pl.when(pl.program_id(0) == 0)
    def _():
        zero_scr[...] = jnp.zeros_like(zero_scr)
        for wait in (False, True):
            lax.fori_loop(0, N_EXPERTS + 1,
                          lambda s, c, wait=wait: (_zero_segment(zero_scr, xb_ref, zsem, zs_ref[s], zl_ref[s], wait),
                                                   c)[1], 0)

    def copy(t, d):
        return pltpu.make_async_copy(_token_rows(src_ref, t), _token_rows(xb_ref, d), sem)

    def issue(t, carry):
        copy(t, _slot(e1_ref, r1_ref, ps_ref, base + t)).start()
        copy(t, _slot(e2_ref, r2_ref, ps_ref, base + t)).start()
        return carry

    lax.fori_loop(0, tm, issue, 0)

    def drain(t, carry):
        copy(0, 0).wait()
        copy(0, 0).wait()
        return carry

    lax.fori_loop(0, tm, drain, 0)


def _scatter(route, zero_start, zero_len, h2_tiles, n_rows):
    tokens = route[0].shape[0]
    tm = min(GATHER_TILE, tokens)
    assert tokens % tm == 0
    return pl.pallas_call(
        functools.partial(_scatter_body, tm=tm),
        grid_spec=pltpu.PrefetchScalarGridSpec(
            num_scalar_prefetch=7, grid=(tokens // tm,),
            in_specs=[pl.BlockSpec((tm * SUBLANE, LANE), lambda i, *_: (i, 0))],
            out_specs=pl.BlockSpec(memory_space=pl.ANY),
            scratch_shapes=[pltpu.VMEM((ZERO_ROWS * SUBLANE, LANE), F32), pltpu.SemaphoreType.DMA(()),
                            pltpu.SemaphoreType.DMA(())]),
        out_shape=jax.ShapeDtypeStruct((n_rows * SUBLANE, LANE), F32),
        compiler_params=_params("arbitrary"),
        name="moe_scatter",
    )(*route, zero_start, zero_len, h2_tiles)


def _experts_body(be_ref, nb_ref, xb_ref, w1_ref, w3_ref, w2_ref, yb_ref, w1_scr, w3_scr, w2_scr, *, blk):
    i = pl.program_id(0)
    live = i < nb_ref[0]

    @pl.when(live & ((i == 0) | (be_ref[i] != be_ref[jnp.maximum(i - 1, 0)])))
    def _():
        w1_scr[...] = w1_ref[...].astype(BF16)
        w3_scr[...] = w3_ref[...].astype(BF16)
        w2_scr[...] = w2_ref[...].astype(BF16)

    @pl.when(live)
    def _():
        x = jnp.concatenate([xb_ref[pl.ds(s, blk, stride=SUBLANE), :] for s in range(D_MODEL // LANE)],
                            axis=1).astype(BF16)
        hid = _silu(jnp.dot(x, w1_scr[...], preferred_element_type=F32)) * \
            jnp.dot(x, w3_scr[...], preferred_element_type=F32)
        y = _dot(hid, w2_scr[...])
        for s in range(D_MODEL // LANE):
            yb_ref[pl.ds(s, blk, stride=SUBLANE), :] = y[:, s * LANE:(s + 1) * LANE]

    @pl.when(jnp.logical_not(live))
    def _():
        yb_ref[...] = jnp.zeros_like(yb_ref)


def _experts(block_e, nb_used, xb, layer, w1, w3, w2, n_blocks):
    blk = MOE_BLK
    live = lambda i, nb: jnp.minimum(i, nb[0] - 1)
    tile_in = pl.BlockSpec((blk * SUBLANE, LANE), lambda i, be, nb: (live(i, nb), 0))
    tile_out = pl.BlockSpec((blk * SUBLANE, LANE), lambda i, be, nb: (i, 0))
    wspec = lambda a: pl.BlockSpec((None, None) + a.shape[2:],
                                   lambda i, be, nb: (layer, be[live(i, nb)], 0, 0))
    return pl.pallas_call(
        functools.partial(_experts_body, blk=blk),
        grid_spec=pltpu.PrefetchScalarGridSpec(
            num_scalar_prefetch=2, grid=(n_blocks,),
            in_specs=[tile_in, wspec(w1), wspec(w3), wspec(w2)], out_specs=tile_out,
            scratch_shapes=[pltpu.VMEM(w1.shape[2:], BF16), pltpu.VMEM(w3.shape[2:], BF16),
                            pltpu.VMEM(w2.shape[2:], BF16)]),
        out_shape=jax.ShapeDtypeStruct(xb.shape, F32),
        compiler_params=_params("arbitrary"),
        name="moe_experts",
    )(block_e, nb_used, xb, w1, w3, w2)


def _combine_body(e1_ref, e2_ref, r1_ref, r2_ref, ps_ref, x1_ref, meta_ref, yb_ref, out_ref, buf_ref, sem, *, tm):
    base = pl.program_id(0) * tm

    def copy(slot, t, d):
        return pltpu.make_async_copy(_token_rows(yb_ref, d), _token_rows(buf_ref.at[slot], t), sem)

    def issue(t, carry):
        copy(0, t, _slot(e1_ref, r1_ref, ps_ref, base + t)).start()
        copy(1, t, _slot(e2_ref, r2_ref, ps_ref, base + t)).start()
        return carry

    lax.fori_loop(0, tm, issue, 0)

    def drain(t, carry):
        copy(0, 0, 0).wait()
        copy(1, 0, 0).wait()
        return carry

    lax.fori_loop(0, tm, drain, 0)

    def rows_of(slot):
        return jnp.concatenate([buf_ref[slot, pl.ds(s, tm, stride=SUBLANE), :] for s in range(D_MODEL // LANE)],
                               axis=1)

    meta = meta_ref[...]
    out_ref[...] = x1_ref[...] + (meta[:, 2:3] * rows_of(0) + meta[:, 3:4] * rows_of(1))


def _combine(route, x1, meta, yb):
    t = x1.shape[0]
    tm = min(GATHER_TILE, t)
    return pl.pallas_call(
        functools.partial(_combine_body, tm=tm),
        grid_spec=pltpu.PrefetchScalarGridSpec(
            num_scalar_prefetch=5, grid=(t // tm,),
            in_specs=[pl.BlockSpec((tm, D_MODEL), lambda i, *_: (i, 0)),
                      pl.BlockSpec((tm, LANE), lambda i, *_: (i, 0)),
                      pl.BlockSpec(memory_space=pl.ANY)],
            out_specs=pl.BlockSpec((tm, D_MODEL), lambda i, *_: (i, 0)),
            scratch_shapes=[pltpu.VMEM((2, tm * SUBLANE, LANE), F32), pltpu.SemaphoreType.DMA(())]),
        out_shape=jax.ShapeDtypeStruct((t, D_MODEL), F32),
        compiler_params=_params("arbitrary"),
        name="moe_combine",
    )(*route, x1, meta, yb)


def _moe(x1, h2_tiles, meta, counts, layer, w1, w3, w2):
    t = x1.shape[0]
    n_blocks = (2 * t + N_EXPERTS * (MOE_BLK - 1)) // MOE_BLK
    e1, e2 = meta[:, 0].astype(I32), meta[:, 1].astype(I32)
    r1, r2 = meta[:, 4].astype(I32), meta[:, 5].astype(I32)
    cnt = counts[0, :N_EXPERTS].astype(I32)
    padded = (cnt + MOE_BLK - 1) // MOE_BLK * MOE_BLK
    pad_end = jnp.cumsum(padded)
    pad_start = pad_end - padded
    route = (e1, e2, r1, r2, pad_start)
    nb_used = (pad_end[-1:] // MOE_BLK).astype(I32)
    blk_start = jnp.arange(n_blocks, dtype=I32) * MOE_BLK
    block_e = jnp.minimum(jnp.searchsorted(pad_end, blk_start, side='right'), N_EXPERTS - 1).astype(I32)
    n_rows = n_blocks * MOE_BLK
    zero_start = jnp.concatenate([pad_start + cnt, pad_end[-1:]]).astype(I32)
    zero_len = jnp.concatenate([padded - cnt, n_rows - pad_end[-1:]]).astype(I32)
    xb = _scatter(route, zero_start, zero_len, h2_tiles, n_rows)
    yb = _experts(block_e, nb_used, xb, layer, w1, w3, w2, n_blocks)
    return _combine(route, x1, meta, yb)


def _final_body(x_ref, nw_ref, o_ref):
    o_ref[...] = _rms(x_ref[...]) * nw_ref[...]


def _final_norm(x, nw):
    t = x.shape[0]
    tm = min(ROW_TILE, t)
    return pl.pallas_call(
        _final_body,
        grid=(t // tm,),
        in_specs=[pl.BlockSpec((tm, D_MODEL), lambda i: (i, 0)), pl.BlockSpec((1, D_MODEL), lambda i: (0, 0))],
        out_specs=pl.BlockSpec((tm, D_MODEL), lambda i: (i, 0)),
        out_shape=jax.ShapeDtypeStruct((t, D_MODEL), F32),
        compiler_params=_params("parallel"),
        name="final_norm",
    )(x, nw)


def _rope_tables(pos):
    half = RET_DK // 2
    inv = ROPE_BASE ** (-jnp.arange(half, dtype=F32) / half)
    ang = pos[:, None] * inv[None, :]
    cos, sin = jnp.cos(ang), jnp.sin(ang)
    cos_t = jnp.tile(jnp.concatenate([cos, cos], axis=1), (1, RET_HEADS))
    sin_t = jnp.tile(jnp.concatenate([-sin, sin], axis=1), (1, RET_HEADS))
    return cos_t, sin_t


def _block_diag(blocks):
    g, r, c = blocks.shape
    eye = jnp.eye(g, dtype=bool)
    return jnp.where(eye[:, None, :, None], blocks[:, :, None, :], 0).reshape(g * r, g * c)


def _layer_weights(l, w_in, s5, s5_b_bar, s5_c_re, s5_c_im, s5_d, s5_glu_w, s5_glu_b, gdn_conv_w, gdn_a_log,
                   gdn_dt_bias, gdn_norm_w, w_out, router_group_w, router_group_b, router_expert_w,
                   router_expert_b):
    wi = w_in[l]
    g0 = RET_IN + S5_IN
    w_ab = jnp.pad(wi[:, g0 + GDN_QKV + GDN_HD:], ((0, 0), (0, LANE - 2 * GDN_HEADS)))
    in_ws = [wi[:, :RET_IN], wi[:, g0:g0 + GDN_QKV], wi[:, RET_IN:g0], wi[:, g0 + GDN_QKV:g0 + GDN_QKV + GDN_HD],
             w_ab]
    in_ws = [w.astype(BF16) for w in in_ws]
    ab_re, ab_im, bb_re, bb_im = s5
    n = S5_LANES
    sl = slice(l * n, (l + 1) * n)
    ab = jnp.concatenate([ab_re[sl].reshape(1, n), ab_im[sl].reshape(1, n)], axis=0)
    to_bd = lambda m: _block_diag(jnp.swapaxes(m[sl].reshape(S5_GROUPS, S5_STATE, S5_GROUP_CH), 1, 2))
    wb = jnp.concatenate([to_bd(bb_re), to_bd(bb_im)], axis=1).astype(BF16)
    wc_re = _block_diag(jnp.swapaxes(s5_c_re[l], 1, 2)).astype(BF16)
    wc_im = _block_diag(jnp.swapaxes(s5_c_im[l], 1, 2)).astype(BF16)
    s5_ws = (ab, wb, wc_re, wc_im, s5_d[l].reshape(1, S5_CH), s5_glu_w[l].astype(BF16),
             s5_glu_b[l].reshape(1, S5_CH))
    pad4 = lambda v: jnp.pad(v.reshape(1, GDN_HEADS), ((0, 0), (0, LANE - GDN_HEADS)))
    gdn_ws = (gdn_conv_w[l], pad4(gdn_a_log[l]), pad4(gdn_dt_bias[l]),
              jnp.tile(gdn_norm_w[l], GDN_HEADS).reshape(1, GDN_HD))
    wo = w_out[l].astype(BF16)
    out_ws = (wo[:RET_VD], wo[RET_VD:RET_VD + S5_CH], wo[RET_VD + S5_CH:])
    wr = jnp.pad(jnp.concatenate([router_expert_w[l], router_group_w[l]], axis=1),
                 ((0, 0), (0, LANE - N_EXPERTS - MOE_GROUPS))).astype(BF16)
    br = jnp.pad(jnp.concatenate([router_expert_b[l], router_group_b[l]]),
                 (0, LANE - N_EXPERTS - MOE_GROUPS)).reshape(1, LANE)
    return in_ws, s5_ws, gdn_ws, out_ws, (wr, br)


def _trunk(x, pos, states, layer_ws, norm_mix, norm_ffn, experts, norm_final):
    bsz, seq, _ = x.shape
    t = bsz * seq
    ret_s, s5_re, s5_im, gdn_s, gdn_buf = states
    cos_t, sin_t = _rope_tables(pos)
    x = x.reshape(t, D_MODEL)
    outs = [[], [], [], [], []]
    for l in range(DEPTH):
        in_ws, s5_ws, gdn_ws, out_ws, (wr, br) = layer_ws[l]
        zr, zq, zs, zg, zab = _inproj(x, norm_mix[l].reshape(1, D_MODEL), in_ws)
        o_ret, ret_fin = _retention(zr, cos_t, sin_t, ret_s[l].reshape(bsz, RET_QD, RET_DV), bsz, seq)
        u_tm = jnp.swapaxes(zs.reshape(bsz, seq, S5_CH), 0, 1)
        o_s5_tm, re_fin, im_fin = _s5(u_tm, s5_re[l].reshape(bsz, S5_LANES), s5_im[l].reshape(bsz, S5_LANES),
                                      *s5_ws)
        o_s5 = jnp.swapaxes(o_s5_tm, 0, 1).reshape(t, S5_CH)
        buf0 = jnp.pad(gdn_buf[l], ((0, 0), (SUBLANE - (GDN_CONV - 1), 0), (0, 0)))
        o_gdn, nbuf, gdn_fin = _gdn(zq, zg, zab, *gdn_ws, buf0, gdn_s[l].reshape(bsz, GDN_HD, GDN_DV), bsz, seq)
        x1, h2_tiles, meta, counts = _outproj(x, o_ret, o_s5, o_gdn, *out_ws, norm_ffn[l].reshape(1, D_MODEL),
                                              wr, br)
        x = _moe(x1, h2_tiles, meta, counts, l, *experts)
        for lst, s in zip(outs, (ret_fin.reshape(bsz, RET_HEADS, RET_DK, RET_DV),
                                 re_fin.reshape(bsz, S5_GROUPS, S5_STATE),
                                 im_fin.reshape(bsz, S5_GROUPS, S5_STATE),
                                 gdn_fin.reshape(bsz, GDN_HEADS, GDN_DK, GDN_DV),
                                 nbuf[:, SUBLANE - (GDN_CONV - 1):, :])):
            lst.append(s)
    y = _final_norm(x, norm_final.reshape(1, D_MODEL)).reshape(bsz, seq, D_MODEL)
    return y, tuple(jnp.stack(o) for o in outs)


def kernel(x_prompt, x_sample, state_ret, state_s5_re, state_s5_im, state_gdn, state_gdn_conv, norm_mix, w_in, s5_a_re, s5_a_im, s5_log_dt, s5_b_re, s5_b_im, s5_c_re, s5_c_im, s5_d, s5_glu_w, s5_glu_b, gdn_conv_w, gdn_a_log, gdn_dt_bias, gdn_norm_w, w_out, norm_ffn, router_group_w, router_group_b, router_expert_w, router_expert_b, expert_w1, expert_w3, expert_w2, norm_final):
    n = DEPTH * S5_LANES
    col = lambda a: a.reshape(n, 1)
    log_dt = jnp.broadcast_to(s5_log_dt[:, :, None], (DEPTH, S5_GROUPS, S5_STATE))
    s5 = _s5_prep(col(s5_a_re), col(s5_a_im), col(log_dt), s5_b_re.reshape(n, S5_GROUP_CH),
                  s5_b_im.reshape(n, S5_GROUP_CH))
    layer_ws = [_layer_weights(l, w_in, s5, None, s5_c_re, s5_c_im, s5_d, s5_glu_w, s5_glu_b, gdn_conv_w,
                               gdn_a_log, gdn_dt_bias, gdn_norm_w, w_out, router_group_w, router_group_b,
                               router_expert_w, router_expert_b) for l in range(DEPTH)]
    experts = (expert_w1, expert_w3, expert_w2)
    bp, lp, _ = x_prompt.shape
    zero_states = (jnp.zeros((DEPTH, bp, RET_HEADS, RET_DK, RET_DV), F32),
                   jnp.zeros((DEPTH, bp, S5_GROUPS, S5_STATE), F32),
                   jnp.zeros((DEPTH, bp, S5_GROUPS, S5_STATE), F32),
                   jnp.zeros((DEPTH, bp, GDN_HEADS, GDN_DK, GDN_DV), F32),
                   jnp.zeros((DEPTH, bp, GDN_CONV - 1, GDN_QKV), F32))
    y_p, p_states = _trunk(x_prompt, jnp.arange(lp, dtype=F32), zero_states, layer_ws, norm_mix, norm_ffn,
                           experts, norm_final)
    y_s, s_states = _trunk(x_sample, PAST_LEN + jnp.arange(x_sample.shape[1], dtype=F32),
                           (state_ret, state_s5_re, state_s5_im, state_gdn, state_gdn_conv), layer_ws, norm_mix,
                           norm_ffn, experts, norm_final)
    return (y_p, y_s) + p_states + s_states
```

```python
import functools
import math

import jax
import jax.numpy as jnp
from jax import lax
from jax.experimental import pallas as pl
from jax.experimental.pallas import tpu as pltpu

F32 = jnp.float32
BF16 = jnp.bfloat16
I32 = jnp.int32

D_MODEL = 1024
DEPTH = 2
RET_HEADS, RET_DK, RET_DV = 4, 64, 128
S5_GROUPS, S5_GROUP_CH, S5_STATE = 16, 16, 64
S5_CH = S5_GROUPS * S5_GROUP_CH
S5_LANES = S5_GROUPS * S5_STATE
GDN_HEADS, GDN_DK, GDN_DV, GDN_CONV = 4, 64, 64, 4
GDN_HD = GDN_HEADS * GDN_DK
GDN_QKV = 3 * GDN_HD
RET_QD = RET_HEADS * RET_DK
RET_VD = RET_HEADS * RET_DV
RET_IN = 2 * RET_QD + 2 * RET_VD
S5_IN = S5_CH
GDN_IN = GDN_QKV + GDN_HD + 2 * GDN_HEADS
GDN_CHUNK = 64
RET_CHUNK = 64
MOE_GROUPS, EXPERTS_PER_GROUP = 4, 8
N_EXPERTS = MOE_GROUPS * EXPERTS_PER_GROUP
D_EXPERT = 512
ROPE_BASE = 10000.0
EPS = 1e-6
PAST_LEN = 16384

LANE = 128
SUBLANE = 8
ROW_TILE = 512
MOE_BLK = 256
GATHER_TILE = 256
VMEM_LIMIT = 56 * 1024 * 1024


def _params(*sem):
    return pltpu.CompilerParams(dimension_semantics=sem, vmem_limit_bytes=VMEM_LIMIT)


def _dot(a, b):
    return jnp.dot(a.astype(BF16), b.astype(BF16), preferred_element_type=F32)


def _dot_nt(a, b):
    return lax.dot_general(a.astype(BF16), b.astype(BF16), (((1,), (1,)), ((), ())),
                           preferred_element_type=F32)


def _dot_tn(a, b):
    return lax.dot_general(a.astype(BF16), b.astype(BF16), (((0,), (0,)), ((), ())),
                           preferred_element_type=F32)


def _dot_hi(a, b):
    a1 = a.astype(BF16)
    a2 = (a - a1.astype(F32)).astype(BF16)
    b1 = b.astype(BF16)
    b2 = (b - b1.astype(F32)).astype(BF16)
    d = lambda x, y: jnp.dot(x, y, preferred_element_type=F32)
    return d(a1, b1) + (d(a1, b2) + d(a2, b1))


def _split3(x):
    p1 = x.astype(BF16)
    r1 = x - p1.astype(F32)
    p2 = r1.astype(BF16)
    p3 = (r1 - p2.astype(F32)).astype(BF16)
    return p1, p2, p3


def _dot_sel_l(sel, x):
    p1, p2, p3 = _split3(x)
    d = lambda p: jnp.dot(sel, p, preferred_element_type=F32)
    return d(p1) + d(p2) + d(p3)


def _dot_sel_r(x, sel):
    p1, p2, p3 = _split3(x)
    d = lambda p: jnp.dot(p, sel, preferred_element_type=F32)
    return d(p1) + d(p2) + d(p3)


def _rms(x):
    return x * lax.rsqrt(jnp.mean(x * x, axis=-1, keepdims=True) + EPS)


def _silu(x):
    return x * jax.nn.sigmoid(x)


def _inproj_body(x_ref, nw_ref, wr_ref, wq_ref, ws_ref, wg_ref, wab_ref,
                 zr_ref, zq_ref, zs_ref, zg_ref, zab_ref):
    h = (_rms(x_ref[...]) * nw_ref[...]).astype(BF16)
    for w_ref, z_ref in ((wr_ref, zr_ref), (wq_ref, zq_ref), (ws_ref, zs_ref),
                         (wg_ref, zg_ref), (wab_ref, zab_ref)):
        z_ref[...] = jnp.dot(h, w_ref[...], preferred_element_type=F32)


def _inproj(x, nw, ws):
    t = x.shape[0]
    tm = min(ROW_TILE, t)
    widths = [w.shape[1] for w in ws]
    row = lambda n: pl.BlockSpec((tm, n), lambda i: (i, 0))
    full = lambda a: pl.BlockSpec(a.shape, lambda i: (0, 0))
    return pl.pallas_call(
        _inproj_body,
        grid=(t // tm,),
        in_specs=[row(D_MODEL), full(nw)] + [full(w) for w in ws],
        out_specs=[row(n) for n in widths],
        out_shape=[jax.ShapeDtypeStruct((t, n), F32) for n in widths],
        compiler_params=_params("parallel"),
        name="inproj",
    )(x, nw, *ws)


def _ret_log_gamma(h):
    return math.log(1.0 - 2.0 ** (-5.0 - h))


def _ret_body(zr_ref, cos_ref, sin_ref, s0_ref, o_ref, sfin_ref, s_scr, *, rows, chunk):
    j = pl.program_id(1)
    stack = RET_HEADS * chunk

    @pl.when(j == 0)
    def _():
        s_scr[...] = s0_ref[...]

    lane = lax.broadcasted_iota(I32, (rows, RET_QD), 1)
    first_half = (lane % RET_DK) < (RET_DK // 2)

    def rotary(x):
        swapped = jnp.where(first_half, pltpu.roll(x, RET_QD - RET_DK // 2, 1),
                            pltpu.roll(x, RET_DK // 2, 1))
        return x * cos_ref[...] + swapped * sin_ref[...]

    q = rotary(zr_ref[:, 0:RET_QD])
    k = rotary(zr_ref[:, RET_QD:2 * RET_QD]) * (RET_DK ** -0.5)

    def by_head(idx, fn):
        out = jnp.zeros(idx.shape, F32)
        for h in range(RET_HEADS):
            out = jnp.where(idx == h, fn(h), out)
        return out

    lg_lane = by_head(lax.broadcasted_iota(I32, (chunk, RET_QD), 1) // RET_DK, _ret_log_gamma)
    pos = lax.broadcasted_iota(I32, (chunk, RET_QD), 0).astype(F32)
    q_scale = jnp.exp((pos + 1.0) * lg_lane)
    k_scale = jnp.exp((chunk - 1.0 - pos) * lg_lane)
    st_row = lax.broadcasted_iota(I32, (stack, stack), 0)
    st_col = lax.broadcasted_iota(I32, (stack, stack), 1)
    causal = ((st_row // chunk) == (st_col // chunk)) & (st_row >= st_col)
    lg_stack = by_head(st_row // chunk, _ret_log_gamma)
    decay = jnp.where(causal, jnp.exp(jnp.where(causal, (st_row - st_col).astype(F32), 0.0) * lg_stack), 0.0)
    q_rows = (lax.broadcasted_iota(I32, (stack, RET_QD), 0) // chunk) == \
             (lax.broadcasted_iota(I32, (stack, RET_QD), 1) // RET_DK)
    v_rows = (lax.broadcasted_iota(I32, (stack, RET_VD), 0) // chunk) == \
             (lax.broadcasted_iota(I32, (stack, RET_VD), 1) // RET_DV)
    s_row_head = lax.broadcasted_iota(I32, (RET_QD, RET_DV), 0) // RET_DK
    s_decay = by_head(s_row_head, lambda h: math.exp(chunk * _ret_log_gamma(h)))
    kv_diag = (lax.broadcasted_iota(I32, (RET_QD, RET_VD), 0) // RET_DK) == \
              (lax.broadcasted_iota(I32, (RET_QD, RET_VD), 1) // RET_DV)

    def tile4(a):
        return jnp.concatenate([a] * RET_HEADS, axis=0)

    def collapse(a):
        out = a[0:chunk]
        for h in range(1, RET_HEADS):
            out = out + a[h * chunk:(h + 1) * chunk]
        return out

    s = s_scr[...]
    for ci in range(rows // chunk):
        r = slice(ci * chunk, (ci + 1) * chunk)
        q_c, k_c = q[r], k[r]
        v_c = zr_ref[r, 2 * RET_QD:2 * RET_QD + RET_VD]
        gate = zr_ref[r, 2 * RET_QD + RET_VD:2 * RET_QD + 2 * RET_VD]
        scores = _dot_nt(jnp.where(q_rows, tile4(q_c), 0.0), tile4(k_c)) * decay
        o_intra = collapse(_dot(scores, jnp.where(v_rows, tile4(v_c), 0.0)))
        s_full = jnp.concatenate([jnp.where(s_row_head == h, s, 0.0) for h in range(RET_HEADS)], axis=1)
        o_c = o_intra + _dot(q_c * q_scale, s_full)
        for h in range(RET_HEADS):
            sl = slice(h * RET_DV, (h + 1) * RET_DV)
            o_ref[r, sl] = _rms(o_c[:, sl]) * _silu(gate[:, sl])
        kv = jnp.where(kv_diag, _dot_tn(k_c * k_scale, v_c), 0.0)
        kv_own = kv[:, 0:RET_DV]
        for h in range(1, RET_HEADS):
            kv_own = kv_own + kv[:, h * RET_DV:(h + 1) * RET_DV]
        s = s * s_decay + kv_own
    s_scr[...] = s

    @pl.when(j == pl.num_programs(1) - 1)
    def _():
        sfin_ref[...] = s


def _retention(zr, cos, sin, s0, bsz, seq):
    chunk = min(RET_CHUNK, seq)
    rows = min(4 * chunk, seq)
    nj = seq // rows
    return pl.pallas_call(
        functools.partial(_ret_body, rows=rows, chunk=chunk),
        grid=(bsz, nj),
        in_specs=[pl.BlockSpec((rows, RET_IN), lambda b, j: (b * nj + j, 0)),
                  pl.BlockSpec((rows, RET_QD), lambda b, j: (j, 0)),
                  pl.BlockSpec((rows, RET_QD), lambda b, j: (j, 0)),
                  pl.BlockSpec((None, RET_QD, RET_DV), lambda b, j: (b, 0, 0))],
        out_specs=[pl.BlockSpec((rows, RET_VD), lambda b, j: (b * nj + j, 0)),
                   pl.BlockSpec((None, RET_QD, RET_DV), lambda b, j: (b, 0, 0))],
        out_shape=[jax.ShapeDtypeStruct((bsz * seq, RET_VD), F32),
                   jax.ShapeDtypeStruct((bsz, RET_QD, RET_DV), F32)],
        scratch_shapes=[pltpu.VMEM((RET_QD, RET_DV), F32)],
        compiler_params=_params("parallel", "arbitrary"),
        name="retention",
    )(zr, cos, sin, s0)


def _s5_prep_body(are_ref, aim_ref, ldt_ref, bre_ref, bim_ref, abre_ref, abim_ref, bbre_ref, bbim_ref):
    lam_re, lam_im = are_ref[...], aim_ref[...]
    dt = jnp.exp(ldt_ref[...])
    mag = jnp.exp(lam_re * dt)
    ab_re = mag * jnp.cos(lam_im * dt)
    ab_im = mag * jnp.sin(lam_im * dt)
    den = lam_re * lam_re + lam_im * lam_im
    f_re = ((ab_re - 1.0) * lam_re + ab_im * lam_im) / den
    f_im = (ab_im * lam_re - (ab_re - 1.0) * lam_im) / den
    abre_ref[...] = ab_re
    abim_ref[...] = ab_im
    bbre_ref[...] = f_re * bre_ref[...] - f_im * bim_ref[...]
    bbim_ref[...] = f_re * bim_ref[...] + f_im * bre_ref[...]


def _s5_prep(a_re, a_im, log_dt, b_re, b_im):
    n = a_re.shape[0]
    col = jax.ShapeDtypeStruct((n, 1), F32)
    mat = jax.ShapeDtypeStruct((n, S5_GROUP_CH), F32)
    return pl.pallas_call(_s5_prep_body, out_shape=[col, col, mat, mat], name="s5_prep")(
        a_re, a_im, log_dt, b_re, b_im)


def _gelu_tanh(x):
    return x * (0.5 * (1.0 + jnp.tanh(math.sqrt(2.0 / math.pi) * (x + 0.044715 * (x * x * x)))))


def _s5_body(u_ref, h0re_ref, h0im_ref, ab_ref, wb_ref, wcre_ref, wcim_ref, d_ref, gw_ref, gb_ref,
             o_ref, hre_ref, him_ref, bu_scr, st_scr, *, steps):
    j = pl.program_id(1)

    @pl.when(j == 0)
    def _():
        st_scr[0] = h0re_ref[...]
        st_scr[1] = h0im_ref[...]

    rows = steps * SUBLANE
    u = u_ref[...].reshape(rows, S5_CH)
    bu_scr[...] = _dot(u, wb_ref[...])
    a_re = jnp.broadcast_to(ab_ref[0:1, :], (SUBLANE, S5_LANES))
    a_im = jnp.broadcast_to(ab_ref[1:2, :], (SUBLANE, S5_LANES))

    def step(t, carry):
        h_re, h_im = carry
        r = pl.ds(pl.multiple_of(t * SUBLANE, SUBLANE), SUBLANE)
        n_re = a_re * h_re - a_im * h_im + bu_scr[r, 0:S5_LANES]
        n_im = a_re * h_im + a_im * h_re + bu_scr[r, S5_LANES:2 * S5_LANES]
        bu_scr[r, 0:S5_LANES] = n_re
        bu_scr[r, S5_LANES:2 * S5_LANES] = n_im
        return n_re, n_im

    h_re, h_im = lax.fori_loop(0, steps, step, (st_scr[0], st_scr[1]))
    st_scr[0] = h_re
    st_scr[1] = h_im
    hre_ref[...] = h_re
    him_ref[...] = h_im

    y = _dot(bu_scr[:, 0:S5_LANES], wcre_ref[...]) - _dot(bu_scr[:, S5_LANES:2 * S5_LANES], wcim_ref[...])
    y = _gelu_tanh(y + d_ref[...] * u)
    y = y * jax.nn.sigmoid(_dot(y, gw_ref[...]) + gb_ref[...])
    o_ref[...] = y.reshape(steps, SUBLANE, S5_CH)


def _s5(u_tm, h0_re, h0_im, ab, wb, wc_re, wc_im, d_skip, glu_w, glu_b):
    seq, bsz, _ = u_tm.shape
    steps = min(64, seq)
    full = lambda a: pl.BlockSpec(a.shape, lambda g, j: (0,) * a.ndim)
    st = pl.BlockSpec((SUBLANE, S5_LANES), lambda g, j: (g, 0))
    return pl.pallas_call(
        functools.partial(_s5_body, steps=steps),
        grid=(bsz // SUBLANE, seq // steps),
        in_specs=[pl.BlockSpec((steps, SUBLANE, S5_CH), lambda g, j: (j, g, 0)), st, st,
                  full(ab), full(wb), full(wc_re), full(wc_im), full(d_skip), full(glu_w), full(glu_b)],
        out_specs=[pl.BlockSpec((steps, SUBLANE, S5_CH), lambda g, j: (j, g, 0)), st, st],
        out_shape=[jax.ShapeDtypeStruct((seq, bsz, S5_CH), F32),
                   jax.ShapeDtypeStruct((bsz, S5_LANES), F32),
                   jax.ShapeDtypeStruct((bsz, S5_LANES), F32)],
        scratch_shapes=[pltpu.VMEM((steps * SUBLANE, 2 * S5_LANES), F32),
                        pltpu.VMEM((2, SUBLANE, S5_LANES), F32)],
        compiler_params=_params("parallel", "arbitrary"),
        name="s5",
    )(u_tm, h0_re, h0_im, ab, wb, wc_re, wc_im, d_skip, glu_w, glu_b)


def _gdn_body(zq_ref, zg_ref, zab_ref, cw_ref, alog_ref, dtb_ref, nw_ref, buf0_ref, s0_ref,
              ones_bd_ref, tri_ref, ea_ref, eb_ref, ec_ref, spread_ref, gather_ref,
              o_ref, nbuf_ref, sfin_ref,
              ext_scr, q_scr, k_scr, kb_scr, vb_scr, g_scr, g4_scr, s_scr, *, rows, chunk):
    j = pl.program_id(1)
    stack = GDN_HEADS * chunk
    bd_state = (lax.broadcasted_iota(I32, (GDN_HD, GDN_HD), 0) // GDN_DK) == \
               (lax.broadcasted_iota(I32, (GDN_HD, GDN_HD), 1) // GDN_DV)

    @pl.when(j == 0)
    def _():
        ext_scr[0:SUBLANE, :] = buf0_ref[...]
        s_scr[...] = jnp.where(bd_state, _dot_sel_r(s0_ref[...], spread_ref[...]), 0.0)

    ext_scr[SUBLANE:SUBLANE + rows, :] = zq_ref[...]
    conv = ext_scr[SUBLANE - 3:SUBLANE - 3 + rows, :] * cw_ref[0:1, :]
    for i in range(1, GDN_CONV):
        conv = conv + ext_scr[SUBLANE - 3 + i:SUBLANE - 3 + i + rows, :] * cw_ref[i:i + 1, :]
    tail = ext_scr[rows:rows + SUBLANE, :]
    nbuf_ref[...] = tail
    ext_scr[0:SUBLANE, :] = tail
    qkv = _silu(conv)

    ones_bd = ones_bd_ref[...]
    q_raw = qkv[:, 0:GDN_HD]
    k_raw = qkv[:, GDN_HD:2 * GDN_HD]
    q_scr[...] = q_raw * lax.rsqrt(_dot_sel_r(q_raw * q_raw, ones_bd) + EPS) * (GDN_DK ** -0.5)
    k_n = k_raw * lax.rsqrt(_dot_sel_r(k_raw * k_raw, ones_bd) + EPS)
    k_scr[...] = k_n

    ab = zab_ref[...]
    x = ab + dtb_ref[...]
    softplus = jnp.maximum(x, 0.0) + jnp.log1p(jnp.exp(-jnp.abs(x)))
    g_pad = -jnp.exp(alog_ref[...]) * softplus
    beta = _dot_sel_r(jax.nn.sigmoid(ab), eb_ref[...])
    g_cum = _dot_sel_l(tri_ref[...], g_pad)
    g_scr[...] = _dot_sel_r(g_cum, ea_ref[...])
    g4_scr[...] = _dot_sel_r(g_pad, ec_ref[...])
    kb_scr[...] = k_n * beta
    vb_scr[...] = qkv[:, 2 * GDN_HD:3 * GDN_HD] * beta

    w_row = lax.broadcasted_iota(I32, (chunk, stack), 0)
    w_col = lax.broadcasted_iota(I32, (chunk, stack), 1) % chunk
    strict_w = w_row > w_col
    causal_w = w_row >= w_col
    same_head = (lax.broadcasted_iota(I32, (stack, stack), 0) // chunk) == \
                (lax.broadcasted_iota(I32, (stack, stack), 1) // chunk)
    head_rows = (lax.broadcasted_iota(I32, (stack, GDN_HD), 0) // chunk) == \
                (lax.broadcasted_iota(I32, (stack, GDN_HD), 1) // GDN_DK)
    head_rows2 = jnp.concatenate([head_rows, head_rows], axis=1)
    tri_c = tri_ref[0:chunk, 0:chunk]
    packed_ok = chunk % (2 * SUBLANE) == 0

    def tile4(a):
        return jnp.concatenate([a] * GDN_HEADS, axis=0)

    def split2(a):
        hi = a.astype(BF16)
        return hi, (a - hi.astype(F32)).astype(BF16)

    def stack_masked(parts, mask):
        if packed_ok:
            return [jnp.where(mask, tile4(p), jnp.zeros((), BF16)) for p in parts]
        return [jnp.where(mask, tile4(p.astype(F32)), 0.0).astype(BF16) for p in parts]

    def on_diag(parts):
        return stack_masked(parts, same_head)

    def mm(a, b):
        return jnp.dot(a, b, preferred_element_type=F32)

    def mm_hi(a_parts, b_parts):
        return mm(a_parts[0], b_parts[0]) + (mm(a_parts[0], b_parts[1]) + mm(a_parts[1], b_parts[0]))

    n_chunks = rows // chunk
    solved = []
    for ci in range(n_chunks):
        r = slice(ci * chunk, (ci + 1) * chunk)
        q_c, k_c, kb_c, vb_c, g_c = q_scr[r, :], k_scr[r, :], kb_scr[r, :], vb_scr[r, :], g_scr[r, :]
        exp_g = jnp.exp(g_c)
        g_diff = _dot_sel_l(tri_c, jnp.where(strict_w, g4_scr[r, :], 0.0))
        decay = jnp.exp(jnp.where(causal_w, g_diff, 0.0))
        k_heads = jnp.where(head_rows, tile4(k_c), 0.0)
        lmat = jnp.where(strict_w, _dot_nt(kb_c, k_heads) * decay, 0.0)
        attn = jnp.where(causal_w, _dot_nt(q_c, k_heads) * decay, 0.0)
        rhs_c = jnp.concatenate([vb_c, kb_c * exp_g], axis=1)
        solved.append((lmat, attn, exp_g, rhs_c))

    every = range(n_chunks)
    t_acc = [-solved[ci][0] for ci in every]
    p_parts = [split2(solved[ci][0]) for ci in every]
    p_diag = [on_diag(p) for p in p_parts]
    span = 2
    while span < chunk + 1:
        power = [mm_hi(p_parts[ci], p_diag[ci]) for ci in every]
        p_parts = [split2(p) for p in power]
        p_diag = [on_diag(p) for p in p_parts]
        t_acc = [t_acc[ci] + power[ci] + mm_hi(split2(t_acc[ci]), p_diag[ci]) for ci in every]
        span *= 2
    rhs_parts = [stack_masked(split2(solved[ci][3]), head_rows2) for ci in every]
    sols = [solved[ci][3] + mm_hi(split2(t_acc[ci]), rhs_parts[ci]) for ci in every]

    s_cur = s_scr[...]
    for ci in every:
        r = slice(ci * chunk, (ci + 1) * chunk)
        _, attn, exp_g, _ = solved[ci]
        u_c, w_c = sols[ci][:, 0:GDN_HD], sols[ci][:, GDN_HD:2 * GDN_HD]
        q_c, k_c, g_c = q_scr[r, :], k_scr[r, :], g_scr[r, :]
        g_last = g_c[chunk - 1:chunk, :]
        v_new = u_c - _dot(w_c, s_cur)
        o_c = _dot(q_c * exp_g, s_cur) + _dot(attn, jnp.where(head_rows, tile4(v_new), 0.0))
        k_dec = k_c * jnp.exp(g_last - g_c)
        s_cur = s_cur * jnp.exp(g_last) + jnp.where(bd_state, _dot_tn(k_dec, v_new), 0.0)
        ms = _dot_sel_r(o_c * o_c, ones_bd) * (1.0 / GDN_DV)
        o_ref[r, :] = (o_c * lax.rsqrt(ms + EPS) * nw_ref[...] * _silu(zg_ref[r, :])).astype(o_ref.dtype)
    s_scr[...] = s_cur

    @pl.when(j == pl.num_programs(1) - 1)
    def _():
        sfin_ref[...] = _dot_sel_r(s_scr[...], gather_ref[...])


def _gdn_consts(rows, chunk):
    stack = GDN_HEADS * chunk
    blk = lambda n, c: (jnp.arange(n)[:, None] // c) == (jnp.arange(n)[None, :] // c)
    ones_bd = blk(GDN_HD, GDN_DK).astype(BF16)
    tri = (blk(rows, chunk) & (jnp.arange(rows)[:, None] >= jnp.arange(rows)[None, :])).astype(BF16)
    src = jnp.arange(LANE)[:, None]
    ea = (src == jnp.arange(GDN_HD)[None, :] // GDN_DK).astype(BF16)
    eb = (src == GDN_HEADS + jnp.arange(GDN_HD)[None, :] // GDN_DK).astype(BF16)
    ec = (src == jnp.arange(stack)[None, :] // chunk).astype(BF16)
    spread = (jnp.arange(GDN_DV)[:, None] == jnp.arange(GDN_HD)[None, :] % GDN_DV).astype(BF16)
    return ones_bd, tri, ea, eb, ec, spread, spread.T


def _gdn(zq, zg, zab, conv_w, alog_pad, dtb_pad, nw, buf0, s0, bsz, seq):
    chunk = min(GDN_CHUNK, seq)
    rows = min(4 * chunk, seq)
    nj = seq // rows
    stack = GDN_HEADS * chunk
    consts = _gdn_consts(rows, chunk)
    row = lambda n: pl.BlockSpec((rows, n), lambda b, j: (b * nj + j, 0))
    full = lambda a: pl.BlockSpec(a.shape, lambda b, j: (0,) * a.ndim)
    per_b = lambda *s: pl.BlockSpec((None,) + s, lambda b, j: (b, 0, 0))
    return pl.pallas_call(
        functools.partial(_gdn_body, rows=rows, chunk=chunk),
        grid=(bsz, nj),
        in_specs=[row(GDN_QKV), row(GDN_HD), row(LANE), full(conv_w), full(alog_pad), full(dtb_pad), full(nw),
                  per_b(SUBLANE, GDN_QKV), per_b(GDN_HD, GDN_DV)] + [full(c) for c in consts],
        out_specs=[row(GDN_HD), per_b(SUBLANE, GDN_QKV), per_b(GDN_HD, GDN_DV)],
        out_shape=[jax.ShapeDtypeStruct((bsz * seq, GDN_HD), F32),
                   jax.ShapeDtypeStruct((bsz, SUBLANE, GDN_QKV), F32),
                   jax.ShapeDtypeStruct((bsz, GDN_HD, GDN_DV), F32)],
        scratch_shapes=[pltpu.VMEM((rows + SUBLANE, GDN_QKV), F32)] +
                       [pltpu.VMEM((rows, GDN_HD), F32)] * 5 +
                       [pltpu.VMEM((rows, stack), F32), pltpu.VMEM((GDN_HD, GDN_HD), F32)],
        compiler_params=_params("parallel", "arbitrary"),
        name="gdn",
    )(zq, zg, zab, conv_w, alog_pad, dtb_pad, nw, buf0, s0, *consts)


def _outproj_body(x_ref, oret_ref, os5_ref, ogdn_ref, w1_ref, w2_ref, w3_ref, nw_ref, wr_ref, br_ref, tri_ref,
                  x1_ref, h2_ref, meta_ref, cnt_ref, carry_scr, *, tm):
    i = pl.program_id(0)

    @pl.when(i == 0)
    def _():
        carry_scr[...] = jnp.zeros_like(carry_scr)

    mix = _dot(oret_ref[...], w1_ref[...]) + _dot(os5_ref[...], w2_ref[...]) + _dot(ogdn_ref[...], w3_ref[...])
    x1 = x_ref[...] + mix
    x1_ref[...] = x1
    h2 = _rms(x1) * nw_ref[...]
    for s in range(D_MODEL // LANE):
        h2_ref[pl.ds(s, tm, stride=SUBLANE), :] = h2[:, s * LANE:(s + 1) * LANE]

    logits = _dot(h2, wr_ref[...]) + br_ref[...]
    lane_i = lax.broadcasted_iota(I32, (tm, LANE), 1)
    lane = lane_i.astype(F32)
    neg = -jnp.inf
    big = float(LANE)
    g_log = jnp.where((lane_i >= N_EXPERTS) & (lane_i < N_EXPERTS + MOE_GROUPS), logits, neg)
    g_max = jnp.max(g_log, axis=-1, keepdims=True)
    grp = jnp.min(jnp.where(g_log == g_max, lane - N_EXPERTS, big), axis=-1, keepdims=True)
    p_grp = 1.0 / jnp.sum(jnp.exp(g_log - g_max), axis=-1, keepdims=True)
    in_grp = (lane >= grp * EXPERTS_PER_GROUP) & (lane < (grp + 1.0) * EXPERTS_PER_GROUP)
    e_log = jnp.where(in_grp, logits, neg)
    v1 = jnp.max(e_log, axis=-1, keepdims=True)
    i1 = jnp.min(jnp.where(e_log == v1, lane, big), axis=-1, keepdims=True)
    e_log2 = jnp.where(lane == i1, neg, e_log)
    v2 = jnp.max(e_log2, axis=-1, keepdims=True)
    i2 = jnp.min(jnp.where(e_log2 == v2, lane, big), axis=-1, keepdims=True)
    e2 = jnp.exp(v2 - v1)
    gate1 = p_grp / (1.0 + e2)
    gate2 = p_grp * e2 / (1.0 + e2)

    oh1 = lane == i1
    oh2 = lane == i2
    picked = jnp.where(oh1 | oh2, 1.0, 0.0)
    before = jnp.dot(tri_ref[...], picked.astype(BF16), preferred_element_type=F32) + carry_scr[0:1, :]
    rank1 = jnp.sum(jnp.where(oh1, before, 0.0), axis=-1, keepdims=True)
    rank2 = jnp.sum(jnp.where(oh2, before, 0.0), axis=-1, keepdims=True)
    total = carry_scr[0:1, :] + jnp.sum(picked, axis=0, keepdims=True)
    carry_scr[...] = jnp.broadcast_to(total, carry_scr.shape)
    cnt_ref[...] = jnp.broadcast_to(total, cnt_ref.shape)

    meta = jnp.where(lane_i == 0, i1, 0.0)
    meta = jnp.where(lane_i == 1, i2, meta)
    meta = jnp.where(lane_i == 2, gate1, meta)
    meta = jnp.where(lane_i == 3, gate2, meta)
    meta = jnp.where(lane_i == 4, rank1, meta)
    meta = jnp.where(lane_i == 5, rank2, meta)
    meta_ref[...] = meta


def _outproj(x, o_ret, o_s5, o_gdn, w1, w2, w3, nw, wr, br):
    t = x.shape[0]
    tm = min(ROW_TILE, t)
    tri = (jnp.arange(tm)[:, None] > jnp.arange(tm)[None, :]).astype(BF16)
    row = lambda n: pl.BlockSpec((tm, n), lambda i: (i, 0))
    full = lambda a: pl.BlockSpec(a.shape, lambda i: (0, 0))
    return pl.pallas_call(
        functools.partial(_outproj_body, tm=tm),
        grid=(t // tm,),
        in_specs=[row(D_MODEL), row(RET_VD), row(S5_CH), row(GDN_HD),
                  full(w1), full(w2), full(w3), full(nw), full(wr), full(br), full(tri)],
        out_specs=[row(D_MODEL), pl.BlockSpec((tm * SUBLANE, LANE), lambda i: (i, 0)), row(LANE),
                   pl.BlockSpec((SUBLANE, LANE), lambda i: (0, 0))],
        out_shape=[jax.ShapeDtypeStruct((t, D_MODEL), F32),
                   jax.ShapeDtypeStruct((t * SUBLANE, LANE), F32),
                   jax.ShapeDtypeStruct((t, LANE), F32),
                   jax.ShapeDtypeStruct((SUBLANE, LANE), F32)],
        scratch_shapes=[pltpu.VMEM((SUBLANE, LANE), F32)],
        compiler_params=_params("arbitrary"),
        name="outproj_router",
    )(x, o_ret, o_s5, o_gdn, w1, w2, w3, nw, wr, br, tri)


def _token_rows(ref, idx):
    return ref.at[pl.ds(pl.multiple_of(idx * SUBLANE, SUBLANE), SUBLANE)]


def _slot(e_ref, r_ref, ps_ref, g):
    return ps_ref[e_ref[g]] + r_ref[g]


ZERO_ROWS = 128


def _zero_segment(zero_scr, xb_ref, sem, start, length, wait):
    def piece(off, n):
        cp = pltpu.make_async_copy(zero_scr.at[pl.ds(0, n * SUBLANE)],
                                   xb_ref.at[pl.ds(pl.multiple_of(off * SUBLANE, SUBLANE), n * SUBLANE)], sem)
        cp.wait() if wait else cp.start()

    n_big = length // ZERO_ROWS

    def big(i, carry):
        piece(start + i * ZERO_ROWS, ZERO_ROWS)
        return carry

    lax.fori_loop(0, n_big, big, 0)
    off = start + n_big * ZERO_ROWS
    rem = length - n_big * ZERO_ROWS
    bit = ZERO_ROWS // 2
    while bit >= 1:
        has = (rem & bit) != 0
        pl.when(has)(functools.partial(piece, off, bit))
        off = off + jnp.where(has, bit, 0)
        bit //= 2


def _scatter_body(e1_ref, e2_ref, r1_ref, r2_ref, ps_ref, zs_ref, zl_ref, src_ref, xb_ref, zero_scr, sem, zsem,
                  *, tm):
    base = pl.program_id(0) * tm

    @pl.when(pl.program_id(0) == 0)
    def _():
        zero_scr[...] = jnp.zeros_like(zero_scr)
        for wait in (False, True):
            lax.fori_loop(0, N_EXPERTS + 1,
                          lambda s, c, wait=wait: (_zero_segment(zero_scr, xb_ref, zsem, zs_ref[s], zl_ref[s], wait),
                                                   c)[1], 0)

    def copy(t, d):
        return pltpu.make_async_copy(_token_rows(src_ref, t), _token_rows(xb_ref, d), sem)

    def issue(t, carry):
        copy(t, _slot(e1_ref, r1_ref, ps_ref, base + t)).start(priority=0)
        copy(t, _slot(e2_ref, r2_ref, ps_ref, base + t)).start(priority=1)
        return carry

    lax.fori_loop(0, tm, issue, 0)

    def drain(t, carry):
        copy(0, 0).wait()
        copy(0, 0).wait()
        return carry

    lax.fori_loop(0, tm, drain, 0)


def _scatter(route, zero_start, zero_len, h2_tiles, n_rows):
    tokens = route[0].shape[0]
    tm = min(GATHER_TILE, tokens)
    assert tokens % tm == 0
    return pl.pallas_call(
        functools.partial(_scatter_body, tm=tm),
        grid_spec=pltpu.PrefetchScalarGridSpec(
            num_scalar_prefetch=7, grid=(tokens // tm,),
            in_specs=[pl.BlockSpec((tm * SUBLANE, LANE), lambda i, *_: (i, 0))],
            out_specs=pl.BlockSpec(memory_space=pl.ANY),
            scratch_shapes=[pltpu.VMEM((ZERO_ROWS * SUBLANE, LANE), F32), pltpu.SemaphoreType.DMA(()),
                            pltpu.SemaphoreType.DMA(())]),
        out_shape=jax.ShapeDtypeStruct((n_rows * SUBLANE, LANE), F32),
        compiler_params=_params("arbitrary"),
        name="moe_scatter",
    )(*route, zero_start, zero_len, h2_tiles)


def _experts_body(be_ref, nb_ref, xb_ref, w1_ref, w3_ref, w2_ref, yb_ref, w1_scr, w3_scr, w2_scr, *, blk):
    i = pl.program_id(0)
    live = i < nb_ref[0]

    @pl.when(live & ((i == 0) | (be_ref[i] != be_ref[jnp.maximum(i - 1, 0)])))
    def _():
        w1_scr[...] = w1_ref[...].astype(BF16)
        w3_scr[...] = w3_ref[...].astype(BF16)
        w2_scr[...] = w2_ref[...].astype(BF16)

    @pl.when(live)
    def _():
        x = jnp.concatenate([xb_ref[pl.ds(s, blk, stride=SUBLANE), :] for s in range(D_MODEL // LANE)],
                            axis=1).astype(BF16)
        hid = _silu(jnp.dot(x, w1_scr[...], preferred_element_type=F32)) * \
            jnp.dot(x, w3_scr[...], preferred_element_type=F32)
        y = _dot(hid, w2_scr[...])
        for s in range(D_MODEL // LANE):
            yb_ref[pl.ds(s, blk, stride=SUBLANE), :] = y[:, s * LANE:(s + 1) * LANE]

    @pl.when(jnp.logical_not(live))
    def _():
        yb_ref[...] = jnp.zeros_like(yb_ref)


def _experts(block_e, nb_used, xb, layer, w1, w3, w2, n_blocks):
    blk = MOE_BLK
    live = lambda i, nb: jnp.minimum(i, nb[0] - 1)
    tile_in = pl.BlockSpec((blk * SUBLANE, LANE), lambda i, be, nb: (live(i, nb), 0))
    tile_out = pl.BlockSpec((blk * SUBLANE, LANE), lambda i, be, nb: (i, 0))
    wspec = lambda a: pl.BlockSpec((None, None) + a.shape[2:],
                                   lambda i, be, nb: (layer, be[live(i, nb)], 0, 0))
    return pl.pallas_call(
        functools.partial(_experts_body, blk=blk),
        grid_spec=pltpu.PrefetchScalarGridSpec(
            num_scalar_prefetch=2, grid=(n_blocks,),
            in_specs=[tile_in, wspec(w1), wspec(w3), wspec(w2)], out_specs=tile_out,
            scratch_shapes=[pltpu.VMEM(w1.shape[2:], BF16), pltpu.VMEM(w3.shape[2:], BF16),
                            pltpu.VMEM(w2.shape[2:], BF16)]),
        out_shape=jax.ShapeDtypeStruct(xb.shape, F32),
        compiler_params=_params("arbitrary"),
        name="moe_experts",
    )(block_e, nb_used, xb, w1, w3, w2)


def _combine_body(e1_ref, e2_ref, r1_ref, r2_ref, ps_ref, x1_ref, meta_ref, yb_ref, out_ref, buf_ref, sem, *, tm):
    base = pl.program_id(0) * tm

    def copy(slot, t, d):
        return pltpu.make_async_copy(_token_rows(yb_ref, d), _token_rows(buf_ref.at[slot], t), sem)

    def issue(t, carry):
        copy(0, t, _slot(e1_ref, r1_ref, ps_ref, base + t)).start(priority=0)
        copy(1, t, _slot(e2_ref, r2_ref, ps_ref, base + t)).start(priority=1)
        return carry

    lax.fori_loop(0, tm, issue, 0)

    def drain(t, carry):
        copy(0, 0, 0).wait()
        copy(1, 0, 0).wait()
        return carry

    lax.fori_loop(0, tm, drain, 0)

    def rows_of(slot):
        return jnp.concatenate([buf_ref[slot, pl.ds(s, tm, stride=SUBLANE), :] for s in range(D_MODEL // LANE)],
                               axis=1)

    meta = meta_ref[...]
    out_ref[...] = x1_ref[...] + (meta[:, 2:3] * rows_of(0) + meta[:, 3:4] * rows_of(1))


def _combine(route, x1, meta, yb):
    t = x1.shape[0]
    tm = min(GATHER_TILE, t)
    return pl.pallas_call(
        functools.partial(_combine_body, tm=tm),
        grid_spec=pltpu.PrefetchScalarGridSpec(
            num_scalar_prefetch=5, grid=(t // tm,),
            in_specs=[pl.BlockSpec((tm, D_MODEL), lambda i, *_: (i, 0)),
                      pl.BlockSpec((tm, LANE), lambda i, *_: (i, 0)),
                      pl.BlockSpec(memory_space=pl.ANY)],
            out_specs=pl.BlockSpec((tm, D_MODEL), lambda i, *_: (i, 0)),
            scratch_shapes=[pltpu.VMEM((2, tm * SUBLANE, LANE), F32), pltpu.SemaphoreType.DMA(())]),
        out_shape=jax.ShapeDtypeStruct((t, D_MODEL), F32),
        compiler_params=_params("arbitrary"),
        name="moe_combine",
    )(*route, x1, meta, yb)


def _moe(x1, h2_tiles, meta, counts, layer, w1, w3, w2):
    t = x1.shape[0]
    n_blocks = (2 * t + N_EXPERTS * (MOE_BLK - 1)) // MOE_BLK
    e1, e2 = meta[:, 0].astype(I32), meta[:, 1].astype(I32)
    r1, r2 = meta[:, 4].astype(I32), meta[:, 5].astype(I32)
    cnt = counts[0, :N_EXPERTS].astype(I32)
    padded = (cnt + MOE_BLK - 1) // MOE_BLK * MOE_BLK
    pad_end = jnp.cumsum(padded)
    pad_start = pad_end - padded
    route = (e1, e2, r1, r2, pad_start)
    nb_used = (pad_end[-1:] // MOE_BLK).astype(I32)
    blk_start = jnp.arange(n_blocks, dtype=I32) * MOE_BLK
    block_e = jnp.minimum(jnp.searchsorted(pad_end, blk_start, side='right'), N_EXPERTS - 1).astype(I32)
    n_rows = n_blocks * MOE_BLK
    zero_start = jnp.concatenate([pad_start + cnt, pad_end[-1:]]).astype(I32)
    zero_len = jnp.concatenate([padded - cnt, n_rows - pad_end[-1:]]).astype(I32)
    xb = _scatter(route, zero_start, zero_len, h2_tiles, n_rows)
    yb = _experts(block_e, nb_used, xb, layer, w1, w3, w2, n_blocks)
    return _combine(route, x1, meta, yb)


def _final_body(x_ref, nw_ref, o_ref):
    o_ref[...] = _rms(x_ref[...]) * nw_ref[...]


def _final_norm(x, nw):
    t = x.shape[0]
    tm = min(ROW_TILE, t)
    return pl.pallas_call(
        _final_body,
        grid=(t // tm,),
        in_specs=[pl.BlockSpec((tm, D_MODEL), lambda i: (i, 0)), pl.BlockSpec((1, D_MODEL), lambda i: (0, 0))],
        out_specs=pl.BlockSpec((tm, D_MODEL), lambda i: (i, 0)),
        out_shape=jax.ShapeDtypeStruct((t, D_MODEL), F32),
        compiler_params=_params("parallel"),
        name="final_norm",
    )(x, nw)


def _rope_tables(pos):
    half = RET_DK // 2
    inv = ROPE_BASE ** (-jnp.arange(half, dtype=F32) / half)
    ang = pos[:, None] * inv[None, :]
    cos, sin = jnp.cos(ang), jnp.sin(ang)
    cos_t = jnp.tile(jnp.concatenate([cos, cos], axis=1), (1, RET_HEADS))
    sin_t = jnp.tile(jnp.concatenate([-sin, sin], axis=1), (1, RET_HEADS))
    return cos_t, sin_t


def _block_diag(blocks):
    g, r, c = blocks.shape
    eye = jnp.eye(g, dtype=bool)
    return jnp.where(eye[:, None, :, None], blocks[:, :, None, :], 0).reshape(g * r, g * c)


def _layer_weights(l, w_in, s5, s5_b_bar, s5_c_re, s5_c_im, s5_d, s5_glu_w, s5_glu_b, gdn_conv_w, gdn_a_log,
                   gdn_dt_bias, gdn_norm_w, w_out, router_group_w, router_group_b, router_expert_w,
                   router_expert_b):
    wi = w_in[l]
    g0 = RET_IN + S5_IN
    w_ab = jnp.pad(wi[:, g0 + GDN_QKV + GDN_HD:], ((0, 0), (0, LANE - 2 * GDN_HEADS)))
    in_ws = [wi[:, :RET_IN], wi[:, g0:g0 + GDN_QKV], wi[:, RET_IN:g0], wi[:, g0 + GDN_QKV:g0 + GDN_QKV + GDN_HD],
             w_ab]
    in_ws = [w.astype(BF16) for w in in_ws]
    ab_re, ab_im, bb_re, bb_im = s5
    n = S5_LANES
    sl = slice(l * n, (l + 1) * n)
    ab = jnp.concatenate([ab_re[sl].reshape(1, n), ab_im[sl].reshape(1, n)], axis=0)
    to_bd = lambda m: _block_diag(jnp.swapaxes(m[sl].reshape(S5_GROUPS, S5_STATE, S5_GROUP_CH), 1, 2))
    wb = jnp.concatenate([to_bd(bb_re), to_bd(bb_im)], axis=1).astype(BF16)
    wc_re = _block_diag(jnp.swapaxes(s5_c_re[l], 1, 2)).astype(BF16)
    wc_im = _block_diag(jnp.swapaxes(s5_c_im[l], 1, 2)).astype(BF16)
    s5_ws = (ab, wb, wc_re, wc_im, s5_d[l].reshape(1, S5_CH), s5_glu_w[l].astype(BF16),
             s5_glu_b[l].reshape(1, S5_CH))
    pad4 = lambda v: jnp.pad(v.reshape(1, GDN_HEADS), ((0, 0), (0, LANE - GDN_HEADS)))
    gdn_ws = (gdn_conv_w[l], pad4(gdn_a_log[l]), pad4(gdn_dt_bias[l]),
              jnp.tile(gdn_norm_w[l], GDN_HEADS).reshape(1, GDN_HD))
    wo = w_out[l].astype(BF16)
    out_ws = (wo[:RET_VD], wo[RET_VD:RET_VD + S5_CH], wo[RET_VD + S5_CH:])
    wr = jnp.pad(jnp.concatenate([router_expert_w[l], router_group_w[l]], axis=1),
                 ((0, 0), (0, LANE - N_EXPERTS - MOE_GROUPS))).astype(BF16)
    br = jnp.pad(jnp.concatenate([router_expert_b[l], router_group_b[l]]),
                 (0, LANE - N_EXPERTS - MOE_GROUPS)).reshape(1, LANE)
    return in_ws, s5_ws, gdn_ws, out_ws, (wr, br)


def _trunk(x, pos, states, layer_ws, norm_mix, norm_ffn, experts, norm_final):
    bsz, seq, _ = x.shape
    t = bsz * seq
    ret_s, s5_re, s5_im, gdn_s, gdn_buf = states
    cos_t, sin_t = _rope_tables(pos)
    x = x.reshape(t, D_MODEL)
    outs = [[], [], [], [], []]
    for l in range(DEPTH):
        in_ws, s5_ws, gdn_ws, out_ws, (wr, br) = layer_ws[l]
        zr, zq, zs, zg, zab = _inproj(x, norm_mix[l].reshape(1, D_MODEL), in_ws)
        o_ret, ret_fin = _retention(zr, cos_t, sin_t, ret_s[l].reshape(bsz, RET_QD, RET_DV), bsz, seq)
        u_tm = jnp.swapaxes(zs.reshape(bsz, seq, S5_CH), 0, 1)
        o_s5_tm, re_fin, im_fin = _s5(u_tm, s5_re[l].reshape(bsz, S5_LANES), s5_im[l].reshape(bsz, S5_LANES),
                                      *s5_ws)
        o_s5 = jnp.swapaxes(o_s5_tm, 0, 1).reshape(t, S5_CH)
        buf0 = jnp.pad(gdn_buf[l], ((0, 0), (SUBLANE - (GDN_CONV - 1), 0), (0, 0)))
        o_gdn, nbuf, gdn_fin = _gdn(zq, zg, zab, *gdn_ws, buf0, gdn_s[l].reshape(bsz, GDN_HD, GDN_DV), bsz, seq)
        x1, h2_tiles, meta, counts = _outproj(x, o_ret, o_s5, o_gdn, *out_ws, norm_ffn[l].reshape(1, D_MODEL),
                                              wr, br)
        x = _moe(x1, h2_tiles, meta, counts, l, *experts)
        for lst, s in zip(outs, (ret_fin.reshape(bsz, RET_HEADS, RET_DK, RET_DV),
                                 re_fin.reshape(bsz, S5_GROUPS, S5_STATE),
                                 im_fin.reshape(bsz, S5_GROUPS, S5_STATE),
                                 gdn_fin.reshape(bsz, GDN_HEADS, GDN_DK, GDN_DV),
                                 nbuf[:, SUBLANE - (GDN_CONV - 1):, :])):
            lst.append(s)
    y = _final_norm(x, norm_final.reshape(1, D_MODEL)).reshape(bsz, seq, D_MODEL)
    return y, tuple(jnp.stack(o) for o in outs)


def kernel(x_prompt, x_sample, state_ret, state_s5_re, state_s5_im, state_gdn, state_gdn_conv, norm_mix, w_in, s5_a_re, s5_a_im, s5_log_dt, s5_b_re, s5_b_im, s5_c_re, s5_c_im, s5_d, s5_glu_w, s5_glu_b, gdn_conv_w, gdn_a_log, gdn_dt_bias, gdn_norm_w, w_out, norm_ffn, router_group_w, router_group_b, router_expert_w, router_expert_b, expert_w1, expert_w3, expert_w2, norm_final):
    n = DEPTH * S5_LANES
    col = lambda a: a.reshape(n, 1)
    log_dt = jnp.broadcast_to(s5_log_dt[:, :, None], (DEPTH, S5_GROUPS, S5_STATE))
    s5 = _s5_prep(col(s5_a_re), col(s5_a_im), col(log_dt), s5_b_re.reshape(n, S5_GROUP_CH),
                  s5_b_im.reshape(n, S5_GROUP_CH))
    layer_ws = [_layer_weights(l, w_in, s5, None, s5_c_re, s5_c_im, s5_d, s5_glu_w, s5_glu_b, gdn_conv_w,
                               gdn_a_log, gdn_dt_bias, gdn_norm_w, w_out, router_group_w, router_group_b,
                               router_expert_w, router_expert_b) for l in range(DEPTH)]
    experts = (expert_w1, expert_w3, expert_w2)
    bp, lp, _ = x_prompt.shape
    zero_states = (jnp.zeros((DEPTH, bp, RET_HEADS, RET_DK, RET_DV), F32),
                   jnp.zeros((DEPTH, bp, S5_GROUPS, S5_STATE), F32),
                   jnp.zeros((DEPTH, bp, S5_GROUPS, S5_STATE), F32),
                   jnp.zeros((DEPTH, bp, GDN_HEADS, GDN_DK, GDN_DV), F32),
                   jnp.zeros((DEPTH, bp, GDN_CONV - 1, GDN_QKV), F32))
    y_p, p_states = _trunk(x_prompt, jnp.arange(lp, dtype=F32), zero_states, layer_ws, norm_mix, norm_ffn,
                           experts, norm_final)
    y_s, s_states = _trunk(x_sample, PAST_LEN + jnp.arange(x_sample.shape[1], dtype=F32),
                           (state_ret, state_s5_re, state_s5_im, state_gdn, state_gdn_conv), layer_ws, norm_mix,
                           norm_ffn, experts, norm_final)
    return (y_p, y_s) + p_states + s_states
```

```python
import functools
import math

import jax
import jax.numpy as jnp
import numpy as np
from jax import lax
from jax.experimental import pallas as pl
from jax.experimental.pallas import tpu as pltpu

F32 = jnp.float32
BF16 = jnp.bfloat16
I32 = jnp.int32

D_MODEL = 1024
DEPTH = 2
RET_HEADS, RET_DK, RET_DV = 4, 64, 128
S5_GROUPS, S5_GROUP_CH, S5_STATE = 16, 16, 64
S5_CH = S5_GROUPS * S5_GROUP_CH
S5_LANES = S5_GROUPS * S5_STATE
GDN_HEADS, GDN_DK, GDN_DV, GDN_CONV = 4, 64, 64, 4
GDN_HD = GDN_HEADS * GDN_DK
GDN_QKV = 3 * GDN_HD
RET_QD = RET_HEADS * RET_DK
RET_VD = RET_HEADS * RET_DV
RET_IN = 2 * RET_QD + 2 * RET_VD
S5_IN = S5_CH
GDN_IN = GDN_QKV + GDN_HD + 2 * GDN_HEADS
GDN_CHUNK = 64
RET_CHUNK = 64
MOE_GROUPS, EXPERTS_PER_GROUP = 4, 8
N_EXPERTS = MOE_GROUPS * EXPERTS_PER_GROUP
D_EXPERT = 512
ROPE_BASE = 10000.0
EPS = 1e-6
PAST_LEN = 16384

LANE = 128
SUBLANE = 8
ROW_TILE = 512
MOE_BLK = 256
GATHER_TILE = 256
ISSUE_UNROLL = 8
SEQS_PER_STEP = 8
VMEM_LIMIT = 56 * 1024 * 1024


def _params(*sem):
    return pltpu.CompilerParams(dimension_semantics=sem, vmem_limit_bytes=VMEM_LIMIT)


def _dot(a, b):
    return jnp.dot(a.astype(BF16), b.astype(BF16), preferred_element_type=F32)


def _dot_nt(a, b):
    return lax.dot_general(a.astype(BF16), b.astype(BF16), (((1,), (1,)), ((), ())),
                           preferred_element_type=F32)


def _dot_tn(a, b):
    return lax.dot_general(a.astype(BF16), b.astype(BF16), (((0,), (0,)), ((), ())),
                           preferred_element_type=F32)


def _dot_hi(a, b):
    a1 = a.astype(BF16)
    a2 = (a - a1.astype(F32)).astype(BF16)
    b1 = b.astype(BF16)
    b2 = (b - b1.astype(F32)).astype(BF16)
    d = lambda x, y: jnp.dot(x, y, preferred_element_type=F32)
    return d(a1, b1) + (d(a1, b2) + d(a2, b1))


def _split3(x):
    p1 = x.astype(BF16)
    r1 = x - p1.astype(F32)
    p2 = r1.astype(BF16)
    p3 = (r1 - p2.astype(F32)).astype(BF16)
    return p1, p2, p3


def _dot_sel_l(sel, x):
    p1, p2, p3 = _split3(x)
    d = lambda p: jnp.dot(sel, p, preferred_element_type=F32)
    return d(p1) + d(p2) + d(p3)


def _dot_sel_r(x, sel):
    p1, p2, p3 = _split3(x)
    d = lambda p: jnp.dot(p, sel, preferred_element_type=F32)
    return d(p1) + d(p2) + d(p3)


def _rms(x):
    return x * lax.rsqrt(jnp.mean(x * x, axis=-1, keepdims=True) + EPS)


def _silu(x):
    return x * jax.nn.sigmoid(x)


def _inproj_body(x_ref, nw_ref, wr_ref, wq_ref, ws_ref, wg_ref, wab_ref,
                 zr_ref, zq_ref, zs_ref, zg_ref, zab_ref):
    h = (_rms(x_ref[...]) * nw_ref[...]).astype(BF16)
    for w_ref, z_ref in ((wr_ref, zr_ref), (wq_ref, zq_ref), (ws_ref, zs_ref),
                         (wg_ref, zg_ref), (wab_ref, zab_ref)):
        z_ref[...] = jnp.dot(h, w_ref[...], preferred_element_type=F32)


def _inproj(x, nw, ws):
    t = x.shape[0]
    tm = min(ROW_TILE, t)
    widths = [w.shape[1] for w in ws]
    row = lambda n: pl.BlockSpec((tm, n), lambda i: (i, 0))
    full = lambda a: pl.BlockSpec(a.shape, lambda i: (0, 0))
    return pl.pallas_call(
        _inproj_body,
        grid=(t // tm,),
        in_specs=[row(D_MODEL), full(nw)] + [full(w) for w in ws],
        out_specs=[row(n) for n in widths],
        out_shape=[jax.ShapeDtypeStruct((t, n), F32) for n in widths],
        compiler_params=_params("parallel"),
        name="inproj",
    )(x, nw, *ws)


def _ret_log_gamma(h):
    return math.log(1.0 - 2.0 ** (-5.0 - h))


def _ret_body(zr_ref, cos_ref, sin_ref, s0_ref, o_ref, sfin_ref, s_scr, *, nseq, rows, chunk):
    j = pl.program_id(1)
    stack = RET_HEADS * chunk

    @pl.when(j == 0)
    def _():
        s_scr[...] = s0_ref[...]

    lane = lax.broadcasted_iota(I32, (nseq * rows, RET_QD), 1)
    first_half = (lane % RET_DK) < (RET_DK // 2)

    def rotary(x):
        swapped = jnp.where(first_half, pltpu.roll(x, RET_QD - RET_DK // 2, 1),
                            pltpu.roll(x, RET_DK // 2, 1))
        return x * cos_ref[...] + swapped * sin_ref[...]

    q = rotary(zr_ref[:, 0:RET_QD])
    k = rotary(zr_ref[:, RET_QD:2 * RET_QD]) * (RET_DK ** -0.5)

    def by_head(idx, fn):
        out = jnp.zeros(idx.shape, F32)
        for h in range(RET_HEADS):
            out = jnp.where(idx == h, fn(h), out)
        return out

    lg_lane = by_head(lax.broadcasted_iota(I32, (chunk, RET_QD), 1) // RET_DK, _ret_log_gamma)
    pos = lax.broadcasted_iota(I32, (chunk, RET_QD), 0).astype(F32)
    q_scale = jnp.exp((pos + 1.0) * lg_lane)
    k_scale = jnp.exp((chunk - 1.0 - pos) * lg_lane)
    st_row = lax.broadcasted_iota(I32, (stack, stack), 0)
    st_col = lax.broadcasted_iota(I32, (stack, stack), 1)
    causal = ((st_row // chunk) == (st_col // chunk)) & (st_row >= st_col)
    lg_stack = by_head(st_row // chunk, _ret_log_gamma)
    decay = jnp.where(causal, jnp.exp(jnp.where(causal, (st_row - st_col).astype(F32), 0.0) * lg_stack), 0.0)
    q_rows = (lax.broadcasted_iota(I32, (stack, RET_QD), 0) // chunk) == \
             (lax.broadcasted_iota(I32, (stack, RET_QD), 1) // RET_DK)
    v_rows = (lax.broadcasted_iota(I32, (stack, RET_VD), 0) // chunk) == \
             (lax.broadcasted_iota(I32, (stack, RET_VD), 1) // RET_DV)
    s_row_head = lax.broadcasted_iota(I32, (RET_QD, RET_DV), 0) // RET_DK
    s_decay = by_head(s_row_head, lambda h: math.exp(chunk * _ret_log_gamma(h)))
    kv_diag = (lax.broadcasted_iota(I32, (RET_QD, RET_VD), 0) // RET_DK) == \
              (lax.broadcasted_iota(I32, (RET_QD, RET_VD), 1) // RET_DV)

    def tile4(a):
        return jnp.concatenate([a] * RET_HEADS, axis=0)

    def collapse(a):
        out = a[0:chunk]
        for h in range(1, RET_HEADS):
            out = out + a[h * chunk:(h + 1) * chunk]
        return out

    seqs = range(nseq)
    n_chunks = rows // chunk
    state = [s_scr[s] for s in seqs]
    for ci in range(n_chunks):
        rs = [slice(s * rows + ci * chunk, s * rows + (ci + 1) * chunk) for s in seqs]
        v_cs = [zr_ref[r, 2 * RET_QD:2 * RET_QD + RET_VD] for r in rs]
        scores = [_dot_nt(jnp.where(q_rows, tile4(q[r]), 0.0), tile4(k[r])) * decay for r in rs]
        o_intra = [collapse(_dot(scores[s], jnp.where(v_rows, tile4(v_cs[s]), 0.0))) for s in seqs]
        s_full = [jnp.concatenate([jnp.where(s_row_head == h, state[s], 0.0) for h in range(RET_HEADS)], axis=1)
                  for s in seqs]
        o_cs = [o_intra[s] + _dot(q[rs[s]] * q_scale, s_full[s]) for s in seqs]
        for s in seqs:
            gate = zr_ref[rs[s], 2 * RET_QD + RET_VD:2 * RET_QD + 2 * RET_VD]
            for h in range(RET_HEADS):
                sl = slice(h * RET_DV, (h + 1) * RET_DV)
                o_ref[rs[s], sl] = _rms(o_cs[s][:, sl]) * _silu(gate[:, sl])
        for s in seqs:
            kv = jnp.where(kv_diag, _dot_tn(k[rs[s]] * k_scale, v_cs[s]), 0.0)
            kv_own = kv[:, 0:RET_DV]
            for h in range(1, RET_HEADS):
                kv_own = kv_own + kv[:, h * RET_DV:(h + 1) * RET_DV]
            state[s] = state[s] * s_decay + kv_own
    for s in seqs:
        s_scr[s] = state[s]

    @pl.when(j == pl.num_programs(1) - 1)
    def _():
        for s in seqs:
            sfin_ref[s] = state[s]


def _retention(zr, cos, sin, s0, bsz, seq):
    chunk = min(RET_CHUNK, seq)
    rows = min(4 * chunk, seq)
    nj = seq // rows
    nseq = SEQS_PER_STEP if nj == 1 and bsz % SEQS_PER_STEP == 0 else 1
    if nseq > 1:
        cos, sin = jnp.tile(cos, (nseq, 1)), jnp.tile(sin, (nseq, 1))
    return pl.pallas_call(
        functools.partial(_ret_body, nseq=nseq, rows=rows, chunk=chunk),
        grid=(bsz // nseq, nj),
        in_specs=[pl.BlockSpec((nseq * rows, RET_IN), lambda b, j: (b * nj + j, 0)),
                  pl.BlockSpec((nseq * rows, RET_QD), lambda b, j: (j, 0)),
                  pl.BlockSpec((nseq * rows, RET_QD), lambda b, j: (j, 0)),
                  pl.BlockSpec((nseq, RET_QD, RET_DV), lambda b, j: (b, 0, 0))],
        out_specs=[pl.BlockSpec((nseq * rows, RET_VD), lambda b, j: (b * nj + j, 0)),
                   pl.BlockSpec((nseq, RET_QD, RET_DV), lambda b, j: (b, 0, 0))],
        out_shape=[jax.ShapeDtypeStruct((bsz * seq, RET_VD), F32),
                   jax.ShapeDtypeStruct((bsz, RET_QD, RET_DV), F32)],
        scratch_shapes=[pltpu.VMEM((nseq, RET_QD, RET_DV), F32)],
        compiler_params=_params("parallel", "arbitrary"),
        name="retention",
    )(zr, cos, sin, s0)


def _s5_prep_body(are_ref, aim_ref, ldt_ref, bre_ref, bim_ref, abre_ref, abim_ref, bbre_ref, bbim_ref):
    lam_re, lam_im = are_ref[...], aim_ref[...]
    dt = jnp.exp(ldt_ref[...])
    mag = jnp.exp(lam_re * dt)
    ab_re = mag * jnp.cos(lam_im * dt)
    ab_im = mag * jnp.sin(lam_im * dt)
    den = lam_re * lam_re + lam_im * lam_im
    f_re = ((ab_re - 1.0) * lam_re + ab_im * lam_im) / den
    f_im = (ab_im * lam_re - (ab_re - 1.0) * lam_im) / den
    abre_ref[...] = ab_re
    abim_ref[...] = ab_im
    bbre_ref[...] = f_re * bre_ref[...] - f_im * bim_ref[...]
    bbim_ref[...] = f_re * bim_ref[...] + f_im * bre_ref[...]


def _s5_prep(a_re, a_im, log_dt, b_re, b_im):
    n = a_re.shape[0]
    col = jax.ShapeDtypeStruct((n, 1), F32)
    mat = jax.ShapeDtypeStruct((n, S5_GROUP_CH), F32)
    return pl.pallas_call(_s5_prep_body, out_shape=[col, col, mat, mat], name="s5_prep")(
        a_re, a_im, log_dt, b_re, b_im)


def _gelu_tanh(x):
    return x * (0.5 * (1.0 + jnp.tanh(math.sqrt(2.0 / math.pi) * (x + 0.044715 * (x * x * x)))))


def _s5_body(u_ref, h0re_ref, h0im_ref, ab_ref, wb_ref, wcre_ref, wcim_ref, d_ref, gw_ref, gb_ref,
             o_ref, hre_ref, him_ref, bu_scr, st_scr, *, steps):
    j = pl.program_id(1)

    @pl.when(j == 0)
    def _():
        st_scr[0] = h0re_ref[...]
        st_scr[1] = h0im_ref[...]

    rows = steps * SUBLANE
    u = u_ref[...].reshape(rows, S5_CH)
    bu_scr[...] = _dot(u, wb_ref[...])
    a_re = jnp.broadcast_to(ab_ref[0:1, :], (SUBLANE, S5_LANES))
    a_im = jnp.broadcast_to(ab_ref[1:2, :], (SUBLANE, S5_LANES))

    def step(t, carry):
        h_re, h_im = carry
        r = pl.ds(pl.multiple_of(t * SUBLANE, SUBLANE), SUBLANE)
        n_re = a_re * h_re - a_im * h_im + bu_scr[r, 0:S5_LANES]
        n_im = a_re * h_im + a_im * h_re + bu_scr[r, S5_LANES:2 * S5_LANES]
        bu_scr[r, 0:S5_LANES] = n_re
        bu_scr[r, S5_LANES:2 * S5_LANES] = n_im
        return n_re, n_im

    h_re, h_im = lax.fori_loop(0, steps, step, (st_scr[0], st_scr[1]))
    st_scr[0] = h_re
    st_scr[1] = h_im
    hre_ref[...] = h_re
    him_ref[...] = h_im

    y = _dot(bu_scr[:, 0:S5_LANES], wcre_ref[...]) - _dot(bu_scr[:, S5_LANES:2 * S5_LANES], wcim_ref[...])
    y = _gelu_tanh(y + d_ref[...] * u)
    y = y * jax.nn.sigmoid(_dot(y, gw_ref[...]) + gb_ref[...])
    o_ref[...] = y.reshape(steps, SUBLANE, S5_CH)


def _s5(u_tm, h0_re, h0_im, ab, wb, wc_re, wc_im, d_skip, glu_w, glu_b):
    seq, bsz, _ = u_tm.shape
    steps = min(64, seq)
    full = lambda a: pl.BlockSpec(a.shape, lambda g, j: (0,) * a.ndim)
    st = pl.BlockSpec((SUBLANE, S5_LANES), lambda g, j: (g, 0))
    return pl.pallas_call(
        functools.partial(_s5_body, steps=steps),
        grid=(bsz // SUBLANE, seq // steps),
        in_specs=[pl.BlockSpec((steps, SUBLANE, S5_CH), lambda g, j: (j, g, 0)), st, st,
                  full(ab), full(wb), full(wc_re), full(wc_im), full(d_skip), full(glu_w), full(glu_b)],
        out_specs=[pl.BlockSpec((steps, SUBLANE, S5_CH), lambda g, j: (j, g, 0)), st, st],
        out_shape=[jax.ShapeDtypeStruct((seq, bsz, S5_CH), F32),
                   jax.ShapeDtypeStruct((bsz, S5_LANES), F32),
                   jax.ShapeDtypeStruct((bsz, S5_LANES), F32)],
        scratch_shapes=[pltpu.VMEM((steps * SUBLANE, 2 * S5_LANES), F32),
                        pltpu.VMEM((2, SUBLANE, S5_LANES), F32)],
        compiler_params=_params("parallel", "arbitrary"),
        name="s5",
    )(u_tm, h0_re, h0_im, ab, wb, wc_re, wc_im, d_skip, glu_w, glu_b)


def _gdn_body(zq_ref, zg_ref, zab_ref, cw_ref, alog_ref, dtb_ref, nw_ref, buf0_ref, s0_ref,
              ones_bd_ref, tri_ref, ea_ref, eb_ref, ec_ref, spread_ref, gather_ref,
              o_ref, nbuf_ref, sfin_ref,
              ext_scr, q_scr, k_scr, kb_scr, vb_scr, g_scr, g4_scr, s_scr, *, nseq, rows, chunk):
    j = pl.program_id(1)
    stack = GDN_HEADS * chunk
    bd_state = (lax.broadcasted_iota(I32, (GDN_HD, GDN_HD), 0) // GDN_DK) == \
               (lax.broadcasted_iota(I32, (GDN_HD, GDN_HD), 1) // GDN_DV)

    @pl.when(j == 0)
    def _():
        for s in range(nseq):
            ext_scr[s, 0:SUBLANE, :] = buf0_ref[s]
            s_scr[s] = jnp.where(bd_state, _dot_sel_r(s0_ref[s], spread_ref[...]), 0.0)

    convs = []
    for s in range(nseq):
        ext_scr[s, SUBLANE:SUBLANE + rows, :] = zq_ref[s * rows:(s + 1) * rows, :]
        conv = ext_scr[s, SUBLANE - 3:SUBLANE - 3 + rows, :] * cw_ref[0:1, :]
        for i in range(1, GDN_CONV):
            conv = conv + ext_scr[s, SUBLANE - 3 + i:SUBLANE - 3 + i + rows, :] * cw_ref[i:i + 1, :]
        tail = ext_scr[s, rows:rows + SUBLANE, :]
        nbuf_ref[s] = tail
        ext_scr[s, 0:SUBLANE, :] = tail
        convs.append(conv)
    qkv = _silu(convs[0] if nseq == 1 else jnp.concatenate(convs, axis=0))

    ones_bd = ones_bd_ref[...]
    q_raw = qkv[:, 0:GDN_HD]
    k_raw = qkv[:, GDN_HD:2 * GDN_HD]
    q_scr[...] = q_raw * lax.rsqrt(_dot_sel_r(q_raw * q_raw, ones_bd) + EPS) * (GDN_DK ** -0.5)
    k_n = k_raw * lax.rsqrt(_dot_sel_r(k_raw * k_raw, ones_bd) + EPS)
    k_scr[...] = k_n

    ab = zab_ref[...]
    x = ab + dtb_ref[...]
    softplus = jnp.maximum(x, 0.0) + jnp.log1p(jnp.exp(-jnp.abs(x)))
    g_pad = -jnp.exp(alog_ref[...]) * softplus
    beta = _dot_sel_r(jax.nn.sigmoid(ab), eb_ref[...])
    g_cum = _dot_sel_l(tri_ref[...], g_pad)
    g_scr[...] = _dot_sel_r(g_cum, ea_ref[...])
    g4_scr[...] = _dot_sel_r(g_pad, ec_ref[...])
    kb_scr[...] = k_n * beta
    vb_scr[...] = qkv[:, 2 * GDN_HD:3 * GDN_HD] * beta

    w_row = lax.broadcasted_iota(I32, (chunk, stack), 0)
    w_col = lax.broadcasted_iota(I32, (chunk, stack), 1) % chunk
    strict_w = w_row > w_col
    causal_w = w_row >= w_col
    same_head = (lax.broadcasted_iota(I32, (stack, stack), 0) // chunk) == \
                (lax.broadcasted_iota(I32, (stack, stack), 1) // chunk)
    head_rows = (lax.broadcasted_iota(I32, (stack, GDN_HD), 0) // chunk) == \
                (lax.broadcasted_iota(I32, (stack, GDN_HD), 1) // GDN_DK)
    head_rows2 = jnp.concatenate([head_rows, head_rows], axis=1)
    tri_c = tri_ref[0:chunk, 0:chunk]
    packed_ok = chunk % (2 * SUBLANE) == 0

    def tile4(a):
        return jnp.concatenate([a] * GDN_HEADS, axis=0)

    def split2(a):
        hi = a.astype(BF16)
        return hi, (a - hi.astype(F32)).astype(BF16)

    def stack_masked(parts, mask):
        if packed_ok:
            return [jnp.where(mask, tile4(p), jnp.zeros((), BF16)) for p in parts]
        return [jnp.where(mask, tile4(p.astype(F32)), 0.0).astype(BF16) for p in parts]

    def on_diag(parts):
        return stack_masked(parts, same_head)

    def mm(a, b):
        return jnp.dot(a, b, preferred_element_type=F32)

    def mm_hi(a_parts, b_parts):
        return mm(a_parts[0], b_parts[0]) + (mm(a_parts[0], b_parts[1]) + mm(a_parts[1], b_parts[0]))

    n_chunks = rows // chunk
    every = range(nseq * n_chunks)
    rows_of = lambda u: slice(u * chunk, (u + 1) * chunk)
    solved = []
    for u in every:
        r = rows_of(u)
        q_c, k_c, kb_c, vb_c, g_c = q_scr[r, :], k_scr[r, :], kb_scr[r, :], vb_scr[r, :], g_scr[r, :]
        exp_g = jnp.exp(g_c)
        g_diff = _dot_sel_l(tri_c, jnp.where(strict_w, g4_scr[r, :], 0.0))
        decay = jnp.exp(jnp.where(causal_w, g_diff, 0.0))
        k_heads = jnp.where(head_rows, tile4(k_c), 0.0)
        lmat = jnp.where(strict_w, _dot_nt(kb_c, k_heads) * decay, 0.0)
        attn = jnp.where(causal_w, _dot_nt(q_c, k_heads) * decay, 0.0)
        rhs_c = jnp.concatenate([vb_c, kb_c * exp_g], axis=1)
        solved.append((lmat, attn, exp_g, rhs_c))

    t_acc = [-solved[ci][0] for ci in every]
    p_parts = [split2(solved[ci][0]) for ci in every]
    p_diag = [on_diag(p) for p in p_parts]
    span = 2
    while span < chunk + 1:
        power = [mm_hi(p_parts[ci], p_diag[ci]) for ci in every]
        p_parts = [split2(p) for p in power]
        p_diag = [on_diag(p) for p in p_parts]
        t_acc = [t_acc[ci] + power[ci] + mm_hi(split2(t_acc[ci]), p_diag[ci]) for ci in every]
        span *= 2
    rhs_parts = [stack_masked(split2(solved[ci][3]), head_rows2) for ci in every]
    sols = [solved[ci][3] + mm_hi(split2(t_acc[ci]), rhs_parts[ci]) for ci in every]

    seqs = range(nseq)
    s_cur = [s_scr[s] for s in seqs]
    for ci in range(n_chunks):
        us = [s * n_chunks + ci for s in seqs]
        v_new = [sols[u][:, 0:GDN_HD] - _dot(sols[u][:, GDN_HD:2 * GDN_HD], s_cur[s]) for s, u in enumerate(us)]
        o_cs = [_dot(q_scr[rows_of(u), :] * solved[u][2], s_cur[s])
                + _dot(solved[u][1], jnp.where(head_rows, tile4(v_new[s]), 0.0)) for s, u in enumerate(us)]
        for s, u in enumerate(us):
            g_c = g_scr[rows_of(u), :]
            g_last = g_c[chunk - 1:chunk, :]
            k_dec = k_scr[rows_of(u), :] * jnp.exp(g_last - g_c)
            s_cur[s] = s_cur[s] * jnp.exp(g_last) + jnp.where(bd_state, _dot_tn(k_dec, v_new[s]), 0.0)
        for s, u in enumerate(us):
            r = rows_of(u)
            ms = _dot_sel_r(o_cs[s] * o_cs[s], ones_bd) * (1.0 / GDN_DV)
            o_ref[r, :] = (o_cs[s] * lax.rsqrt(ms + EPS) * nw_ref[...] * _silu(zg_ref[r, :])).astype(o_ref.dtype)
    for s in seqs:
        s_scr[s] = s_cur[s]

    @pl.when(j == pl.num_programs(1) - 1)
    def _():
        for s in seqs:
            sfin_ref[s] = _dot_sel_r(s_scr[s], gather_ref[...])


def _gdn_consts(rows, chunk):
    stack = GDN_HEADS * chunk
    blk = lambda n, c: (np.arange(n)[:, None] // c) == (np.arange(n)[None, :] // c)
    ones_bd = blk(GDN_HD, GDN_DK)
    tri = blk(rows, chunk) & (np.arange(rows)[:, None] >= np.arange(rows)[None, :])
    src = np.arange(LANE)[:, None]
    ea = src == np.arange(GDN_HD)[None, :] // GDN_DK
    eb = src == GDN_HEADS + np.arange(GDN_HD)[None, :] // GDN_DK
    ec = src == np.arange(stack)[None, :] // chunk
    spread = np.arange(GDN_DV)[:, None] == np.arange(GDN_HD)[None, :] % GDN_DV
    return tuple(jnp.asarray(m, dtype=BF16) for m in (ones_bd, tri, ea, eb, ec, spread, spread.T))


def _gdn(zq, zg, zab, conv_w, alog_pad, dtb_pad, nw, buf0, s0, bsz, seq):
    chunk = min(GDN_CHUNK, seq)
    rows = min(4 * chunk, seq)
    nj = seq // rows
    nseq = SEQS_PER_STEP if nj == 1 and bsz % SEQS_PER_STEP == 0 else 1
    stack = GDN_HEADS * chunk
    consts = _gdn_consts(nseq * rows, chunk)
    row = lambda n: pl.BlockSpec((nseq * rows, n), lambda b, j: (b * nj + j, 0))
    full = lambda a: pl.BlockSpec(a.shape, lambda b, j: (0,) * a.ndim)
    per_b = lambda *s: pl.BlockSpec((nseq,) + s, lambda b, j: (b, 0, 0))
    return pl.pallas_call(
        functools.partial(_gdn_body, nseq=nseq, rows=rows, chunk=chunk),
        grid=(bsz // nseq, nj),
        in_specs=[row(GDN_QKV), row(GDN_HD), row(LANE), full(conv_w), full(alog_pad), full(dtb_pad), full(nw),
                  per_b(SUBLANE, GDN_QKV), per_b(GDN_HD, GDN_DV)] + [full(c) for c in consts],
        out_specs=[row(GDN_HD), per_b(SUBLANE, GDN_QKV), per_b(GDN_HD, GDN_DV)],
        out_shape=[jax.ShapeDtypeStruct((bsz * seq, GDN_HD), F32),
                   jax.ShapeDtypeStruct((bsz, SUBLANE, GDN_QKV), F32),
                   jax.ShapeDtypeStruct((bsz, GDN_HD, GDN_DV), F32)],
        scratch_shapes=[pltpu.VMEM((nseq, rows + SUBLANE, GDN_QKV), F32)] +
                       [pltpu.VMEM((nseq * rows, GDN_HD), F32)] * 5 +
                       [pltpu.VMEM((nseq * rows, stack), F32), pltpu.VMEM((nseq, GDN_HD, GDN_HD), F32)],
        compiler_params=_params("parallel", "arbitrary"),
        name="gdn",
    )(zq, zg, zab, conv_w, alog_pad, dtb_pad, nw, buf0, s0, *consts)


def _outproj_body(x_ref, oret_ref, os5_ref, ogdn_ref, w1_ref, w2_ref, w3_ref, nw_ref, wr_ref, br_ref, tri_ref,
                  x1_ref, h2_ref, meta_ref, cnt_ref, carry_scr, *, tm):
    i = pl.program_id(0)

    @pl.when(i == 0)
    def _():
        carry_scr[...] = jnp.zeros_like(carry_scr)

    mix = _dot(oret_ref[...], w1_ref[...]) + _dot(os5_ref[...], w2_ref[...]) + _dot(ogdn_ref[...], w3_ref[...])
    x1 = x_ref[...] + mix
    x1_ref[...] = x1
    h2 = _rms(x1) * nw_ref[...]
    for s in range(D_MODEL // LANE):
        h2_ref[pl.ds(s, tm, stride=SUBLANE), :] = h2[:, s * LANE:(s + 1) * LANE]

    logits = _dot(h2, wr_ref[...]) + br_ref[...]
    lane_i = lax.broadcasted_iota(I32, (tm, LANE), 1)
    lane = lane_i.astype(F32)
    neg = -jnp.inf
    big = float(LANE)
    g_log = jnp.where((lane_i >= N_EXPERTS) & (lane_i < N_EXPERTS + MOE_GROUPS), logits, neg)
    g_max = jnp.max(g_log, axis=-1, keepdims=True)
    grp = jnp.min(jnp.where(g_log == g_max, lane - N_EXPERTS, big), axis=-1, keepdims=True)
    p_grp = 1.0 / jnp.sum(jnp.exp(g_log - g_max), axis=-1, keepdims=True)
    in_grp = (lane >= grp * EXPERTS_PER_GROUP) & (lane < (grp + 1.0) * EXPERTS_PER_GROUP)
    e_log = jnp.where(in_grp, logits, neg)
    v1 = jnp.max(e_log, axis=-1, keepdims=True)
    i1 = jnp.min(jnp.where(e_log == v1, lane, big), axis=-1, keepdims=True)
    e_log2 = jnp.where(lane == i1, neg, e_log)
    v2 = jnp.max(e_log2, axis=-1, keepdims=True)
    i2 = jnp.min(jnp.where(e_log2 == v2, lane, big), axis=-1, keepdims=True)
    e2 = jnp.exp(v2 - v1)
    gate1 = p_grp / (1.0 + e2)
    gate2 = p_grp * e2 / (1.0 + e2)

    oh1 = lane == i1
    oh2 = lane == i2
    picked = jnp.where(oh1 | oh2, 1.0, 0.0)
    before = jnp.dot(tri_ref[...], picked.astype(BF16), preferred_element_type=F32) + carry_scr[0:1, :]
    rank1 = jnp.sum(jnp.where(oh1, before, 0.0), axis=-1, keepdims=True)
    rank2 = jnp.sum(jnp.where(oh2, before, 0.0), axis=-1, keepdims=True)
    total = carry_scr[0:1, :] + jnp.sum(picked, axis=0, keepdims=True)
    carry_scr[...] = jnp.broadcast_to(total, carry_scr.shape)
    cnt_ref[...] = jnp.broadcast_to(total, cnt_ref.shape)

    meta = jnp.where(lane_i == 0, i1, 0.0)
    meta = jnp.where(lane_i == 1, i2, meta)
    meta = jnp.where(lane_i == 2, gate1, meta)
    meta = jnp.where(lane_i == 3, gate2, meta)
    meta = jnp.where(lane_i == 4, rank1, meta)
    meta = jnp.where(lane_i == 5, rank2, meta)
    meta_ref[...] = meta


def _outproj(x, o_ret, o_s5, o_gdn, w1, w2, w3, nw, wr, br):
    t = x.shape[0]
    tm = min(ROW_TILE, t)
    tri = jnp.asarray(np.arange(tm)[:, None] > np.arange(tm)[None, :], dtype=BF16)
    row = lambda n: pl.BlockSpec((tm, n), lambda i: (i, 0))
    full = lambda a: pl.BlockSpec(a.shape, lambda i: (0, 0))
    return pl.pallas_call(
        functools.partial(_outproj_body, tm=tm),
        grid=(t // tm,),
        in_specs=[row(D_MODEL), row(RET_VD), row(S5_CH), row(GDN_HD),
                  full(w1), full(w2), full(w3), full(nw), full(wr), full(br), full(tri)],
        out_specs=[row(D_MODEL), pl.BlockSpec((tm * SUBLANE, LANE), lambda i: (i, 0)), row(LANE),
                   pl.BlockSpec((SUBLANE, LANE), lambda i: (0, 0))],
        out_shape=[jax.ShapeDtypeStruct((t, D_MODEL), F32),
                   jax.ShapeDtypeStruct((t * SUBLANE, LANE), F32),
                   jax.ShapeDtypeStruct((t, LANE), F32),
                   jax.ShapeDtypeStruct((SUBLANE, LANE), F32)],
        scratch_shapes=[pltpu.VMEM((SUBLANE, LANE), F32)],
        compiler_params=_params("arbitrary"),
        name="outproj_router",
    )(x, o_ret, o_s5, o_gdn, w1, w2, w3, nw, wr, br, tri)


def _token_rows(ref, idx):
    return ref.at[pl.ds(pl.multiple_of(idx * SUBLANE, SUBLANE), SUBLANE)]


def _slot(e_ref, r_ref, ps_ref, g):
    return ps_ref[e_ref[g]] + r_ref[g]


ZERO_ROWS = 128


def _zero_segment(zero_scr, xb_ref, sem, start, length, wait):
    def piece(off, n):
        cp = pltpu.make_async_copy(zero_scr.at[pl.ds(0, n * SUBLANE)],
                                   xb_ref.at[pl.ds(pl.multiple_of(off * SUBLANE, SUBLANE), n * SUBLANE)], sem)
        cp.wait() if wait else cp.start()

    n_big = length // ZERO_ROWS

    def big(i, carry):
        piece(start + i * ZERO_ROWS, ZERO_ROWS)
        return carry

    lax.fori_loop(0, n_big, big, 0)
    off = start + n_big * ZERO_ROWS
    rem = length - n_big * ZERO_ROWS
    bit = ZERO_ROWS // 2
    while bit >= 1:
        has = (rem & bit) != 0
        pl.when(has)(functools.partial(piece, off, bit))
        off = off + jnp.where(has, bit, 0)
        bit //= 2


def _scatter_body(e1_ref, e2_ref, r1_ref, r2_ref, ps_ref, zs_ref, zl_ref, src_ref, xb_ref, zero_scr, sem, zsem,
                  *, tm):
    base = pl.program_id(0) * tm

    @pl.when(pl.program_id(0) == 0)
    def _():
        zero_scr[...] = jnp.zeros_like(zero_scr)
        for wait in (False, True):
            lax.fori_loop(0, N_EXPERTS + 1,
                          lambda s, c, wait=wait: (_zero_segment(zero_scr, xb_ref, zsem, zs_ref[s], zl_ref[s], wait),
                                                   c)[1], 0)

    def copy(t, d):
        return pltpu.make_async_copy(_token_rows(src_ref, t), _token_rows(xb_ref, d), sem)

    def issue(t, carry):
        copy(t, _slot(e1_ref, r1_ref, ps_ref, base + t)).start(priority=0)
        copy(t, _slot(e2_ref, r2_ref, ps_ref, base + t)).start(priority=1)
        return carry

    lax.fori_loop(0, tm, issue, 0, unroll=ISSUE_UNROLL)

    for _ in range(2):
        pltpu.make_async_copy(src_ref, src_ref, sem).wait()


def _scatter(route, zero_start, zero_len, h2_tiles, n_rows):
    tokens = route[0].shape[0]
    tm = min(GATHER_TILE, tokens)
    assert tokens % tm == 0
    return pl.pallas_call(
        functools.partial(_scatter_body, tm=tm),
        grid_spec=pltpu.PrefetchScalarGridSpec(
            num_scalar_prefetch=7, grid=(tokens // tm,),
            in_specs=[pl.BlockSpec((tm * SUBLANE, LANE), lambda i, *_: (i, 0))],
            out_specs=pl.BlockSpec(memory_space=pl.ANY),
            scratch_shapes=[pltpu.VMEM((ZERO_ROWS * SUBLANE, LANE), F32), pltpu.SemaphoreType.DMA(()),
                            pltpu.SemaphoreType.DMA(())]),
        out_shape=jax.ShapeDtypeStruct((n_rows * SUBLANE, LANE), F32),
        compiler_params=_params("arbitrary"),
        name="moe_scatter",
    )(*route, zero_start, zero_len, h2_tiles)


def _experts_body(be_ref, nb_ref, xb_ref, w1_ref, w3_ref, w2_ref, yb_ref, w1_scr, w3_scr, w2_scr, *, blk):
    i = pl.program_id(0)
    live = i < nb_ref[0]

    @pl.when(live & ((i == 0) | (be_ref[i] != be_ref[jnp.maximum(i - 1, 0)])))
    def _():
        w1_scr[...] = w1_ref[...].astype(BF16)
        w3_scr[...] = w3_ref[...].astype(BF16)
        w2_scr[...] = w2_ref[...].astype(BF16)

    @pl.when(live)
    def _():
        x = jnp.concatenate([xb_ref[pl.ds(s, blk, stride=SUBLANE), :] for s in range(D_MODEL // LANE)],
                            axis=1).astype(BF16)
        hid = _silu(jnp.dot(x, w1_scr[...], preferred_element_type=F32)) * \
            jnp.dot(x, w3_scr[...], preferred_element_type=F32)
        y = _dot(hid, w2_scr[...])
        for s in range(D_MODEL // LANE):
            yb_ref[pl.ds(s, blk, stride=SUBLANE), :] = y[:, s * LANE:(s + 1) * LANE]

    @pl.when(jnp.logical_not(live))
    def _():
        yb_ref[...] = jnp.zeros_like(yb_ref)


def _experts(block_e, nb_used, xb, layer, w1, w3, w2, n_blocks):
    blk = MOE_BLK
    live = lambda i, nb: jnp.minimum(i, nb[0] - 1)
    tile_in = pl.BlockSpec((blk * SUBLANE, LANE), lambda i, be, nb: (live(i, nb), 0))
    tile_out = pl.BlockSpec((blk * SUBLANE, LANE), lambda i, be, nb: (i, 0))
    wspec = lambda a: pl.BlockSpec((None, None) + a.shape[2:],
                                   lambda i, be, nb: (layer, be[live(i, nb)], 0, 0))
    return pl.pallas_call(
        functools.partial(_experts_body, blk=blk),
        grid_spec=pltpu.PrefetchScalarGridSpec(
            num_scalar_prefetch=2, grid=(n_blocks,),
            in_specs=[tile_in, wspec(w1), wspec(w3), wspec(w2)], out_specs=tile_out,
            scratch_shapes=[pltpu.VMEM(w1.shape[2:], BF16), pltpu.VMEM(w3.shape[2:], BF16),
                            pltpu.VMEM(w2.shape[2:], BF16)]),
        out_shape=jax.ShapeDtypeStruct(xb.shape, F32),
        compiler_params=_params("arbitrary"),
        name="moe_experts",
    )(block_e, nb_used, xb, w1, w3, w2)


def _combine_body(e1_ref, e2_ref, r1_ref, r2_ref, ps_ref, x1_ref, meta_ref, yb_ref, out_ref, buf_ref, sem, *, tm):
    i = pl.program_id(0)
    phase = i % 2

    def gather(tile, ph):
        base = tile * tm

        def issue(t, carry):
            for k, (e_ref, r_ref) in enumerate(((e1_ref, r1_ref), (e2_ref, r2_ref))):
                pltpu.make_async_copy(_token_rows(yb_ref, _slot(e_ref, r_ref, ps_ref, base + t)),
                                      _token_rows(buf_ref.at[ph, k], t), sem.at[ph]).start(priority=k)
            return carry

        lax.fori_loop(0, tm, issue, 0, unroll=ISSUE_UNROLL)

    @pl.when(i == 0)
    def _():
        gather(0, 0)

    @pl.when(i + 1 < pl.num_programs(0))
    def _():
        gather(i + 1, 1 - phase)

    for k in range(2):
        pltpu.make_async_copy(buf_ref.at[phase, k], buf_ref.at[phase, k], sem.at[phase]).wait()

    def rows_of(slot):
        return jnp.concatenate([buf_ref[phase, slot, pl.ds(s, tm, stride=SUBLANE), :]
                                for s in range(D_MODEL // LANE)], axis=1)

    meta = meta_ref[...]
    out_ref[...] = x1_ref[...] + (meta[:, 2:3] * rows_of(0) + meta[:, 3:4] * rows_of(1))


def _combine(route, x1, meta, yb):
    t = x1.shape[0]
    tm = min(GATHER_TILE, t)
    return pl.pallas_call(
        functools.partial(_combine_body, tm=tm),
        grid_spec=pltpu.PrefetchScalarGridSpec(
            num_scalar_prefetch=5, grid=(t // tm,),
            in_specs=[pl.BlockSpec((tm, D_MODEL), lambda i, *_: (i, 0)),
                      pl.BlockSpec((tm, LANE), lambda i, *_: (i, 0)),
                      pl.BlockSpec(memory_space=pl.ANY)],
            out_specs=pl.BlockSpec((tm, D_MODEL), lambda i, *_: (i, 0)),
            scratch_shapes=[pltpu.VMEM((2, 2, tm * SUBLANE, LANE), F32), pltpu.SemaphoreType.DMA((2,))]),
        out_shape=jax.ShapeDtypeStruct((t, D_MODEL), F32),
        compiler_params=_params("arbitrary"),
        name="moe_combine",
    )(*route, x1, meta, yb)


def _moe(x1, h2_tiles, meta, counts, layer, w1, w3, w2):
    t = x1.shape[0]
    n_blocks = (2 * t + N_EXPERTS * (MOE_BLK - 1)) // MOE_BLK
    e1, e2 = meta[:, 0].astype(I32), meta[:, 1].astype(I32)
    r1, r2 = meta[:, 4].astype(I32), meta[:, 5].astype(I32)
    cnt = counts[0, :N_EXPERTS].astype(I32)
    padded = (cnt + MOE_BLK - 1) // MOE_BLK * MOE_BLK
    pad_end = jnp.cumsum(padded)
    pad_start = pad_end - padded
    route = (e1, e2, r1, r2, pad_start)
    nb_used = (pad_end[-1:] // MOE_BLK).astype(I32)
    blk_start = jnp.arange(n_blocks, dtype=I32) * MOE_BLK
    block_e = jnp.minimum(jnp.sum(blk_start[:, None] >= pad_end[None, :], axis=1), N_EXPERTS - 1).astype(I32)
    n_rows = n_blocks * MOE_BLK
    zero_start = jnp.concatenate([pad_start + cnt, pad_end[-1:]]).astype(I32)
    zero_len = jnp.concatenate([padded - cnt, n_rows - pad_end[-1:]]).astype(I32)
    xb = _scatter(route, zero_start, zero_len, h2_tiles, n_rows)
    yb = _experts(block_e, nb_used, xb, layer, w1, w3, w2, n_blocks)
    return _combine(route, x1, meta, yb)


def _final_body(x_ref, nw_ref, o_ref):
    o_ref[...] = _rms(x_ref[...]) * nw_ref[...]


def _final_norm(x, nw):
    t = x.shape[0]
    tm = min(ROW_TILE, t)
    return pl.pallas_call(
        _final_body,
        grid=(t // tm,),
        in_specs=[pl.BlockSpec((tm, D_MODEL), lambda i: (i, 0)), pl.BlockSpec((1, D_MODEL), lambda i: (0, 0))],
        out_specs=pl.BlockSpec((tm, D_MODEL), lambda i: (i, 0)),
        out_shape=jax.ShapeDtypeStruct((t, D_MODEL), F32),
        compiler_params=_params("parallel"),
        name="final_norm",
    )(x, nw)


def _rope_tables(pos):
    half = RET_DK // 2
    inv = ROPE_BASE ** (-jnp.arange(half, dtype=F32) / half)
    ang = pos[:, None] * inv[None, :]
    cos, sin = jnp.cos(ang), jnp.sin(ang)
    cos_t = jnp.tile(jnp.concatenate([cos, cos], axis=1), (1, RET_HEADS))
    sin_t = jnp.tile(jnp.concatenate([-sin, sin], axis=1), (1, RET_HEADS))
    return cos_t, sin_t


def _block_diag(blocks):
    g, r, c = blocks.shape
    eye = jnp.eye(g, dtype=bool)
    return jnp.where(eye[:, None, :, None], blocks[:, :, None, :], 0).reshape(g * r, g * c)


def _layer_weights(l, w_in, s5, s5_b_bar, s5_c_re, s5_c_im, s5_d, s5_glu_w, s5_glu_b, gdn_conv_w, gdn_a_log,
                   gdn_dt_bias, gdn_norm_w, w_out, router_group_w, router_group_b, router_expert_w,
                   router_expert_b):
    wi = w_in[l]
    g0 = RET_IN + S5_IN
    w_ab = jnp.pad(wi[:, g0 + GDN_QKV + GDN_HD:], ((0, 0), (0, LANE - 2 * GDN_HEADS)))
    in_ws = [wi[:, :RET_IN], wi[:, g0:g0 + GDN_QKV], wi[:, RET_IN:g0], wi[:, g0 + GDN_QKV:g0 + GDN_QKV + GDN_HD],
             w_ab]
    in_ws = [w.astype(BF16) for w in in_ws]
    ab_re, ab_im, bb_re, bb_im = s5
    n = S5_LANES
    sl = slice(l * n, (l + 1) * n)
    ab = jnp.concatenate([ab_re[sl].reshape(1, n), ab_im[sl].reshape(1, n)], axis=0)
    to_bd = lambda m: _block_diag(jnp.swapaxes(m[sl].reshape(S5_GROUPS, S5_STATE, S5_GROUP_CH), 1, 2))
    wb = jnp.concatenate([to_bd(bb_re), to_bd(bb_im)], axis=1).astype(BF16)
    wc_re = _block_diag(jnp.swapaxes(s5_c_re[l], 1, 2)).astype(BF16)
    wc_im = _block_diag(jnp.swapaxes(s5_c_im[l], 1, 2)).astype(BF16)
    s5_ws = (ab, wb, wc_re, wc_im, s5_d[l].reshape(1, S5_CH), s5_glu_w[l].astype(BF16),
             s5_glu_b[l].reshape(1, S5_CH))
    pad4 = lambda v: jnp.pad(v.reshape(1, GDN_HEADS), ((0, 0), (0, LANE - GDN_HEADS)))
    gdn_ws = (gdn_conv_w[l], pad4(gdn_a_log[l]), pad4(gdn_dt_bias[l]),
              jnp.tile(gdn_norm_w[l], GDN_HEADS).reshape(1, GDN_HD))
    wo = w_out[l].astype(BF16)
    out_ws = (wo[:RET_VD], wo[RET_VD:RET_VD + S5_CH], wo[RET_VD + S5_CH:])
    wr = jnp.pad(jnp.concatenate([router_expert_w[l], router_group_w[l]], axis=1),
                 ((0, 0), (0, LANE - N_EXPERTS - MOE_GROUPS))).astype(BF16)
    br = jnp.pad(jnp.concatenate([router_expert_b[l], router_group_b[l]]),
                 (0, LANE - N_EXPERTS - MOE_GROUPS)).reshape(1, LANE)
    return in_ws, s5_ws, gdn_ws, out_ws, (wr, br)


def _trunk(x, pos, states, layer_ws, norm_mix, norm_ffn, experts, norm_final):
    bsz, seq, _ = x.shape
    t = bsz * seq
    ret_s, s5_re, s5_im, gdn_s, gdn_buf = states
    cos_t, sin_t = _rope_tables(pos)
    x = x.reshape(t, D_MODEL)
    outs = [[], [], [], [], []]
    for l in range(DEPTH):
        in_ws, s5_ws, gdn_ws, out_ws, (wr, br) = layer_ws[l]
        zr, zq, zs, zg, zab = _inproj(x, norm_mix[l].reshape(1, D_MODEL), in_ws)
        o_ret, ret_fin = _retention(zr, cos_t, sin_t, ret_s[l].reshape(bsz, RET_QD, RET_DV), bsz, seq)
        u_tm = jnp.swapaxes(zs.reshape(bsz, seq, S5_CH), 0, 1)
        o_s5_tm, re_fin, im_fin = _s5(u_tm, s5_re[l].reshape(bsz, S5_LANES), s5_im[l].reshape(bsz, S5_LANES),
                                      *s5_ws)
        o_s5 = jnp.swapaxes(o_s5_tm, 0, 1).reshape(t, S5_CH)
        buf0 = jnp.pad(gdn_buf[l], ((0, 0), (SUBLANE - (GDN_CONV - 1), 0), (0, 0)))
        o_gdn, nbuf, gdn_fin = _gdn(zq, zg, zab, *gdn_ws, buf0, gdn_s[l].reshape(bsz, GDN_HD, GDN_DV), bsz, seq)
        x1, h2_tiles, meta, counts = _outproj(x, o_ret, o_s5, o_gdn, *out_ws, norm_ffn[l].reshape(1, D_MODEL),
                                              wr, br)
        x = _moe(x1, h2_tiles, meta, counts, l, *experts)
        for lst, s in zip(outs, (ret_fin.reshape(bsz, RET_HEADS, RET_DK, RET_DV),
                                 re_fin.reshape(bsz, S5_GROUPS, S5_STATE),
                                 im_fin.reshape(bsz, S5_GROUPS, S5_STATE),
                                 gdn_fin.reshape(bsz, GDN_HEADS, GDN_DK, GDN_DV),
                                 nbuf[:, SUBLANE - (GDN_CONV - 1):, :])):
            lst.append(s)
    y = _final_norm(x, norm_final.reshape(1, D_MODEL)).reshape(bsz, seq, D_MODEL)
    return y, tuple(jnp.stack(o) for o in outs)


def kernel(x_prompt, x_sample, state_ret, state_s5_re, state_s5_im, state_gdn, state_gdn_conv, norm_mix, w_in, s5_a_re, s5_a_im, s5_log_dt, s5_b_re, s5_b_im, s5_c_re, s5_c_im, s5_d, s5_glu_w, s5_glu_b, gdn_conv_w, gdn_a_log, gdn_dt_bias, gdn_norm_w, w_out, norm_ffn, router_group_w, router_group_b, router_expert_w, router_expert_b, expert_w1, expert_w3, expert_w2, norm_final):
    n = DEPTH * S5_LANES
    col = lambda a: a.reshape(n, 1)
    log_dt = jnp.broadcast_to(s5_log_dt[:, :, None], (DEPTH, S5_GROUPS, S5_STATE))
    s5 = _s5_prep(col(s5_a_re), col(s5_a_im), col(log_dt), s5_b_re.reshape(n, S5_GROUP_CH),
                  s5_b_im.reshape(n, S5_GROUP_CH))
    layer_ws = [_layer_weights(l, w_in, s5, None, s5_c_re, s5_c_im, s5_d, s5_glu_w, s5_glu_b, gdn_conv_w,
                               gdn_a_log, gdn_dt_bias, gdn_norm_w, w_out, router_group_w, router_group_b,
                               router_expert_w, router_expert_b) for l in range(DEPTH)]
    experts = (expert_w1, expert_w3, expert_w2)
    bp, lp, _ = x_prompt.shape
    zero_states = (jnp.zeros((DEPTH, bp, RET_HEADS, RET_DK, RET_DV), F32),
                   jnp.zeros((DEPTH, bp, S5_GROUPS, S5_STATE), F32),
                   jnp.zeros((DEPTH, bp, S5_GROUPS, S5_STATE), F32),
                   jnp.zeros((DEPTH, bp, GDN_HEADS, GDN_DK, GDN_DV), F32),
                   jnp.zeros((DEPTH, bp, GDN_CONV - 1, GDN_QKV), F32))
    y_p, p_states = _trunk(x_prompt, jnp.arange(lp, dtype=F32), zero_states, layer_ws, norm_mix, norm_ffn,
                           experts, norm_final)
    y_s, s_states = _trunk(x_sample, PAST_LEN + jnp.arange(x_sample.shape[1], dtype=F32),
                           (state_ret, state_s5_re, state_s5_im, state_gdn, state_gdn_conv), layer_ws, norm_mix,
                           norm_ffn, experts, norm_final)
    return (y_p, y_s) + p_states + s_states
```

```python
import functools
import math

import jax
import jax.numpy as jnp
import numpy as np
from jax import lax
from jax.experimental import pallas as pl
from jax.experimental.pallas import tpu as pltpu

F32 = jnp.float32
BF16 = jnp.bfloat16
I32 = jnp.int32

D_MODEL = 1024
DEPTH = 2
RET_HEADS, RET_DK, RET_DV = 4, 64, 128
S5_GROUPS, S5_GROUP_CH, S5_STATE = 16, 16, 64
S5_CH = S5_GROUPS * S5_GROUP_CH
S5_LANES = S5_GROUPS * S5_STATE
GDN_HEADS, GDN_DK, GDN_DV, GDN_CONV = 4, 64, 64, 4
GDN_HD = GDN_HEADS * GDN_DK
GDN_QKV = 3 * GDN_HD
RET_QD = RET_HEADS * RET_DK
RET_VD = RET_HEADS * RET_DV
RET_IN = 2 * RET_QD + 2 * RET_VD
S5_IN = S5_CH
GDN_IN = GDN_QKV + GDN_HD + 2 * GDN_HEADS
GDN_CHUNK = 64
RET_CHUNK = 64
MOE_GROUPS, EXPERTS_PER_GROUP = 4, 8
N_EXPERTS = MOE_GROUPS * EXPERTS_PER_GROUP
D_EXPERT = 512
ROPE_BASE = 10000.0
EPS = 1e-6
PAST_LEN = 16384

LANE = 128
SUBLANE = 8
ROW_TILE = 512
MOE_BLK = 256
GATHER_TILE = 256
ISSUE_UNROLL = 8
SEQS_PER_STEP = 8
LONG_SEQS_PER_STEP = 2
VMEM_LIMIT = 56 * 1024 * 1024


def _params(*sem):
    return pltpu.CompilerParams(dimension_semantics=sem, vmem_limit_bytes=VMEM_LIMIT)


def _dot(a, b):
    return jnp.dot(a.astype(BF16), b.astype(BF16), preferred_element_type=F32)


def _dot_nt(a, b):
    return lax.dot_general(a.astype(BF16), b.astype(BF16), (((1,), (1,)), ((), ())),
                           preferred_element_type=F32)


def _dot_tn(a, b):
    return lax.dot_general(a.astype(BF16), b.astype(BF16), (((0,), (0,)), ((), ())),
                           preferred_element_type=F32)


def _dot_hi(a, b):
    a1 = a.astype(BF16)
    a2 = (a - a1.astype(F32)).astype(BF16)
    b1 = b.astype(BF16)
    b2 = (b - b1.astype(F32)).astype(BF16)
    d = lambda x, y: jnp.dot(x, y, preferred_element_type=F32)
    return d(a1, b1) + (d(a1, b2) + d(a2, b1))


def _split3(x):
    p1 = x.astype(BF16)
    r1 = x - p1.astype(F32)
    p2 = r1.astype(BF16)
    p3 = (r1 - p2.astype(F32)).astype(BF16)
    return p1, p2, p3


def _dot_sel_l(sel, x):
    p1, p2, p3 = _split3(x)
    d = lambda p: jnp.dot(sel, p, preferred_element_type=F32)
    return d(p1) + d(p2) + d(p3)


def _dot_sel_r(x, sel):
    p1, p2, p3 = _split3(x)
    d = lambda p: jnp.dot(p, sel, preferred_element_type=F32)
    return d(p1) + d(p2) + d(p3)


def _rms(x):
    return x * lax.rsqrt(jnp.mean(x * x, axis=-1, keepdims=True) + EPS)


def _silu(x):
    return x * jax.nn.sigmoid(x)


def _inproj_body(x_ref, nw_ref, wr_ref, wq_ref, ws_ref, wg_ref, wab_ref,
                 zr_ref, zq_ref, zs_ref, zg_ref, zab_ref):
    h = (_rms(x_ref[...]) * nw_ref[...]).astype(BF16)
    for w_ref, z_ref in ((wr_ref, zr_ref), (wq_ref, zq_ref), (ws_ref, zs_ref),
                         (wg_ref, zg_ref), (wab_ref, zab_ref)):
        z_ref[...] = jnp.dot(h, w_ref[...], preferred_element_type=F32)


def _inproj(x, nw, ws):
    t = x.shape[0]
    tm = min(ROW_TILE, t)
    widths = [w.shape[1] for w in ws]
    row = lambda n: pl.BlockSpec((tm, n), lambda i: (i, 0))
    full = lambda a: pl.BlockSpec(a.shape, lambda i: (0, 0))
    return pl.pallas_call(
        _inproj_body,
        grid=(t // tm,),
        in_specs=[row(D_MODEL), full(nw)] + [full(w) for w in ws],
        out_specs=[row(n) for n in widths],
        out_shape=[jax.ShapeDtypeStruct((t, n), F32) for n in widths],
        compiler_params=_params("parallel"),
        name="inproj",
    )(x, nw, *ws)


def _seqs_per_step(bsz, blocks_per_seq):
    want = SEQS_PER_STEP if blocks_per_seq == 1 else LONG_SEQS_PER_STEP
    return want if bsz % want == 0 else 1


def _ret_log_gamma(h):
    return math.log(1.0 - 2.0 ** (-5.0 - h))


def _ret_body(zr_ref, cos_ref, sin_ref, s0_ref, o_ref, sfin_ref, s_scr, *, nseq, rows, chunk):
    j = pl.program_id(1)
    stack = RET_HEADS * chunk

    @pl.when(j == 0)
    def _():
        s_scr[...] = s0_ref[...]

    lane = lax.broadcasted_iota(I32, (rows, RET_QD), 1)
    first_half = (lane % RET_DK) < (RET_DK // 2)

    def rotary(x):
        swapped = jnp.where(first_half, pltpu.roll(x, RET_QD - RET_DK // 2, 1),
                            pltpu.roll(x, RET_DK // 2, 1))
        return x * cos_ref[...] + swapped * sin_ref[...]

    q = [rotary(zr_ref[s, :, 0:RET_QD]) for s in range(nseq)]
    k = [rotary(zr_ref[s, :, RET_QD:2 * RET_QD]) * (RET_DK ** -0.5) for s in range(nseq)]

    def by_head(idx, fn):
        out = jnp.zeros(idx.shape, F32)
        for h in range(RET_HEADS):
            out = jnp.where(idx == h, fn(h), out)
        return out

    lg_lane = by_head(lax.broadcasted_iota(I32, (chunk, RET_QD), 1) // RET_DK, _ret_log_gamma)
    pos = lax.broadcasted_iota(I32, (chunk, RET_QD), 0).astype(F32)
    q_scale = jnp.exp((pos + 1.0) * lg_lane)
    k_scale = jnp.exp((chunk - 1.0 - pos) * lg_lane)
    st_row = lax.broadcasted_iota(I32, (stack, stack), 0)
    st_col = lax.broadcasted_iota(I32, (stack, stack), 1)
    causal = ((st_row // chunk) == (st_col // chunk)) & (st_row >= st_col)
    lg_stack = by_head(st_row // chunk, _ret_log_gamma)
    decay = jnp.where(causal, jnp.exp(jnp.where(causal, (st_row - st_col).astype(F32), 0.0) * lg_stack), 0.0)
    q_rows = (lax.broadcasted_iota(I32, (stack, RET_QD), 0) // chunk) == \
             (lax.broadcasted_iota(I32, (stack, RET_QD), 1) // RET_DK)
    v_rows = (lax.broadcasted_iota(I32, (stack, RET_VD), 0) // chunk) == \
             (lax.broadcasted_iota(I32, (stack, RET_VD), 1) // RET_DV)
    s_row_head = lax.broadcasted_iota(I32, (RET_QD, RET_DV), 0) // RET_DK
    s_decay = by_head(s_row_head, lambda h: math.exp(chunk * _ret_log_gamma(h)))
    kv_diag = (lax.broadcasted_iota(I32, (RET_QD, RET_VD), 0) // RET_DK) == \
              (lax.broadcasted_iota(I32, (RET_QD, RET_VD), 1) // RET_DV)

    def tile4(a):
        return jnp.concatenate([a] * RET_HEADS, axis=0)

    def collapse(a):
        out = a[0:chunk]
        for h in range(1, RET_HEADS):
            out = out + a[h * chunk:(h + 1) * chunk]
        return out

    seqs = range(nseq)
    n_chunks = rows // chunk
    state = [s_scr[s] for s in seqs]
    for ci in range(n_chunks):
        r = slice(ci * chunk, (ci + 1) * chunk)
        v_cs = [zr_ref[s, r, 2 * RET_QD:2 * RET_QD + RET_VD] for s in seqs]
        scores = [_dot_nt(jnp.where(q_rows, tile4(q[s][r]), 0.0), tile4(k[s][r])) * decay for s in seqs]
        o_intra = [collapse(_dot(scores[s], jnp.where(v_rows, tile4(v_cs[s]), 0.0))) for s in seqs]
        s_full = [jnp.concatenate([jnp.where(s_row_head == h, state[s], 0.0) for h in range(RET_HEADS)], axis=1)
                  for s in seqs]
        o_cs = [o_intra[s] + _dot(q[s][r] * q_scale, s_full[s]) for s in seqs]
        for s in seqs:
            gate = zr_ref[s, r, 2 * RET_QD + RET_VD:2 * RET_QD + 2 * RET_VD]
            for h in range(RET_HEADS):
                sl = slice(h * RET_DV, (h + 1) * RET_DV)
                o_ref[s, r, sl] = _rms(o_cs[s][:, sl]) * _silu(gate[:, sl])
        for s in seqs:
            kv = jnp.where(kv_diag, _dot_tn(k[s][r] * k_scale, v_cs[s]), 0.0)
            kv_own = kv[:, 0:RET_DV]
            for h in range(1, RET_HEADS):
                kv_own = kv_own + kv[:, h * RET_DV:(h + 1) * RET_DV]
            state[s] = state[s] * s_decay + kv_own
    for s in seqs:
        s_scr[s] = state[s]

    @pl.when(j == pl.num_programs(1) - 1)
    def _():
        for s in seqs:
            sfin_ref[s] = state[s]


def _retention(zr, cos, sin, s0, bsz, seq):
    chunk = min(RET_CHUNK, seq)
    rows = min(4 * chunk, seq)
    nj = seq // rows
    nseq = _seqs_per_step(bsz, nj)
    o, s_fin = pl.pallas_call(
        functools.partial(_ret_body, nseq=nseq, rows=rows, chunk=chunk),
        grid=(bsz // nseq, nj),
        in_specs=[pl.BlockSpec((nseq, rows, RET_IN), lambda b, j: (b, j, 0)),
                  pl.BlockSpec((rows, RET_QD), lambda b, j: (j, 0)),
                  pl.BlockSpec((rows, RET_QD), lambda b, j: (j, 0)),
                  pl.BlockSpec((nseq, RET_QD, RET_DV), lambda b, j: (b, 0, 0))],
        out_specs=[pl.BlockSpec((nseq, rows, RET_VD), lambda b, j: (b, j, 0)),
                   pl.BlockSpec((nseq, RET_QD, RET_DV), lambda b, j: (b, 0, 0))],
        out_shape=[jax.ShapeDtypeStruct((bsz, seq, RET_VD), F32),
                   jax.ShapeDtypeStruct((bsz, RET_QD, RET_DV), F32)],
        scratch_shapes=[pltpu.VMEM((nseq, RET_QD, RET_DV), F32)],
        compiler_params=_params("parallel", "arbitrary"),
        name="retention",
    )(zr.reshape(bsz, seq, RET_IN), cos, sin, s0)
    return o.reshape(bsz * seq, RET_VD), s_fin


def _s5_prep_body(are_ref, aim_ref, ldt_ref, bre_ref, bim_ref, abre_ref, abim_ref, bbre_ref, bbim_ref):
    lam_re, lam_im = are_ref[...], aim_ref[...]
    dt = jnp.exp(ldt_ref[...])
    mag = jnp.exp(lam_re * dt)
    ab_re = mag * jnp.cos(lam_im * dt)
    ab_im = mag * jnp.sin(lam_im * dt)
    den = lam_re * lam_re + lam_im * lam_im
    f_re = ((ab_re - 1.0) * lam_re + ab_im * lam_im) / den
    f_im = (ab_im * lam_re - (ab_re - 1.0) * lam_im) / den
    abre_ref[...] = ab_re
    abim_ref[...] = ab_im
    bbre_ref[...] = f_re * bre_ref[...] - f_im * bim_ref[...]
    bbim_ref[...] = f_re * bim_ref[...] + f_im * bre_ref[...]


def _s5_prep(a_re, a_im, log_dt, b_re, b_im):
    n = a_re.shape[0]
    col = jax.ShapeDtypeStruct((n, 1), F32)
    mat = jax.ShapeDtypeStruct((n, S5_GROUP_CH), F32)
    return pl.pallas_call(_s5_prep_body, out_shape=[col, col, mat, mat], name="s5_prep")(
        a_re, a_im, log_dt, b_re, b_im)


def _gelu_tanh(x):
    return x * (0.5 * (1.0 + jnp.tanh(math.sqrt(2.0 / math.pi) * (x + 0.044715 * (x * x * x)))))


def _s5_body(u_ref, h0re_ref, h0im_ref, ab_ref, wb_ref, wcre_ref, wcim_ref, d_ref, gw_ref, gb_ref,
             o_ref, hre_ref, him_ref, bu_scr, st_scr, *, steps):
    j = pl.program_id(1)

    @pl.when(j == 0)
    def _():
        st_scr[0] = h0re_ref[...]
        st_scr[1] = h0im_ref[...]

    rows = steps * SUBLANE
    u = u_ref[...].reshape(rows, S5_CH)
    bu_scr[...] = _dot(u, wb_ref[...])
    a_re = jnp.broadcast_to(ab_ref[0:1, :], (SUBLANE, S5_LANES))
    a_im = jnp.broadcast_to(ab_ref[1:2, :], (SUBLANE, S5_LANES))

    def step(t, carry):
        h_re, h_im = carry
        r = pl.ds(pl.multiple_of(t * SUBLANE, SUBLANE), SUBLANE)
        n_re = a_re * h_re - a_im * h_im + bu_scr[r, 0:S5_LANES]
        n_im = a_re * h_im + a_im * h_re + bu_scr[r, S5_LANES:2 * S5_LANES]
        bu_scr[r, 0:S5_LANES] = n_re
        bu_scr[r, S5_LANES:2 * S5_LANES] = n_im
        return n_re, n_im

    h_re, h_im = lax.fori_loop(0, steps, step, (st_scr[0], st_scr[1]))
    st_scr[0] = h_re
    st_scr[1] = h_im
    hre_ref[...] = h_re
    him_ref[...] = h_im

    y = _dot(bu_scr[:, 0:S5_LANES], wcre_ref[...]) - _dot(bu_scr[:, S5_LANES:2 * S5_LANES], wcim_ref[...])
    y = _gelu_tanh(y + d_ref[...] * u)
    y = y * jax.nn.sigmoid(_dot(y, gw_ref[...]) + gb_ref[...])
    o_ref[...] = y.reshape(steps, SUBLANE, S5_CH)


def _s5(u_tm, h0_re, h0_im, ab, wb, wc_re, wc_im, d_skip, glu_w, glu_b):
    seq, bsz, _ = u_tm.shape
    steps = min(64, seq)
    full = lambda a: pl.BlockSpec(a.shape, lambda g, j: (0,) * a.ndim)
    st = pl.BlockSpec((SUBLANE, S5_LANES), lambda g, j: (g, 0))
    return pl.pallas_call(
        functools.partial(_s5_body, steps=steps),
        grid=(bsz // SUBLANE, seq // steps),
        in_specs=[pl.BlockSpec((steps, SUBLANE, S5_CH), lambda g, j: (j, g, 0)), st, st,
                  full(ab), full(wb), full(wc_re), full(wc_im), full(d_skip), full(glu_w), full(glu_b)],
        out_specs=[pl.BlockSpec((steps, SUBLANE, S5_CH), lambda g, j: (j, g, 0)), st, st],
        out_shape=[jax.ShapeDtypeStruct((seq, bsz, S5_CH), F32),
                   jax.ShapeDtypeStruct((bsz, S5_LANES), F32),
                   jax.ShapeDtypeStruct((bsz, S5_LANES), F32)],
        scratch_shapes=[pltpu.VMEM((steps * SUBLANE, 2 * S5_LANES), F32),
                        pltpu.VMEM((2, SUBLANE, S5_LANES), F32)],
        compiler_params=_params("parallel", "arbitrary"),
        name="s5",
    )(u_tm, h0_re, h0_im, ab, wb, wc_re, wc_im, d_skip, glu_w, glu_b)


def _gdn_body(zq_ref, zg_ref, zab_ref, cw_ref, alog_ref, dtb_ref, nw_ref, buf0_ref, s0_ref,
              ones_bd_ref, tri_ref, ea_ref, eb_ref, ec_ref, spread_ref, gather_ref,
              o_ref, nbuf_ref, sfin_ref,
              ext_scr, q_scr, k_scr, kb_scr, vb_scr, g_scr, g4_scr, s_scr, *, nseq, rows, chunk):
    j = pl.program_id(1)
    stack = GDN_HEADS * chunk
    bd_state = (lax.broadcasted_iota(I32, (GDN_HD, GDN_HD), 0) // GDN_DK) == \
               (lax.broadcasted_iota(I32, (GDN_HD, GDN_HD), 1) // GDN_DV)

    @pl.when(j == 0)
    def _():
        for s in range(nseq):
            ext_scr[s, 0:SUBLANE, :] = buf0_ref[s]
            s_scr[s] = jnp.where(bd_state, _dot_sel_r(s0_ref[s], spread_ref[...]), 0.0)

    convs = []
    for s in range(nseq):
        ext_scr[s, SUBLANE:SUBLANE + rows, :] = zq_ref[s]
        conv = ext_scr[s, SUBLANE - 3:SUBLANE - 3 + rows, :] * cw_ref[0:1, :]
        for i in range(1, GDN_CONV):
            conv = conv + ext_scr[s, SUBLANE - 3 + i:SUBLANE - 3 + i + rows, :] * cw_ref[i:i + 1, :]
        tail = ext_scr[s, rows:rows + SUBLANE, :]
        nbuf_ref[s] = tail
        ext_scr[s, 0:SUBLANE, :] = tail
        convs.append(conv)
    qkv = _silu(convs[0] if nseq == 1 else jnp.concatenate(convs, axis=0))

    ones_bd = ones_bd_ref[...]
    q_raw = qkv[:, 0:GDN_HD]
    k_raw = qkv[:, GDN_HD:2 * GDN_HD]
    q_scr[...] = q_raw * lax.rsqrt(_dot_sel_r(q_raw * q_raw, ones_bd) + EPS) * (GDN_DK ** -0.5)
    k_n = k_raw * lax.rsqrt(_dot_sel_r(k_raw * k_raw, ones_bd) + EPS)
    k_scr[...] = k_n

    ab = zab_ref[...].reshape(nseq * rows, LANE)
    x = ab + dtb_ref[...]
    softplus = jnp.maximum(x, 0.0) + jnp.log1p(jnp.exp(-jnp.abs(x)))
    g_pad = -jnp.exp(alog_ref[...]) * softplus
    beta = _dot_sel_r(jax.nn.sigmoid(ab), eb_ref[...])
    g_cum = _dot_sel_l(tri_ref[...], g_pad)
    g_scr[...] = _dot_sel_r(g_cum, ea_ref[...])
    g4_scr[...] = _dot_sel_r(g_pad, ec_ref[...])
    kb_scr[...] = k_n * beta
    vb_scr[...] = qkv[:, 2 * GDN_HD:3 * GDN_HD] * beta

    w_row = lax.broadcasted_iota(I32, (chunk, stack), 0)
    w_col = lax.broadcasted_iota(I32, (chunk, stack), 1) % chunk
    strict_w = w_row > w_col
    causal_w = w_row >= w_col
    same_head = (lax.broadcasted_iota(I32, (stack, stack), 0) // chunk) == \
                (lax.broadcasted_iota(I32, (stack, stack), 1) // chunk)
    head_rows = (lax.broadcasted_iota(I32, (stack, GDN_HD), 0) // chunk) == \
                (lax.broadcasted_iota(I32, (stack, GDN_HD), 1) // GDN_DK)
    head_rows2 = jnp.concatenate([head_rows, head_rows], axis=1)
    tri_c = tri_ref[0:chunk, 0:chunk]
    packed_ok = chunk % (2 * SUBLANE) == 0

    def tile4(a):
        return jnp.concatenate([a] * GDN_HEADS, axis=0)

    def split2(a):
        hi = a.astype(BF16)
        return hi, (a - hi.astype(F32)).astype(BF16)

    def stack_masked(parts, mask):
        if packed_ok:
            return [jnp.where(mask, tile4(p), jnp.zeros((), BF16)) for p in parts]
        return [jnp.where(mask, tile4(p.astype(F32)), 0.0).astype(BF16) for p in parts]

    def on_diag(parts):
        return stack_masked(parts, same_head)

    def mm(a, b):
        return jnp.dot(a, b, preferred_element_type=F32)

    def mm_hi(a_parts, b_parts):
        return mm(a_parts[0], b_parts[0]) + (mm(a_parts[0], b_parts[1]) + mm(a_parts[1], b_parts[0]))

    n_chunks = rows // chunk
    every = range(nseq * n_chunks)
    rows_of = lambda u: slice(u * chunk, (u + 1) * chunk)
    solved = []
    for u in every:
        r = rows_of(u)
        q_c, k_c, kb_c, vb_c, g_c = q_scr[r, :], k_scr[r, :], kb_scr[r, :], vb_scr[r, :], g_scr[r, :]
        exp_g = jnp.exp(g_c)
        g_diff = _dot_sel_l(tri_c, jnp.where(strict_w, g4_scr[r, :], 0.0))
        decay = jnp.exp(jnp.where(causal_w, g_diff, 0.0))
        k_heads = jnp.where(head_rows, tile4(k_c), 0.0)
        lmat = jnp.where(strict_w, _dot_nt(kb_c, k_heads) * decay, 0.0)
        attn = jnp.where(causal_w, _dot_nt(q_c, k_heads) * decay, 0.0)
        rhs_c = jnp.concatenate([vb_c, kb_c * exp_g], axis=1)
        solved.append((lmat, attn, exp_g, rhs_c))

    t_acc = [-solved[ci][0] for ci in every]
    p_parts = [split2(solved[ci][0]) for ci in every]
    p_diag = [on_diag(p) for p in p_parts]
    span = 2
    while span < chunk + 1:
        power = [mm_hi(p_parts[ci], p_diag[ci]) for ci in every]
        p_parts = [split2(p) for p in power]
        p_diag = [on_diag(p) for p in p_parts]
        t_acc = [t_acc[ci] + power[ci] + mm_hi(split2(t_acc[ci]), p_diag[ci]) for ci in every]
        span *= 2
    rhs_parts = [stack_masked(split2(solved[ci][3]), head_rows2) for ci in every]
    sols = [solved[ci][3] + mm_hi(split2(t_acc[ci]), rhs_parts[ci]) for ci in every]

    seqs = range(nseq)
    s_cur = [s_scr[s] for s in seqs]
    for ci in range(n_chunks):
        us = [s * n_chunks + ci for s in seqs]
        v_new = [sols[u][:, 0:GDN_HD] - _dot(sols[u][:, GDN_HD:2 * GDN_HD], s_cur[s]) for s, u in enumerate(us)]
        o_cs = [_dot(q_scr[rows_of(u), :] * solved[u][2], s_cur[s])
                + _dot(solved[u][1], jnp.where(head_rows, tile4(v_new[s]), 0.0)) for s, u in enumerate(us)]
        for s, u in enumerate(us):
            g_c = g_scr[rows_of(u), :]
            g_last = g_c[chunk - 1:chunk, :]
            k_dec = k_scr[rows_of(u), :] * jnp.exp(g_last - g_c)
            s_cur[s] = s_cur[s] * jnp.exp(g_last) + jnp.where(bd_state, _dot_tn(k_dec, v_new[s]), 0.0)
        for s, u in enumerate(us):
            r = slice(ci * chunk, (ci + 1) * chunk)
            ms = _dot_sel_r(o_cs[s] * o_cs[s], ones_bd) * (1.0 / GDN_DV)
            o_ref[s, r, :] = o_cs[s] * lax.rsqrt(ms + EPS) * nw_ref[...] * _silu(zg_ref[s, r, :])
    for s in seqs:
        s_scr[s] = s_cur[s]

    @pl.when(j == pl.num_programs(1) - 1)
    def _():
        for s in seqs:
            sfin_ref[s] = _dot_sel_r(s_scr[s], gather_ref[...])


def _gdn_consts(rows, chunk):
    stack = GDN_HEADS * chunk
    blk = lambda n, c: (np.arange(n)[:, None] // c) == (np.arange(n)[None, :] // c)
    ones_bd = blk(GDN_HD, GDN_DK)
    tri = blk(rows, chunk) & (np.arange(rows)[:, None] >= np.arange(rows)[None, :])
    src = np.arange(LANE)[:, None]
    ea = src == np.arange(GDN_HD)[None, :] // GDN_DK
    eb = src == GDN_HEADS + np.arange(GDN_HD)[None, :] // GDN_DK
    ec = src == np.arange(stack)[None, :] // chunk
    spread = np.arange(GDN_DV)[:, None] == np.arange(GDN_HD)[None, :] % GDN_DV
    return tuple(jnp.asarray(m, dtype=BF16) for m in (ones_bd, tri, ea, eb, ec, spread, spread.T))


def _gdn(zq, zg, zab, conv_w, alog_pad, dtb_pad, nw, buf0, s0, bsz, seq):
    chunk = min(GDN_CHUNK, seq)
    rows = min(4 * chunk, seq)
    nj = seq // rows
    nseq = _seqs_per_step(bsz, nj)
    stack = GDN_HEADS * chunk
    consts = _gdn_consts(nseq * rows, chunk)
    row = lambda n: pl.BlockSpec((nseq, rows, n), lambda b, j: (b, j, 0))
    full = lambda a: pl.BlockSpec(a.shape, lambda b, j: (0,) * a.ndim)
    per_b = lambda *s: pl.BlockSpec((nseq,) + s, lambda b, j: (b, 0, 0))
    by_seq = lambda a: a.reshape(bsz, seq, a.shape[-1])
    o, nbuf, s_fin = pl.pallas_call(
        functools.partial(_gdn_body, nseq=nseq, rows=rows, chunk=chunk),
        grid=(bsz // nseq, nj),
        in_specs=[row(GDN_QKV), row(GDN_HD), row(LANE), full(conv_w), full(alog_pad), full(dtb_pad), full(nw),
                  per_b(SUBLANE, GDN_QKV), per_b(GDN_HD, GDN_DV)] + [full(c) for c in consts],
        out_specs=[row(GDN_HD), per_b(SUBLANE, GDN_QKV), per_b(GDN_HD, GDN_DV)],
        out_shape=[jax.ShapeDtypeStruct((bsz, seq, GDN_HD), F32),
                   jax.ShapeDtypeStruct((bsz, SUBLANE, GDN_QKV), F32),
                   jax.ShapeDtypeStruct((bsz, GDN_HD, GDN_DV), F32)],
        scratch_shapes=[pltpu.VMEM((nseq, rows + SUBLANE, GDN_QKV), F32)] +
                       [pltpu.VMEM((nseq * rows, GDN_HD), F32)] * 5 +
                       [pltpu.VMEM((nseq * rows, stack), F32), pltpu.VMEM((nseq, GDN_HD, GDN_HD), F32)],
        compiler_params=_params("parallel", "arbitrary"),
        name="gdn",
    )(by_seq(zq), by_seq(zg), by_seq(zab), conv_w, alog_pad, dtb_pad, nw, buf0, s0, *consts)
    return o.reshape(bsz * seq, GDN_HD), nbuf, s_fin


def _outproj_body(x_ref, oret_ref, os5_ref, ogdn_ref, w1_ref, w2_ref, w3_ref, nw_ref, wr_ref, br_ref, tri_ref,
                  x1_ref, h2_ref, meta_ref, cnt_ref, carry_scr, *, tm):
    i = pl.program_id(0)

    @pl.when(i == 0)
    def _():
        carry_scr[...] = jnp.zeros_like(carry_scr)

    mix = _dot(oret_ref[...], w1_ref[...]) + _dot(os5_ref[...], w2_ref[...]) + _dot(ogdn_ref[...], w3_ref[...])
    x1 = x_ref[...] + mix
    x1_ref[...] = x1
    h2 = _rms(x1) * nw_ref[...]
    for s in range(D_MODEL // LANE):
        h2_ref[pl.ds(s, tm, stride=SUBLANE), :] = h2[:, s * LANE:(s + 1) * LANE]

    logits = _dot(h2, wr_ref[...]) + br_ref[...]
    lane_i = lax.broadcasted_iota(I32, (tm, LANE), 1)
    lane = lane_i.astype(F32)
    neg = -jnp.inf
    big = float(LANE)
    g_log = jnp.where((lane_i >= N_EXPERTS) & (lane_i < N_EXPERTS + MOE_GROUPS), logits, neg)
    g_max = jnp.max(g_log, axis=-1, keepdims=True)
    grp = jnp.min(jnp.where(g_log == g_max, lane - N_EXPERTS, big), axis=-1, keepdims=True)
    p_grp = 1.0 / jnp.sum(jnp.exp(g_log - g_max), axis=-1, keepdims=True)
    in_grp = (lane >= grp * EXPERTS_PER_GROUP) & (lane < (grp + 1.0) * EXPERTS_PER_GROUP)
    e_log = jnp.where(in_grp, logits, neg)
    v1 = jnp.max(e_log, axis=-1, keepdims=True)
    i1 = jnp.min(jnp.where(e_log == v1, lane, big), axis=-1, keepdims=True)
    e_log2 = jnp.where(lane == i1, neg, e_log)
    v2 = jnp.max(e_log2, axis=-1, keepdims=True)
    i2 = jnp.min(jnp.where(e_log2 == v2, lane, big), axis=-1, keepdims=True)
    e2 = jnp.exp(v2 - v1)
    gate1 = p_grp / (1.0 + e2)
    gate2 = p_grp * e2 / (1.0 + e2)

    oh1 = lane == i1
    oh2 = lane == i2
    picked = jnp.where(oh1 | oh2, 1.0, 0.0)
    before = jnp.dot(tri_ref[...], picked.astype(BF16), preferred_element_type=F32) + carry_scr[0:1, :]
    rank1 = jnp.sum(jnp.where(oh1, before, 0.0), axis=-1, keepdims=True)
    rank2 = jnp.sum(jnp.where(oh2, before, 0.0), axis=-1, keepdims=True)
    total = carry_scr[0:1, :] + jnp.sum(picked, axis=0, keepdims=True)
    carry_scr[...] = jnp.broadcast_to(total, carry_scr.shape)
    cnt_ref[...] = jnp.broadcast_to(total, cnt_ref.shape)

    meta = jnp.where(lane_i == 0, i1, 0.0)
    meta = jnp.where(lane_i == 1, i2, meta)
    meta = jnp.where(lane_i == 2, gate1, meta)
    meta = jnp.where(lane_i == 3, gate2, meta)
    meta = jnp.where(lane_i == 4, rank1, meta)
    meta = jnp.where(lane_i == 5, rank2, meta)
    meta_ref[...] = meta


def _outproj(x, o_ret, o_s5, o_gdn, w1, w2, w3, nw, wr, br):
    t = x.shape[0]
    tm = min(ROW_TILE, t)
    tri = jnp.asarray(np.arange(tm)[:, None] > np.arange(tm)[None, :], dtype=BF16)
    row = lambda n: pl.BlockSpec((tm, n), lambda i: (i, 0))
    full = lambda a: pl.BlockSpec(a.shape, lambda i: (0, 0))
    return pl.pallas_call(
        functools.partial(_outproj_body, tm=tm),
        grid=(t // tm,),
        in_specs=[row(D_MODEL), row(RET_VD), row(S5_CH), row(GDN_HD),
                  full(w1), full(w2), full(w3), full(nw), full(wr), full(br), full(tri)],
        out_specs=[row(D_MODEL), pl.BlockSpec((tm * SUBLANE, LANE), lambda i: (i, 0)), row(LANE),
                   pl.BlockSpec((SUBLANE, LANE), lambda i: (0, 0))],
        out_shape=[jax.ShapeDtypeStruct((t, D_MODEL), F32),
                   jax.ShapeDtypeStruct((t * SUBLANE, LANE), F32),
                   jax.ShapeDtypeStruct((t, LANE), F32),
                   jax.ShapeDtypeStruct((SUBLANE, LANE), F32)],
        scratch_shapes=[pltpu.VMEM((SUBLANE, LANE), F32)],
        compiler_params=_params("arbitrary"),
        name="outproj_router",
    )(x, o_ret, o_s5, o_gdn, w1, w2, w3, nw, wr, br, tri)


def _token_rows(ref, idx):
    return ref.at[pl.ds(pl.multiple_of(idx * SUBLANE, SUBLANE), SUBLANE)]


def _slot(e_ref, r_ref, ps_ref, g):
    return ps_ref[e_ref[g]] + r_ref[g]


ZERO_ROWS = 128


def _zero_segment(zero_scr, xb_ref, sem, start, length, wait):
    def piece(off, n):
        cp = pltpu.make_async_copy(zero_scr.at[pl.ds(0, n * SUBLANE)],
                                   xb_ref.at[pl.ds(pl.multiple_of(off * SUBLANE, SUBLANE), n * SUBLANE)], sem)
        cp.wait() if wait else cp.start()

    n_big = length // ZERO_ROWS

    def big(i, carry):
        piece(start + i * ZERO_ROWS, ZERO_ROWS)
        return carry

    lax.fori_loop(0, n_big, big, 0)
    off = start + n_big * ZERO_ROWS
    rem = length - n_big * ZERO_ROWS
    bit = ZERO_ROWS // 2
    while bit >= 1:
        has = (rem & bit) != 0
        pl.when(has)(functools.partial(piece, off, bit))
        off = off + jnp.where(has, bit, 0)
        bit //= 2


def _scatter_body(e1_ref, e2_ref, r1_ref, r2_ref, ps_ref, zs_ref, zl_ref, src_ref, xb_ref, zero_scr, sem, zsem,
                  *, tm):
    base = pl.program_id(0) * tm

    @pl.when(pl.program_id(0) == 0)
    def _():
        zero_scr[...] = jnp.zeros_like(zero_scr)
        for wait in (False, True):
            lax.fori_loop(0, N_EXPERTS + 1,
                          lambda s, c, wait=wait: (_zero_segment(zero_scr, xb_ref, zsem, zs_ref[s], zl_ref[s], wait),
                                                   c)[1], 0)

    def copy(t, d):
        return pltpu.make_async_copy(_token_rows(src_ref, t), _token_rows(xb_ref, d), sem)

    def issue(t, carry):
        copy(t, _slot(e1_ref, r1_ref, ps_ref, base + t)).start(priority=0)
        copy(t, _slot(e2_ref, r2_ref, ps_ref, base + t)).start(priority=1)
        return carry

    lax.fori_loop(0, tm, issue, 0, unroll=ISSUE_UNROLL)

    for _ in range(2):
        pltpu.make_async_copy(src_ref, src_ref, sem).wait()


def _scatter(route, zero_start, zero_len, h2_tiles, n_rows):
    tokens = route[0].shape[0]
    tm = min(GATHER_TILE, tokens)
    assert tokens % tm == 0
    return pl.pallas_call(
        functools.partial(_scatter_body, tm=tm),
        grid_spec=pltpu.PrefetchScalarGridSpec(
            num_scalar_prefetch=7, grid=(tokens // tm,),
            in_specs=[pl.BlockSpec((tm * SUBLANE, LANE), lambda i, *_: (i, 0))],
            out_specs=pl.BlockSpec(memory_space=pl.ANY),
            scratch_shapes=[pltpu.VMEM((ZERO_ROWS * SUBLANE, LANE), F32), pltpu.SemaphoreType.DMA(()),
                            pltpu.SemaphoreType.DMA(())]),
        out_shape=jax.ShapeDtypeStruct((n_rows * SUBLANE, LANE), F32),
        compiler_params=_params("arbitrary"),
        name="moe_scatter",
    )(*route, zero_start, zero_len, h2_tiles)


def _experts_body(be_ref, nb_ref, xb_ref, w1_ref, w3_ref, w2_ref, yb_ref, w1_scr, w3_scr, w2_scr, *, blk):
    i = pl.program_id(0)
    live = i < nb_ref[0]

    @pl.when(live & ((i == 0) | (be_ref[i] != be_ref[jnp.maximum(i - 1, 0)])))
    def _():
        w1_scr[...] = w1_ref[...].astype(BF16)
        w3_scr[...] = w3_ref[...].astype(BF16)
        w2_scr[...] = w2_ref[...].astype(BF16)

    @pl.when(live)
    def _():
        x = jnp.concatenate([xb_ref[pl.ds(s, blk, stride=SUBLANE), :] for s in range(D_MODEL // LANE)],
                            axis=1).astype(BF16)
        hid = _silu(jnp.dot(x, w1_scr[...], preferred_element_type=F32)) * \
            jnp.dot(x, w3_scr[...], preferred_element_type=F32)
        y = _dot(hid, w2_scr[...])
        for s in range(D_MODEL // LANE):
            yb_ref[pl.ds(s, blk, stride=SUBLANE), :] = y[:, s * LANE:(s + 1) * LANE]

    @pl.when(jnp.logical_not(live))
    def _():
        yb_ref[...] = jnp.zeros_like(yb_ref)


def _experts(block_e, nb_used, xb, layer, w1, w3, w2, n_blocks):
    blk = MOE_BLK
    live = lambda i, nb: jnp.minimum(i, nb[0] - 1)
    tile_in = pl.BlockSpec((blk * SUBLANE, LANE), lambda i, be, nb: (live(i, nb), 0))
    tile_out = pl.BlockSpec((blk * SUBLANE, LANE), lambda i, be, nb: (i, 0))
    wspec = lambda a: pl.BlockSpec((None, None) + a.shape[2:],
                                   lambda i, be, nb: (layer, be[live(i, nb)], 0, 0))
    return pl.pallas_call(
        functools.partial(_experts_body, blk=blk),
        grid_spec=pltpu.PrefetchScalarGridSpec(
            num_scalar_prefetch=2, grid=(n_blocks,),
            in_specs=[tile_in, wspec(w1), wspec(w3), wspec(w2)], out_specs=tile_out,
            scratch_shapes=[pltpu.VMEM(w1.shape[2:], BF16), pltpu.VMEM(w3.shape[2:], BF16),
                            pltpu.VMEM(w2.shape[2:], BF16)]),
        out_shape=jax.ShapeDtypeStruct(xb.shape, F32),
        compiler_params=_params("arbitrary"),
        name="moe_experts",
    )(block_e, nb_used, xb, w1, w3, w2)


def _combine_body(e1_ref, e2_ref, r1_ref, r2_ref, ps_ref, x1_ref, meta_ref, yb_ref, nw_ref, out_ref, buf_ref, sem,
                  *, tm, final_norm):
    i = pl.program_id(0)
    phase = i % 2

    def gather(tile, ph):
        base = tile * tm

        def issue(t, carry):
            for k, (e_ref, r_ref) in enumerate(((e1_ref, r1_ref), (e2_ref, r2_ref))):
                pltpu.make_async_copy(_token_rows(yb_ref, _slot(e_ref, r_ref, ps_ref, base + t)),
                                      _token_rows(buf_ref.at[ph, k], t), sem.at[ph]).start(priority=k)
            return carry

        lax.fori_loop(0, tm, issue, 0, unroll=ISSUE_UNROLL)

    @pl.when(i == 0)
    def _():
        gather(0, 0)

    @pl.when(i + 1 < pl.num_programs(0))
    def _():
        gather(i + 1, 1 - phase)

    for k in range(2):
        pltpu.make_async_copy(buf_ref.at[phase, k], buf_ref.at[phase, k], sem.at[phase]).wait()

    def rows_of(slot):
        return jnp.concatenate([buf_ref[phase, slot, pl.ds(s, tm, stride=SUBLANE), :]
                                for s in range(D_MODEL // LANE)], axis=1)

    meta = meta_ref[...]
    x2 = x1_ref[...] + (meta[:, 2:3] * rows_of(0) + meta[:, 3:4] * rows_of(1))
    out_ref[...] = _rms(x2) * nw_ref[...] if final_norm else x2


def _combine(route, x1, meta, yb, norm_w, final_norm):
    t = x1.shape[0]
    tm = min(GATHER_TILE, t)
    return pl.pallas_call(
        functools.partial(_combine_body, tm=tm, final_norm=final_norm),
        grid_spec=pltpu.PrefetchScalarGridSpec(
            num_scalar_prefetch=5, grid=(t // tm,),
            in_specs=[pl.BlockSpec((tm, D_MODEL), lambda i, *_: (i, 0)),
                      pl.BlockSpec((tm, LANE), lambda i, *_: (i, 0)),
                      pl.BlockSpec(memory_space=pl.ANY),
                      pl.BlockSpec((1, D_MODEL), lambda i, *_: (0, 0))],
            out_specs=pl.BlockSpec((tm, D_MODEL), lambda i, *_: (i, 0)),
            scratch_shapes=[pltpu.VMEM((2, 2, tm * SUBLANE, LANE), F32), pltpu.SemaphoreType.DMA((2,))]),
        out_shape=jax.ShapeDtypeStruct((t, D_MODEL), F32),
        compiler_params=_params("arbitrary"),
        name="moe_combine",
    )(*route, x1, meta, yb, norm_w)


def _moe(x1, h2_tiles, meta, counts, layer, w1, w3, w2, norm_w, final_norm):
    t = x1.shape[0]
    n_blocks = (2 * t + N_EXPERTS * (MOE_BLK - 1)) // MOE_BLK
    e1, e2 = meta[:, 0].astype(I32), meta[:, 1].astype(I32)
    r1, r2 = meta[:, 4].astype(I32), meta[:, 5].astype(I32)
    cnt = counts[0, :N_EXPERTS].astype(I32)
    padded = (cnt + MOE_BLK - 1) // MOE_BLK * MOE_BLK
    pad_end = jnp.cumsum(padded)
    pad_start = pad_end - padded
    route = (e1, e2, r1, r2, pad_start)
    nb_used = (pad_end[-1:] // MOE_BLK).astype(I32)
    blk_start = jnp.arange(n_blocks, dtype=I32) * MOE_BLK
    block_e = jnp.minimum(jnp.sum(blk_start[:, None] >= pad_end[None, :], axis=1), N_EXPERTS - 1).astype(I32)
    n_rows = n_blocks * MOE_BLK
    zero_start = jnp.concatenate([pad_start + cnt, pad_end[-1:]]).astype(I32)
    zero_len = jnp.concatenate([padded - cnt, n_rows - pad_end[-1:]]).astype(I32)
    xb = _scatter(route, zero_start, zero_len, h2_tiles, n_rows)
    yb = _experts(block_e, nb_used, xb, layer, w1, w3, w2, n_blocks)
    return _combine(route, x1, meta, yb, norm_w, final_norm)


def _rope_tables(pos):
    half = RET_DK // 2
    inv = ROPE_BASE ** (-jnp.arange(half, dtype=F32) / half)
    ang = pos[:, None] * inv[None, :]
    cos, sin = jnp.cos(ang), jnp.sin(ang)
    cos_t = jnp.tile(jnp.concatenate([cos, cos], axis=1), (1, RET_HEADS))
    sin_t = jnp.tile(jnp.concatenate([-sin, sin], axis=1), (1, RET_HEADS))
    return cos_t, sin_t


def _block_diag(blocks):
    g, r, c = blocks.shape
    eye = jnp.eye(g, dtype=bool)
    return jnp.where(eye[:, None, :, None], blocks[:, :, None, :], 0).reshape(g * r, g * c)


def _layer_weights(l, w_in, s5, s5_b_bar, s5_c_re, s5_c_im, s5_d, s5_glu_w, s5_glu_b, gdn_conv_w, gdn_a_log,
                   gdn_dt_bias, gdn_norm_w, w_out, router_group_w, router_group_b, router_expert_w,
                   router_expert_b):
    wi = w_in[l]
    g0 = RET_IN + S5_IN
    w_ab = jnp.pad(wi[:, g0 + GDN_QKV + GDN_HD:], ((0, 0), (0, LANE - 2 * GDN_HEADS)))
    in_ws = [wi[:, :RET_IN], wi[:, g0:g0 + GDN_QKV], wi[:, RET_IN:g0], wi[:, g0 + GDN_QKV:g0 + GDN_QKV + GDN_HD],
             w_ab]
    in_ws = [w.astype(BF16) for w in in_ws]
    ab_re, ab_im, bb_re, bb_im = s5
    n = S5_LANES
    sl = slice(l * n, (l + 1) * n)
    ab = jnp.concatenate([ab_re[sl].reshape(1, n), ab_im[sl].reshape(1, n)], axis=0)
    to_bd = lambda m: _block_diag(jnp.swapaxes(m[sl].reshape(S5_GROUPS, S5_STATE, S5_GROUP_CH), 1, 2))
    wb = jnp.concatenate([to_bd(bb_re), to_bd(bb_im)], axis=1).astype(BF16)
    wc_re = _block_diag(jnp.swapaxes(s5_c_re[l], 1, 2)).astype(BF16)
    wc_im = _block_diag(jnp.swapaxes(s5_c_im[l], 1, 2)).astype(BF16)
    s5_ws = (ab, wb, wc_re, wc_im, s5_d[l].reshape(1, S5_CH), s5_glu_w[l].astype(BF16),
             s5_glu_b[l].reshape(1, S5_CH))
    pad4 = lambda v: jnp.pad(v.reshape(1, GDN_HEADS), ((0, 0), (0, LANE - GDN_HEADS)))
    gdn_ws = (gdn_conv_w[l], pad4(gdn_a_log[l]), pad4(gdn_dt_bias[l]),
              jnp.tile(gdn_norm_w[l], GDN_HEADS).reshape(1, GDN_HD))
    wo = w_out[l].astype(BF16)
    out_ws = (wo[:RET_VD], wo[RET_VD:RET_VD + S5_CH], wo[RET_VD + S5_CH:])
    wr = jnp.pad(jnp.concatenate([router_expert_w[l], router_group_w[l]], axis=1),
                 ((0, 0), (0, LANE - N_EXPERTS - MOE_GROUPS))).astype(BF16)
    br = jnp.pad(jnp.concatenate([router_expert_b[l], router_group_b[l]]),
                 (0, LANE - N_EXPERTS - MOE_GROUPS)).reshape(1, LANE)
    return in_ws, s5_ws, gdn_ws, out_ws, (wr, br)


def _trunk(x, pos, states, layer_ws, norm_mix, norm_ffn, experts, norm_final):
    bsz, seq, _ = x.shape
    t = bsz * seq
    ret_s, s5_re, s5_im, gdn_s, gdn_buf = states
    cos_t, sin_t = _rope_tables(pos)
    x = x.reshape(t, D_MODEL)
    outs = [[], [], [], [], []]
    for l in range(DEPTH):
        in_ws, s5_ws, gdn_ws, out_ws, (wr, br) = layer_ws[l]
        zr, zq, zs, zg, zab = _inproj(x, norm_mix[l].reshape(1, D_MODEL), in_ws)
        o_ret, ret_fin = _retention(zr, cos_t, sin_t, ret_s[l].reshape(bsz, RET_QD, RET_DV), bsz, seq)
        u_tm = jnp.swapaxes(zs.reshape(bsz, seq, S5_CH), 0, 1)
        o_s5_tm, re_fin, im_fin = _s5(u_tm, s5_re[l].reshape(bsz, S5_LANES), s5_im[l].reshape(bsz, S5_LANES),
                                      *s5_ws)
        o_s5 = jnp.swapaxes(o_s5_tm, 0, 1).reshape(t, S5_CH)
        buf0 = jnp.pad(gdn_buf[l], ((0, 0), (SUBLANE - (GDN_CONV - 1), 0), (0, 0)))
        o_gdn, nbuf, gdn_fin = _gdn(zq, zg, zab, *gdn_ws, buf0, gdn_s[l].reshape(bsz, GDN_HD, GDN_DV), bsz, seq)
        x1, h2_tiles, meta, counts = _outproj(x, o_ret, o_s5, o_gdn, *out_ws, norm_ffn[l].reshape(1, D_MODEL),
                                              wr, br)
        x = _moe(x1, h2_tiles, meta, counts, l, *experts, norm_final.reshape(1, D_MODEL), l == DEPTH - 1)
        for lst, s in zip(outs, (ret_fin.reshape(bsz, RET_HEADS, RET_DK, RET_DV),
                                 re_fin.reshape(bsz, S5_GROUPS, S5_STATE),
                                 im_fin.reshape(bsz, S5_GROUPS, S5_STATE),
                                 gdn_fin.reshape(bsz, GDN_HEADS, GDN_DK, GDN_DV),
                                 nbuf[:, SUBLANE - (GDN_CONV - 1):, :])):
            lst.append(s)
    return x.reshape(bsz, seq, D_MODEL), tuple(jnp.stack(o) for o in outs)


def kernel(x_prompt, x_sample, state_ret, state_s5_re, state_s5_im, state_gdn, state_gdn_conv, norm_mix, w_in, s5_a_re, s5_a_im, s5_log_dt, s5_b_re, s5_b_im, s5_c_re, s5_c_im, s5_d, s5_glu_w, s5_glu_b, gdn_conv_w, gdn_a_log, gdn_dt_bias, gdn_norm_w, w_out, norm_ffn, router_group_w, router_group_b, router_expert_w, router_expert_b, expert_w1, expert_w3, expert_w2, norm_final):
    n = DEPTH * S5_LANES
    col = lambda a: a.reshape(n, 1)
    log_dt = jnp.broadcast_to(s5_log_dt[:, :, None], (DEPTH, S5_GROUPS, S5_STATE))
    s5 = _s5_prep(col(s5_a_re), col(s5_a_im), col(log_dt), s5_b_re.reshape(n, S5_GROUP_CH),
                  s5_b_im.reshape(n, S5_GROUP_CH))
    layer_ws = [_layer_weights(l, w_in, s5, None, s5_c_re, s5_c_im, s5_d, s5_glu_w, s5_glu_b, gdn_conv_w,
                               gdn_a_log, gdn_dt_bias, gdn_norm_w, w_out, router_group_w, router_group_b,
                               router_expert_w, router_expert_b) for l in range(DEPTH)]
    experts = (expert_w1, expert_w3, expert_w2)
    bp, lp, _ = x_prompt.shape
    zero_states = (jnp.zeros((DEPTH, bp, RET_HEADS, RET_DK, RET_DV), F32),
                   jnp.zeros((DEPTH, bp, S5_GROUPS, S5_STATE), F32),
                   jnp.zeros((DEPTH, bp, S5_GROUPS, S5_STATE), F32),
                   jnp.zeros((DEPTH, bp, GDN_HEADS, GDN_DK, GDN_DV), F32),
                   jnp.zeros((DEPTH, bp, GDN_CONV - 1, GDN_QKV), F32))
    y_p, p_states = _trunk(x_prompt, jnp.arange(lp, dtype=F32), zero_states, layer_ws, norm_mix, norm_ffn,
                           experts, norm_final)
    y_s, s_states = _trunk(x_sample, PAST_LEN + jnp.arange(x_sample.shape[1], dtype=F32),
                           (state_ret, state_s5_re, state_s5_im, state_gdn, state_gdn_conv), layer_ws, norm_mix,
                           norm_ffn, experts, norm_final)
    return (y_p, y_s) + p_states + s_states
```

```python
import functools
import math

import jax
import jax.numpy as jnp
import numpy as np
from jax import lax
from jax.experimental import pallas as pl
from jax.experimental.pallas import tpu as pltpu

F32 = jnp.float32
BF16 = jnp.bfloat16
I32 = jnp.int32

D_MODEL = 1024
DEPTH = 2
RET_HEADS, RET_DK, RET_DV = 4, 64, 128
S5_GROUPS, S5_GROUP_CH, S5_STATE = 16, 16, 64
S5_CH = S5_GROUPS * S5_GROUP_CH
S5_LANES = S5_GROUPS * S5_STATE
GDN_HEADS, GDN_DK, GDN_DV, GDN_CONV = 4, 64, 64, 4
GDN_HD = GDN_HEADS * GDN_DK
GDN_QKV = 3 * GDN_HD
RET_QD = RET_HEADS * RET_DK
RET_VD = RET_HEADS * RET_DV
RET_IN = 2 * RET_QD + 2 * RET_VD
S5_IN = S5_CH
GDN_IN = GDN_QKV + GDN_HD + 2 * GDN_HEADS
GDN_CHUNK = 64
RET_CHUNK = 64
MOE_GROUPS, EXPERTS_PER_GROUP = 4, 8
N_EXPERTS = MOE_GROUPS * EXPERTS_PER_GROUP
D_EXPERT = 512
ROPE_BASE = 10000.0
EPS = 1e-6
PAST_LEN = 16384

LANE = 128
SUBLANE = 8
ROW_TILE = 512
MOE_BLK = 256
GATHER_TILE = 256
ISSUE_UNROLL = 8
SEQS_PER_STEP = 8
LONG_SEQS_PER_STEP = 2
VMEM_LIMIT = 56 * 1024 * 1024


def _params(*sem):
    return pltpu.CompilerParams(dimension_semantics=sem, vmem_limit_bytes=VMEM_LIMIT)


def _dot(a, b):
    return jnp.dot(a.astype(BF16), b.astype(BF16), preferred_element_type=F32)


def _dot_nt(a, b):
    return lax.dot_general(a.astype(BF16), b.astype(BF16), (((1,), (1,)), ((), ())),
                           preferred_element_type=F32)


def _dot_tn(a, b):
    return lax.dot_general(a.astype(BF16), b.astype(BF16), (((0,), (0,)), ((), ())),
                           preferred_element_type=F32)


def _dot_hi(a, b):
    a1 = a.astype(BF16)
    a2 = (a - a1.astype(F32)).astype(BF16)
    b1 = b.astype(BF16)
    b2 = (b - b1.astype(F32)).astype(BF16)
    d = lambda x, y: jnp.dot(x, y, preferred_element_type=F32)
    return d(a1, b1) + (d(a1, b2) + d(a2, b1))


def _split3(x):
    p1 = x.astype(BF16)
    r1 = x - p1.astype(F32)
    p2 = r1.astype(BF16)
    p3 = (r1 - p2.astype(F32)).astype(BF16)
    return p1, p2, p3


def _dot_sel_l(sel, x):
    p1, p2, p3 = _split3(x)
    d = lambda p: jnp.dot(sel, p, preferred_element_type=F32)
    return d(p1) + d(p2) + d(p3)


def _dot_sel_r(x, sel):
    p1, p2, p3 = _split3(x)
    d = lambda p: jnp.dot(p, sel, preferred_element_type=F32)
    return d(p1) + d(p2) + d(p3)


def _rms(x):
    return x * lax.rsqrt(jnp.mean(x * x, axis=-1, keepdims=True) + EPS)


def _silu(x):
    return x * jax.nn.sigmoid(x)


def _inproj_body(x_ref, nw_ref, wr_ref, wq_ref, ws_ref, wg_ref, wab_ref,
                 zr_ref, zq_ref, zs_ref, zg_ref, zab_ref):
    h = (_rms(x_ref[...]) * nw_ref[...]).astype(BF16)
    for w_ref, z_ref in ((wr_ref, zr_ref), (wq_ref, zq_ref), (ws_ref, zs_ref),
                         (wg_ref, zg_ref), (wab_ref, zab_ref)):
        z_ref[...] = jnp.dot(h, w_ref[...], preferred_element_type=F32)


def _inproj(x, nw, ws):
    t = x.shape[0]
    tm = min(ROW_TILE, t)
    widths = [w.shape[1] for w in ws]
    row = lambda n: pl.BlockSpec((tm, n), lambda i: (i, 0))
    full = lambda a: pl.BlockSpec(a.shape, lambda i: (0, 0))
    return pl.pallas_call(
        _inproj_body,
        grid=(t // tm,),
        in_specs=[row(D_MODEL), full(nw)] + [full(w) for w in ws],
        out_specs=[row(n) for n in widths],
        out_shape=[jax.ShapeDtypeStruct((t, n), F32) for n in widths],
        compiler_params=_params("parallel"),
        name="inproj",
    )(x, nw, *ws)


def _seqs_per_step(bsz, blocks_per_seq):
    want = SEQS_PER_STEP if blocks_per_seq == 1 else LONG_SEQS_PER_STEP
    return want if bsz % want == 0 else 1


def _ret_log_gamma(h):
    return math.log(1.0 - 2.0 ** (-5.0 - h))


def _ret_body(zr_ref, cos_ref, sin_ref, s0_ref, o_ref, sfin_ref, s_scr, *, nseq, rows, chunk):
    j = pl.program_id(1)
    stack = RET_HEADS * chunk

    @pl.when(j == 0)
    def _():
        s_scr[...] = s0_ref[...]

    lane = lax.broadcasted_iota(I32, (rows, RET_QD), 1)
    first_half = (lane % RET_DK) < (RET_DK // 2)

    def rotary(x):
        swapped = jnp.where(first_half, pltpu.roll(x, RET_QD - RET_DK // 2, 1),
                            pltpu.roll(x, RET_DK // 2, 1))
        return x * cos_ref[...] + swapped * sin_ref[...]

    q = [rotary(zr_ref[s, :, 0:RET_QD]) for s in range(nseq)]
    k = [rotary(zr_ref[s, :, RET_QD:2 * RET_QD]) * (RET_DK ** -0.5) for s in range(nseq)]

    def by_head(idx, fn):
        out = jnp.zeros(idx.shape, F32)
        for h in range(RET_HEADS):
            out = jnp.where(idx == h, fn(h), out)
        return out

    lg_lane = by_head(lax.broadcasted_iota(I32, (chunk, RET_QD), 1) // RET_DK, _ret_log_gamma)
    pos = lax.broadcasted_iota(I32, (chunk, RET_QD), 0).astype(F32)
    q_scale = jnp.exp((pos + 1.0) * lg_lane)
    k_scale = jnp.exp((chunk - 1.0 - pos) * lg_lane)
    st_row = lax.broadcasted_iota(I32, (stack, stack), 0)
    st_col = lax.broadcasted_iota(I32, (stack, stack), 1)
    causal = ((st_row // chunk) == (st_col // chunk)) & (st_row >= st_col)
    lg_stack = by_head(st_row // chunk, _ret_log_gamma)
    decay = jnp.where(causal, jnp.exp(jnp.where(causal, (st_row - st_col).astype(F32), 0.0) * lg_stack), 0.0)
    q_rows = (lax.broadcasted_iota(I32, (stack, RET_QD), 0) // chunk) == \
             (lax.broadcasted_iota(I32, (stack, RET_QD), 1) // RET_DK)
    v_rows = (lax.broadcasted_iota(I32, (stack, RET_VD), 0) // chunk) == \
             (lax.broadcasted_iota(I32, (stack, RET_VD), 1) // RET_DV)
    s_row_head = lax.broadcasted_iota(I32, (RET_QD, RET_DV), 0) // RET_DK
    s_decay = by_head(s_row_head, lambda h: math.exp(chunk * _ret_log_gamma(h)))
    kv_diag = (lax.broadcasted_iota(I32, (RET_QD, RET_VD), 0) // RET_DK) == \
              (lax.broadcasted_iota(I32, (RET_QD, RET_VD), 1) // RET_DV)

    def tile4(a):
        return jnp.concatenate([a] * RET_HEADS, axis=0)

    def collapse(a):
        out = a[0:chunk]
        for h in range(1, RET_HEADS):
            out = out + a[h * chunk:(h + 1) * chunk]
        return out

    seqs = range(nseq)
    n_chunks = rows // chunk
    state = [s_scr[s] for s in seqs]
    for ci in range(n_chunks):
        r = slice(ci * chunk, (ci + 1) * chunk)
        v_cs = [zr_ref[s, r, 2 * RET_QD:2 * RET_QD + RET_VD] for s in seqs]
        scores = [_dot_nt(jnp.where(q_rows, tile4(q[s][r]), 0.0), tile4(k[s][r])) * decay for s in seqs]
        o_intra = [collapse(_dot(scores[s], jnp.where(v_rows, tile4(v_cs[s]), 0.0))) for s in seqs]
        s_full = [jnp.concatenate([jnp.where(s_row_head == h, state[s], 0.0) for h in range(RET_HEADS)], axis=1)
                  for s in seqs]
        o_cs = [o_intra[s] + _dot(q[s][r] * q_scale, s_full[s]) for s in seqs]
        for s in seqs:
            gate = zr_ref[s, r, 2 * RET_QD + RET_VD:2 * RET_QD + 2 * RET_VD]
            for h in range(RET_HEADS):
                sl = slice(h * RET_DV, (h + 1) * RET_DV)
                o_ref[s, r, sl] = _rms(o_cs[s][:, sl]) * _silu(gate[:, sl])
        for s in seqs:
            kv = jnp.where(kv_diag, _dot_tn(k[s][r] * k_scale, v_cs[s]), 0.0)
            kv_own = kv[:, 0:RET_DV]
            for h in range(1, RET_HEADS):
                kv_own = kv_own + kv[:, h * RET_DV:(h + 1) * RET_DV]
            state[s] = state[s] * s_decay + kv_own
    for s in seqs:
        s_scr[s] = state[s]

    @pl.when(j == pl.num_programs(1) - 1)
    def _():
        for s in seqs:
            sfin_ref[s] = state[s]


def _retention(zr, cos, sin, s0, bsz, seq):
    chunk = min(RET_CHUNK, seq)
    rows = min(4 * chunk, seq)
    nj = seq // rows
    nseq = _seqs_per_step(bsz, nj)
    o, s_fin = pl.pallas_call(
        functools.partial(_ret_body, nseq=nseq, rows=rows, chunk=chunk),
        grid=(bsz // nseq, nj),
        in_specs=[pl.BlockSpec((nseq, rows, RET_IN), lambda b, j: (b, j, 0)),
                  pl.BlockSpec((rows, RET_QD), lambda b, j: (j, 0)),
                  pl.BlockSpec((rows, RET_QD), lambda b, j: (j, 0)),
                  pl.BlockSpec((nseq, RET_QD, RET_DV), lambda b, j: (b, 0, 0))],
        out_specs=[pl.BlockSpec((nseq, rows, RET_VD), lambda b, j: (b, j, 0)),
                   pl.BlockSpec((nseq, RET_QD, RET_DV), lambda b, j: (b, 0, 0))],
        out_shape=[jax.ShapeDtypeStruct((bsz, seq, RET_VD), F32),
                   jax.ShapeDtypeStruct((bsz, RET_QD, RET_DV), F32)],
        scratch_shapes=[pltpu.VMEM((nseq, RET_QD, RET_DV), F32)],
        compiler_params=_params("parallel", "arbitrary"),
        name="retention",
    )(zr.reshape(bsz, seq, RET_IN), cos, sin, s0)
    return o.reshape(bsz * seq, RET_VD), s_fin


def _s5_prep_body(are_ref, aim_ref, ldt_ref, bre_ref, bim_ref, abre_ref, abim_ref, bbre_ref, bbim_ref):
    lam_re, lam_im = are_ref[...], aim_ref[...]
    dt = jnp.exp(ldt_ref[...])
    mag = jnp.exp(lam_re * dt)
    ab_re = mag * jnp.cos(lam_im * dt)
    ab_im = mag * jnp.sin(lam_im * dt)
    den = lam_re * lam_re + lam_im * lam_im
    f_re = ((ab_re - 1.0) * lam_re + ab_im * lam_im) / den
    f_im = (ab_im * lam_re - (ab_re - 1.0) * lam_im) / den
    abre_ref[...] = ab_re
    abim_ref[...] = ab_im
    bbre_ref[...] = f_re * bre_ref[...] - f_im * bim_ref[...]
    bbim_ref[...] = f_re * bim_ref[...] + f_im * bre_ref[...]


def _s5_prep(a_re, a_im, log_dt, b_re, b_im):
    n = a_re.shape[0]
    col = jax.ShapeDtypeStruct((n, 1), F32)
    mat = jax.ShapeDtypeStruct((n, S5_GROUP_CH), F32)
    return pl.pallas_call(_s5_prep_body, out_shape=[col, col, mat, mat], name="s5_prep")(
        a_re, a_im, log_dt, b_re, b_im)


def _gelu_tanh(x):
    return x * (0.5 * (1.0 + jnp.tanh(math.sqrt(2.0 / math.pi) * (x + 0.044715 * (x * x * x)))))


def _s5_body(u_ref, h0re_ref, h0im_ref, ab_ref, wb_ref, wcre_ref, wcim_ref, d_ref, gw_ref, gb_ref,
             o_ref, hre_ref, him_ref, bu_scr, st_scr, *, steps):
    j = pl.program_id(1)

    @pl.when(j == 0)
    def _():
        st_scr[0] = h0re_ref[...]
        st_scr[1] = h0im_ref[...]

    rows = steps * SUBLANE
    u = u_ref[...].reshape(rows, S5_CH)
    bu_scr[...] = _dot(u, wb_ref[...])
    a_re = jnp.broadcast_to(ab_ref[0:1, :], (SUBLANE, S5_LANES))
    a_im = jnp.broadcast_to(ab_ref[1:2, :], (SUBLANE, S5_LANES))

    def step(t, carry):
        h_re, h_im = carry
        r = pl.ds(pl.multiple_of(t * SUBLANE, SUBLANE), SUBLANE)
        n_re = a_re * h_re - a_im * h_im + bu_scr[r, 0:S5_LANES]
        n_im = a_re * h_im + a_im * h_re + bu_scr[r, S5_LANES:2 * S5_LANES]
        bu_scr[r, 0:S5_LANES] = n_re
        bu_scr[r, S5_LANES:2 * S5_LANES] = n_im
        return n_re, n_im

    h_re, h_im = lax.fori_loop(0, steps, step, (st_scr[0], st_scr[1]))
    st_scr[0] = h_re
    st_scr[1] = h_im
    hre_ref[...] = h_re
    him_ref[...] = h_im

    y = _dot(bu_scr[:, 0:S5_LANES], wcre_ref[...]) - _dot(bu_scr[:, S5_LANES:2 * S5_LANES], wcim_ref[...])
    y = _gelu_tanh(y + d_ref[...] * u)
    y = y * jax.nn.sigmoid(_dot(y, gw_ref[...]) + gb_ref[...])
    o_ref[...] = y.reshape(steps, SUBLANE, S5_CH)


def _s5(u_tm, h0_re, h0_im, ab, wb, wc_re, wc_im, d_skip, glu_w, glu_b):
    seq, bsz, _ = u_tm.shape
    steps = min(64, seq)
    full = lambda a: pl.BlockSpec(a.shape, lambda g, j: (0,) * a.ndim)
    st = pl.BlockSpec((SUBLANE, S5_LANES), lambda g, j: (g, 0))
    return pl.pallas_call(
        functools.partial(_s5_body, steps=steps),
        grid=(bsz // SUBLANE, seq // steps),
        in_specs=[pl.BlockSpec((steps, SUBLANE, S5_CH), lambda g, j: (j, g, 0)), st, st,
                  full(ab), full(wb), full(wc_re), full(wc_im), full(d_skip), full(glu_w), full(glu_b)],
        out_specs=[pl.BlockSpec((steps, SUBLANE, S5_CH), lambda g, j: (j, g, 0)), st, st],
        out_shape=[jax.ShapeDtypeStruct((seq, bsz, S5_CH), F32),
                   jax.ShapeDtypeStruct((bsz, S5_LANES), F32),
                   jax.ShapeDtypeStruct((bsz, S5_LANES), F32)],
        scratch_shapes=[pltpu.VMEM((steps * SUBLANE, 2 * S5_LANES), F32),
                        pltpu.VMEM((2, SUBLANE, S5_LANES), F32)],
        compiler_params=_params("parallel", "arbitrary"),
        name="s5",
    )(u_tm, h0_re, h0_im, ab, wb, wc_re, wc_im, d_skip, glu_w, glu_b)


def _gdn_body(zq_ref, zg_ref, zab_ref, cw_ref, alog_ref, dtb_ref, nw_ref, buf0_ref, s0_ref,
              ones_bd_ref, tri_ref, ea_ref, eb_ref, ec_ref, spread_ref, gather_ref,
              o_ref, nbuf_ref, sfin_ref,
              ext_scr, q_scr, k_scr, kb_scr, vb_scr, g_scr, g4_scr, s_scr, *, nseq, rows, chunk):
    j = pl.program_id(1)
    stack = GDN_HEADS * chunk
    bd_state = (lax.broadcasted_iota(I32, (GDN_HD, GDN_HD), 0) // GDN_DK) == \
               (lax.broadcasted_iota(I32, (GDN_HD, GDN_HD), 1) // GDN_DV)

    @pl.when(j == 0)
    def _():
        for s in range(nseq):
            ext_scr[s, 0:SUBLANE, :] = buf0_ref[s]
            s_scr[s] = jnp.where(bd_state, _dot_sel_r(s0_ref[s], spread_ref[...]), 0.0)

    convs = []
    for s in range(nseq):
        ext_scr[s, SUBLANE:SUBLANE + rows, :] = zq_ref[s]
        conv = ext_scr[s, SUBLANE - 3:SUBLANE - 3 + rows, :] * cw_ref[0:1, :]
        for i in range(1, GDN_CONV):
            conv = conv + ext_scr[s, SUBLANE - 3 + i:SUBLANE - 3 + i + rows, :] * cw_ref[i:i + 1, :]
        tail = ext_scr[s, rows:rows + SUBLANE, :]
        nbuf_ref[s] = tail
        ext_scr[s, 0:SUBLANE, :] = tail
        convs.append(conv)
    qkv = _silu(convs[0] if nseq == 1 else jnp.concatenate(convs, axis=0))

    ones_bd = ones_bd_ref[...]
    q_raw = qkv[:, 0:GDN_HD]
    k_raw = qkv[:, GDN_HD:2 * GDN_HD]
    q_scr[...] = q_raw * lax.rsqrt(_dot_sel_r(q_raw * q_raw, ones_bd) + EPS) * (GDN_DK ** -0.5)
    k_n = k_raw * lax.rsqrt(_dot_sel_r(k_raw * k_raw, ones_bd) + EPS)
    k_scr[...] = k_n

    ab = zab_ref[...].reshape(nseq * rows, LANE)
    x = ab + dtb_ref[...]
    softplus = jnp.maximum(x, 0.0) + jnp.log1p(jnp.exp(-jnp.abs(x)))
    g_pad = -jnp.exp(alog_ref[...]) * softplus
    beta = _dot_sel_r(jax.nn.sigmoid(ab), eb_ref[...])
    g_cum = _dot_sel_l(tri_ref[...], g_pad)
    g_scr[...] = _dot_sel_r(g_cum, ea_ref[...])
    g4_scr[...] = _dot_sel_r(g_pad, ec_ref[...])
    kb_scr[...] = k_n * beta
    vb_scr[...] = qkv[:, 2 * GDN_HD:3 * GDN_HD] * beta

    w_row = lax.broadcasted_iota(I32, (chunk, stack), 0)
    w_col = lax.broadcasted_iota(I32, (chunk, stack), 1) % chunk
    strict_w = w_row > w_col
    causal_w = w_row >= w_col
    same_head = (lax.broadcasted_iota(I32, (stack, stack), 0) // chunk) == \
                (lax.broadcasted_iota(I32, (stack, stack), 1) // chunk)
    head_rows = (lax.broadcasted_iota(I32, (stack, GDN_HD), 0) // chunk) == \
                (lax.broadcasted_iota(I32, (stack, GDN_HD), 1) // GDN_DK)
    head_rows2 = jnp.concatenate([head_rows, head_rows], axis=1)
    tri_c = tri_ref[0:chunk, 0:chunk]
    packed_ok = chunk % (2 * SUBLANE) == 0

    def tile4(a):
        return jnp.concatenate([a] * GDN_HEADS, axis=0)

    def split2(a):
        hi = a.astype(BF16)
        return hi, (a - hi.astype(F32)).astype(BF16)

    def stack_masked(parts, mask):
        if packed_ok:
            return [jnp.where(mask, tile4(p), jnp.zeros((), BF16)) for p in parts]
        return [jnp.where(mask, tile4(p.astype(F32)), 0.0).astype(BF16) for p in parts]

    def on_diag(parts):
        return stack_masked(parts, same_head)

    def mm(a, b):
        return jnp.dot(a, b, preferred_element_type=F32)

    def mm_hi(a_parts, b_parts):
        return mm(a_parts[0], b_parts[0]) + (mm(a_parts[0], b_parts[1]) + mm(a_parts[1], b_parts[0]))

    n_chunks = rows // chunk
    every = range(nseq * n_chunks)
    rows_of = lambda u: slice(u * chunk, (u + 1) * chunk)
    solved = []
    for u in every:
        r = rows_of(u)
        q_c, k_c, kb_c, vb_c, g_c = q_scr[r, :], k_scr[r, :], kb_scr[r, :], vb_scr[r, :], g_scr[r, :]
        exp_g = jnp.exp(g_c)
        g_diff = _dot_sel_l(tri_c, jnp.where(strict_w, g4_scr[r, :], 0.0))
        decay = jnp.exp(jnp.where(causal_w, g_diff, 0.0))
        k_heads = jnp.where(head_rows, tile4(k_c), 0.0)
        lmat = jnp.where(strict_w, _dot_nt(kb_c, k_heads) * decay, 0.0)
        attn = jnp.where(causal_w, _dot_nt(q_c, k_heads) * decay, 0.0)
        rhs_c = jnp.concatenate([vb_c, kb_c * exp_g], axis=1)
        solved.append((lmat, attn, exp_g, rhs_c))

    t_acc = [-solved[ci][0] for ci in every]
    p_parts = [split2(solved[ci][0]) for ci in every]
    p_diag = [on_diag(p) for p in p_parts]
    span = 2
    while span < chunk + 1:
        power = [mm_hi(p_parts[ci], p_diag[ci]) for ci in every]
        p_parts = [split2(p) for p in power]
        p_diag = [on_diag(p) for p in p_parts]
        t_acc = [t_acc[ci] + power[ci] + mm_hi(split2(t_acc[ci]), p_diag[ci]) for ci in every]
        span *= 2
    rhs_parts = [stack_masked(split2(solved[ci][3]), head_rows2) for ci in every]
    sols = [solved[ci][3] + mm_hi(split2(t_acc[ci]), rhs_parts[ci]) for ci in every]

    seqs = range(nseq)
    s_cur = [s_scr[s] for s in seqs]
    for ci in range(n_chunks):
        us = [s * n_chunks + ci for s in seqs]
        v_new = [sols[u][:, 0:GDN_HD] - _dot(sols[u][:, GDN_HD:2 * GDN_HD], s_cur[s]) for s, u in enumerate(us)]
        o_cs = [_dot(q_scr[rows_of(u), :] * solved[u][2], s_cur[s])
                + _dot(solved[u][1], jnp.where(head_rows, tile4(v_new[s]), 0.0)) for s, u in enumerate(us)]
        for s, u in enumerate(us):
            g_c = g_scr[rows_of(u), :]
            g_last = g_c[chunk - 1:chunk, :]
            k_dec = k_scr[rows_of(u), :] * jnp.exp(g_last - g_c)
            s_cur[s] = s_cur[s] * jnp.exp(g_last) + jnp.where(bd_state, _dot_tn(k_dec, v_new[s]), 0.0)
        for s, u in enumerate(us):
            r = slice(ci * chunk, (ci + 1) * chunk)
            ms = _dot_sel_r(o_cs[s] * o_cs[s], ones_bd) * (1.0 / GDN_DV)
            o_ref[s, r, :] = o_cs[s] * lax.rsqrt(ms + EPS) * nw_ref[...] * _silu(zg_ref[s, r, :])
    for s in seqs:
        s_scr[s] = s_cur[s]

    @pl.when(j == pl.num_programs(1) - 1)
    def _():
        for s in seqs:
            sfin_ref[s] = _dot_sel_r(s_scr[s], gather_ref[...])


def _gdn_consts(rows, chunk):
    stack = GDN_HEADS * chunk
    blk = lambda n, c: (np.arange(n)[:, None] // c) == (np.arange(n)[None, :] // c)
    ones_bd = blk(GDN_HD, GDN_DK)
    tri = blk(rows, chunk) & (np.arange(rows)[:, None] >= np.arange(rows)[None, :])
    src = np.arange(LANE)[:, None]
    ea = src == np.arange(GDN_HD)[None, :] // GDN_DK
    eb = src == GDN_HEADS + np.arange(GDN_HD)[None, :] // GDN_DK
    ec = src == np.arange(stack)[None, :] // chunk
    spread = np.arange(GDN_DV)[:, None] == np.arange(GDN_HD)[None, :] % GDN_DV
    return tuple(jnp.asarray(m, dtype=BF16) for m in (ones_bd, tri, ea, eb, ec, spread, spread.T))


def _gdn(zq, zg, zab, conv_w, alog_pad, dtb_pad, nw, buf0, s0, bsz, seq):
    chunk = min(GDN_CHUNK, seq)
    rows = min(4 * chunk, seq)
    nj = seq // rows
    nseq = _seqs_per_step(bsz, nj)
    stack = GDN_HEADS * chunk
    consts = _gdn_consts(nseq * rows, chunk)
    row = lambda n: pl.BlockSpec((nseq, rows, n), lambda b, j: (b, j, 0))
    full = lambda a: pl.BlockSpec(a.shape, lambda b, j: (0,) * a.ndim)
    per_b = lambda *s: pl.BlockSpec((nseq,) + s, lambda b, j: (b, 0, 0))
    by_seq = lambda a: a.reshape(bsz, seq, a.shape[-1])
    o, nbuf, s_fin = pl.pallas_call(
        functools.partial(_gdn_body, nseq=nseq, rows=rows, chunk=chunk),
        grid=(bsz // nseq, nj),
        in_specs=[row(GDN_QKV), row(GDN_HD), row(LANE), full(conv_w), full(alog_pad), full(dtb_pad), full(nw),
                  per_b(SUBLANE, GDN_QKV), per_b(GDN_HD, GDN_DV)] + [full(c) for c in consts],
        out_specs=[row(GDN_HD), per_b(SUBLANE, GDN_QKV), per_b(GDN_HD, GDN_DV)],
        out_shape=[jax.ShapeDtypeStruct((bsz, seq, GDN_HD), F32),
                   jax.ShapeDtypeStruct((bsz, SUBLANE, GDN_QKV), F32),
                   jax.ShapeDtypeStruct((bsz, GDN_HD, GDN_DV), F32)],
        scratch_shapes=[pltpu.VMEM((nseq, rows + SUBLANE, GDN_QKV), F32)] +
                       [pltpu.VMEM((nseq * rows, GDN_HD), F32)] * 5 +
                       [pltpu.VMEM((nseq * rows, stack), F32), pltpu.VMEM((nseq, GDN_HD, GDN_HD), F32)],
        compiler_params=_params("parallel", "arbitrary"),
        name="gdn",
    )(by_seq(zq), by_seq(zg), by_seq(zab), conv_w, alog_pad, dtb_pad, nw, buf0, s0, *consts)
    return o.reshape(bsz * seq, GDN_HD), nbuf, s_fin


def _outproj_body(x_ref, oret_ref, os5_ref, ogdn_ref, w1_ref, w2_ref, w3_ref, nw_ref, wr_ref, br_ref, tri_ref,
                  cnt0_ref, x1_ref, h2_ref, meta_ref, cnt_ref, carry_scr, *, tm):
    i = pl.program_id(0)

    @pl.when(i == 0)
    def _():
        carry_scr[...] = cnt0_ref[...]

    mix = _dot(oret_ref[...], w1_ref[...]) + _dot(os5_ref[...], w2_ref[...]) + _dot(ogdn_ref[...], w3_ref[...])
    x1 = x_ref[...] + mix
    x1_ref[...] = x1
    h2 = _rms(x1) * nw_ref[...]
    for s in range(D_MODEL // LANE):
        h2_ref[pl.ds(s, tm, stride=SUBLANE), :] = h2[:, s * LANE:(s + 1) * LANE]

    logits = _dot(h2, wr_ref[...]) + br_ref[...]
    lane_i = lax.broadcasted_iota(I32, (tm, LANE), 1)
    lane = lane_i.astype(F32)
    neg = -jnp.inf
    big = float(LANE)
    g_log = jnp.where((lane_i >= N_EXPERTS) & (lane_i < N_EXPERTS + MOE_GROUPS), logits, neg)
    g_max = jnp.max(g_log, axis=-1, keepdims=True)
    grp = jnp.min(jnp.where(g_log == g_max, lane - N_EXPERTS, big), axis=-1, keepdims=True)
    p_grp = 1.0 / jnp.sum(jnp.exp(g_log - g_max), axis=-1, keepdims=True)
    in_grp = (lane >= grp * EXPERTS_PER_GROUP) & (lane < (grp + 1.0) * EXPERTS_PER_GROUP)
    e_log = jnp.where(in_grp, logits, neg)
    v1 = jnp.max(e_log, axis=-1, keepdims=True)
    i1 = jnp.min(jnp.where(e_log == v1, lane, big), axis=-1, keepdims=True)
    e_log2 = jnp.where(lane == i1, neg, e_log)
    v2 = jnp.max(e_log2, axis=-1, keepdims=True)
    i2 = jnp.min(jnp.where(e_log2 == v2, lane, big), axis=-1, keepdims=True)
    e2 = jnp.exp(v2 - v1)
    gate1 = p_grp / (1.0 + e2)
    gate2 = p_grp * e2 / (1.0 + e2)

    oh1 = lane == i1
    oh2 = lane == i2
    picked = jnp.where(oh1 | oh2, 1.0, 0.0)
    before = jnp.dot(tri_ref[...], picked.astype(BF16), preferred_element_type=F32) + carry_scr[0:1, :]
    rank1 = jnp.sum(jnp.where(oh1, before, 0.0), axis=-1, keepdims=True)
    rank2 = jnp.sum(jnp.where(oh2, before, 0.0), axis=-1, keepdims=True)
    total = carry_scr[0:1, :] + jnp.sum(picked, axis=0, keepdims=True)
    carry_scr[...] = jnp.broadcast_to(total, carry_scr.shape)
    cnt_ref[...] = jnp.broadcast_to(total, cnt_ref.shape)

    meta = jnp.where(lane_i == 0, i1, 0.0)
    meta = jnp.where(lane_i == 1, i2, meta)
    meta = jnp.where(lane_i == 2, gate1, meta)
    meta = jnp.where(lane_i == 3, gate2, meta)
    meta = jnp.where(lane_i == 4, rank1, meta)
    meta = jnp.where(lane_i == 5, rank2, meta)
    meta_ref[...] = meta


def _outproj(x, o_ret, o_s5, o_gdn, w1, w2, w3, nw, wr, br, cnt0):
    t = x.shape[0]
    tm = min(ROW_TILE, t)
    tri = jnp.asarray(np.arange(tm)[:, None] > np.arange(tm)[None, :], dtype=BF16)
    row = lambda n: pl.BlockSpec((tm, n), lambda i: (i, 0))
    full = lambda a: pl.BlockSpec(a.shape, lambda i: (0, 0))
    return pl.pallas_call(
        functools.partial(_outproj_body, tm=tm),
        grid=(t // tm,),
        in_specs=[row(D_MODEL), row(RET_VD), row(S5_CH), row(GDN_HD),
                  full(w1), full(w2), full(w3), full(nw), full(wr), full(br), full(tri), full(cnt0)],
        out_specs=[row(D_MODEL), pl.BlockSpec((tm * SUBLANE, LANE), lambda i: (i, 0)), row(LANE),
                   pl.BlockSpec((SUBLANE, LANE), lambda i: (0, 0))],
        out_shape=[jax.ShapeDtypeStruct((t, D_MODEL), F32),
                   jax.ShapeDtypeStruct((t * SUBLANE, LANE), F32),
                   jax.ShapeDtypeStruct((t, LANE), F32),
                   jax.ShapeDtypeStruct((SUBLANE, LANE), F32)],
        scratch_shapes=[pltpu.VMEM((SUBLANE, LANE), F32)],
        compiler_params=_params("arbitrary"),
        name="outproj_router",
    )(x, o_ret, o_s5, o_gdn, w1, w2, w3, nw, wr, br, tri, cnt0)


def _token_rows(ref, idx):
    return ref.at[pl.ds(pl.multiple_of(idx * SUBLANE, SUBLANE), SUBLANE)]


def _slot(e_ref, r_ref, ps_ref, g):
    return ps_ref[e_ref[g]] + r_ref[g]


ZERO_ROWS = 128


def _zero_segment(zero_scr, xb_ref, sem, start, length, wait):
    def piece(off, n):
        cp = pltpu.make_async_copy(zero_scr.at[pl.ds(0, n * SUBLANE)],
                                   xb_ref.at[pl.ds(pl.multiple_of(off * SUBLANE, SUBLANE), n * SUBLANE)], sem)
        cp.wait() if wait else cp.start()

    n_big = length // ZERO_ROWS

    def big(i, carry):
        piece(start + i * ZERO_ROWS, ZERO_ROWS)
        return carry

    lax.fori_loop(0, n_big, big, 0)
    off = start + n_big * ZERO_ROWS
    rem = length - n_big * ZERO_ROWS
    bit = ZERO_ROWS // 2
    while bit >= 1:
        has = (rem & bit) != 0
        pl.when(has)(functools.partial(piece, off, bit))
        off = off + jnp.where(has, bit, 0)
        bit //= 2


def _scatter_body(e1_ref, e2_ref, r1_ref, r2_ref, ps_ref, zs_ref, zl_ref, src_a_ref, src_b_ref, xb_ref,
                  zero_scr, sem, zsem, *, tm, tiles_a):
    i = pl.program_id(0)
    base = i * tm

    @pl.when(i == 0)
    def _():
        zero_scr[...] = jnp.zeros_like(zero_scr)
        for wait in (False, True):
            lax.fori_loop(0, N_EXPERTS + 1,
                          lambda s, c, wait=wait: (_zero_segment(zero_scr, xb_ref, zsem, zs_ref[s], zl_ref[s], wait),
                                                   c)[1], 0)

    def scatter_tile(src_ref):
        def copy(t, d):
            return pltpu.make_async_copy(_token_rows(src_ref, t), _token_rows(xb_ref, d), sem)

        def issue(t, carry):
            copy(t, _slot(e1_ref, r1_ref, ps_ref, base + t)).start(priority=0)
            copy(t, _slot(e2_ref, r2_ref, ps_ref, base + t)).start(priority=1)
            return carry

        lax.fori_loop(0, tm, issue, 0, unroll=ISSUE_UNROLL)
        for _ in range(2):
            pltpu.make_async_copy(src_ref, src_ref, sem).wait()

    pl.when(i < tiles_a)(functools.partial(scatter_tile, src_a_ref))
    pl.when(i >= tiles_a)(functools.partial(scatter_tile, src_b_ref))


def _scatter(route, zero_start, zero_len, h2_a, h2_b, n_rows):
    tm = GATHER_TILE
    tiles_a, tiles_b = h2_a.shape[0] // (tm * SUBLANE), h2_b.shape[0] // (tm * SUBLANE)
    assert tiles_a * tm * SUBLANE == h2_a.shape[0] and tiles_b * tm * SUBLANE == h2_b.shape[0]
    return pl.pallas_call(
        functools.partial(_scatter_body, tm=tm, tiles_a=tiles_a),
        grid_spec=pltpu.PrefetchScalarGridSpec(
            num_scalar_prefetch=7, grid=(tiles_a + tiles_b,),
            in_specs=[pl.BlockSpec((tm * SUBLANE, LANE), lambda i, *_: (jnp.minimum(i, tiles_a - 1), 0)),
                      pl.BlockSpec((tm * SUBLANE, LANE), lambda i, *_: (jnp.maximum(i - tiles_a, 0), 0))],
            out_specs=pl.BlockSpec(memory_space=pl.ANY),
            scratch_shapes=[pltpu.VMEM((ZERO_ROWS * SUBLANE, LANE), F32), pltpu.SemaphoreType.DMA(()),
                            pltpu.SemaphoreType.DMA(())]),
        out_shape=jax.ShapeDtypeStruct((n_rows * SUBLANE, LANE), F32),
        compiler_params=_params("arbitrary"),
        name="moe_scatter",
    )(*route, zero_start, zero_len, h2_a, h2_b)


def _experts_body(be_ref, nb_ref, xb_ref, w1_ref, w3_ref, w2_ref, yb_ref, w1_scr, w3_scr, w2_scr, *, blk):
    i = pl.program_id(0)
    live = i < nb_ref[0]

    @pl.when(live & ((i == 0) | (be_ref[i] != be_ref[jnp.maximum(i - 1, 0)])))
    def _():
        w1_scr[...] = w1_ref[...].astype(BF16)
        w3_scr[...] = w3_ref[...].astype(BF16)
        w2_scr[...] = w2_ref[...].astype(BF16)

    @pl.when(live)
    def _():
        x = jnp.concatenate([xb_ref[pl.ds(s, blk, stride=SUBLANE), :] for s in range(D_MODEL // LANE)],
                            axis=1).astype(BF16)
        hid = _silu(jnp.dot(x, w1_scr[...], preferred_element_type=F32)) * \
            jnp.dot(x, w3_scr[...], preferred_element_type=F32)
        y = _dot(hid, w2_scr[...])
        for s in range(D_MODEL // LANE):
            yb_ref[pl.ds(s, blk, stride=SUBLANE), :] = y[:, s * LANE:(s + 1) * LANE]

    @pl.when(jnp.logical_not(live))
    def _():
        yb_ref[...] = jnp.zeros_like(yb_ref)


def _experts(block_e, nb_used, xb, layer, w1, w3, w2, n_blocks):
    blk = MOE_BLK
    live = lambda i, nb: jnp.minimum(i, nb[0] - 1)
    tile_in = pl.BlockSpec((blk * SUBLANE, LANE), lambda i, be, nb: (live(i, nb), 0))
    tile_out = pl.BlockSpec((blk * SUBLANE, LANE), lambda i, be, nb: (i, 0))
    wspec = lambda a: pl.BlockSpec((None, None) + a.shape[2:],
                                   lambda i, be, nb: (layer, be[live(i, nb)], 0, 0))
    return pl.pallas_call(
        functools.partial(_experts_body, blk=blk),
        grid_spec=pltpu.PrefetchScalarGridSpec(
            num_scalar_prefetch=2, grid=(n_blocks,),
            in_specs=[tile_in, wspec(w1), wspec(w3), wspec(w2)], out_specs=tile_out,
            scratch_shapes=[pltpu.VMEM(w1.shape[2:], BF16), pltpu.VMEM(w3.shape[2:], BF16),
                            pltpu.VMEM(w2.shape[2:], BF16)]),
        out_shape=jax.ShapeDtypeStruct(xb.shape, F32),
        compiler_params=_params("arbitrary"),
        name="moe_experts",
    )(block_e, nb_used, xb, w1, w3, w2)


def _combine_body(e1_ref, e2_ref, r1_ref, r2_ref, ps_ref, x1_a_ref, x1_b_ref, meta_a_ref, meta_b_ref, yb_ref, nw_ref,
                  out_a_ref, out_b_ref, buf_ref, sem, *, tm, tiles_a, final_norm):
    i = pl.program_id(0)
    phase = i % 2

    def gather(tile, ph):
        base = tile * tm

        def issue(t, carry):
            for k, (e_ref, r_ref) in enumerate(((e1_ref, r1_ref), (e2_ref, r2_ref))):
                pltpu.make_async_copy(_token_rows(yb_ref, _slot(e_ref, r_ref, ps_ref, base + t)),
                                      _token_rows(buf_ref.at[ph, k], t), sem.at[ph]).start(priority=k)
            return carry

        lax.fori_loop(0, tm, issue, 0, unroll=ISSUE_UNROLL)

    @pl.when(i == 0)
    def _():
        gather(0, 0)

    @pl.when(i + 1 < pl.num_programs(0))
    def _():
        gather(i + 1, 1 - phase)

    for k in range(2):
        pltpu.make_async_copy(buf_ref.at[phase, k], buf_ref.at[phase, k], sem.at[phase]).wait()

    def rows_of(slot):
        return jnp.concatenate([buf_ref[phase, slot, pl.ds(s, tm, stride=SUBLANE), :]
                                for s in range(D_MODEL // LANE)], axis=1)

    is_a = i < tiles_a
    meta = jnp.where(is_a, meta_a_ref[...], meta_b_ref[...])
    x2 = jnp.where(is_a, x1_a_ref[...], x1_b_ref[...]) + (meta[:, 2:3] * rows_of(0) + meta[:, 3:4] * rows_of(1))
    res = _rms(x2) * nw_ref[...] if final_norm else x2

    @pl.when(is_a)
    def _():
        out_a_ref[...] = res

    @pl.when(jnp.logical_not(is_a))
    def _():
        out_b_ref[...] = res


def _combine(route, x1_a, x1_b, meta_a, meta_b, yb, norm_w, final_norm):
    tm = GATHER_TILE
    tiles_a, tiles_b = x1_a.shape[0] // tm, x1_b.shape[0] // tm
    assert tiles_a * tm == x1_a.shape[0] and tiles_b * tm == x1_b.shape[0]
    of_a = lambda n: pl.BlockSpec((tm, n), lambda i, *_: (jnp.minimum(i, tiles_a - 1), 0))
    of_b = lambda n: pl.BlockSpec((tm, n), lambda i, *_: (jnp.maximum(i - tiles_a, 0), 0))
    return pl.pallas_call(
        functools.partial(_combine_body, tm=tm, tiles_a=tiles_a, final_norm=final_norm),
        grid_spec=pltpu.PrefetchScalarGridSpec(
            num_scalar_prefetch=5, grid=(tiles_a + tiles_b,),
            in_specs=[of_a(D_MODEL), of_b(D_MODEL), of_a(LANE), of_b(LANE),
                      pl.BlockSpec(memory_space=pl.ANY),
                      pl.BlockSpec((1, D_MODEL), lambda i, *_: (0, 0))],
            out_specs=[of_a(D_MODEL), of_b(D_MODEL)],
            scratch_shapes=[pltpu.VMEM((2, 2, tm * SUBLANE, LANE), F32), pltpu.SemaphoreType.DMA((2,))]),
        out_shape=[jax.ShapeDtypeStruct(x1_a.shape, F32), jax.ShapeDtypeStruct(x1_b.shape, F32)],
        compiler_params=_params("arbitrary"),
        name="moe_combine",
    )(*route, x1_a, x1_b, meta_a, meta_b, yb, norm_w)


def _moe(stream_a, stream_b, counts, layer, w1, w3, w2, norm_w, final_norm):
    (x1_a, h2_a, meta_a), (x1_b, h2_b, meta_b) = stream_a, stream_b
    t = x1_a.shape[0] + x1_b.shape[0]
    n_blocks = (2 * t + N_EXPERTS * (MOE_BLK - 1)) // MOE_BLK
    col = lambda c: jnp.concatenate([meta_a[:, c], meta_b[:, c]]).astype(I32)
    e1, e2, r1, r2 = col(0), col(1), col(4), col(5)
    cnt = counts[0, :N_EXPERTS].astype(I32)
    padded = (cnt + MOE_BLK - 1) // MOE_BLK * MOE_BLK
    pad_end = jnp.cumsum(padded)
    pad_start = pad_end - padded
    route = (e1, e2, r1, r2, pad_start)
    nb_used = (pad_end[-1:] // MOE_BLK).astype(I32)
    blk_start = jnp.arange(n_blocks, dtype=I32) * MOE_BLK
    block_e = jnp.minimum(jnp.sum(blk_start[:, None] >= pad_end[None, :], axis=1), N_EXPERTS - 1).astype(I32)
    n_rows = n_blocks * MOE_BLK
    zero_start = jnp.concatenate([pad_start + cnt, pad_end[-1:]]).astype(I32)
    zero_len = jnp.concatenate([padded - cnt, n_rows - pad_end[-1:]]).astype(I32)
    xb = _scatter(route, zero_start, zero_len, h2_a, h2_b, n_rows)
    yb = _experts(block_e, nb_used, xb, layer, w1, w3, w2, n_blocks)
    return _combine(route, x1_a, x1_b, meta_a, meta_b, yb, norm_w, final_norm)


def _rope_tables(pos):
    half = RET_DK // 2
    inv = ROPE_BASE ** (-jnp.arange(half, dtype=F32) / half)
    ang = pos[:, None] * inv[None, :]
    cos, sin = jnp.cos(ang), jnp.sin(ang)
    cos_t = jnp.tile(jnp.concatenate([cos, cos], axis=1), (1, RET_HEADS))
    sin_t = jnp.tile(jnp.concatenate([-sin, sin], axis=1), (1, RET_HEADS))
    return cos_t, sin_t


def _block_diag(blocks):
    g, r, c = blocks.shape
    eye = jnp.eye(g, dtype=bool)
    return jnp.where(eye[:, None, :, None], blocks[:, :, None, :], 0).reshape(g * r, g * c)


def _layer_weights(l, w_in, s5, s5_b_bar, s5_c_re, s5_c_im, s5_d, s5_glu_w, s5_glu_b, gdn_conv_w, gdn_a_log,
                   gdn_dt_bias, gdn_norm_w, w_out, router_group_w, router_group_b, router_expert_w,
                   router_expert_b):
    wi = w_in[l]
    g0 = RET_IN + S5_IN
    w_ab = jnp.pad(wi[:, g0 + GDN_QKV + GDN_HD:], ((0, 0), (0, LANE - 2 * GDN_HEADS)))
    in_ws = [wi[:, :RET_IN], wi[:, g0:g0 + GDN_QKV], wi[:, RET_IN:g0], wi[:, g0 + GDN_QKV:g0 + GDN_QKV + GDN_HD],
             w_ab]
    in_ws = [w.astype(BF16) for w in in_ws]
    ab_re, ab_im, bb_re, bb_im = s5
    n = S5_LANES
    sl = slice(l * n, (l + 1) * n)
    ab = jnp.concatenate([ab_re[sl].reshape(1, n), ab_im[sl].reshape(1, n)], axis=0)
    to_bd = lambda m: _block_diag(jnp.swapaxes(m[sl].reshape(S5_GROUPS, S5_STATE, S5_GROUP_CH), 1, 2))
    wb = jnp.concatenate([to_bd(bb_re), to_bd(bb_im)], axis=1).astype(BF16)
    wc_re = _block_diag(jnp.swapaxes(s5_c_re[l], 1, 2)).astype(BF16)
    wc_im = _block_diag(jnp.swapaxes(s5_c_im[l], 1, 2)).astype(BF16)
    s5_ws = (ab, wb, wc_re, wc_im, s5_d[l].reshape(1, S5_CH), s5_glu_w[l].astype(BF16),
             s5_glu_b[l].reshape(1, S5_CH))
    pad4 = lambda v: jnp.pad(v.reshape(1, GDN_HEADS), ((0, 0), (0, LANE - GDN_HEADS)))
    gdn_ws = (gdn_conv_w[l], pad4(gdn_a_log[l]), pad4(gdn_dt_bias[l]),
              jnp.tile(gdn_norm_w[l], GDN_HEADS).reshape(1, GDN_HD))
    wo = w_out[l].astype(BF16)
    out_ws = (wo[:RET_VD], wo[RET_VD:RET_VD + S5_CH], wo[RET_VD + S5_CH:])
    wr = jnp.pad(jnp.concatenate([router_expert_w[l], router_group_w[l]], axis=1),
                 ((0, 0), (0, LANE - N_EXPERTS - MOE_GROUPS))).astype(BF16)
    br = jnp.pad(jnp.concatenate([router_expert_b[l], router_group_b[l]]),
                 (0, LANE - N_EXPERTS - MOE_GROUPS)).reshape(1, LANE)
    return in_ws, s5_ws, gdn_ws, out_ws, (wr, br)


def _mix_and_route(l, x, bsz, seq, rope, states, layer_w, norm_mix, norm_ffn, cnt0):
    t = bsz * seq
    ret_s, s5_re, s5_im, gdn_s, gdn_buf = states
    in_ws, s5_ws, gdn_ws, out_ws, (wr, br) = layer_w
    zr, zq, zs, zg, zab = _inproj(x, norm_mix[l].reshape(1, D_MODEL), in_ws)
    o_ret, ret_fin = _retention(zr, *rope, ret_s[l].reshape(bsz, RET_QD, RET_DV), bsz, seq)
    u_tm = jnp.swapaxes(zs.reshape(bsz, seq, S5_CH), 0, 1)
    o_s5_tm, re_fin, im_fin = _s5(u_tm, s5_re[l].reshape(bsz, S5_LANES), s5_im[l].reshape(bsz, S5_LANES), *s5_ws)
    o_s5 = jnp.swapaxes(o_s5_tm, 0, 1).reshape(t, S5_CH)
    buf0 = jnp.pad(gdn_buf[l], ((0, 0), (SUBLANE - (GDN_CONV - 1), 0), (0, 0)))
    o_gdn, nbuf, gdn_fin = _gdn(zq, zg, zab, *gdn_ws, buf0, gdn_s[l].reshape(bsz, GDN_HD, GDN_DV), bsz, seq)
    x1, h2_tiles, meta, counts = _outproj(x, o_ret, o_s5, o_gdn, *out_ws, norm_ffn[l].reshape(1, D_MODEL), wr, br,
                                          cnt0)
    new_states = (ret_fin.reshape(bsz, RET_HEADS, RET_DK, RET_DV), re_fin.reshape(bsz, S5_GROUPS, S5_STATE),
                  im_fin.reshape(bsz, S5_GROUPS, S5_STATE), gdn_fin.reshape(bsz, GDN_HEADS, GDN_DK, GDN_DV),
                  nbuf[:, SUBLANE - (GDN_CONV - 1):, :])
    return (x1, h2_tiles, meta), counts, new_states


def _trunks(xs, positions, states, layer_ws, norm_mix, norm_ffn, experts, norm_final):
    shapes = [x.shape[:2] for x in xs]
    ropes = [_rope_tables(pos) for pos in positions]
    xs = [x.reshape(b * s, D_MODEL) for x, (b, s) in zip(xs, shapes)]
    outs = [[[] for _ in range(5)] for _ in xs]
    for l in range(DEPTH):
        counts = jnp.zeros((SUBLANE, LANE), F32)
        routed = []
        for i, x in enumerate(xs):
            stream, counts, new_states = _mix_and_route(l, x, *shapes[i], ropes[i], states[i], layer_ws[l],
                                                        norm_mix, norm_ffn, counts)
            routed.append(stream)
            for lst, s in zip(outs[i], new_states):
                lst.append(s)
        xs = _moe(*routed, counts, l, *experts, norm_final.reshape(1, D_MODEL), l == DEPTH - 1)
    ys = [x.reshape(b, s, D_MODEL) for x, (b, s) in zip(xs, shapes)]
    return ys, [tuple(jnp.stack(o) for o in per_stream) for per_stream in outs]


def kernel(x_prompt, x_sample, state_ret, state_s5_re, state_s5_im, state_gdn, state_gdn_conv, norm_mix, w_in, s5_a_re, s5_a_im, s5_log_dt, s5_b_re, s5_b_im, s5_c_re, s5_c_im, s5_d, s5_glu_w, s5_glu_b, gdn_conv_w, gdn_a_log, gdn_dt_bias, gdn_norm_w, w_out, norm_ffn, router_group_w, router_group_b, router_expert_w, router_expert_b, expert_w1, expert_w3, expert_w2, norm_final):
    n = DEPTH * S5_LANES
    col = lambda a: a.reshape(n, 1)
    log_dt = jnp.broadcast_to(s5_log_dt[:, :, None], (DEPTH, S5_GROUPS, S5_STATE))
    s5 = _s5_prep(col(s5_a_re), col(s5_a_im), col(log_dt), s5_b_re.reshape(n, S5_GROUP_CH),
                  s5_b_im.reshape(n, S5_GROUP_CH))
    layer_ws = [_layer_weights(l, w_in, s5, None, s5_c_re, s5_c_im, s5_d, s5_glu_w, s5_glu_b, gdn_conv_w,
                               gdn_a_log, gdn_dt_bias, gdn_norm_w, w_out, router_group_w, router_group_b,
                               router_expert_w, router_expert_b) for l in range(DEPTH)]
    experts = (expert_w1, expert_w3, expert_w2)
    bp, lp, _ = x_prompt.shape
    zero_states = (jnp.zeros((DEPTH, bp, RET_HEADS, RET_DK, RET_DV), F32),
                   jnp.zeros((DEPTH, bp, S5_GROUPS, S5_STATE), F32),
                   jnp.zeros((DEPTH, bp, S5_GROUPS, S5_STATE), F32),
                   jnp.zeros((DEPTH, bp, GDN_HEADS, GDN_DK, GDN_DV), F32),
                   jnp.zeros((DEPTH, bp, GDN_CONV - 1, GDN_QKV), F32))
    sample_states = (state_ret, state_s5_re, state_s5_im, state_gdn, state_gdn_conv)
    positions = (jnp.arange(lp, dtype=F32), PAST_LEN + jnp.arange(x_sample.shape[1], dtype=F32))
    (y_p, y_s), (p_states, s_states) = _trunks((x_prompt, x_sample), positions, (zero_states, sample_states),
                                               layer_ws, norm_mix, norm_ffn, experts, norm_final)
    return (y_p, y_s) + p_states + s_states
```

```python
import functools
import math

import jax
import jax.numpy as jnp
import numpy as np
from jax import lax
from jax.experimental import pallas as pl
from jax.experimental.pallas import tpu as pltpu

F32 = jnp.float32
BF16 = jnp.bfloat16
I32 = jnp.int32

D_MODEL = 1024
DEPTH = 2
RET_HEADS, RET_DK, RET_DV = 4, 64, 128
S5_GROUPS, S5_GROUP_CH, S5_STATE = 16, 16, 64
S5_CH = S5_GROUPS * S5_GROUP_CH
S5_LANES = S5_GROUPS * S5_STATE
GDN_HEADS, GDN_DK, GDN_DV, GDN_CONV = 4, 64, 64, 4
GDN_HD = GDN_HEADS * GDN_DK
GDN_QKV = 3 * GDN_HD
RET_QD = RET_HEADS * RET_DK
RET_VD = RET_HEADS * RET_DV
RET_IN = 2 * RET_QD + 2 * RET_VD
S5_IN = S5_CH
GDN_IN = GDN_QKV + GDN_HD + 2 * GDN_HEADS
GDN_CHUNK = 64
RET_CHUNK = 64
MOE_GROUPS, EXPERTS_PER_GROUP = 4, 8
N_EXPERTS = MOE_GROUPS * EXPERTS_PER_GROUP
D_EXPERT = 512
ROPE_BASE = 10000.0
EPS = 1e-6
PAST_LEN = 16384

LANE = 128
SUBLANE = 8
ROW_TILE = 512
MOE_BLK = 256
GATHER_TILE = 256
ISSUE_UNROLL = 8
SEQS_PER_STEP = 8
LONG_SEQS_PER_STEP = 2
VMEM_LIMIT = 56 * 1024 * 1024


def _params(*sem):
    return pltpu.CompilerParams(dimension_semantics=sem, vmem_limit_bytes=VMEM_LIMIT)


def _dot(a, b):
    return jnp.dot(a.astype(BF16), b.astype(BF16), preferred_element_type=F32)


def _dot_nt(a, b):
    return lax.dot_general(a.astype(BF16), b.astype(BF16), (((1,), (1,)), ((), ())),
                           preferred_element_type=F32)


def _dot_tn(a, b):
    return lax.dot_general(a.astype(BF16), b.astype(BF16), (((0,), (0,)), ((), ())),
                           preferred_element_type=F32)


def _dot_hi(a, b):
    a1 = a.astype(BF16)
    a2 = (a - a1.astype(F32)).astype(BF16)
    b1 = b.astype(BF16)
    b2 = (b - b1.astype(F32)).astype(BF16)
    d = lambda x, y: jnp.dot(x, y, preferred_element_type=F32)
    return d(a1, b1) + (d(a1, b2) + d(a2, b1))


def _split3(x):
    p1 = x.astype(BF16)
    r1 = x - p1.astype(F32)
    p2 = r1.astype(BF16)
    p3 = (r1 - p2.astype(F32)).astype(BF16)
    return p1, p2, p3


def _dot_sel_l(sel, x):
    p1, p2, p3 = _split3(x)
    d = lambda p: jnp.dot(sel, p, preferred_element_type=F32)
    return d(p1) + d(p2) + d(p3)


def _dot_sel_r(x, sel):
    p1, p2, p3 = _split3(x)
    d = lambda p: jnp.dot(p, sel, preferred_element_type=F32)
    return d(p1) + d(p2) + d(p3)


def _rms(x):
    return x * lax.rsqrt(jnp.mean(x * x, axis=-1, keepdims=True) + EPS)


def _silu(x):
    return x * jax.nn.sigmoid(x)


def _inproj_body(x_ref, nw_ref, wr_ref, wq_ref, ws_ref, wg_ref, wab_ref,
                 zr_ref, zq_ref, zs_ref, zg_ref, zab_ref):
    h = (_rms(x_ref[...]) * nw_ref[...]).astype(BF16)
    for w_ref, z_ref in ((wr_ref, zr_ref), (wq_ref, zq_ref), (ws_ref, zs_ref),
                         (wg_ref, zg_ref), (wab_ref, zab_ref)):
        z_ref[...] = jnp.dot(h, w_ref[...], preferred_element_type=F32)


def _inproj(x, nw, ws):
    t = x.shape[0]
    tm = min(ROW_TILE, t)
    widths = [w.shape[1] for w in ws]
    row = lambda n: pl.BlockSpec((tm, n), lambda i: (i, 0))
    full = lambda a: pl.BlockSpec(a.shape, lambda i: (0, 0))
    return pl.pallas_call(
        _inproj_body,
        grid=(t // tm,),
        in_specs=[row(D_MODEL), full(nw)] + [full(w) for w in ws],
        out_specs=[row(n) for n in widths],
        out_shape=[jax.ShapeDtypeStruct((t, n), F32) for n in widths],
        compiler_params=_params("parallel"),
        name="inproj",
    )(x, nw, *ws)


def _seqs_per_step(bsz, blocks_per_seq):
    want = SEQS_PER_STEP if blocks_per_seq == 1 else LONG_SEQS_PER_STEP
    return want if bsz % want == 0 else 1


def _ret_log_gamma(h):
    return math.log(1.0 - 2.0 ** (-5.0 - h))


def _ret_body(zr_ref, cos_ref, sin_ref, s0_ref, o_ref, sfin_ref, s_scr, *, nseq, rows, chunk):
    j = pl.program_id(1)
    stack = RET_HEADS * chunk

    @pl.when(j == 0)
    def _():
        s_scr[...] = s0_ref[...]

    lane = lax.broadcasted_iota(I32, (rows, RET_QD), 1)
    first_half = (lane % RET_DK) < (RET_DK // 2)

    def rotary(x):
        swapped = jnp.where(first_half, pltpu.roll(x, RET_QD - RET_DK // 2, 1),
                            pltpu.roll(x, RET_DK // 2, 1))
        return x * cos_ref[...] + swapped * sin_ref[...]

    q = [rotary(zr_ref[s, :, 0:RET_QD]) for s in range(nseq)]
    k = [rotary(zr_ref[s, :, RET_QD:2 * RET_QD]) * (RET_DK ** -0.5) for s in range(nseq)]

    def by_head(idx, fn):
        out = jnp.zeros(idx.shape, F32)
        for h in range(RET_HEADS):
            out = jnp.where(idx == h, fn(h), out)
        return out

    lg_lane = by_head(lax.broadcasted_iota(I32, (chunk, RET_QD), 1) // RET_DK, _ret_log_gamma)
    pos = lax.broadcasted_iota(I32, (chunk, RET_QD), 0).astype(F32)
    q_scale = jnp.exp((pos + 1.0) * lg_lane)
    k_scale = jnp.exp((chunk - 1.0 - pos) * lg_lane)
    st_row = lax.broadcasted_iota(I32, (stack, stack), 0)
    st_col = lax.broadcasted_iota(I32, (stack, stack), 1)
    causal = ((st_row // chunk) == (st_col // chunk)) & (st_row >= st_col)
    lg_stack = by_head(st_row // chunk, _ret_log_gamma)
    decay = jnp.where(causal, jnp.exp(jnp.where(causal, (st_row - st_col).astype(F32), 0.0) * lg_stack), 0.0)
    q_rows = (lax.broadcasted_iota(I32, (stack, RET_QD), 0) // chunk) == \
             (lax.broadcasted_iota(I32, (stack, RET_QD), 1) // RET_DK)
    v_rows = (lax.broadcasted_iota(I32, (stack, RET_VD), 0) // chunk) == \
             (lax.broadcasted_iota(I32, (stack, RET_VD), 1) // RET_DV)
    s_row_head = lax.broadcasted_iota(I32, (RET_QD, RET_DV), 0) // RET_DK
    s_decay = by_head(s_row_head, lambda h: math.exp(chunk * _ret_log_gamma(h)))
    kv_diag = (lax.broadcasted_iota(I32, (RET_QD, RET_VD), 0) // RET_DK) == \
              (lax.broadcasted_iota(I32, (RET_QD, RET_VD), 1) // RET_DV)

    def tile4(a):
        return jnp.concatenate([a] * RET_HEADS, axis=0)

    def collapse(a):
        out = a[0:chunk]
        for h in range(1, RET_HEADS):
            out = out + a[h * chunk:(h + 1) * chunk]
        return out

    seqs = range(nseq)
    n_chunks = rows // chunk
    state = [s_scr[s] for s in seqs]
    for ci in range(n_chunks):
        r = slice(ci * chunk, (ci + 1) * chunk)
        v_cs = [zr_ref[s, r, 2 * RET_QD:2 * RET_QD + RET_VD] for s in seqs]
        scores = [_dot_nt(jnp.where(q_rows, tile4(q[s][r]), 0.0), tile4(k[s][r])) * decay for s in seqs]
        o_intra = [collapse(_dot(scores[s], jnp.where(v_rows, tile4(v_cs[s]), 0.0))) for s in seqs]
        s_full = [jnp.concatenate([jnp.where(s_row_head == h, state[s], 0.0) for h in range(RET_HEADS)], axis=1)
                  for s in seqs]
        o_cs = [o_intra[s] + _dot(q[s][r] * q_scale, s_full[s]) for s in seqs]
        for s in seqs:
            gate = zr_ref[s, r, 2 * RET_QD + RET_VD:2 * RET_QD + 2 * RET_VD]
            for h in range(RET_HEADS):
                sl = slice(h * RET_DV, (h + 1) * RET_DV)
                o_ref[s, r, sl] = _rms(o_cs[s][:, sl]) * _silu(gate[:, sl])
        for s in seqs:
            kv = jnp.where(kv_diag, _dot_tn(k[s][r] * k_scale, v_cs[s]), 0.0)
            kv_own = kv[:, 0:RET_DV]
            for h in range(1, RET_HEADS):
                kv_own = kv_own + kv[:, h * RET_DV:(h + 1) * RET_DV]
            state[s] = state[s] * s_decay + kv_own
    for s in seqs:
        s_scr[s] = state[s]

    @pl.when(j == pl.num_programs(1) - 1)
    def _():
        for s in seqs:
            sfin_ref[s] = state[s]


def _retention(zr, cos, sin, s0, bsz, seq):
    chunk = min(RET_CHUNK, seq)
    rows = min(4 * chunk, seq)
    nj = seq // rows
    nseq = _seqs_per_step(bsz, nj)
    o, s_fin = pl.pallas_call(
        functools.partial(_ret_body, nseq=nseq, rows=rows, chunk=chunk),
        grid=(bsz // nseq, nj),
        in_specs=[pl.BlockSpec((nseq, rows, RET_IN), lambda b, j: (b, j, 0)),
                  pl.BlockSpec((rows, RET_QD), lambda b, j: (j, 0)),
                  pl.BlockSpec((rows, RET_QD), lambda b, j: (j, 0)),
                  pl.BlockSpec((nseq, RET_QD, RET_DV), lambda b, j: (b, 0, 0))],
        out_specs=[pl.BlockSpec((nseq, rows, RET_VD), lambda b, j: (b, j, 0)),
                   pl.BlockSpec((nseq, RET_QD, RET_DV), lambda b, j: (b, 0, 0))],
        out_shape=[jax.ShapeDtypeStruct((bsz, seq, RET_VD), F32),
                   jax.ShapeDtypeStruct((bsz, RET_QD, RET_DV), F32)],
        scratch_shapes=[pltpu.VMEM((nseq, RET_QD, RET_DV), F32)],
        compiler_params=_params("parallel", "arbitrary"),
        name="retention",
    )(zr.reshape(bsz, seq, RET_IN), cos, sin, s0)
    return o.reshape(bsz * seq, RET_VD), s_fin


def _s5_prep_body(are_ref, aim_ref, ldt_ref, bre_ref, bim_ref, abre_ref, abim_ref, bbre_ref, bbim_ref):
    lam_re, lam_im = are_ref[...], aim_ref[...]
    dt = jnp.exp(ldt_ref[...])
    mag = jnp.exp(lam_re * dt)
    ab_re = mag * jnp.cos(lam_im * dt)
    ab_im = mag * jnp.sin(lam_im * dt)
    den = lam_re * lam_re + lam_im * lam_im
    f_re = ((ab_re - 1.0) * lam_re + ab_im * lam_im) / den
    f_im = (ab_im * lam_re - (ab_re - 1.0) * lam_im) / den
    abre_ref[...] = ab_re
    abim_ref[...] = ab_im
    bbre_ref[...] = f_re * bre_ref[...] - f_im * bim_ref[...]
    bbim_ref[...] = f_re * bim_ref[...] + f_im * bre_ref[...]


def _s5_prep(a_re, a_im, log_dt, b_re, b_im):
    n = a_re.shape[0]
    col = jax.ShapeDtypeStruct((n, 1), F32)
    mat = jax.ShapeDtypeStruct((n, S5_GROUP_CH), F32)
    return pl.pallas_call(_s5_prep_body, out_shape=[col, col, mat, mat], name="s5_prep")(
        a_re, a_im, log_dt, b_re, b_im)


def _gelu_tanh(x):
    return x * (0.5 * (1.0 + jnp.tanh(math.sqrt(2.0 / math.pi) * (x + 0.044715 * (x * x * x)))))


def _s5_body(u_ref, h0re_ref, h0im_ref, ab_ref, wb_ref, wcre_ref, wcim_ref, d_ref, gw_ref, gb_ref,
             o_ref, hre_ref, him_ref, bu_scr, st_scr, *, steps):
    j = pl.program_id(1)

    @pl.when(j == 0)
    def _():
        st_scr[0] = h0re_ref[...]
        st_scr[1] = h0im_ref[...]

    rows = steps * SUBLANE
    u = u_ref[...].reshape(rows, S5_CH)
    bu_scr[...] = _dot(u, wb_ref[...])
    a_re = jnp.broadcast_to(ab_ref[0:1, :], (SUBLANE, S5_LANES))
    a_im = jnp.broadcast_to(ab_ref[1:2, :], (SUBLANE, S5_LANES))

    def step(t, carry):
        h_re, h_im = carry
        r = pl.ds(pl.multiple_of(t * SUBLANE, SUBLANE), SUBLANE)
        n_re = a_re * h_re - a_im * h_im + bu_scr[r, 0:S5_LANES]
        n_im = a_re * h_im + a_im * h_re + bu_scr[r, S5_LANES:2 * S5_LANES]
        bu_scr[r, 0:S5_LANES] = n_re
        bu_scr[r, S5_LANES:2 * S5_LANES] = n_im
        return n_re, n_im

    h_re, h_im = lax.fori_loop(0, steps, step, (st_scr[0], st_scr[1]))
    st_scr[0] = h_re
    st_scr[1] = h_im
    hre_ref[...] = h_re
    him_ref[...] = h_im

    y = _dot(bu_scr[:, 0:S5_LANES], wcre_ref[...]) - _dot(bu_scr[:, S5_LANES:2 * S5_LANES], wcim_ref[...])
    y = _gelu_tanh(y + d_ref[...] * u)
    y = y * jax.nn.sigmoid(_dot(y, gw_ref[...]) + gb_ref[...])
    o_ref[...] = y.reshape(steps, SUBLANE, S5_CH)


def _s5(u_tm, h0_re, h0_im, ab, wb, wc_re, wc_im, d_skip, glu_w, glu_b):
    seq, bsz, _ = u_tm.shape
    steps = min(64, seq)
    full = lambda a: pl.BlockSpec(a.shape, lambda g, j: (0,) * a.ndim)
    st = pl.BlockSpec((SUBLANE, S5_LANES), lambda g, j: (g, 0))
    return pl.pallas_call(
        functools.partial(_s5_body, steps=steps),
        grid=(bsz // SUBLANE, seq // steps),
        in_specs=[pl.BlockSpec((steps, SUBLANE, S5_CH), lambda g, j: (j, g, 0)), st, st,
                  full(ab), full(wb), full(wc_re), full(wc_im), full(d_skip), full(glu_w), full(glu_b)],
        out_specs=[pl.BlockSpec((steps, SUBLANE, S5_CH), lambda g, j: (j, g, 0)), st, st],
        out_shape=[jax.ShapeDtypeStruct((seq, bsz, S5_CH), F32),
                   jax.ShapeDtypeStruct((bsz, S5_LANES), F32),
                   jax.ShapeDtypeStruct((bsz, S5_LANES), F32)],
        scratch_shapes=[pltpu.VMEM((steps * SUBLANE, 2 * S5_LANES), F32),
                        pltpu.VMEM((2, SUBLANE, S5_LANES), F32)],
        compiler_params=_params("parallel", "arbitrary"),
        name="s5",
    )(u_tm, h0_re, h0_im, ab, wb, wc_re, wc_im, d_skip, glu_w, glu_b)


def _gdn_body(zq_ref, zg_ref, zab_ref, cw_ref, alog_ref, dtb_ref, nw_ref, buf0_ref, s0_ref,
              ones_bd_ref, tri_ref, ea_ref, eb_ref, ec_ref, spread_ref, gather_ref,
              o_ref, nbuf_ref, sfin_ref,
              ext_scr, q_scr, k_scr, kb_scr, vb_scr, g_scr, g4_scr, s_scr, *, nseq, rows, chunk):
    j = pl.program_id(1)
    stack = GDN_HEADS * chunk
    bd_state = (lax.broadcasted_iota(I32, (GDN_HD, GDN_HD), 0) // GDN_DK) == \
               (lax.broadcasted_iota(I32, (GDN_HD, GDN_HD), 1) // GDN_DV)

    @pl.when(j == 0)
    def _():
        for s in range(nseq):
            ext_scr[s, 0:SUBLANE, :] = buf0_ref[s]
            s_scr[s] = jnp.where(bd_state, _dot_sel_r(s0_ref[s], spread_ref[...]), 0.0)

    convs = []
    for s in range(nseq):
        ext_scr[s, SUBLANE:SUBLANE + rows, :] = zq_ref[s]
        conv = ext_scr[s, SUBLANE - 3:SUBLANE - 3 + rows, :] * cw_ref[0:1, :]
        for i in range(1, GDN_CONV):
            conv = conv + ext_scr[s, SUBLANE - 3 + i:SUBLANE - 3 + i + rows, :] * cw_ref[i:i + 1, :]
        tail = ext_scr[s, rows:rows + SUBLANE, :]
        nbuf_ref[s] = tail
        ext_scr[s, 0:SUBLANE, :] = tail
        convs.append(conv)
    qkv = _silu(convs[0] if nseq == 1 else jnp.concatenate(convs, axis=0))

    ones_bd = ones_bd_ref[...]
    q_raw = qkv[:, 0:GDN_HD]
    k_raw = qkv[:, GDN_HD:2 * GDN_HD]
    q_scr[...] = q_raw * lax.rsqrt(_dot_sel_r(q_raw * q_raw, ones_bd) + EPS) * (GDN_DK ** -0.5)
    k_n = k_raw * lax.rsqrt(_dot_sel_r(k_raw * k_raw, ones_bd) + EPS)
    k_scr[...] = k_n

    ab = zab_ref[...].reshape(nseq * rows, LANE)
    x = ab + dtb_ref[...]
    softplus = jnp.maximum(x, 0.0) + jnp.log1p(jnp.exp(-jnp.abs(x)))
    g_pad = -jnp.exp(alog_ref[...]) * softplus
    beta = _dot_sel_r(jax.nn.sigmoid(ab), eb_ref[...])
    g_cum = _dot_sel_l(tri_ref[...], g_pad)
    g_scr[...] = _dot_sel_r(g_cum, ea_ref[...])
    g4_scr[...] = _dot_sel_r(g_pad, ec_ref[...])
    kb_scr[...] = k_n * beta
    vb_scr[...] = qkv[:, 2 * GDN_HD:3 * GDN_HD] * beta

    w_row = lax.broadcasted_iota(I32, (chunk, stack), 0)
    w_col = lax.broadcasted_iota(I32, (chunk, stack), 1) % chunk
    strict_w = w_row > w_col
    causal_w = w_row >= w_col
    same_head = (lax.broadcasted_iota(I32, (stack, stack), 0) // chunk) == \
                (lax.broadcasted_iota(I32, (stack, stack), 1) // chunk)
    head_rows = (lax.broadcasted_iota(I32, (stack, GDN_HD), 0) // chunk) == \
                (lax.broadcasted_iota(I32, (stack, GDN_HD), 1) // GDN_DK)
    head_rows2 = jnp.concatenate([head_rows, head_rows], axis=1)
    tri_c = tri_ref[0:chunk, 0:chunk]
    packed_ok = chunk % (2 * SUBLANE) == 0

    def tile4(a):
        return jnp.concatenate([a] * GDN_HEADS, axis=0)

    def split2(a):
        hi = a.astype(BF16)
        return hi, (a - hi.astype(F32)).astype(BF16)

    def stack_masked(parts, mask):
        if packed_ok:
            return [jnp.where(mask, tile4(p), jnp.zeros((), BF16)) for p in parts]
        return [jnp.where(mask, tile4(p.astype(F32)), 0.0).astype(BF16) for p in parts]

    def on_diag(parts):
        return stack_masked(parts, same_head)

    def mm(a, b):
        return jnp.dot(a, b, preferred_element_type=F32)

    def mm_hi(a_parts, b_parts):
        return mm(a_parts[0], b_parts[0]) + (mm(a_parts[0], b_parts[1]) + mm(a_parts[1], b_parts[0]))

    n_chunks = rows // chunk
    every = range(nseq * n_chunks)
    rows_of = lambda u: slice(u * chunk, (u + 1) * chunk)
    solved = []
    for u in every:
        r = rows_of(u)
        q_c, k_c, kb_c, vb_c, g_c = q_scr[r, :], k_scr[r, :], kb_scr[r, :], vb_scr[r, :], g_scr[r, :]
        exp_g = jnp.exp(g_c)
        g_diff = _dot_sel_l(tri_c, jnp.where(strict_w, g4_scr[r, :], 0.0))
        decay = jnp.exp(jnp.where(causal_w, g_diff, 0.0))
        k_heads = jnp.where(head_rows, tile4(k_c), 0.0)
        lmat = jnp.where(strict_w, _dot_nt(kb_c, k_heads) * decay, 0.0)
        attn = jnp.where(causal_w, _dot_nt(q_c, k_heads) * decay, 0.0)
        rhs_c = jnp.concatenate([vb_c, kb_c * exp_g], axis=1)
        solved.append((lmat, attn, exp_g, rhs_c))

    t_acc = [-solved[ci][0] for ci in every]
    p_parts = [split2(solved[ci][0]) for ci in every]
    p_diag = [on_diag(p) for p in p_parts]
    span = 2
    while span < chunk + 1:
        power = [mm_hi(p_parts[ci], p_diag[ci]) for ci in every]
        p_parts = [split2(p) for p in power]
        p_diag = [on_diag(p) for p in p_parts]
        t_acc = [t_acc[ci] + power[ci] + mm_hi(split2(t_acc[ci]), p_diag[ci]) for ci in every]
        span *= 2
    rhs_parts = [stack_masked(split2(solved[ci][3]), head_rows2) for ci in every]
    sols = [solved[ci][3] + mm_hi(split2(t_acc[ci]), rhs_parts[ci]) for ci in every]

    seqs = range(nseq)
    s_cur = [s_scr[s] for s in seqs]
    for ci in range(n_chunks):
        us = [s * n_chunks + ci for s in seqs]
        v_new = [sols[u][:, 0:GDN_HD] - _dot(sols[u][:, GDN_HD:2 * GDN_HD], s_cur[s]) for s, u in enumerate(us)]
        o_cs = [_dot(q_scr[rows_of(u), :] * solved[u][2], s_cur[s])
                + _dot(solved[u][1], jnp.where(head_rows, tile4(v_new[s]), 0.0)) for s, u in enumerate(us)]
        for s, u in enumerate(us):
            g_c = g_scr[rows_of(u), :]
            g_last = g_c[chunk - 1:chunk, :]
            k_dec = k_scr[rows_of(u), :] * jnp.exp(g_last - g_c)
            s_cur[s] = s_cur[s] * jnp.exp(g_last) + jnp.where(bd_state, _dot_tn(k_dec, v_new[s]), 0.0)
        for s, u in enumerate(us):
            r = slice(ci * chunk, (ci + 1) * chunk)
            ms = _dot_sel_r(o_cs[s] * o_cs[s], ones_bd) * (1.0 / GDN_DV)
            o_ref[s, r, :] = o_cs[s] * lax.rsqrt(ms + EPS) * nw_ref[...] * _silu(zg_ref[s, r, :])
    for s in seqs:
        s_scr[s] = s_cur[s]

    @pl.when(j == pl.num_programs(1) - 1)
    def _():
        for s in seqs:
            sfin_ref[s] = _dot_sel_r(s_scr[s], gather_ref[...])


def _gdn_consts(rows, chunk):
    stack = GDN_HEADS * chunk
    blk = lambda n, c: (np.arange(n)[:, None] // c) == (np.arange(n)[None, :] // c)
    ones_bd = blk(GDN_HD, GDN_DK)
    tri = blk(rows, chunk) & (np.arange(rows)[:, None] >= np.arange(rows)[None, :])
    src = np.arange(LANE)[:, None]
    ea = src == np.arange(GDN_HD)[None, :] // GDN_DK
    eb = src == GDN_HEADS + np.arange(GDN_HD)[None, :] // GDN_DK
    ec = src == np.arange(stack)[None, :] // chunk
    spread = np.arange(GDN_DV)[:, None] == np.arange(GDN_HD)[None, :] % GDN_DV
    return tuple(jnp.asarray(m, dtype=BF16) for m in (ones_bd, tri, ea, eb, ec, spread, spread.T))


def _gdn(zq, zg, zab, conv_w, alog_pad, dtb_pad, nw, buf0, s0, bsz, seq):
    chunk = min(GDN_CHUNK, seq)
    rows = min(4 * chunk, seq)
    nj = seq // rows
    nseq = _seqs_per_step(bsz, nj)
    stack = GDN_HEADS * chunk
    consts = _gdn_consts(nseq * rows, chunk)
    row = lambda n: pl.BlockSpec((nseq, rows, n), lambda b, j: (b, j, 0))
    full = lambda a: pl.BlockSpec(a.shape, lambda b, j: (0,) * a.ndim)
    per_b = lambda *s: pl.BlockSpec((nseq,) + s, lambda b, j: (b, 0, 0))
    by_seq = lambda a: a.reshape(bsz, seq, a.shape[-1])
    o, nbuf, s_fin = pl.pallas_call(
        functools.partial(_gdn_body, nseq=nseq, rows=rows, chunk=chunk),
        grid=(bsz // nseq, nj),
        in_specs=[row(GDN_QKV), row(GDN_HD), row(LANE), full(conv_w), full(alog_pad), full(dtb_pad), full(nw),
                  per_b(SUBLANE, GDN_QKV), per_b(GDN_HD, GDN_DV)] + [full(c) for c in consts],
        out_specs=[row(GDN_HD), per_b(SUBLANE, GDN_QKV), per_b(GDN_HD, GDN_DV)],
        out_shape=[jax.ShapeDtypeStruct((bsz, seq, GDN_HD), F32),
                   jax.ShapeDtypeStruct((bsz, SUBLANE, GDN_QKV), F32),
                   jax.ShapeDtypeStruct((bsz, GDN_HD, GDN_DV), F32)],
        scratch_shapes=[pltpu.VMEM((nseq, rows + SUBLANE, GDN_QKV), F32)] +
                       [pltpu.VMEM((nseq * rows, GDN_HD), F32)] * 5 +
                       [pltpu.VMEM((nseq * rows, stack), F32), pltpu.VMEM((nseq, GDN_HD, GDN_HD), F32)],
        compiler_params=_params("parallel", "arbitrary"),
        name="gdn",
    )(by_seq(zq), by_seq(zg), by_seq(zab), conv_w, alog_pad, dtb_pad, nw, buf0, s0, *consts)
    return o.reshape(bsz * seq, GDN_HD), nbuf, s_fin


def _outproj_body(x_ref, oret_ref, os5_ref, ogdn_ref, w1_ref, w2_ref, w3_ref, nw_ref, wr_ref, br_ref, tri_ref,
                  cnt0_ref, x1_ref, h2_ref, meta_ref, cnt_ref, carry_scr, *, tm):
    i = pl.program_id(0)

    @pl.when(i == 0)
    def _():
        carry_scr[...] = cnt0_ref[...]

    mix = _dot(oret_ref[...], w1_ref[...]) + _dot(os5_ref[...], w2_ref[...]) + _dot(ogdn_ref[...], w3_ref[...])
    x1 = x_ref[...] + mix
    x1_ref[...] = x1
    h2 = _rms(x1) * nw_ref[...]
    for s in range(D_MODEL // LANE):
        h2_ref[pl.ds(s, tm, stride=SUBLANE), :] = h2[:, s * LANE:(s + 1) * LANE]

    logits = _dot_nt(wr_ref[...], h2) + br_ref[...]
    row_i = lax.broadcasted_iota(I32, (LANE, tm), 0)
    row = row_i.astype(F32)
    neg = -jnp.inf
    big = float(LANE)
    g_log = jnp.where((row_i >= N_EXPERTS) & (row_i < N_EXPERTS + MOE_GROUPS), logits, neg)
    g_max = jnp.max(g_log, axis=0, keepdims=True)
    grp = jnp.min(jnp.where(g_log == g_max, row - N_EXPERTS, big), axis=0, keepdims=True)
    p_grp = 1.0 / jnp.sum(jnp.exp(g_log - g_max), axis=0, keepdims=True)
    in_grp = (row >= grp * EXPERTS_PER_GROUP) & (row < (grp + 1.0) * EXPERTS_PER_GROUP)
    e_log = jnp.where(in_grp, logits, neg)
    v1 = jnp.max(e_log, axis=0, keepdims=True)
    i1 = jnp.min(jnp.where(e_log == v1, row, big), axis=0, keepdims=True)
    e_log2 = jnp.where(row == i1, neg, e_log)
    v2 = jnp.max(e_log2, axis=0, keepdims=True)
    i2 = jnp.min(jnp.where(e_log2 == v2, row, big), axis=0, keepdims=True)
    e2 = jnp.exp(v2 - v1)
    gate1 = p_grp / (1.0 + e2)
    gate2 = p_grp * e2 / (1.0 + e2)

    oh1 = row == i1
    oh2 = row == i2
    picked = jnp.where(oh1 | oh2, 1.0, 0.0)
    seen = carry_scr[:, 0:1]
    before = jnp.dot(picked.astype(BF16), tri_ref[...], preferred_element_type=F32) + seen
    rank1 = jnp.sum(jnp.where(oh1, before, 0.0), axis=0, keepdims=True)
    rank2 = jnp.sum(jnp.where(oh2, before, 0.0), axis=0, keepdims=True)
    total = seen + jnp.sum(picked, axis=1, keepdims=True)
    carry_scr[...] = jnp.broadcast_to(total, carry_scr.shape)
    cnt_ref[...] = jnp.broadcast_to(total, cnt_ref.shape)
    meta_ref[...] = jnp.concatenate([i1, i2, gate1, gate2, rank1, rank2, jnp.zeros((2, tm), F32)], axis=0)


def _outproj(x, o_ret, o_s5, o_gdn, w1, w2, w3, nw, wr, br, cnt0):
    t = x.shape[0]
    tm = min(ROW_TILE, t)
    tri = jnp.asarray(np.arange(tm)[:, None] < np.arange(tm)[None, :], dtype=BF16)
    row = lambda n: pl.BlockSpec((tm, n), lambda i: (i, 0))
    full = lambda a: pl.BlockSpec(a.shape, lambda i: (0, 0))
    return pl.pallas_call(
        functools.partial(_outproj_body, tm=tm),
        grid=(t // tm,),
        in_specs=[row(D_MODEL), row(RET_VD), row(S5_CH), row(GDN_HD),
                  full(w1), full(w2), full(w3), full(nw), full(wr), full(br), full(tri), full(cnt0)],
        out_specs=[row(D_MODEL), pl.BlockSpec((tm * SUBLANE, LANE), lambda i: (i, 0)),
                   pl.BlockSpec((SUBLANE, tm), lambda i: (0, i)), pl.BlockSpec((LANE, LANE), lambda i: (0, 0))],
        out_shape=[jax.ShapeDtypeStruct((t, D_MODEL), F32),
                   jax.ShapeDtypeStruct((t * SUBLANE, LANE), F32),
                   jax.ShapeDtypeStruct((SUBLANE, t), F32),
                   jax.ShapeDtypeStruct((LANE, LANE), F32)],
        scratch_shapes=[pltpu.VMEM((LANE, LANE), F32)],
        compiler_params=_params("arbitrary"),
        name="outproj_router",
    )(x, o_ret, o_s5, o_gdn, w1, w2, w3, nw, wr, br, tri, cnt0)


def _token_rows(ref, idx):
    return ref.at[pl.ds(pl.multiple_of(idx * SUBLANE, SUBLANE), SUBLANE)]


def _slot(e_ref, r_ref, ps_ref, g):
    return ps_ref[e_ref[g]] + r_ref[g]


ZERO_ROWS = 128


def _zero_segment(zero_scr, xb_ref, sem, start, length, wait):
    def piece(off, n):
        cp = pltpu.make_async_copy(zero_scr.at[pl.ds(0, n * SUBLANE)],
                                   xb_ref.at[pl.ds(pl.multiple_of(off * SUBLANE, SUBLANE), n * SUBLANE)], sem)
        cp.wait() if wait else cp.start()

    n_big = length // ZERO_ROWS

    def big(i, carry):
        piece(start + i * ZERO_ROWS, ZERO_ROWS)
        return carry

    lax.fori_loop(0, n_big, big, 0)
    off = start + n_big * ZERO_ROWS
    rem = length - n_big * ZERO_ROWS
    bit = ZERO_ROWS // 2
    while bit >= 1:
        has = (rem & bit) != 0
        pl.when(has)(functools.partial(piece, off, bit))
        off = off + jnp.where(has, bit, 0)
        bit //= 2


def _scatter_body(e1_ref, e2_ref, r1_ref, r2_ref, ps_ref, zs_ref, zl_ref, src_a_ref, src_b_ref, xb_ref,
                  zero_scr, sem, zsem, *, tm, tiles_a):
    i = pl.program_id(0)
    base = i * tm

    @pl.when(i == 0)
    def _():
        zero_scr[...] = jnp.zeros_like(zero_scr)
        for wait in (False, True):
            lax.fori_loop(0, N_EXPERTS + 1,
                          lambda s, c, wait=wait: (_zero_segment(zero_scr, xb_ref, zsem, zs_ref[s], zl_ref[s], wait),
                                                   c)[1], 0)

    def scatter_tile(src_ref):
        def copy(t, d):
            return pltpu.make_async_copy(_token_rows(src_ref, t), _token_rows(xb_ref, d), sem)

        def issue(t, carry):
            copy(t, _slot(e1_ref, r1_ref, ps_ref, base + t)).start(priority=0)
            copy(t, _slot(e2_ref, r2_ref, ps_ref, base + t)).start(priority=1)
            return carry

        lax.fori_loop(0, tm, issue, 0, unroll=ISSUE_UNROLL)
        for _ in range(2):
            pltpu.make_async_copy(src_ref, src_ref, sem).wait()

    pl.when(i < tiles_a)(functools.partial(scatter_tile, src_a_ref))
    pl.when(i >= tiles_a)(functools.partial(scatter_tile, src_b_ref))


def _scatter(route, zero_start, zero_len, h2_a, h2_b, n_rows):
    tm = GATHER_TILE
    tiles_a, tiles_b = h2_a.shape[0] // (tm * SUBLANE), h2_b.shape[0] // (tm * SUBLANE)
    assert tiles_a * tm * SUBLANE == h2_a.shape[0] and tiles_b * tm * SUBLANE == h2_b.shape[0]
    return pl.pallas_call(
        functools.partial(_scatter_body, tm=tm, tiles_a=tiles_a),
        grid_spec=pltpu.PrefetchScalarGridSpec(
            num_scalar_prefetch=7, grid=(tiles_a + tiles_b,),
            in_specs=[pl.BlockSpec((tm * SUBLANE, LANE), lambda i, *_: (jnp.minimum(i, tiles_a - 1), 0)),
                      pl.BlockSpec((tm * SUBLANE, LANE), lambda i, *_: (jnp.maximum(i - tiles_a, 0), 0))],
            out_specs=pl.BlockSpec(memory_space=pl.ANY),
            scratch_shapes=[pltpu.VMEM((ZERO_ROWS * SUBLANE, LANE), F32), pltpu.SemaphoreType.DMA(()),
                            pltpu.SemaphoreType.DMA(())]),
        out_shape=jax.ShapeDtypeStruct((n_rows * SUBLANE, LANE), F32),
        compiler_params=_params("arbitrary"),
        name="moe_scatter",
    )(*route, zero_start, zero_len, h2_a, h2_b)


def _experts_body(be_ref, nb_ref, xb_ref, w1_ref, w3_ref, w2_ref, yb_ref, w1_scr, w3_scr, w2_scr, *, blk):
    i = pl.program_id(0)
    live = i < nb_ref[0]

    @pl.when(live & ((i == 0) | (be_ref[i] != be_ref[jnp.maximum(i - 1, 0)])))
    def _():
        w1_scr[...] = w1_ref[...].astype(BF16)
        w3_scr[...] = w3_ref[...].astype(BF16)
        w2_scr[...] = w2_ref[...].astype(BF16)

    @pl.when(live)
    def _():
        x = jnp.concatenate([xb_ref[pl.ds(s, blk, stride=SUBLANE), :] for s in range(D_MODEL // LANE)],
                            axis=1).astype(BF16)
        hid = _silu(jnp.dot(x, w1_scr[...], preferred_element_type=F32)) * \
            jnp.dot(x, w3_scr[...], preferred_element_type=F32)
        y = _dot(hid, w2_scr[...])
        for s in range(D_MODEL // LANE):
            yb_ref[pl.ds(s, blk, stride=SUBLANE), :] = y[:, s * LANE:(s + 1) * LANE]

    @pl.when(jnp.logical_not(live))
    def _():
        yb_ref[...] = jnp.zeros_like(yb_ref)


def _experts(block_e, nb_used, xb, layer, w1, w3, w2, n_blocks):
    blk = MOE_BLK
    live = lambda i, nb: jnp.minimum(i, nb[0] - 1)
    tile_in = pl.BlockSpec((blk * SUBLANE, LANE), lambda i, be, nb: (live(i, nb), 0))
    tile_out = pl.BlockSpec((blk * SUBLANE, LANE), lambda i, be, nb: (i, 0))
    wspec = lambda a: pl.BlockSpec((None, None) + a.shape[2:],
                                   lambda i, be, nb: (layer, be[live(i, nb)], 0, 0))
    return pl.pallas_call(
        functools.partial(_experts_body, blk=blk),
        grid_spec=pltpu.PrefetchScalarGridSpec(
            num_scalar_prefetch=2, grid=(n_blocks,),
            in_specs=[tile_in, wspec(w1), wspec(w3), wspec(w2)], out_specs=tile_out,
            scratch_shapes=[pltpu.VMEM(w1.shape[2:], BF16), pltpu.VMEM(w3.shape[2:], BF16),
                            pltpu.VMEM(w2.shape[2:], BF16)]),
        out_shape=jax.ShapeDtypeStruct(xb.shape, F32),
        compiler_params=_params("arbitrary"),
        name="moe_experts",
    )(block_e, nb_used, xb, w1, w3, w2)


def _combine_body(*refs, tm, tiles_a, final_norm, n_proj):
    e1_ref, e2_ref, r1_ref, r2_ref, ps_ref, x1_a_ref, x1_b_ref, meta_a_ref, meta_b_ref, yb_ref, nw_ref = refs[:11]
    w_refs = refs[11:11 + n_proj]
    out_a_ref, out_b_ref = refs[11 + n_proj:13 + n_proj]
    za_refs = refs[13 + n_proj:13 + 2 * n_proj]
    zb_refs = refs[13 + 2 * n_proj:13 + 3 * n_proj]
    buf_ref, sem = refs[-2:]
    i = pl.program_id(0)
    phase = i % 2

    def gather(tile, ph):
        base = tile * tm

        def issue(t, carry):
            for k, (e_ref, r_ref) in enumerate(((e1_ref, r1_ref), (e2_ref, r2_ref))):
                pltpu.make_async_copy(_token_rows(yb_ref, _slot(e_ref, r_ref, ps_ref, base + t)),
                                      _token_rows(buf_ref.at[ph, k], t), sem.at[ph]).start(priority=k)
            return carry

        lax.fori_loop(0, tm, issue, 0, unroll=ISSUE_UNROLL)

    @pl.when(i == 0)
    def _():
        gather(0, 0)

    @pl.when(i + 1 < pl.num_programs(0))
    def _():
        gather(i + 1, 1 - phase)

    for k in range(2):
        pltpu.make_async_copy(buf_ref.at[phase, k], buf_ref.at[phase, k], sem.at[phase]).wait()

    def rows_of(slot):
        return jnp.concatenate([buf_ref[phase, slot, pl.ds(s, tm, stride=SUBLANE), :]
                                for s in range(D_MODEL // LANE)], axis=1)

    is_a = i < tiles_a
    meta = jnp.where(is_a, meta_a_ref[...], meta_b_ref[...])
    on_diag = lax.broadcasted_iota(I32, (tm, tm), 0) == lax.broadcasted_iota(I32, (tm, tm), 1)
    as_col = lambda r: jnp.sum(jnp.where(on_diag, r, 0.0), axis=1, keepdims=True)
    x2 = jnp.where(is_a, x1_a_ref[...], x1_b_ref[...]) + (as_col(meta[2:3, :]) * rows_of(0)
                                                         + as_col(meta[3:4, :]) * rows_of(1))
    normed = _rms(x2) * nw_ref[...]

    def emit(out_ref, z_refs):
        out_ref[...] = normed if final_norm else x2
        h = normed.astype(BF16)
        for w_ref, z_ref in zip(w_refs, z_refs):
            z_ref[...] = jnp.dot(h, w_ref[...], preferred_element_type=F32)

    pl.when(is_a)(functools.partial(emit, out_a_ref, za_refs))
    pl.when(jnp.logical_not(is_a))(functools.partial(emit, out_b_ref, zb_refs))


def _combine(route, x1_a, x1_b, meta_a, meta_b, yb, norm_w, final_norm, next_in_ws):
    tm = GATHER_TILE
    tiles_a, tiles_b = x1_a.shape[0] // tm, x1_b.shape[0] // tm
    assert tiles_a * tm == x1_a.shape[0] and tiles_b * tm == x1_b.shape[0]
    assert final_norm == (not next_in_ws)
    of_a = lambda n: pl.BlockSpec((tm, n), lambda i, *_: (jnp.minimum(i, tiles_a - 1), 0))
    of_b = lambda n: pl.BlockSpec((tm, n), lambda i, *_: (jnp.maximum(i - tiles_a, 0), 0))
    widths = [w.shape[1] for w in next_in_ws]
    outs = pl.pallas_call(
        functools.partial(_combine_body, tm=tm, tiles_a=tiles_a, final_norm=final_norm, n_proj=len(widths)),
        grid_spec=pltpu.PrefetchScalarGridSpec(
            num_scalar_prefetch=5, grid=(tiles_a + tiles_b,),
            in_specs=[of_a(D_MODEL), of_b(D_MODEL),
                      pl.BlockSpec((SUBLANE, tm), lambda i, *_: (0, jnp.minimum(i, tiles_a - 1))),
                      pl.BlockSpec((SUBLANE, tm), lambda i, *_: (0, jnp.maximum(i - tiles_a, 0))),
                      pl.BlockSpec(memory_space=pl.ANY),
                      pl.BlockSpec((1, D_MODEL), lambda i, *_: (0, 0))] +
                     [pl.BlockSpec(w.shape, lambda i, *_: (0, 0)) for w in next_in_ws],
            out_specs=[of_a(D_MODEL), of_b(D_MODEL)] + [of_a(n) for n in widths] + [of_b(n) for n in widths],
            scratch_shapes=[pltpu.VMEM((2, 2, tm * SUBLANE, LANE), F32), pltpu.SemaphoreType.DMA((2,))]),
        out_shape=[jax.ShapeDtypeStruct(x1_a.shape, F32), jax.ShapeDtypeStruct(x1_b.shape, F32)] +
                  [jax.ShapeDtypeStruct((x.shape[0], n), F32) for x in (x1_a, x1_b) for n in widths],
        compiler_params=_params("arbitrary"),
        name="moe_combine",
    )(*route, x1_a, x1_b, meta_a, meta_b, yb, norm_w, *next_in_ws)
    n = len(widths)
    return (outs[0], outs[1]), (outs[2:2 + n], outs[2 + n:2 + 2 * n])


def _moe(stream_a, stream_b, counts, layer, w1, w3, w2, norm_w, final_norm, next_in_ws):
    (x1_a, h2_a, meta_a), (x1_b, h2_b, meta_b) = stream_a, stream_b
    t = x1_a.shape[0] + x1_b.shape[0]
    n_blocks = (2 * t + N_EXPERTS * (MOE_BLK - 1)) // MOE_BLK
    col = lambda c: jnp.concatenate([meta_a[c], meta_b[c]]).astype(I32)
    e1, e2, r1, r2 = col(0), col(1), col(4), col(5)
    cnt = counts[:N_EXPERTS, 0].astype(I32)
    padded = (cnt + MOE_BLK - 1) // MOE_BLK * MOE_BLK
    pad_end = jnp.cumsum(padded)
    pad_start = pad_end - padded
    route = (e1, e2, r1, r2, pad_start)
    nb_used = (pad_end[-1:] // MOE_BLK).astype(I32)
    blk_start = jnp.arange(n_blocks, dtype=I32) * MOE_BLK
    block_e = jnp.minimum(jnp.sum(blk_start[:, None] >= pad_end[None, :], axis=1), N_EXPERTS - 1).astype(I32)
    n_rows = n_blocks * MOE_BLK
    zero_start = jnp.concatenate([pad_start + cnt, pad_end[-1:]]).astype(I32)
    zero_len = jnp.concatenate([padded - cnt, n_rows - pad_end[-1:]]).astype(I32)
    xb = _scatter(route, zero_start, zero_len, h2_a, h2_b, n_rows)
    yb = _experts(block_e, nb_used, xb, layer, w1, w3, w2, n_blocks)
    return _combine(route, x1_a, x1_b, meta_a, meta_b, yb, norm_w, final_norm, next_in_ws)


def _rope_tables(pos):
    half = RET_DK // 2
    inv = ROPE_BASE ** (-jnp.arange(half, dtype=F32) / half)
    ang = pos[:, None] * inv[None, :]
    cos, sin = jnp.cos(ang), jnp.sin(ang)
    cos_t = jnp.tile(jnp.concatenate([cos, cos], axis=1), (1, RET_HEADS))
    sin_t = jnp.tile(jnp.concatenate([-sin, sin], axis=1), (1, RET_HEADS))
    return cos_t, sin_t


def _block_diag(blocks):
    g, r, c = blocks.shape
    eye = jnp.eye(g, dtype=bool)
    return jnp.where(eye[:, None, :, None], blocks[:, :, None, :], 0).reshape(g * r, g * c)


def _layer_weights(l, w_in, s5, s5_b_bar, s5_c_re, s5_c_im, s5_d, s5_glu_w, s5_glu_b, gdn_conv_w, gdn_a_log,
                   gdn_dt_bias, gdn_norm_w, w_out, router_group_w, router_group_b, router_expert_w,
                   router_expert_b):
    wi = w_in[l]
    g0 = RET_IN + S5_IN
    w_ab = jnp.pad(wi[:, g0 + GDN_QKV + GDN_HD:], ((0, 0), (0, LANE - 2 * GDN_HEADS)))
    in_ws = [wi[:, :RET_IN], wi[:, g0:g0 + GDN_QKV], wi[:, RET_IN:g0], wi[:, g0 + GDN_QKV:g0 + GDN_QKV + GDN_HD],
             w_ab]
    in_ws = [w.astype(BF16) for w in in_ws]
    ab_re, ab_im, bb_re, bb_im = s5
    n = S5_LANES
    sl = slice(l * n, (l + 1) * n)
    ab = jnp.concatenate([ab_re[sl].reshape(1, n), ab_im[sl].reshape(1, n)], axis=0)
    to_bd = lambda m: _block_diag(jnp.swapaxes(m[sl].reshape(S5_GROUPS, S5_STATE, S5_GROUP_CH), 1, 2))
    wb = jnp.concatenate([to_bd(bb_re), to_bd(bb_im)], axis=1).astype(BF16)
    wc_re = _block_diag(jnp.swapaxes(s5_c_re[l], 1, 2)).astype(BF16)
    wc_im = _block_diag(jnp.swapaxes(s5_c_im[l], 1, 2)).astype(BF16)
    s5_ws = (ab, wb, wc_re, wc_im, s5_d[l].reshape(1, S5_CH), s5_glu_w[l].astype(BF16),
             s5_glu_b[l].reshape(1, S5_CH))
    pad4 = lambda v: jnp.pad(v.reshape(1, GDN_HEADS), ((0, 0), (0, LANE - GDN_HEADS)))
    gdn_ws = (gdn_conv_w[l], pad4(gdn_a_log[l]), pad4(gdn_dt_bias[l]),
              jnp.tile(gdn_norm_w[l], GDN_HEADS).reshape(1, GDN_HD))
    wo = w_out[l].astype(BF16)
    out_ws = (wo[:RET_VD], wo[RET_VD:RET_VD + S5_CH], wo[RET_VD + S5_CH:])
    wr = jnp.pad(jnp.concatenate([router_expert_w[l], router_group_w[l]], axis=1).T,
                 ((0, LANE - N_EXPERTS - MOE_GROUPS), (0, 0))).astype(BF16)
    br = jnp.pad(jnp.concatenate([router_expert_b[l], router_group_b[l]]),
                 (0, LANE - N_EXPERTS - MOE_GROUPS)).reshape(LANE, 1)
    return in_ws, s5_ws, gdn_ws, out_ws, (wr, br)


def _mix_and_route(l, x, projected, bsz, seq, rope, states, layer_w, norm_mix, norm_ffn, cnt0):
    t = bsz * seq
    ret_s, s5_re, s5_im, gdn_s, gdn_buf = states
    in_ws, s5_ws, gdn_ws, out_ws, (wr, br) = layer_w
    zr, zq, zs, zg, zab = projected or _inproj(x, norm_mix[l].reshape(1, D_MODEL), in_ws)
    o_ret, ret_fin = _retention(zr, *rope, ret_s[l].reshape(bsz, RET_QD, RET_DV), bsz, seq)
    u_tm = jnp.swapaxes(zs.reshape(bsz, seq, S5_CH), 0, 1)
    o_s5_tm, re_fin, im_fin = _s5(u_tm, s5_re[l].reshape(bsz, S5_LANES), s5_im[l].reshape(bsz, S5_LANES), *s5_ws)
    o_s5 = jnp.swapaxes(o_s5_tm, 0, 1).reshape(t, S5_CH)
    buf0 = jnp.pad(gdn_buf[l], ((0, 0), (SUBLANE - (GDN_CONV - 1), 0), (0, 0)))
    o_gdn, nbuf, gdn_fin = _gdn(zq, zg, zab, *gdn_ws, buf0, gdn_s[l].reshape(bsz, GDN_HD, GDN_DV), bsz, seq)
    x1, h2_tiles, meta, counts = _outproj(x, o_ret, o_s5, o_gdn, *out_ws, norm_ffn[l].reshape(1, D_MODEL), wr, br,
                                          cnt0)
    new_states = (ret_fin.reshape(bsz, RET_HEADS, RET_DK, RET_DV), re_fin.reshape(bsz, S5_GROUPS, S5_STATE),
                  im_fin.reshape(bsz, S5_GROUPS, S5_STATE), gdn_fin.reshape(bsz, GDN_HEADS, GDN_DK, GDN_DV),
                  nbuf[:, SUBLANE - (GDN_CONV - 1):, :])
    return (x1, h2_tiles, meta), counts, new_states


def _trunks(xs, positions, states, layer_ws, norm_mix, norm_ffn, experts, norm_final):
    shapes = [x.shape[:2] for x in xs]
    ropes = [_rope_tables(pos) for pos in positions]
    xs = [x.reshape(b * s, D_MODEL) for x, (b, s) in zip(xs, shapes)]
    outs = [[[] for _ in range(5)] for _ in xs]
    projected = [None for _ in xs]
    for l in range(DEPTH):
        counts = jnp.zeros((LANE, LANE), F32)
        routed = []
        for i, x in enumerate(xs):
            stream, counts, new_states = _mix_and_route(l, x, projected[i], *shapes[i], ropes[i], states[i],
                                                        layer_ws[l], norm_mix, norm_ffn, counts)
            routed.append(stream)
            for lst, s in zip(outs[i], new_states):
                lst.append(s)
        last = l == DEPTH - 1
        norm_w = norm_final if last else norm_mix[l + 1]
        xs, projected = _moe(*routed, counts, l, *experts, norm_w.reshape(1, D_MODEL), last,
                             [] if last else layer_ws[l + 1][0])
    ys = [x.reshape(b, s, D_MODEL) for x, (b, s) in zip(xs, shapes)]
    return ys, [tuple(jnp.stack(o) for o in per_stream) for per_stream in outs]


def kernel(x_prompt, x_sample, state_ret, state_s5_re, state_s5_im, state_gdn, state_gdn_conv, norm_mix, w_in, s5_a_re, s5_a_im, s5_log_dt, s5_b_re, s5_b_im, s5_c_re, s5_c_im, s5_d, s5_glu_w, s5_glu_b, gdn_conv_w, gdn_a_log, gdn_dt_bias, gdn_norm_w, w_out, norm_ffn, router_group_w, router_group_b, router_expert_w, router_expert_b, expert_w1, expert_w3, expert_w2, norm_final):
    n = DEPTH * S5_LANES
    col = lambda a: a.reshape(n, 1)
    log_dt = jnp.broadcast_to(s5_log_dt[:, :, None], (DEPTH, S5_GROUPS, S5_STATE))
    s5 = _s5_prep(col(s5_a_re), col(s5_a_im), col(log_dt), s5_b_re.reshape(n, S5_GROUP_CH),
                  s5_b_im.reshape(n, S5_GROUP_CH))
    layer_ws = [_layer_weights(l, w_in, s5, None, s5_c_re, s5_c_im, s5_d, s5_glu_w, s5_glu_b, gdn_conv_w,
                               gdn_a_log, gdn_dt_bias, gdn_norm_w, w_out, router_group_w, router_group_b,
                               router_expert_w, router_expert_b) for l in range(DEPTH)]
    experts = (expert_w1, expert_w3, expert_w2)
    bp, lp, _ = x_prompt.shape
    zero_states = (jnp.zeros((DEPTH, bp, RET_HEADS, RET_DK, RET_DV), F32),
                   jnp.zeros((DEPTH, bp, S5_GROUPS, S5_STATE), F32),
                   jnp.zeros((DEPTH, bp, S5_GROUPS, S5_STATE), F32),
                   jnp.zeros((DEPTH, bp, GDN_HEADS, GDN_DK, GDN_DV), F32),
                   jnp.zeros((DEPTH, bp, GDN_CONV - 1, GDN_QKV), F32))
    sample_states = (state_ret, state_s5_re, state_s5_im, state_gdn, state_gdn_conv)
    positions = (jnp.arange(lp, dtype=F32), PAST_LEN + jnp.arange(x_sample.shape[1], dtype=F32))
    (y_p, y_s), (p_states, s_states) = _trunks((x_prompt, x_sample), positions, (zero_states, sample_states),
                                               layer_ws, norm_mix, norm_ffn, experts, norm_final)
    return (y_p, y_s) + p_states + s_states
```

```python
import functools
import math

import jax
import jax.numpy as jnp
import numpy as np
from jax import lax
from jax.experimental import pallas as pl
from jax.experimental.pallas import tpu as pltpu

F32 = jnp.float32
BF16 = jnp.bfloat16
I32 = jnp.int32

D_MODEL = 1024
DEPTH = 2
RET_HEADS, RET_DK, RET_DV = 4, 64, 128
S5_GROUPS, S5_GROUP_CH, S5_STATE = 16, 16, 64
S5_CH = S5_GROUPS * S5_GROUP_CH
S5_LANES = S5_GROUPS * S5_STATE
GDN_HEADS, GDN_DK, GDN_DV, GDN_CONV = 4, 64, 64, 4
GDN_HD = GDN_HEADS * GDN_DK
GDN_QKV = 3 * GDN_HD
RET_QD = RET_HEADS * RET_DK
RET_VD = RET_HEADS * RET_DV
RET_IN = 2 * RET_QD + 2 * RET_VD
S5_IN = S5_CH
GDN_IN = GDN_QKV + GDN_HD + 2 * GDN_HEADS
GDN_CHUNK = 64
RET_CHUNK = 64
MOE_GROUPS, EXPERTS_PER_GROUP = 4, 8
N_EXPERTS = MOE_GROUPS * EXPERTS_PER_GROUP
D_EXPERT = 512
ROPE_BASE = 10000.0
EPS = 1e-6
PAST_LEN = 16384

LANE = 128
SUBLANE = 8
ROW_TILE = 512
MOE_BLK = 256
GATHER_TILE = 256
ISSUE_UNROLL = 8
SEQS_PER_STEP = 8
LONG_SEQS_PER_STEP = 2
VMEM_LIMIT = 56 * 1024 * 1024


def _params(*sem):
    return pltpu.CompilerParams(dimension_semantics=sem, vmem_limit_bytes=VMEM_LIMIT)


def _dot(a, b):
    return jnp.dot(a.astype(BF16), b.astype(BF16), preferred_element_type=F32)


def _dot_nt(a, b):
    return lax.dot_general(a.astype(BF16), b.astype(BF16), (((1,), (1,)), ((), ())),
                           preferred_element_type=F32)


def _dot_tn(a, b):
    return lax.dot_general(a.astype(BF16), b.astype(BF16), (((0,), (0,)), ((), ())),
                           preferred_element_type=F32)


def _dot_hi(a, b):
    a1 = a.astype(BF16)
    a2 = (a - a1.astype(F32)).astype(BF16)
    b1 = b.astype(BF16)
    b2 = (b - b1.astype(F32)).astype(BF16)
    d = lambda x, y: jnp.dot(x, y, preferred_element_type=F32)
    return d(a1, b1) + (d(a1, b2) + d(a2, b1))


def _split3(x):
    p1 = x.astype(BF16)
    r1 = x - p1.astype(F32)
    p2 = r1.astype(BF16)
    p3 = (r1 - p2.astype(F32)).astype(BF16)
    return p1, p2, p3


def _dot_sel_l(sel, x):
    p1, p2, p3 = _split3(x)
    d = lambda p: jnp.dot(sel, p, preferred_element_type=F32)
    return d(p1) + d(p2) + d(p3)


def _dot_sel_r(x, sel):
    p1, p2, p3 = _split3(x)
    d = lambda p: jnp.dot(p, sel, preferred_element_type=F32)
    return d(p1) + d(p2) + d(p3)


def _rms(x):
    return x * lax.rsqrt(jnp.mean(x * x, axis=-1, keepdims=True) + EPS)


def _silu(x):
    return x * jax.nn.sigmoid(x)


def _inproj_body(x_ref, nw_ref, wr_ref, wq_ref, ws_ref, wg_ref, wab_ref,
                 zr_ref, zq_ref, zs_ref, zg_ref, zab_ref):
    h = (_rms(x_ref[...]) * nw_ref[...]).astype(BF16)
    for w_ref, z_ref in ((wr_ref, zr_ref), (wq_ref, zq_ref), (ws_ref, zs_ref),
                         (wg_ref, zg_ref), (wab_ref, zab_ref)):
        z_ref[...] = jnp.dot(h, w_ref[...], preferred_element_type=F32)


def _inproj(x, nw, ws):
    t = x.shape[0]
    tm = min(ROW_TILE, t)
    widths = [w.shape[1] for w in ws]
    row = lambda n: pl.BlockSpec((tm, n), lambda i: (i, 0))
    full = lambda a: pl.BlockSpec(a.shape, lambda i: (0, 0))
    return pl.pallas_call(
        _inproj_body,
        grid=(t // tm,),
        in_specs=[row(D_MODEL), full(nw)] + [full(w) for w in ws],
        out_specs=[row(n) for n in widths],
        out_shape=[jax.ShapeDtypeStruct((t, n), F32) for n in widths],
        compiler_params=_params("parallel"),
        name="inproj",
    )(x, nw, *ws)


def _seqs_per_step(bsz, blocks_per_seq):
    want = SEQS_PER_STEP if blocks_per_seq == 1 else LONG_SEQS_PER_STEP
    return want if bsz % want == 0 else 1


def _ret_log_gamma(h):
    return math.log(1.0 - 2.0 ** (-5.0 - h))


def _ret_body(zr_ref, cos_ref, sin_ref, s0_ref, o_ref, sfin_ref, s_scr, *, nseq, rows, chunk):
    j = pl.program_id(1)
    stack = RET_HEADS * chunk

    @pl.when(j == 0)
    def _():
        s_scr[...] = s0_ref[...]

    lane = lax.broadcasted_iota(I32, (rows, RET_QD), 1)
    first_half = (lane % RET_DK) < (RET_DK // 2)

    def rotary(x):
        swapped = jnp.where(first_half, pltpu.roll(x, RET_QD - RET_DK // 2, 1),
                            pltpu.roll(x, RET_DK // 2, 1))
        return x * cos_ref[...] + swapped * sin_ref[...]

    q = [rotary(zr_ref[s, :, 0:RET_QD]) for s in range(nseq)]
    k = [rotary(zr_ref[s, :, RET_QD:2 * RET_QD]) * (RET_DK ** -0.5) for s in range(nseq)]

    def by_head(idx, fn):
        out = jnp.zeros(idx.shape, F32)
        for h in range(RET_HEADS):
            out = jnp.where(idx == h, fn(h), out)
        return out

    lg_lane = by_head(lax.broadcasted_iota(I32, (chunk, RET_QD), 1) // RET_DK, _ret_log_gamma)
    pos = lax.broadcasted_iota(I32, (chunk, RET_QD), 0).astype(F32)
    q_scale = jnp.exp((pos + 1.0) * lg_lane)
    k_scale = jnp.exp((chunk - 1.0 - pos) * lg_lane)
    st_row = lax.broadcasted_iota(I32, (stack, stack), 0)
    st_col = lax.broadcasted_iota(I32, (stack, stack), 1)
    causal = ((st_row // chunk) == (st_col // chunk)) & (st_row >= st_col)
    lg_stack = by_head(st_row // chunk, _ret_log_gamma)
    decay = jnp.where(causal, jnp.exp(jnp.where(causal, (st_row - st_col).astype(F32), 0.0) * lg_stack), 0.0)
    q_rows = (lax.broadcasted_iota(I32, (stack, RET_QD), 0) // chunk) == \
             (lax.broadcasted_iota(I32, (stack, RET_QD), 1) // RET_DK)
    v_rows = (lax.broadcasted_iota(I32, (stack, RET_VD), 0) // chunk) == \
             (lax.broadcasted_iota(I32, (stack, RET_VD), 1) // RET_DV)
    s_row_head = lax.broadcasted_iota(I32, (RET_QD, RET_DV), 0) // RET_DK
    s_decay = by_head(s_row_head, lambda h: math.exp(chunk * _ret_log_gamma(h)))
    kv_diag = (lax.broadcasted_iota(I32, (RET_QD, RET_VD), 0) // RET_DK) == \
              (lax.broadcasted_iota(I32, (RET_QD, RET_VD), 1) // RET_DV)

    def tile4(a):
        return jnp.concatenate([a] * RET_HEADS, axis=0)

    def collapse(a):
        out = a[0:chunk]
        for h in range(1, RET_HEADS):
            out = out + a[h * chunk:(h + 1) * chunk]
        return out

    seqs = range(nseq)
    n_chunks = rows // chunk
    state = [s_scr[s] for s in seqs]
    for ci in range(n_chunks):
        r = slice(ci * chunk, (ci + 1) * chunk)
        v_cs = [zr_ref[s, r, 2 * RET_QD:2 * RET_QD + RET_VD] for s in seqs]
        scores = [_dot_nt(jnp.where(q_rows, tile4(q[s][r]), 0.0), tile4(k[s][r])) * decay for s in seqs]
        o_intra = [collapse(_dot(scores[s], jnp.where(v_rows, tile4(v_cs[s]), 0.0))) for s in seqs]
        s_full = [jnp.concatenate([jnp.where(s_row_head == h, state[s], 0.0) for h in range(RET_HEADS)], axis=1)
                  for s in seqs]
        o_cs = [o_intra[s] + _dot(q[s][r] * q_scale, s_full[s]) for s in seqs]
        for s in seqs:
            gate = zr_ref[s, r, 2 * RET_QD + RET_VD:2 * RET_QD + 2 * RET_VD]
            for h in range(RET_HEADS):
                sl = slice(h * RET_DV, (h + 1) * RET_DV)
                o_ref[s, r, sl] = _rms(o_cs[s][:, sl]) * _silu(gate[:, sl])
        for s in seqs:
            kv = jnp.where(kv_diag, _dot_tn(k[s][r] * k_scale, v_cs[s]), 0.0)
            kv_own = kv[:, 0:RET_DV]
            for h in range(1, RET_HEADS):
                kv_own = kv_own + kv[:, h * RET_DV:(h + 1) * RET_DV]
            state[s] = state[s] * s_decay + kv_own
    for s in seqs:
        s_scr[s] = state[s]

    @pl.when(j == pl.num_programs(1) - 1)
    def _():
        for s in seqs:
            sfin_ref[s] = state[s]


def _retention(zr, cos, sin, s0, layer, bsz, seq):
    chunk = min(RET_CHUNK, seq)
    rows = min(4 * chunk, seq)
    nj = seq // rows
    nseq = _seqs_per_step(bsz, nj)
    o, s_fin = pl.pallas_call(
        functools.partial(_ret_body, nseq=nseq, rows=rows, chunk=chunk),
        grid=(bsz // nseq, nj),
        in_specs=[pl.BlockSpec((nseq, rows, RET_IN), lambda b, j: (b, j, 0)),
                  pl.BlockSpec((rows, RET_QD), lambda b, j: (j, 0)),
                  pl.BlockSpec((rows, RET_QD), lambda b, j: (j, 0)),
                  pl.BlockSpec((None, nseq, RET_QD, RET_DV), lambda b, j: (layer, b, 0, 0))],
        out_specs=[pl.BlockSpec((nseq, rows, RET_VD), lambda b, j: (b, j, 0)),
                   pl.BlockSpec((nseq, RET_QD, RET_DV), lambda b, j: (b, 0, 0))],
        out_shape=[jax.ShapeDtypeStruct((bsz, seq, RET_VD), F32),
                   jax.ShapeDtypeStruct((bsz, RET_QD, RET_DV), F32)],
        scratch_shapes=[pltpu.VMEM((nseq, RET_QD, RET_DV), F32)],
        compiler_params=_params("parallel", "arbitrary"),
        name="retention",
    )(zr.reshape(bsz, seq, RET_IN), cos, sin, s0)
    return o.reshape(bsz * seq, RET_VD), s_fin


def _s5_prep_body(are_ref, aim_ref, ldt_ref, bre_ref, bim_ref, abre_ref, abim_ref, bbre_ref, bbim_ref):
    lam_re, lam_im = are_ref[...], aim_ref[...]
    dt = jnp.exp(ldt_ref[...])
    mag = jnp.exp(lam_re * dt)
    ab_re = mag * jnp.cos(lam_im * dt)
    ab_im = mag * jnp.sin(lam_im * dt)
    den = lam_re * lam_re + lam_im * lam_im
    f_re = ((ab_re - 1.0) * lam_re + ab_im * lam_im) / den
    f_im = (ab_im * lam_re - (ab_re - 1.0) * lam_im) / den
    abre_ref[...] = ab_re
    abim_ref[...] = ab_im
    bbre_ref[...] = f_re * bre_ref[...] - f_im * bim_ref[...]
    bbim_ref[...] = f_re * bim_ref[...] + f_im * bre_ref[...]


def _s5_prep(a_re, a_im, log_dt, b_re, b_im):
    n = a_re.shape[0]
    col = jax.ShapeDtypeStruct((n, 1), F32)
    mat = jax.ShapeDtypeStruct((n, S5_GROUP_CH), F32)
    return pl.pallas_call(_s5_prep_body, out_shape=[col, col, mat, mat], name="s5_prep")(
        a_re, a_im, log_dt, b_re, b_im)


def _gelu_tanh(x):
    return x * (0.5 * (1.0 + jnp.tanh(math.sqrt(2.0 / math.pi) * (x + 0.044715 * (x * x * x)))))


def _s5_body(u_ref, h0re_ref, h0im_ref, ab_ref, wb_ref, wcre_ref, wcim_ref, d_ref, gw_ref, gb_ref,
             o_ref, hre_ref, him_ref, bu_scr, st_scr, *, steps):
    j = pl.program_id(1)

    @pl.when(j == 0)
    def _():
        st_scr[0] = h0re_ref[...]
        st_scr[1] = h0im_ref[...]

    rows = steps * SUBLANE
    u = u_ref[...].reshape(rows, S5_CH)
    bu_scr[...] = _dot(u, wb_ref[...])
    a_re = jnp.broadcast_to(ab_ref[0:1, :], (SUBLANE, S5_LANES))
    a_im = jnp.broadcast_to(ab_ref[1:2, :], (SUBLANE, S5_LANES))

    def step(t, carry):
        h_re, h_im = carry
        r = pl.ds(pl.multiple_of(t * SUBLANE, SUBLANE), SUBLANE)
        n_re = a_re * h_re - a_im * h_im + bu_scr[r, 0:S5_LANES]
        n_im = a_re * h_im + a_im * h_re + bu_scr[r, S5_LANES:2 * S5_LANES]
        bu_scr[r, 0:S5_LANES] = n_re
        bu_scr[r, S5_LANES:2 * S5_LANES] = n_im
        return n_re, n_im

    h_re, h_im = lax.fori_loop(0, steps, step, (st_scr[0], st_scr[1]))
    st_scr[0] = h_re
    st_scr[1] = h_im
    hre_ref[...] = h_re
    him_ref[...] = h_im

    y = _dot(bu_scr[:, 0:S5_LANES], wcre_ref[...]) - _dot(bu_scr[:, S5_LANES:2 * S5_LANES], wcim_ref[...])
    y = _gelu_tanh(y + d_ref[...] * u)
    y = y * jax.nn.sigmoid(_dot(y, gw_ref[...]) + gb_ref[...])
    o_ref[...] = y.reshape(steps, SUBLANE, S5_CH)


def _s5(u_tm, h0_re, h0_im, ab, wb, wc_re, wc_im, d_skip, glu_w, glu_b):
    seq, bsz, _ = u_tm.shape
    steps = min(64, seq)
    full = lambda a: pl.BlockSpec(a.shape, lambda g, j: (0,) * a.ndim)
    st = pl.BlockSpec((SUBLANE, S5_LANES), lambda g, j: (g, 0))
    return pl.pallas_call(
        functools.partial(_s5_body, steps=steps),
        grid=(bsz // SUBLANE, seq // steps),
        in_specs=[pl.BlockSpec((steps, SUBLANE, S5_CH), lambda g, j: (j, g, 0)), st, st,
                  full(ab), full(wb), full(wc_re), full(wc_im), full(d_skip), full(glu_w), full(glu_b)],
        out_specs=[pl.BlockSpec((steps, SUBLANE, S5_CH), lambda g, j: (j, g, 0)), st, st],
        out_shape=[jax.ShapeDtypeStruct((seq, bsz, S5_CH), F32),
                   jax.ShapeDtypeStruct((bsz, S5_LANES), F32),
                   jax.ShapeDtypeStruct((bsz, S5_LANES), F32)],
        scratch_shapes=[pltpu.VMEM((steps * SUBLANE, 2 * S5_LANES), F32),
                        pltpu.VMEM((2, SUBLANE, S5_LANES), F32)],
        compiler_params=_params("parallel", "arbitrary"),
        name="s5",
    )(u_tm, h0_re, h0_im, ab, wb, wc_re, wc_im, d_skip, glu_w, glu_b)


def _gdn_body(zq_ref, zg_ref, zab_ref, cw_ref, alog_ref, dtb_ref, nw_ref, buf0_ref, s0_ref,
              ones_bd_ref, tri_ref, ea_ref, eb_ref, ec_ref, spread_ref, gather_ref,
              o_ref, nbuf_ref, sfin_ref,
              ext_scr, q_scr, k_scr, kb_scr, vb_scr, g_scr, g4_scr, s_scr, *, nseq, rows, chunk):
    j = pl.program_id(1)
    stack = GDN_HEADS * chunk
    bd_state = (lax.broadcasted_iota(I32, (GDN_HD, GDN_HD), 0) // GDN_DK) == \
               (lax.broadcasted_iota(I32, (GDN_HD, GDN_HD), 1) // GDN_DV)

    @pl.when(j == 0)
    def _():
        for s in range(nseq):
            ext_scr[s, 0:SUBLANE, :] = buf0_ref[s]
            s_scr[s] = jnp.where(bd_state, _dot_sel_r(s0_ref[s], spread_ref[...]), 0.0)

    convs = []
    for s in range(nseq):
        ext_scr[s, SUBLANE:SUBLANE + rows, :] = zq_ref[s]
        conv = ext_scr[s, SUBLANE - 3:SUBLANE - 3 + rows, :] * cw_ref[0:1, :]
        for i in range(1, GDN_CONV):
            conv = conv + ext_scr[s, SUBLANE - 3 + i:SUBLANE - 3 + i + rows, :] * cw_ref[i:i + 1, :]
        tail = ext_scr[s, rows:rows + SUBLANE, :]
        nbuf_ref[s] = tail
        ext_scr[s, 0:SUBLANE, :] = tail
        convs.append(conv)
    qkv = _silu(convs[0] if nseq == 1 else jnp.concatenate(convs, axis=0))

    ones_bd = ones_bd_ref[...]
    q_raw = qkv[:, 0:GDN_HD]
    k_raw = qkv[:, GDN_HD:2 * GDN_HD]
    q_scr[...] = q_raw * lax.rsqrt(_dot_sel_r(q_raw * q_raw, ones_bd) + EPS) * (GDN_DK ** -0.5)
    k_n = k_raw * lax.rsqrt(_dot_sel_r(k_raw * k_raw, ones_bd) + EPS)
    k_scr[...] = k_n

    ab = zab_ref[...].reshape(nseq * rows, LANE)
    x = ab + dtb_ref[...]
    softplus = jnp.maximum(x, 0.0) + jnp.log1p(jnp.exp(-jnp.abs(x)))
    g_pad = -jnp.exp(alog_ref[...]) * softplus
    beta = _dot_sel_r(jax.nn.sigmoid(ab), eb_ref[...])
    g_cum = _dot_sel_l(tri_ref[...], g_pad)
    g_scr[...] = _dot_sel_r(g_cum, ea_ref[...])
    g4_scr[...] = _dot_sel_r(g_pad, ec_ref[...])
    kb_scr[...] = k_n * beta
    vb_scr[...] = qkv[:, 2 * GDN_HD:3 * GDN_HD] * beta

    w_row = lax.broadcasted_iota(I32, (chunk, stack), 0)
    w_col = lax.broadcasted_iota(I32, (chunk, stack), 1) % chunk
    strict_w = w_row > w_col
    causal_w = w_row >= w_col
    same_head = (lax.broadcasted_iota(I32, (stack, stack), 0) // chunk) == \
                (lax.broadcasted_iota(I32, (stack, stack), 1) // chunk)
    head_rows = (lax.broadcasted_iota(I32, (stack, GDN_HD), 0) // chunk) == \
                (lax.broadcasted_iota(I32, (stack, GDN_HD), 1) // GDN_DK)
    head_rows2 = jnp.concatenate([head_rows, head_rows], axis=1)
    tri_c = tri_ref[0:chunk, 0:chunk]
    packed_ok = chunk % (2 * SUBLANE) == 0

    def tile4(a):
        return jnp.concatenate([a] * GDN_HEADS, axis=0)

    def split2(a):
        hi = a.astype(BF16)
        return hi, (a - hi.astype(F32)).astype(BF16)

    def stack_masked(parts, mask):
        if packed_ok:
            return [jnp.where(mask, tile4(p), jnp.zeros((), BF16)) for p in parts]
        return [jnp.where(mask, tile4(p.astype(F32)), 0.0).astype(BF16) for p in parts]

    def on_diag(parts):
        return stack_masked(parts, same_head)

    def mm(a, b):
        return jnp.dot(a, b, preferred_element_type=F32)

    def mm_hi(a_parts, b_parts):
        return mm(a_parts[0], b_parts[0]) + (mm(a_parts[0], b_parts[1]) + mm(a_parts[1], b_parts[0]))

    n_chunks = rows // chunk
    every = range(nseq * n_chunks)
    rows_of = lambda u: slice(u * chunk, (u + 1) * chunk)
    solved = []
    for u in every:
        r = rows_of(u)
        q_c, k_c, kb_c, vb_c, g_c = q_scr[r, :], k_scr[r, :], kb_scr[r, :], vb_scr[r, :], g_scr[r, :]
        exp_g = jnp.exp(g_c)
        g_diff = _dot_sel_l(tri_c, jnp.where(strict_w, g4_scr[r, :], 0.0))
        decay = jnp.exp(jnp.where(causal_w, g_diff, 0.0))
        k_heads = jnp.where(head_rows, tile4(k_c), 0.0)
        lmat = jnp.where(strict_w, _dot_nt(kb_c, k_heads) * decay, 0.0)
        attn = jnp.where(causal_w, _dot_nt(q_c, k_heads) * decay, 0.0)
        rhs_c = jnp.concatenate([vb_c, kb_c * exp_g], axis=1)
        solved.append((lmat, attn, exp_g, rhs_c))

    t_acc = [-solved[ci][0] for ci in every]
    p_parts = [split2(solved[ci][0]) for ci in every]
    p_diag = [on_diag(p) for p in p_parts]
    span = 2
    while span < chunk + 1:
        power = [mm_hi(p_parts[ci], p_diag[ci]) for ci in every]
        p_parts = [split2(p) for p in power]
        p_diag = [on_diag(p) for p in p_parts]
        t_acc = [t_acc[ci] + power[ci] + mm_hi(split2(t_acc[ci]), p_diag[ci]) for ci in every]
        span *= 2
    rhs_parts = [stack_masked(split2(solved[ci][3]), head_rows2) for ci in every]
    sols = [solved[ci][3] + mm_hi(split2(t_acc[ci]), rhs_parts[ci]) for ci in every]

    seqs = range(nseq)
    s_cur = [s_scr[s] for s in seqs]
    for ci in range(n_chunks):
        us = [s * n_chunks + ci for s in seqs]
        v_new = [sols[u][:, 0:GDN_HD] - _dot(sols[u][:, GDN_HD:2 * GDN_HD], s_cur[s]) for s, u in enumerate(us)]
        o_cs = [_dot(q_scr[rows_of(u), :] * solved[u][2], s_cur[s])
                + _dot(solved[u][1], jnp.where(head_rows, tile4(v_new[s]), 0.0)) for s, u in enumerate(us)]
        for s, u in enumerate(us):
            g_c = g_scr[rows_of(u), :]
            g_last = g_c[chunk - 1:chunk, :]
            k_dec = k_scr[rows_of(u), :] * jnp.exp(g_last - g_c)
            s_cur[s] = s_cur[s] * jnp.exp(g_last) + jnp.where(bd_state, _dot_tn(k_dec, v_new[s]), 0.0)
        for s, u in enumerate(us):
            r = slice(ci * chunk, (ci + 1) * chunk)
            ms = _dot_sel_r(o_cs[s] * o_cs[s], ones_bd) * (1.0 / GDN_DV)
            o_ref[s, r, :] = o_cs[s] * lax.rsqrt(ms + EPS) * nw_ref[...] * _silu(zg_ref[s, r, :])
    for s in seqs:
        s_scr[s] = s_cur[s]

    @pl.when(j == pl.num_programs(1) - 1)
    def _():
        for s in seqs:
            sfin_ref[s] = _dot_sel_r(s_scr[s], gather_ref[...])


def _gdn_consts(rows, chunk):
    stack = GDN_HEADS * chunk
    blk = lambda n, c: (np.arange(n)[:, None] // c) == (np.arange(n)[None, :] // c)
    ones_bd = blk(GDN_HD, GDN_DK)
    tri = blk(rows, chunk) & (np.arange(rows)[:, None] >= np.arange(rows)[None, :])
    src = np.arange(LANE)[:, None]
    ea = src == np.arange(GDN_HD)[None, :] // GDN_DK
    eb = src == GDN_HEADS + np.arange(GDN_HD)[None, :] // GDN_DK
    ec = src == np.arange(stack)[None, :] // chunk
    spread = np.arange(GDN_DV)[:, None] == np.arange(GDN_HD)[None, :] % GDN_DV
    return tuple(jnp.asarray(m, dtype=BF16) for m in (ones_bd, tri, ea, eb, ec, spread, spread.T))


def _gdn(zq, zg, zab, conv_w, alog_pad, dtb_pad, nw, buf0, s0, layer, bsz, seq):
    chunk = min(GDN_CHUNK, seq)
    rows = min(4 * chunk, seq)
    nj = seq // rows
    nseq = _seqs_per_step(bsz, nj)
    stack = GDN_HEADS * chunk
    consts = _gdn_consts(nseq * rows, chunk)
    row = lambda n: pl.BlockSpec((nseq, rows, n), lambda b, j: (b, j, 0))
    full = lambda a: pl.BlockSpec(a.shape, lambda b, j: (0,) * a.ndim)
    per_b = lambda *s: pl.BlockSpec((nseq,) + s, lambda b, j: (b, 0, 0))
    by_seq = lambda a: a.reshape(bsz, seq, a.shape[-1])
    o, nbuf, s_fin = pl.pallas_call(
        functools.partial(_gdn_body, nseq=nseq, rows=rows, chunk=chunk),
        grid=(bsz // nseq, nj),
        in_specs=[row(GDN_QKV), row(GDN_HD), row(LANE), full(conv_w), full(alog_pad), full(dtb_pad), full(nw),
                  per_b(SUBLANE, GDN_QKV),
                  pl.BlockSpec((None, nseq, GDN_HD, GDN_DV), lambda b, j: (layer, b, 0, 0))] +
                 [full(c) for c in consts],
        out_specs=[row(GDN_HD), per_b(SUBLANE, GDN_QKV), per_b(GDN_HD, GDN_DV)],
        out_shape=[jax.ShapeDtypeStruct((bsz, seq, GDN_HD), F32),
                   jax.ShapeDtypeStruct((bsz, SUBLANE, GDN_QKV), F32),
                   jax.ShapeDtypeStruct((bsz, GDN_HD, GDN_DV), F32)],
        scratch_shapes=[pltpu.VMEM((nseq, rows + SUBLANE, GDN_QKV), F32)] +
                       [pltpu.VMEM((nseq * rows, GDN_HD), F32)] * 5 +
                       [pltpu.VMEM((nseq * rows, stack), F32), pltpu.VMEM((nseq, GDN_HD, GDN_HD), F32)],
        compiler_params=_params("parallel", "arbitrary"),
        name="gdn",
    )(by_seq(zq), by_seq(zg), by_seq(zab), conv_w, alog_pad, dtb_pad, nw, buf0, s0, *consts)
    return o.reshape(bsz * seq, GDN_HD), nbuf, s_fin


def _outproj_body(x_ref, oret_ref, os5_ref, ogdn_ref, w1_ref, w2_ref, w3_ref, nw_ref, wr_ref, br_ref, tri_ref,
                  cnt0_ref, x1_ref, h2_ref, meta_ref, cnt_ref, carry_scr, *, tm):
    i = pl.program_id(0)

    @pl.when(i == 0)
    def _():
        carry_scr[...] = cnt0_ref[...]

    mix = _dot(oret_ref[...], w1_ref[...]) + _dot(os5_ref[...], w2_ref[...]) + _dot(ogdn_ref[...], w3_ref[...])
    x1 = x_ref[...] + mix
    x1_ref[...] = x1
    h2 = _rms(x1) * nw_ref[...]
    for s in range(D_MODEL // LANE):
        h2_ref[pl.ds(s, tm, stride=SUBLANE), :] = h2[:, s * LANE:(s + 1) * LANE]

    logits = _dot_nt(wr_ref[...], h2) + br_ref[...]
    row_i = lax.broadcasted_iota(I32, (LANE, tm), 0)
    row = row_i.astype(F32)
    neg = -jnp.inf
    big = float(LANE)
    g_log = jnp.where((row_i >= N_EXPERTS) & (row_i < N_EXPERTS + MOE_GROUPS), logits, neg)
    g_max = jnp.max(g_log, axis=0, keepdims=True)
    grp = jnp.min(jnp.where(g_log == g_max, row - N_EXPERTS, big), axis=0, keepdims=True)
    p_grp = 1.0 / jnp.sum(jnp.exp(g_log - g_max), axis=0, keepdims=True)
    in_grp = (row >= grp * EXPERTS_PER_GROUP) & (row < (grp + 1.0) * EXPERTS_PER_GROUP)
    e_log = jnp.where(in_grp, logits, neg)
    v1 = jnp.max(e_log, axis=0, keepdims=True)
    i1 = jnp.min(jnp.where(e_log == v1, row, big), axis=0, keepdims=True)
    e_log2 = jnp.where(row == i1, neg, e_log)
    v2 = jnp.max(e_log2, axis=0, keepdims=True)
    i2 = jnp.min(jnp.where(e_log2 == v2, row, big), axis=0, keepdims=True)
    e2 = jnp.exp(v2 - v1)
    gate1 = p_grp / (1.0 + e2)
    gate2 = p_grp * e2 / (1.0 + e2)

    oh1 = row == i1
    oh2 = row == i2
    picked = jnp.where(oh1 | oh2, 1.0, 0.0)
    seen = carry_scr[:, 0:1]
    before = jnp.dot(picked.astype(BF16), tri_ref[...], preferred_element_type=F32) + seen
    rank1 = jnp.sum(jnp.where(oh1, before, 0.0), axis=0, keepdims=True)
    rank2 = jnp.sum(jnp.where(oh2, before, 0.0), axis=0, keepdims=True)
    total = seen + jnp.sum(picked, axis=1, keepdims=True)
    carry_scr[...] = jnp.broadcast_to(total, carry_scr.shape)
    cnt_ref[...] = jnp.broadcast_to(total, cnt_ref.shape)
    meta_ref[...] = jnp.concatenate([i1, i2, gate1, gate2, rank1, rank2, jnp.zeros((2, tm), F32)], axis=0)


def _outproj(x, o_ret, o_s5, o_gdn, w1, w2, w3, nw, wr, br, cnt0):
    t = x.shape[0]
    tm = min(ROW_TILE, t)
    tri = jnp.asarray(np.arange(tm)[:, None] < np.arange(tm)[None, :], dtype=BF16)
    row = lambda n: pl.BlockSpec((tm, n), lambda i: (i, 0))
    full = lambda a: pl.BlockSpec(a.shape, lambda i: (0, 0))
    return pl.pallas_call(
        functools.partial(_outproj_body, tm=tm),
        grid=(t // tm,),
        in_specs=[row(D_MODEL), row(RET_VD), row(S5_CH), row(GDN_HD),
                  full(w1), full(w2), full(w3), full(nw), full(wr), full(br), full(tri), full(cnt0)],
        out_specs=[row(D_MODEL), pl.BlockSpec((tm * SUBLANE, LANE), lambda i: (i, 0)),
                   pl.BlockSpec((SUBLANE, tm), lambda i: (0, i)), pl.BlockSpec((LANE, LANE), lambda i: (0, 0))],
        out_shape=[jax.ShapeDtypeStruct((t, D_MODEL), F32),
                   jax.ShapeDtypeStruct((t * SUBLANE, LANE), F32),
                   jax.ShapeDtypeStruct((SUBLANE, t), F32),
                   jax.ShapeDtypeStruct((LANE, LANE), F32)],
        scratch_shapes=[pltpu.VMEM((LANE, LANE), F32)],
        compiler_params=_params("arbitrary"),
        name="outproj_router",
    )(x, o_ret, o_s5, o_gdn, w1, w2, w3, nw, wr, br, tri, cnt0)


def _token_rows(ref, idx):
    return ref.at[pl.ds(pl.multiple_of(idx * SUBLANE, SUBLANE), SUBLANE)]


ZERO_ROWS = 128


def _zero_segment(zero_scr, xb_ref, sem, start, length, wait):
    def piece(off, n):
        cp = pltpu.make_async_copy(zero_scr.at[pl.ds(0, n * SUBLANE)],
                                   xb_ref.at[pl.ds(pl.multiple_of(off * SUBLANE, SUBLANE), n * SUBLANE)], sem)
        cp.wait() if wait else cp.start()

    n_big = length // ZERO_ROWS

    def big(i, carry):
        piece(start + i * ZERO_ROWS, ZERO_ROWS)
        return carry

    lax.fori_loop(0, n_big, big, 0)
    off = start + n_big * ZERO_ROWS
    rem = length - n_big * ZERO_ROWS
    bit = ZERO_ROWS // 2
    while bit >= 1:
        has = (rem & bit) != 0
        pl.when(has)(functools.partial(piece, off, bit))
        off = off + jnp.where(has, bit, 0)
        bit //= 2


def _scatter_body(d1_ref, d2_ref, zs_ref, zl_ref, src_a_ref, src_b_ref, xb_ref,
                  zero_scr, sem, zsem, *, tm, tiles_a):
    i = pl.program_id(0)
    base = i * tm

    @pl.when(i == 0)
    def _():
        zero_scr[...] = jnp.zeros_like(zero_scr)
        for wait in (False, True):
            lax.fori_loop(0, N_EXPERTS + 1,
                          lambda s, c, wait=wait: (_zero_segment(zero_scr, xb_ref, zsem, zs_ref[s], zl_ref[s], wait),
                                                   c)[1], 0)

    def scatter_tile(src_ref):
        def copy(t, d):
            return pltpu.make_async_copy(_token_rows(src_ref, t), _token_rows(xb_ref, d), sem)

        def issue(t, carry):
            copy(t, d1_ref[base + t]).start(priority=0)
            copy(t, d2_ref[base + t]).start(priority=1)
            return carry

        lax.fori_loop(0, tm, issue, 0, unroll=ISSUE_UNROLL)
        for _ in range(2):
            pltpu.make_async_copy(src_ref, src_ref, sem).wait()

    pl.when(i < tiles_a)(functools.partial(scatter_tile, src_a_ref))
    pl.when(i >= tiles_a)(functools.partial(scatter_tile, src_b_ref))


def _scatter(route, zero_start, zero_len, h2_a, h2_b, n_rows):
    tm = GATHER_TILE
    tiles_a, tiles_b = h2_a.shape[0] // (tm * SUBLANE), h2_b.shape[0] // (tm * SUBLANE)
    assert tiles_a * tm * SUBLANE == h2_a.shape[0] and tiles_b * tm * SUBLANE == h2_b.shape[0]
    return pl.pallas_call(
        functools.partial(_scatter_body, tm=tm, tiles_a=tiles_a),
        grid_spec=pltpu.PrefetchScalarGridSpec(
            num_scalar_prefetch=4, grid=(tiles_a + tiles_b,),
            in_specs=[pl.BlockSpec((tm * SUBLANE, LANE), lambda i, *_: (jnp.minimum(i, tiles_a - 1), 0)),
                      pl.BlockSpec((tm * SUBLANE, LANE), lambda i, *_: (jnp.maximum(i - tiles_a, 0), 0))],
            out_specs=pl.BlockSpec(memory_space=pl.ANY),
            scratch_shapes=[pltpu.VMEM((ZERO_ROWS * SUBLANE, LANE), F32), pltpu.SemaphoreType.DMA(()),
                            pltpu.SemaphoreType.DMA(())]),
        out_shape=jax.ShapeDtypeStruct((n_rows * SUBLANE, LANE), F32),
        compiler_params=_params("arbitrary"),
        name="moe_scatter",
    )(*route, zero_start, zero_len, h2_a, h2_b)


def _experts_body(be_ref, nb_ref, xb_ref, w1_ref, w3_ref, w2_ref, yb_ref, w1_scr, w3_scr, w2_scr, *, blk):
    i = pl.program_id(0)
    live = i < nb_ref[0]

    @pl.when(live & ((i == 0) | (be_ref[i] != be_ref[jnp.maximum(i - 1, 0)])))
    def _():
        w1_scr[...] = w1_ref[...].astype(BF16)
        w3_scr[...] = w3_ref[...].astype(BF16)
        w2_scr[...] = w2_ref[...].astype(BF16)

    @pl.when(live)
    def _():
        x = jnp.concatenate([xb_ref[pl.ds(s, blk, stride=SUBLANE), :] for s in range(D_MODEL // LANE)],
                            axis=1).astype(BF16)
        hid = _silu(jnp.dot(x, w1_scr[...], preferred_element_type=F32)) * \
            jnp.dot(x, w3_scr[...], preferred_element_type=F32)
        y = _dot(hid, w2_scr[...])
        for s in range(D_MODEL // LANE):
            yb_ref[pl.ds(s, blk, stride=SUBLANE), :] = y[:, s * LANE:(s + 1) * LANE]

    @pl.when(jnp.logical_not(live))
    def _():
        yb_ref[...] = jnp.zeros_like(yb_ref)


def _experts(block_e, nb_used, xb, layer, w1, w3, w2, n_blocks):
    blk = MOE_BLK
    live = lambda i, nb: jnp.minimum(i, nb[0] - 1)
    tile_in = pl.BlockSpec((blk * SUBLANE, LANE), lambda i, be, nb: (live(i, nb), 0))
    tile_out = pl.BlockSpec((blk * SUBLANE, LANE), lambda i, be, nb: (i, 0))
    wspec = lambda a: pl.BlockSpec((None, None) + a.shape[2:],
                                   lambda i, be, nb: (layer, be[live(i, nb)], 0, 0))
    return pl.pallas_call(
        functools.partial(_experts_body, blk=blk),
        grid_spec=pltpu.PrefetchScalarGridSpec(
            num_scalar_prefetch=2, grid=(n_blocks,),
            in_specs=[tile_in, wspec(w1), wspec(w3), wspec(w2)], out_specs=tile_out,
            scratch_shapes=[pltpu.VMEM(w1.shape[2:], BF16), pltpu.VMEM(w3.shape[2:], BF16),
                            pltpu.VMEM(w2.shape[2:], BF16)]),
        out_shape=jax.ShapeDtypeStruct(xb.shape, F32),
        compiler_params=_params("arbitrary"),
        name="moe_experts",
    )(block_e, nb_used, xb, w1, w3, w2)


def _combine_body(*refs, tm, tiles_a, final_norm, n_proj):
    d1_ref, d2_ref, x1_a_ref, x1_b_ref, meta_a_ref, meta_b_ref, yb_ref, nw_ref = refs[:8]
    w_refs = refs[8:8 + n_proj]
    out_a_ref, out_b_ref = refs[8 + n_proj:10 + n_proj]
    za_refs = refs[10 + n_proj:10 + 2 * n_proj]
    zb_refs = refs[10 + 2 * n_proj:10 + 3 * n_proj]
    buf_ref, sem = refs[-2:]
    i = pl.program_id(0)
    phase = i % 2

    def gather(tile, ph):
        base = tile * tm

        def issue(t, carry):
            for k, d_ref in enumerate((d1_ref, d2_ref)):
                pltpu.make_async_copy(_token_rows(yb_ref, d_ref[base + t]),
                                      _token_rows(buf_ref.at[ph, k], t), sem.at[ph]).start(priority=k)
            return carry

        lax.fori_loop(0, tm, issue, 0, unroll=ISSUE_UNROLL)

    @pl.when(i == 0)
    def _():
        gather(0, 0)

    @pl.when(i + 1 < pl.num_programs(0))
    def _():
        gather(i + 1, 1 - phase)

    for k in range(2):
        pltpu.make_async_copy(buf_ref.at[phase, k], buf_ref.at[phase, k], sem.at[phase]).wait()

    def rows_of(slot):
        return jnp.concatenate([buf_ref[phase, slot, pl.ds(s, tm, stride=SUBLANE), :]
                                for s in range(D_MODEL // LANE)], axis=1)

    is_a = i < tiles_a
    meta = jnp.where(is_a, meta_a_ref[...], meta_b_ref[...])
    on_diag = lax.broadcasted_iota(I32, (tm, tm), 0) == lax.broadcasted_iota(I32, (tm, tm), 1)
    as_col = lambda r: jnp.sum(jnp.where(on_diag, r, 0.0), axis=1, keepdims=True)
    x2 = jnp.where(is_a, x1_a_ref[...], x1_b_ref[...]) + (as_col(meta[2:3, :]) * rows_of(0)
                                                         + as_col(meta[3:4, :]) * rows_of(1))
    normed = _rms(x2) * nw_ref[...]

    def emit(out_ref, z_refs):
        out_ref[...] = normed if final_norm else x2
        h = normed.astype(BF16)
        for w_ref, z_ref in zip(w_refs, z_refs):
            z_ref[...] = jnp.dot(h, w_ref[...], preferred_element_type=F32)

    pl.when(is_a)(functools.partial(emit, out_a_ref, za_refs))
    pl.when(jnp.logical_not(is_a))(functools.partial(emit, out_b_ref, zb_refs))


def _combine(route, x1_a, x1_b, meta_a, meta_b, yb, norm_w, final_norm, next_in_ws):
    tm = GATHER_TILE
    tiles_a, tiles_b = x1_a.shape[0] // tm, x1_b.shape[0] // tm
    assert tiles_a * tm == x1_a.shape[0] and tiles_b * tm == x1_b.shape[0]
    assert final_norm == (not next_in_ws)
    of_a = lambda n: pl.BlockSpec((tm, n), lambda i, *_: (jnp.minimum(i, tiles_a - 1), 0))
    of_b = lambda n: pl.BlockSpec((tm, n), lambda i, *_: (jnp.maximum(i - tiles_a, 0), 0))
    widths = [w.shape[1] for w in next_in_ws]
    outs = pl.pallas_call(
        functools.partial(_combine_body, tm=tm, tiles_a=tiles_a, final_norm=final_norm, n_proj=len(widths)),
        grid_spec=pltpu.PrefetchScalarGridSpec(
            num_scalar_prefetch=2, grid=(tiles_a + tiles_b,),
            in_specs=[of_a(D_MODEL), of_b(D_MODEL),
                      pl.BlockSpec((SUBLANE, tm), lambda i, *_: (0, jnp.minimum(i, tiles_a - 1))),
                      pl.BlockSpec((SUBLANE, tm), lambda i, *_: (0, jnp.maximum(i - tiles_a, 0))),
                      pl.BlockSpec(memory_space=pl.ANY),
                      pl.BlockSpec((1, D_MODEL), lambda i, *_: (0, 0))] +
                     [pl.BlockSpec(w.shape, lambda i, *_: (0, 0)) for w in next_in_ws],
            out_specs=[of_a(D_MODEL), of_b(D_MODEL)] + [of_a(n) for n in widths] + [of_b(n) for n in widths],
            scratch_shapes=[pltpu.VMEM((2, 2, tm * SUBLANE, LANE), F32), pltpu.SemaphoreType.DMA((2,))]),
        out_shape=[jax.ShapeDtypeStruct(x1_a.shape, F32), jax.ShapeDtypeStruct(x1_b.shape, F32)] +
                  [jax.ShapeDtypeStruct((x.shape[0], n), F32) for x in (x1_a, x1_b) for n in widths],
        compiler_params=_params("arbitrary"),
        name="moe_combine",
    )(*route, x1_a, x1_b, meta_a, meta_b, yb, norm_w, *next_in_ws)
    n = len(widths)
    return (outs[0], outs[1]), (outs[2:2 + n], outs[2 + n:2 + 2 * n])


def _moe(stream_a, stream_b, counts, layer, w1, w3, w2, norm_w, final_norm, next_in_ws):
    (x1_a, h2_a, meta_a), (x1_b, h2_b, meta_b) = stream_a, stream_b
    t = x1_a.shape[0] + x1_b.shape[0]
    n_blocks = (2 * t + N_EXPERTS * (MOE_BLK - 1)) // MOE_BLK
    col = lambda c: jnp.concatenate([meta_a[c], meta_b[c]]).astype(I32)
    e1, e2, r1, r2 = col(0), col(1), col(4), col(5)
    cnt = counts[:N_EXPERTS, 0].astype(I32)
    padded = (cnt + MOE_BLK - 1) // MOE_BLK * MOE_BLK
    pad_end = jnp.cumsum(padded)
    pad_start = pad_end - padded
    slot = lambda e, r: r + jnp.sum(jnp.where(e[:, None] == jnp.arange(N_EXPERTS, dtype=I32)[None, :],
                                              pad_start[None, :], 0), axis=1)
    route = (slot(e1, r1), slot(e2, r2))
    nb_used = (pad_end[-1:] // MOE_BLK).astype(I32)
    blk_start = jnp.arange(n_blocks, dtype=I32) * MOE_BLK
    block_e = jnp.minimum(jnp.sum(blk_start[:, None] >= pad_end[None, :], axis=1), N_EXPERTS - 1).astype(I32)
    n_rows = n_blocks * MOE_BLK
    zero_start = jnp.concatenate([pad_start + cnt, pad_end[-1:]]).astype(I32)
    zero_len = jnp.concatenate([padded - cnt, n_rows - pad_end[-1:]]).astype(I32)
    xb = _scatter(route, zero_start, zero_len, h2_a, h2_b, n_rows)
    yb = _experts(block_e, nb_used, xb, layer, w1, w3, w2, n_blocks)
    return _combine(route, x1_a, x1_b, meta_a, meta_b, yb, norm_w, final_norm, next_in_ws)


def _rope_tables(pos):
    half = RET_DK // 2
    inv = ROPE_BASE ** (-jnp.arange(half, dtype=F32) / half)
    ang = pos[:, None] * inv[None, :]
    cos, sin = jnp.cos(ang), jnp.sin(ang)
    cos_t = jnp.tile(jnp.concatenate([cos, cos], axis=1), (1, RET_HEADS))
    sin_t = jnp.tile(jnp.concatenate([-sin, sin], axis=1), (1, RET_HEADS))
    return cos_t, sin_t


def _block_diag(blocks):
    g, r, c = blocks.shape
    eye = jnp.eye(g, dtype=bool)
    return jnp.where(eye[:, None, :, None], blocks[:, :, None, :], 0).reshape(g * r, g * c)


def _layer_weights(l, w_in, s5, s5_b_bar, s5_c_re, s5_c_im, s5_d, s5_glu_w, s5_glu_b, gdn_conv_w, gdn_a_log,
                   gdn_dt_bias, gdn_norm_w, w_out, router_group_w, router_group_b, router_expert_w,
                   router_expert_b):
    wi = w_in[l]
    g0 = RET_IN + S5_IN
    w_ab = jnp.pad(wi[:, g0 + GDN_QKV + GDN_HD:], ((0, 0), (0, LANE - 2 * GDN_HEADS)))
    in_ws = [wi[:, :RET_IN], wi[:, g0:g0 + GDN_QKV], wi[:, RET_IN:g0], wi[:, g0 + GDN_QKV:g0 + GDN_QKV + GDN_HD],
             w_ab]
    in_ws = [w.astype(BF16) for w in in_ws]
    ab_re, ab_im, bb_re, bb_im = s5
    n = S5_LANES
    sl = slice(l * n, (l + 1) * n)
    ab = jnp.concatenate([ab_re[sl].reshape(1, n), ab_im[sl].reshape(1, n)], axis=0)
    to_bd = lambda m: _block_diag(jnp.swapaxes(m[sl].reshape(S5_GROUPS, S5_STATE, S5_GROUP_CH), 1, 2))
    wb = jnp.concatenate([to_bd(bb_re), to_bd(bb_im)], axis=1).astype(BF16)
    wc_re = _block_diag(jnp.swapaxes(s5_c_re[l], 1, 2)).astype(BF16)
    wc_im = _block_diag(jnp.swapaxes(s5_c_im[l], 1, 2)).astype(BF16)
    s5_ws = (ab, wb, wc_re, wc_im, s5_d[l].reshape(1, S5_CH), s5_glu_w[l].astype(BF16),
             s5_glu_b[l].reshape(1, S5_CH))
    pad4 = lambda v: jnp.pad(v.reshape(1, GDN_HEADS), ((0, 0), (0, LANE - GDN_HEADS)))
    gdn_ws = (gdn_conv_w[l], pad4(gdn_a_log[l]), pad4(gdn_dt_bias[l]),
              jnp.tile(gdn_norm_w[l], GDN_HEADS).reshape(1, GDN_HD))
    wo = w_out[l].astype(BF16)
    out_ws = (wo[:RET_VD], wo[RET_VD:RET_VD + S5_CH], wo[RET_VD + S5_CH:])
    wr = jnp.pad(jnp.concatenate([router_expert_w[l], router_group_w[l]], axis=1).T,
                 ((0, LANE - N_EXPERTS - MOE_GROUPS), (0, 0))).astype(BF16)
    br = jnp.pad(jnp.concatenate([router_expert_b[l], router_group_b[l]]),
                 (0, LANE - N_EXPERTS - MOE_GROUPS)).reshape(LANE, 1)
    return in_ws, s5_ws, gdn_ws, out_ws, (wr, br)


def _mix_and_route(l, x, projected, bsz, seq, rope, states, layer_w, norm_mix, norm_ffn, cnt0):
    t = bsz * seq
    ret_s, s5_re, s5_im, gdn_s, gdn_buf = states
    in_ws, s5_ws, gdn_ws, out_ws, (wr, br) = layer_w
    zr, zq, zs, zg, zab = projected or _inproj(x, norm_mix[l].reshape(1, D_MODEL), in_ws)
    o_ret, ret_fin = _retention(zr, *rope, ret_s.reshape(DEPTH, bsz, RET_QD, RET_DV), l, bsz, seq)
    u_tm = jnp.swapaxes(zs.reshape(bsz, seq, S5_CH), 0, 1)
    o_s5_tm, re_fin, im_fin = _s5(u_tm, s5_re[l].reshape(bsz, S5_LANES), s5_im[l].reshape(bsz, S5_LANES), *s5_ws)
    o_s5 = jnp.swapaxes(o_s5_tm, 0, 1).reshape(t, S5_CH)
    buf0 = jnp.pad(gdn_buf[l], ((0, 0), (SUBLANE - (GDN_CONV - 1), 0), (0, 0)))
    o_gdn, nbuf, gdn_fin = _gdn(zq, zg, zab, *gdn_ws, buf0, gdn_s.reshape(DEPTH, bsz, GDN_HD, GDN_DV), l, bsz, seq)
    x1, h2_tiles, meta, counts = _outproj(x, o_ret, o_s5, o_gdn, *out_ws, norm_ffn[l].reshape(1, D_MODEL), wr, br,
                                          cnt0)
    new_states = (ret_fin.reshape(bsz, RET_HEADS, RET_DK, RET_DV), re_fin.reshape(bsz, S5_GROUPS, S5_STATE),
                  im_fin.reshape(bsz, S5_GROUPS, S5_STATE), gdn_fin.reshape(bsz, GDN_HEADS, GDN_DK, GDN_DV),
                  nbuf[:, SUBLANE - (GDN_CONV - 1):, :])
    return (x1, h2_tiles, meta), counts, new_states


def _trunks(xs, positions, states, layer_ws, norm_mix, norm_ffn, experts, norm_final):
    shapes = [x.shape[:2] for x in xs]
    ropes = [_rope_tables(pos) for pos in positions]
    xs = [x.reshape(b * s, D_MODEL) for x, (b, s) in zip(xs, shapes)]
    outs = [[[] for _ in range(5)] for _ in xs]
    projected = [None for _ in xs]
    for l in range(DEPTH):
        counts = jnp.zeros((LANE, LANE), F32)
        routed = []
        for i, x in enumerate(xs):
            stream, counts, new_states = _mix_and_route(l, x, projected[i], *shapes[i], ropes[i], states[i],
                                                        layer_ws[l], norm_mix, norm_ffn, counts)
            routed.append(stream)
            for lst, s in zip(outs[i], new_states):
                lst.append(s)
        last = l == DEPTH - 1
        norm_w = norm_final if last else norm_mix[l + 1]
        xs, projected = _moe(*routed, counts, l, *experts, norm_w.reshape(1, D_MODEL), last,
                             [] if last else layer_ws[l + 1][0])
    ys = [x.reshape(b, s, D_MODEL) for x, (b, s) in zip(xs, shapes)]
    return ys, [tuple(jnp.stack(o) for o in per_stream) for per_stream in outs]


def kernel(x_prompt, x_sample, state_ret, state_s5_re, state_s5_im, state_gdn, state_gdn_conv, norm_mix, w_in, s5_a_re, s5_a_im, s5_log_dt, s5_b_re, s5_b_im, s5_c_re, s5_c_im, s5_d, s5_glu_w, s5_glu_b, gdn_conv_w, gdn_a_log, gdn_dt_bias, gdn_norm_w, w_out, norm_ffn, router_group_w, router_group_b, router_expert_w, router_expert_b, expert_w1, expert_w3, expert_w2, norm_final):
    n = DEPTH * S5_LANES
    col = lambda a: a.reshape(n, 1)
    log_dt = jnp.broadcast_to(s5_log_dt[:, :, None], (DEPTH, S5_GROUPS, S5_STATE))
    s5 = _s5_prep(col(s5_a_re), col(s5_a_im), col(log_dt), s5_b_re.reshape(n, S5_GROUP_CH),
                  s5_b_im.reshape(n, S5_GROUP_CH))
    layer_ws = [_layer_weights(l, w_in, s5, None, s5_c_re, s5_c_im, s5_d, s5_glu_w, s5_glu_b, gdn_conv_w,
                               gdn_a_log, gdn_dt_bias, gdn_norm_w, w_out, router_group_w, router_group_b,
                               router_expert_w, router_expert_b) for l in range(DEPTH)]
    experts = (expert_w1, expert_w3, expert_w2)
    bp, lp, _ = x_prompt.shape
    zero_states = (jnp.zeros((DEPTH, bp, RET_HEADS, RET_DK, RET_DV), F32),
                   jnp.zeros((DEPTH, bp, S5_GROUPS, S5_STATE), F32),
                   jnp.zeros((DEPTH, bp, S5_GROUPS, S5_STATE), F32),
                   jnp.zeros((DEPTH, bp, GDN_HEADS, GDN_DK, GDN_DV), F32),
                   jnp.zeros((DEPTH, bp, GDN_CONV - 1, GDN_QKV), F32))
    sample_states = (state_ret, state_s5_re, state_s5_im, state_gdn, state_gdn_conv)
    positions = (jnp.arange(lp, dtype=F32), PAST_LEN + jnp.arange(x_sample.shape[1], dtype=F32))
    (y_p, y_s), (p_states, s_states) = _trunks((x_prompt, x_sample), positions, (zero_states, sample_states),
                                               layer_ws, norm_mix, norm_ffn, experts, norm_final)
    return (y_p, y_s) + p_states + s_states
```

```python
import functools
import math

import jax
import jax.numpy as jnp
import numpy as np
from jax import lax
from jax.experimental import pallas as pl
from jax.experimental.pallas import tpu as pltpu

F32 = jnp.float32
BF16 = jnp.bfloat16
I32 = jnp.int32

D_MODEL = 1024
DEPTH = 2
RET_HEADS, RET_DK, RET_DV = 4, 64, 128
S5_GROUPS, S5_GROUP_CH, S5_STATE = 16, 16, 64
S5_CH = S5_GROUPS * S5_GROUP_CH
S5_LANES = S5_GROUPS * S5_STATE
GDN_HEADS, GDN_DK, GDN_DV, GDN_CONV = 4, 64, 64, 4
GDN_HD = GDN_HEADS * GDN_DK
GDN_QKV = 3 * GDN_HD
RET_QD = RET_HEADS * RET_DK
RET_VD = RET_HEADS * RET_DV
RET_IN = 2 * RET_QD + 2 * RET_VD
S5_IN = S5_CH
GDN_IN = GDN_QKV + GDN_HD + 2 * GDN_HEADS
GDN_CHUNK = 64
RET_CHUNK = 64
MOE_GROUPS, EXPERTS_PER_GROUP = 4, 8
N_EXPERTS = MOE_GROUPS * EXPERTS_PER_GROUP
D_EXPERT = 512
ROPE_BASE = 10000.0
EPS = 1e-6
PAST_LEN = 16384

LANE = 128
SUBLANE = 8
ROW_TILE = 512
MOE_BLK = 256
GATHER_TILE = 256
ISSUE_UNROLL = 8
SEQS_PER_STEP = 8
LONG_SEQS_PER_STEP = 2
VMEM_LIMIT = 56 * 1024 * 1024


def _params(*sem):
    return pltpu.CompilerParams(dimension_semantics=sem, vmem_limit_bytes=VMEM_LIMIT)


def _dot(a, b):
    return jnp.dot(a.astype(BF16), b.astype(BF16), preferred_element_type=F32)


def _dot_nt(a, b):
    return lax.dot_general(a.astype(BF16), b.astype(BF16), (((1,), (1,)), ((), ())),
                           preferred_element_type=F32)


def _dot_tn(a, b):
    return lax.dot_general(a.astype(BF16), b.astype(BF16), (((0,), (0,)), ((), ())),
                           preferred_element_type=F32)


def _dot_hi(a, b):
    a1 = a.astype(BF16)
    a2 = (a - a1.astype(F32)).astype(BF16)
    b1 = b.astype(BF16)
    b2 = (b - b1.astype(F32)).astype(BF16)
    d = lambda x, y: jnp.dot(x, y, preferred_element_type=F32)
    return d(a1, b1) + (d(a1, b2) + d(a2, b1))


def _split3(x):
    p1 = x.astype(BF16)
    r1 = x - p1.astype(F32)
    p2 = r1.astype(BF16)
    p3 = (r1 - p2.astype(F32)).astype(BF16)
    return p1, p2, p3


def _dot_sel_l(sel, x):
    p1, p2, p3 = _split3(x)
    d = lambda p: jnp.dot(sel, p, preferred_element_type=F32)
    return d(p1) + d(p2) + d(p3)


def _dot_sel_r(x, sel):
    p1, p2, p3 = _split3(x)
    d = lambda p: jnp.dot(p, sel, preferred_element_type=F32)
    return d(p1) + d(p2) + d(p3)


def _rms(x):
    return x * lax.rsqrt(jnp.mean(x * x, axis=-1, keepdims=True) + EPS)


def _silu(x):
    return x * jax.nn.sigmoid(x)


def _inproj_body(x_ref, nw_ref, wr_ref, wq_ref, ws_ref, wg_ref, wab_ref,
                 zr_ref, zq_ref, zs_ref, zg_ref, zab_ref):
    h = (_rms(x_ref[...]) * nw_ref[...]).astype(BF16)
    for w_ref, z_ref in ((wr_ref, zr_ref), (wq_ref, zq_ref), (ws_ref, zs_ref),
                         (wg_ref, zg_ref), (wab_ref, zab_ref)):
        z_ref[...] = jnp.dot(h, w_ref[...], preferred_element_type=F32)


def _inproj(x, nw, ws):
    t = x.shape[0]
    tm = min(ROW_TILE, t)
    widths = [w.shape[1] for w in ws]
    row = lambda n: pl.BlockSpec((tm, n), lambda i: (i, 0))
    full = lambda a: pl.BlockSpec(a.shape, lambda i: (0, 0))
    return pl.pallas_call(
        _inproj_body,
        grid=(t // tm,),
        in_specs=[row(D_MODEL), full(nw)] + [full(w) for w in ws],
        out_specs=[row(n) for n in widths],
        out_shape=[jax.ShapeDtypeStruct((t, n), F32) for n in widths],
        compiler_params=_params("parallel"),
        name="inproj",
    )(x, nw, *ws)


def _seqs_per_step(bsz, blocks_per_seq):
    want = SEQS_PER_STEP if blocks_per_seq == 1 else LONG_SEQS_PER_STEP
    return want if bsz % want == 0 else 1


def _ret_log_gamma(h):
    return math.log(1.0 - 2.0 ** (-5.0 - h))


def _ret_body(zr_ref, cos_ref, sin_ref, s0_ref, o_ref, sfin_ref, s_scr, *, nseq, rows, chunk):
    j = pl.program_id(1)
    stack = RET_HEADS * chunk

    @pl.when(j == 0)
    def _():
        s_scr[...] = s0_ref[...]

    lane = lax.broadcasted_iota(I32, (rows, RET_QD), 1)
    first_half = (lane % RET_DK) < (RET_DK // 2)

    def rotary(x):
        swapped = jnp.where(first_half, pltpu.roll(x, RET_QD - RET_DK // 2, 1),
                            pltpu.roll(x, RET_DK // 2, 1))
        return x * cos_ref[...] + swapped * sin_ref[...]

    q = [rotary(zr_ref[s, :, 0:RET_QD]) for s in range(nseq)]
    k = [rotary(zr_ref[s, :, RET_QD:2 * RET_QD]) * (RET_DK ** -0.5) for s in range(nseq)]

    def by_head(idx, fn):
        out = jnp.zeros(idx.shape, F32)
        for h in range(RET_HEADS):
            out = jnp.where(idx == h, fn(h), out)
        return out

    lg_lane = by_head(lax.broadcasted_iota(I32, (chunk, RET_QD), 1) // RET_DK, _ret_log_gamma)
    pos = lax.broadcasted_iota(I32, (chunk, RET_QD), 0).astype(F32)
    q_scale = jnp.exp((pos + 1.0) * lg_lane)
    k_scale = jnp.exp((chunk - 1.0 - pos) * lg_lane)
    st_row = lax.broadcasted_iota(I32, (stack, stack), 0)
    st_col = lax.broadcasted_iota(I32, (stack, stack), 1)
    causal = ((st_row // chunk) == (st_col // chunk)) & (st_row >= st_col)
    lg_stack = by_head(st_row // chunk, _ret_log_gamma)
    decay = jnp.where(causal, jnp.exp(jnp.where(causal, (st_row - st_col).astype(F32), 0.0) * lg_stack), 0.0)
    q_rows = (lax.broadcasted_iota(I32, (stack, RET_QD), 0) // chunk) == \
             (lax.broadcasted_iota(I32, (stack, RET_QD), 1) // RET_DK)
    v_rows = (lax.broadcasted_iota(I32, (stack, RET_VD), 0) // chunk) == \
             (lax.broadcasted_iota(I32, (stack, RET_VD), 1) // RET_DV)
    s_row_head = lax.broadcasted_iota(I32, (RET_QD, RET_DV), 0) // RET_DK
    s_decay = by_head(s_row_head, lambda h: math.exp(chunk * _ret_log_gamma(h)))
    kv_diag = (lax.broadcasted_iota(I32, (RET_QD, RET_VD), 0) // RET_DK) == \
              (lax.broadcasted_iota(I32, (RET_QD, RET_VD), 1) // RET_DV)

    def tile4(a):
        return jnp.concatenate([a] * RET_HEADS, axis=0)

    def collapse(a):
        out = a[0:chunk]
        for h in range(1, RET_HEADS):
            out = out + a[h * chunk:(h + 1) * chunk]
        return out

    seqs = range(nseq)
    n_chunks = rows // chunk
    state = [s_scr[s] for s in seqs]
    for ci in range(n_chunks):
        r = slice(ci * chunk, (ci + 1) * chunk)
        v_cs = [zr_ref[s, r, 2 * RET_QD:2 * RET_QD + RET_VD] for s in seqs]
        scores = [_dot_nt(jnp.where(q_rows, tile4(q[s][r]), 0.0), tile4(k[s][r])) * decay for s in seqs]
        o_intra = [collapse(_dot(scores[s], jnp.where(v_rows, tile4(v_cs[s]), 0.0))) for s in seqs]
        s_full = [jnp.concatenate([jnp.where(s_row_head == h, state[s], 0.0) for h in range(RET_HEADS)], axis=1)
                  for s in seqs]
        o_cs = [o_intra[s] + _dot(q[s][r] * q_scale, s_full[s]) for s in seqs]
        for s in seqs:
            gate = zr_ref[s, r, 2 * RET_QD + RET_VD:2 * RET_QD + 2 * RET_VD]
            for h in range(RET_HEADS):
                sl = slice(h * RET_DV, (h + 1) * RET_DV)
                o_ref[s, r, sl] = _rms(o_cs[s][:, sl]) * _silu(gate[:, sl])
        for s in seqs:
            kv = jnp.where(kv_diag, _dot_tn(k[s][r] * k_scale, v_cs[s]), 0.0)
            kv_own = kv[:, 0:RET_DV]
            for h in range(1, RET_HEADS):
                kv_own = kv_own + kv[:, h * RET_DV:(h + 1) * RET_DV]
            state[s] = state[s] * s_decay + kv_own
    for s in seqs:
        s_scr[s] = state[s]

    @pl.when(j == pl.num_programs(1) - 1)
    def _():
        for s in seqs:
            sfin_ref[s] = state[s]


def _retention(zr, cos, sin, s0, layer, bsz, seq):
    chunk = min(RET_CHUNK, seq)
    rows = min(4 * chunk, seq)
    nj = seq // rows
    nseq = _seqs_per_step(bsz, nj)
    o, s_fin = pl.pallas_call(
        functools.partial(_ret_body, nseq=nseq, rows=rows, chunk=chunk),
        grid=(bsz // nseq, nj),
        in_specs=[pl.BlockSpec((nseq, rows, RET_IN), lambda b, j: (b, j, 0)),
                  pl.BlockSpec((rows, RET_QD), lambda b, j: (j, 0)),
                  pl.BlockSpec((rows, RET_QD), lambda b, j: (j, 0)),
                  pl.BlockSpec((None, nseq, RET_QD, RET_DV), lambda b, j: (layer, b, 0, 0))],
        out_specs=[pl.BlockSpec((nseq, rows, RET_VD), lambda b, j: (b, j, 0)),
                   pl.BlockSpec((nseq, RET_QD, RET_DV), lambda b, j: (b, 0, 0))],
        out_shape=[jax.ShapeDtypeStruct((bsz, seq, RET_VD), F32),
                   jax.ShapeDtypeStruct((bsz, RET_QD, RET_DV), F32)],
        scratch_shapes=[pltpu.VMEM((nseq, RET_QD, RET_DV), F32)],
        compiler_params=_params("parallel", "arbitrary"),
        name="retention",
    )(zr.reshape(bsz, seq, RET_IN), cos, sin, s0)
    return o.reshape(bsz * seq, RET_VD), s_fin


def _s5_prep_body(are_ref, aim_ref, ldt_ref, bre_ref, bim_ref, abre_ref, abim_ref, bbre_ref, bbim_ref):
    lam_re, lam_im = are_ref[...], aim_ref[...]
    dt = jnp.exp(ldt_ref[...])
    mag = jnp.exp(lam_re * dt)
    ab_re = mag * jnp.cos(lam_im * dt)
    ab_im = mag * jnp.sin(lam_im * dt)
    den = lam_re * lam_re + lam_im * lam_im
    f_re = ((ab_re - 1.0) * lam_re + ab_im * lam_im) / den
    f_im = (ab_im * lam_re - (ab_re - 1.0) * lam_im) / den
    abre_ref[...] = ab_re
    abim_ref[...] = ab_im
    bbre_ref[...] = f_re * bre_ref[...] - f_im * bim_ref[...]
    bbim_ref[...] = f_re * bim_ref[...] + f_im * bre_ref[...]


def _s5_prep(a_re, a_im, log_dt, b_re, b_im):
    n = a_re.shape[0]
    col = jax.ShapeDtypeStruct((n, 1), F32)
    mat = jax.ShapeDtypeStruct((n, S5_GROUP_CH), F32)
    return pl.pallas_call(_s5_prep_body, out_shape=[col, col, mat, mat], name="s5_prep")(
        a_re, a_im, log_dt, b_re, b_im)


def _gelu_tanh(x):
    return x * (0.5 * (1.0 + jnp.tanh(math.sqrt(2.0 / math.pi) * (x + 0.044715 * (x * x * x)))))


def _s5_body(u_ref, h0re_ref, h0im_ref, ab_ref, wb_ref, wcre_ref, wcim_ref, d_ref, gw_ref, gb_ref,
             o_ref, hre_ref, him_ref, bu_scr, st_scr, *, steps):
    j = pl.program_id(1)

    @pl.when(j == 0)
    def _():
        st_scr[0] = h0re_ref[...]
        st_scr[1] = h0im_ref[...]

    rows = steps * SUBLANE
    u = u_ref[...].reshape(rows, S5_CH)
    bu_scr[...] = _dot(u, wb_ref[...])
    a_re = jnp.broadcast_to(ab_ref[0:1, :], (SUBLANE, S5_LANES))
    a_im = jnp.broadcast_to(ab_ref[1:2, :], (SUBLANE, S5_LANES))

    def step(t, carry):
        h_re, h_im = carry
        r = pl.ds(pl.multiple_of(t * SUBLANE, SUBLANE), SUBLANE)
        n_re = a_re * h_re - a_im * h_im + bu_scr[r, 0:S5_LANES]
        n_im = a_re * h_im + a_im * h_re + bu_scr[r, S5_LANES:2 * S5_LANES]
        bu_scr[r, 0:S5_LANES] = n_re
        bu_scr[r, S5_LANES:2 * S5_LANES] = n_im
        return n_re, n_im

    h_re, h_im = lax.fori_loop(0, steps, step, (st_scr[0], st_scr[1]))
    st_scr[0] = h_re
    st_scr[1] = h_im
    hre_ref[...] = h_re
    him_ref[...] = h_im

    y = _dot(bu_scr[:, 0:S5_LANES], wcre_ref[...]) - _dot(bu_scr[:, S5_LANES:2 * S5_LANES], wcim_ref[...])
    y = _gelu_tanh(y + d_ref[...] * u)
    y = y * jax.nn.sigmoid(_dot(y, gw_ref[...]) + gb_ref[...])
    o_ref[...] = y.reshape(steps, SUBLANE, S5_CH)


def _s5(u_tm, h0_re, h0_im, ab, wb, wc_re, wc_im, d_skip, glu_w, glu_b):
    seq, bsz, _ = u_tm.shape
    steps = min(64, seq)
    full = lambda a: pl.BlockSpec(a.shape, lambda g, j: (0,) * a.ndim)
    st = pl.BlockSpec((SUBLANE, S5_LANES), lambda g, j: (g, 0))
    return pl.pallas_call(
        functools.partial(_s5_body, steps=steps),
        grid=(bsz // SUBLANE, seq // steps),
        in_specs=[pl.BlockSpec((steps, SUBLANE, S5_CH), lambda g, j: (j, g, 0)), st, st,
                  full(ab), full(wb), full(wc_re), full(wc_im), full(d_skip), full(glu_w), full(glu_b)],
        out_specs=[pl.BlockSpec((steps, SUBLANE, S5_CH), lambda g, j: (j, g, 0)), st, st],
        out_shape=[jax.ShapeDtypeStruct((seq, bsz, S5_CH), F32),
                   jax.ShapeDtypeStruct((bsz, S5_LANES), F32),
                   jax.ShapeDtypeStruct((bsz, S5_LANES), F32)],
        scratch_shapes=[pltpu.VMEM((steps * SUBLANE, 2 * S5_LANES), F32),
                        pltpu.VMEM((2, SUBLANE, S5_LANES), F32)],
        compiler_params=_params("parallel", "arbitrary"),
        name="s5",
    )(u_tm, h0_re, h0_im, ab, wb, wc_re, wc_im, d_skip, glu_w, glu_b)


def _gdn_body(zq_ref, zg_ref, zab_ref, cw_ref, alog_ref, dtb_ref, nw_ref, buf0_ref, s0_ref,
              ones_bd_ref, tri_ref, ea_ref, eb_ref, ec_ref, spread_ref, gather_ref,
              o_ref, nbuf_ref, sfin_ref,
              ext_scr, q_scr, k_scr, kb_scr, vb_scr, g_scr, g4_scr, s_scr, *, nseq, rows, chunk):
    j = pl.program_id(1)
    stack = GDN_HEADS * chunk
    bd_state = (lax.broadcasted_iota(I32, (GDN_HD, GDN_HD), 0) // GDN_DK) == \
               (lax.broadcasted_iota(I32, (GDN_HD, GDN_HD), 1) // GDN_DV)

    @pl.when(j == 0)
    def _():
        for s in range(nseq):
            ext_scr[s, 0:SUBLANE, :] = buf0_ref[s]
            s_scr[s] = jnp.where(bd_state, _dot_sel_r(s0_ref[s], spread_ref[...]), 0.0)

    convs = []
    for s in range(nseq):
        ext_scr[s, SUBLANE:SUBLANE + rows, :] = zq_ref[s]
        conv = ext_scr[s, SUBLANE - 3:SUBLANE - 3 + rows, :] * cw_ref[0:1, :]
        for i in range(1, GDN_CONV):
            conv = conv + ext_scr[s, SUBLANE - 3 + i:SUBLANE - 3 + i + rows, :] * cw_ref[i:i + 1, :]
        tail = ext_scr[s, rows:rows + SUBLANE, :]
        nbuf_ref[s] = tail
        ext_scr[s, 0:SUBLANE, :] = tail
        convs.append(conv)
    qkv = _silu(convs[0] if nseq == 1 else jnp.concatenate(convs, axis=0))

    ones_bd = ones_bd_ref[...]
    q_raw = qkv[:, 0:GDN_HD]
    k_raw = qkv[:, GDN_HD:2 * GDN_HD]
    q_scr[...] = q_raw * lax.rsqrt(_dot_sel_r(q_raw * q_raw, ones_bd) + EPS) * (GDN_DK ** -0.5)
    k_n = k_raw * lax.rsqrt(_dot_sel_r(k_raw * k_raw, ones_bd) + EPS)
    k_scr[...] = k_n

    ab = zab_ref[...].reshape(nseq * rows, LANE)
    x = ab + dtb_ref[...]
    softplus = jnp.maximum(x, 0.0) + jnp.log1p(jnp.exp(-jnp.abs(x)))
    g_pad = -jnp.exp(alog_ref[...]) * softplus
    beta = _dot_sel_r(jax.nn.sigmoid(ab), eb_ref[...])
    g_cum = _dot_sel_l(tri_ref[...], g_pad)
    g_scr[...] = _dot_sel_r(g_cum, ea_ref[...])
    g4_scr[...] = _dot_sel_r(g_pad, ec_ref[...])
    kb_scr[...] = k_n * beta
    vb_scr[...] = qkv[:, 2 * GDN_HD:3 * GDN_HD] * beta

    w_row = lax.broadcasted_iota(I32, (chunk, stack), 0)
    w_col = lax.broadcasted_iota(I32, (chunk, stack), 1) % chunk
    strict_w = w_row > w_col
    causal_w = w_row >= w_col
    same_head = (lax.broadcasted_iota(I32, (stack, stack), 0) // chunk) == \
                (lax.broadcasted_iota(I32, (stack, stack), 1) // chunk)
    head_rows = (lax.broadcasted_iota(I32, (stack, GDN_HD), 0) // chunk) == \
                (lax.broadcasted_iota(I32, (stack, GDN_HD), 1) // GDN_DK)
    head_rows2 = jnp.concatenate([head_rows, head_rows], axis=1)
    tri_c = tri_ref[0:chunk, 0:chunk]
    packed_ok = chunk % (2 * SUBLANE) == 0

    def tile4(a):
        return jnp.concatenate([a] * GDN_HEADS, axis=0)

    def split2(a):
        hi = a.astype(BF16)
        return hi, (a - hi.astype(F32)).astype(BF16)

    def stack_masked(parts, mask):
        if packed_ok:
            return [jnp.where(mask, tile4(p), jnp.zeros((), BF16)) for p in parts]
        return [jnp.where(mask, tile4(p.astype(F32)), 0.0).astype(BF16) for p in parts]

    def on_diag(parts):
        return stack_masked(parts, same_head)

    def mm(a, b):
        return jnp.dot(a, b, preferred_element_type=F32)

    def mm_hi(a_parts, b_parts):
        return mm(a_parts[0], b_parts[0]) + (mm(a_parts[0], b_parts[1]) + mm(a_parts[1], b_parts[0]))

    n_chunks = rows // chunk
    every = range(nseq * n_chunks)
    rows_of = lambda u: slice(u * chunk, (u + 1) * chunk)
    solved = []
    for u in every:
        r = rows_of(u)
        q_c, k_c, kb_c, vb_c, g_c = q_scr[r, :], k_scr[r, :], kb_scr[r, :], vb_scr[r, :], g_scr[r, :]
        exp_g = jnp.exp(g_c)
        g_diff = _dot_sel_l(tri_c, jnp.where(strict_w, g4_scr[r, :], 0.0))
        decay = jnp.exp(jnp.where(causal_w, g_diff, 0.0))
        k_heads = jnp.where(head_rows, tile4(k_c), 0.0)
        lmat = jnp.where(strict_w, _dot_nt(kb_c, k_heads) * decay, 0.0)
        attn = jnp.where(causal_w, _dot_nt(q_c, k_heads) * decay, 0.0)
        rhs_c = jnp.concatenate([vb_c, kb_c * exp_g], axis=1)
        solved.append((lmat, attn, exp_g, rhs_c))

    one = lambda a: a.astype(BF16)
    t_acc = [-solved[ci][0] for ci in every]
    p_one = [one(solved[ci][0]) for ci in every]
    p_diag = [on_diag([p])[0] for p in p_one]
    span = 2
    while span < chunk + 1:
        power = [mm(p_one[ci], p_diag[ci]) for ci in every]
        p_one = [one(p) for p in power]
        p_diag = [on_diag([p])[0] for p in p_one]
        t_acc = [t_acc[ci] + power[ci] + mm(one(t_acc[ci]), p_diag[ci]) for ci in every]
        span *= 2
    t_one = [one(t) for t in t_acc]
    apply_inv = lambda ci, v: v + mm(t_one[ci], stack_masked([one(v)], head_rows2)[0])
    sol0 = [apply_inv(ci, solved[ci][3]) for ci in every]
    resid = [solved[ci][3] - (sol0[ci] + mm_hi(split2(solved[ci][0]), stack_masked(split2(sol0[ci]), head_rows2)))
             for ci in every]
    sols = [sol0[ci] + apply_inv(ci, resid[ci]) for ci in every]

    seqs = range(nseq)
    s_cur = [s_scr[s] for s in seqs]
    for ci in range(n_chunks):
        us = [s * n_chunks + ci for s in seqs]
        v_new = [sols[u][:, 0:GDN_HD] - _dot(sols[u][:, GDN_HD:2 * GDN_HD], s_cur[s]) for s, u in enumerate(us)]
        o_cs = [_dot(q_scr[rows_of(u), :] * solved[u][2], s_cur[s])
                + _dot(solved[u][1], jnp.where(head_rows, tile4(v_new[s]), 0.0)) for s, u in enumerate(us)]
        for s, u in enumerate(us):
            g_c = g_scr[rows_of(u), :]
            g_last = g_c[chunk - 1:chunk, :]
            k_dec = k_scr[rows_of(u), :] * jnp.exp(g_last - g_c)
            s_cur[s] = s_cur[s] * jnp.exp(g_last) + jnp.where(bd_state, _dot_tn(k_dec, v_new[s]), 0.0)
        for s, u in enumerate(us):
            r = slice(ci * chunk, (ci + 1) * chunk)
            ms = _dot_sel_r(o_cs[s] * o_cs[s], ones_bd) * (1.0 / GDN_DV)
            o_ref[s, r, :] = o_cs[s] * lax.rsqrt(ms + EPS) * nw_ref[...] * _silu(zg_ref[s, r, :])
    for s in seqs:
        s_scr[s] = s_cur[s]

    @pl.when(j == pl.num_programs(1) - 1)
    def _():
        for s in seqs:
            sfin_ref[s] = _dot_sel_r(s_scr[s], gather_ref[...])


def _gdn_consts(rows, chunk):
    stack = GDN_HEADS * chunk
    blk = lambda n, c: (np.arange(n)[:, None] // c) == (np.arange(n)[None, :] // c)
    ones_bd = blk(GDN_HD, GDN_DK)
    tri = blk(rows, chunk) & (np.arange(rows)[:, None] >= np.arange(rows)[None, :])
    src = np.arange(LANE)[:, None]
    ea = src == np.arange(GDN_HD)[None, :] // GDN_DK
    eb = src == GDN_HEADS + np.arange(GDN_HD)[None, :] // GDN_DK
    ec = src == np.arange(stack)[None, :] // chunk
    spread = np.arange(GDN_DV)[:, None] == np.arange(GDN_HD)[None, :] % GDN_DV
    return tuple(jnp.asarray(m, dtype=BF16) for m in (ones_bd, tri, ea, eb, ec, spread, spread.T))


def _gdn(zq, zg, zab, conv_w, alog_pad, dtb_pad, nw, buf0, s0, layer, bsz, seq):
    chunk = min(GDN_CHUNK, seq)
    rows = min(4 * chunk, seq)
    nj = seq // rows
    nseq = _seqs_per_step(bsz, nj)
    stack = GDN_HEADS * chunk
    consts = _gdn_consts(nseq * rows, chunk)
    row = lambda n: pl.BlockSpec((nseq, rows, n), lambda b, j: (b, j, 0))
    full = lambda a: pl.BlockSpec(a.shape, lambda b, j: (0,) * a.ndim)
    per_b = lambda *s: pl.BlockSpec((nseq,) + s, lambda b, j: (b, 0, 0))
    by_seq = lambda a: a.reshape(bsz, seq, a.shape[-1])
    o, nbuf, s_fin = pl.pallas_call(
        functools.partial(_gdn_body, nseq=nseq, rows=rows, chunk=chunk),
        grid=(bsz // nseq, nj),
        in_specs=[row(GDN_QKV), row(GDN_HD), row(LANE), full(conv_w), full(alog_pad), full(dtb_pad), full(nw),
                  per_b(SUBLANE, GDN_QKV),
                  pl.BlockSpec((None, nseq, GDN_HD, GDN_DV), lambda b, j: (layer, b, 0, 0))] +
                 [full(c) for c in consts],
        out_specs=[row(GDN_HD), per_b(SUBLANE, GDN_QKV), per_b(GDN_HD, GDN_DV)],
        out_shape=[jax.ShapeDtypeStruct((bsz, seq, GDN_HD), F32),
                   jax.ShapeDtypeStruct((bsz, SUBLANE, GDN_QKV), F32),
                   jax.ShapeDtypeStruct((bsz, GDN_HD, GDN_DV), F32)],
        scratch_shapes=[pltpu.VMEM((nseq, rows + SUBLANE, GDN_QKV), F32)] +
                       [pltpu.VMEM((nseq * rows, GDN_HD), F32)] * 5 +
                       [pltpu.VMEM((nseq * rows, stack), F32), pltpu.VMEM((nseq, GDN_HD, GDN_HD), F32)],
        compiler_params=_params("parallel", "arbitrary"),
        name="gdn",
    )(by_seq(zq), by_seq(zg), by_seq(zab), conv_w, alog_pad, dtb_pad, nw, buf0, s0, *consts)
    return o.reshape(bsz * seq, GDN_HD), nbuf, s_fin


def _outproj_body(x_ref, oret_ref, os5_ref, ogdn_ref, w1_ref, w2_ref, w3_ref, nw_ref, wr_ref, br_ref, tri_ref,
                  cnt0_ref, x1_ref, h2_ref, meta_ref, cnt_ref, carry_scr, *, tm):
    i = pl.program_id(0)

    @pl.when(i == 0)
    def _():
        carry_scr[...] = cnt0_ref[...]

    mix = _dot(oret_ref[...], w1_ref[...]) + _dot(os5_ref[...], w2_ref[...]) + _dot(ogdn_ref[...], w3_ref[...])
    x1 = x_ref[...] + mix
    x1_ref[...] = x1
    h2 = _rms(x1) * nw_ref[...]
    for s in range(D_MODEL // LANE):
        h2_ref[pl.ds(s, tm, stride=SUBLANE), :] = h2[:, s * LANE:(s + 1) * LANE]

    logits = _dot_nt(wr_ref[...], h2) + br_ref[...]
    row_i = lax.broadcasted_iota(I32, (LANE, tm), 0)
    row = row_i.astype(F32)
    neg = -jnp.inf
    big = float(LANE)
    g_log = jnp.where((row_i >= N_EXPERTS) & (row_i < N_EXPERTS + MOE_GROUPS), logits, neg)
    g_max = jnp.max(g_log, axis=0, keepdims=True)
    grp = jnp.min(jnp.where(g_log == g_max, row - N_EXPERTS, big), axis=0, keepdims=True)
    p_grp = 1.0 / jnp.sum(jnp.exp(g_log - g_max), axis=0, keepdims=True)
    in_grp = (row >= grp * EXPERTS_PER_GROUP) & (row < (grp + 1.0) * EXPERTS_PER_GROUP)
    e_log = jnp.where(in_grp, logits, neg)
    v1 = jnp.max(e_log, axis=0, keepdims=True)
    i1 = jnp.min(jnp.where(e_log == v1, row, big), axis=0, keepdims=True)
    e_log2 = jnp.where(row == i1, neg, e_log)
    v2 = jnp.max(e_log2, axis=0, keepdims=True)
    i2 = jnp.min(jnp.where(e_log2 == v2, row, big), axis=0, keepdims=True)
    e2 = jnp.exp(v2 - v1)
    gate1 = p_grp / (1.0 + e2)
    gate2 = p_grp * e2 / (1.0 + e2)

    oh1 = row == i1
    oh2 = row == i2
    picked = jnp.where(oh1 | oh2, 1.0, 0.0)
    seen = carry_scr[:, 0:1]
    before = jnp.dot(picked.astype(BF16), tri_ref[...], preferred_element_type=F32) + seen
    rank1 = jnp.sum(jnp.where(oh1, before, 0.0), axis=0, keepdims=True)
    rank2 = jnp.sum(jnp.where(oh2, before, 0.0), axis=0, keepdims=True)
    total = seen + jnp.sum(picked, axis=1, keepdims=True)
    carry_scr[...] = jnp.broadcast_to(total, carry_scr.shape)
    cnt_ref[...] = jnp.broadcast_to(total, cnt_ref.shape)
    meta_ref[...] = jnp.concatenate([i1, i2, gate1, gate2, rank1, rank2, jnp.zeros((2, tm), F32)], axis=0)


def _outproj(x, o_ret, o_s5, o_gdn, w1, w2, w3, nw, wr, br, cnt0):
    t = x.shape[0]
    tm = min(ROW_TILE, t)
    tri = jnp.asarray(np.arange(tm)[:, None] < np.arange(tm)[None, :], dtype=BF16)
    row = lambda n: pl.BlockSpec((tm, n), lambda i: (i, 0))
    full = lambda a: pl.BlockSpec(a.shape, lambda i: (0, 0))
    return pl.pallas_call(
        functools.partial(_outproj_body, tm=tm),
        grid=(t // tm,),
        in_specs=[row(D_MODEL), row(RET_VD), row(S5_CH), row(GDN_HD),
                  full(w1), full(w2), full(w3), full(nw), full(wr), full(br), full(tri), full(cnt0)],
        out_specs=[row(D_MODEL), pl.BlockSpec((tm * SUBLANE, LANE), lambda i: (i, 0)),
                   pl.BlockSpec((SUBLANE, tm), lambda i: (0, i)), pl.BlockSpec((LANE, LANE), lambda i: (0, 0))],
        out_shape=[jax.ShapeDtypeStruct((t, D_MODEL), F32),
                   jax.ShapeDtypeStruct((t * SUBLANE, LANE), F32),
                   jax.ShapeDtypeStruct((SUBLANE, t), F32),
                   jax.ShapeDtypeStruct((LANE, LANE), F32)],
        scratch_shapes=[pltpu.VMEM((LANE, LANE), F32)],
        compiler_params=_params("arbitrary"),
        name="outproj_router",
    )(x, o_ret, o_s5, o_gdn, w1, w2, w3, nw, wr, br, tri, cnt0)


def _token_rows(ref, idx):
    return ref.at[pl.ds(pl.multiple_of(idx * SUBLANE, SUBLANE), SUBLANE)]


ZERO_ROWS = 128


def _zero_segment(zero_scr, xb_ref, sem, start, length, wait):
    def piece(off, n):
        cp = pltpu.make_async_copy(zero_scr.at[pl.ds(0, n * SUBLANE)],
                                   xb_ref.at[pl.ds(pl.multiple_of(off * SUBLANE, SUBLANE), n * SUBLANE)], sem)
        cp.wait() if wait else cp.start()

    n_big = length // ZERO_ROWS

    def big(i, carry):
        piece(start + i * ZERO_ROWS, ZERO_ROWS)
        return carry

    lax.fori_loop(0, n_big, big, 0)
    off = start + n_big * ZERO_ROWS
    rem = length - n_big * ZERO_ROWS
    bit = ZERO_ROWS // 2
    while bit >= 1:
        has = (rem & bit) != 0
        pl.when(has)(functools.partial(piece, off, bit))
        off = off + jnp.where(has, bit, 0)
        bit //= 2


def _scatter_body(d1_ref, d2_ref, zs_ref, zl_ref, src_a_ref, src_b_ref, xb_ref,
                  zero_scr, sem, zsem, *, tm, tiles_a):
    i = pl.program_id(0)
    base = i * tm

    @pl.when(i == 0)
    def _():
        zero_scr[...] = jnp.zeros_like(zero_scr)
        for wait in (False, True):
            lax.fori_loop(0, N_EXPERTS + 1,
                          lambda s, c, wait=wait: (_zero_segment(zero_scr, xb_ref, zsem, zs_ref[s], zl_ref[s], wait),
                                                   c)[1], 0)

    def scatter_tile(src_ref):
        def copy(t, d):
            return pltpu.make_async_copy(_token_rows(src_ref, t), _token_rows(xb_ref, d), sem)

        def issue(t, carry):
            copy(t, d1_ref[base + t]).start(priority=0)
            copy(t, d2_ref[base + t]).start(priority=1)
            return carry

        lax.fori_loop(0, tm, issue, 0, unroll=ISSUE_UNROLL)
        for _ in range(2):
            pltpu.make_async_copy(src_ref, src_ref, sem).wait()

    pl.when(i < tiles_a)(functools.partial(scatter_tile, src_a_ref))
    pl.when(i >= tiles_a)(functools.partial(scatter_tile, src_b_ref))


def _scatter(route, zero_start, zero_len, h2_a, h2_b, n_rows):
    tm = GATHER_TILE
    tiles_a, tiles_b = h2_a.shape[0] // (tm * SUBLANE), h2_b.shape[0] // (tm * SUBLANE)
    assert tiles_a * tm * SUBLANE == h2_a.shape[0] and tiles_b * tm * SUBLANE == h2_b.shape[0]
    return pl.pallas_call(
        functools.partial(_scatter_body, tm=tm, tiles_a=tiles_a),
        grid_spec=pltpu.PrefetchScalarGridSpec(
            num_scalar_prefetch=4, grid=(tiles_a + tiles_b,),
            in_specs=[pl.BlockSpec((tm * SUBLANE, LANE), lambda i, *_: (jnp.minimum(i, tiles_a - 1), 0)),
                      pl.BlockSpec((tm * SUBLANE, LANE), lambda i, *_: (jnp.maximum(i - tiles_a, 0), 0))],
            out_specs=pl.BlockSpec(memory_space=pl.ANY),
            scratch_shapes=[pltpu.VMEM((ZERO_ROWS * SUBLANE, LANE), F32), pltpu.SemaphoreType.DMA(()),
                            pltpu.SemaphoreType.DMA(())]),
        out_shape=jax.ShapeDtypeStruct((n_rows * SUBLANE, LANE), F32),
        compiler_params=_params("arbitrary"),
        name="moe_scatter",
    )(*route, zero_start, zero_len, h2_a, h2_b)


def _experts_body(be_ref, nb_ref, xb_ref, w1_ref, w3_ref, w2_ref, yb_ref, w1_scr, w3_scr, w2_scr, *, blk):
    i = pl.program_id(0)
    live = i < nb_ref[0]

    @pl.when(live & ((i == 0) | (be_ref[i] != be_ref[jnp.maximum(i - 1, 0)])))
    def _():
        w1_scr[...] = w1_ref[...].astype(BF16)
        w3_scr[...] = w3_ref[...].astype(BF16)
        w2_scr[...] = w2_ref[...].astype(BF16)

    @pl.when(live)
    def _():
        x = jnp.concatenate([xb_ref[pl.ds(s, blk, stride=SUBLANE), :] for s in range(D_MODEL // LANE)],
                            axis=1).astype(BF16)
        hid = _silu(jnp.dot(x, w1_scr[...], preferred_element_type=F32)) * \
            jnp.dot(x, w3_scr[...], preferred_element_type=F32)
        y = _dot(hid, w2_scr[...])
        for s in range(D_MODEL // LANE):
            yb_ref[pl.ds(s, blk, stride=SUBLANE), :] = y[:, s * LANE:(s + 1) * LANE]

    @pl.when(jnp.logical_not(live))
    def _():
        yb_ref[...] = jnp.zeros_like(yb_ref)


def _experts(block_e, nb_used, xb, layer, w1, w3, w2, n_blocks):
    blk = MOE_BLK
    live = lambda i, nb: jnp.minimum(i, nb[0] - 1)
    tile_in = pl.BlockSpec((blk * SUBLANE, LANE), lambda i, be, nb: (live(i, nb), 0))
    tile_out = pl.BlockSpec((blk * SUBLANE, LANE), lambda i, be, nb: (i, 0))
    wspec = lambda a: pl.BlockSpec((None, None) + a.shape[2:],
                                   lambda i, be, nb: (layer, be[live(i, nb)], 0, 0))
    return pl.pallas_call(
        functools.partial(_experts_body, blk=blk),
        grid_spec=pltpu.PrefetchScalarGridSpec(
            num_scalar_prefetch=2, grid=(n_blocks,),
            in_specs=[tile_in, wspec(w1), wspec(w3), wspec(w2)], out_specs=tile_out,
            scratch_shapes=[pltpu.VMEM(w1.shape[2:], BF16), pltpu.VMEM(w3.shape[2:], BF16),
                            pltpu.VMEM(w2.shape[2:], BF16)]),
        out_shape=jax.ShapeDtypeStruct(xb.shape, F32),
        compiler_params=_params("arbitrary"),
        name="moe_experts",
    )(block_e, nb_used, xb, w1, w3, w2)


def _combine_body(*refs, tm, tiles_a, final_norm, n_proj):
    d1_ref, d2_ref, x1_a_ref, x1_b_ref, meta_a_ref, meta_b_ref, yb_ref, nw_ref = refs[:8]
    w_refs = refs[8:8 + n_proj]
    out_a_ref, out_b_ref = refs[8 + n_proj:10 + n_proj]
    za_refs = refs[10 + n_proj:10 + 2 * n_proj]
    zb_refs = refs[10 + 2 * n_proj:10 + 3 * n_proj]
    buf_ref, sem = refs[-2:]
    i = pl.program_id(0)
    phase = i % 2

    def gather(tile, ph):
        base = tile * tm

        def issue(t, carry):
            for k, d_ref in enumerate((d1_ref, d2_ref)):
                pltpu.make_async_copy(_token_rows(yb_ref, d_ref[base + t]),
                                      _token_rows(buf_ref.at[ph, k], t), sem.at[ph]).start(priority=k)
            return carry

        lax.fori_loop(0, tm, issue, 0, unroll=ISSUE_UNROLL)

    @pl.when(i == 0)
    def _():
        gather(0, 0)

    @pl.when(i + 1 < pl.num_programs(0))
    def _():
        gather(i + 1, 1 - phase)

    for k in range(2):
        pltpu.make_async_copy(buf_ref.at[phase, k], buf_ref.at[phase, k], sem.at[phase]).wait()

    def rows_of(slot):
        return jnp.concatenate([buf_ref[phase, slot, pl.ds(s, tm, stride=SUBLANE), :]
                                for s in range(D_MODEL // LANE)], axis=1)

    is_a = i < tiles_a
    meta = jnp.where(is_a, meta_a_ref[...], meta_b_ref[...])
    on_diag = lax.broadcasted_iota(I32, (tm, tm), 0) == lax.broadcasted_iota(I32, (tm, tm), 1)
    as_col = lambda r: jnp.sum(jnp.where(on_diag, r, 0.0), axis=1, keepdims=True)
    x2 = jnp.where(is_a, x1_a_ref[...], x1_b_ref[...]) + (as_col(meta[2:3, :]) * rows_of(0)
                                                         + as_col(meta[3:4, :]) * rows_of(1))
    normed = _rms(x2) * nw_ref[...]

    def emit(out_ref, z_refs):
        out_ref[...] = normed if final_norm else x2
        h = normed.astype(BF16)
        for w_ref, z_ref in zip(w_refs, z_refs):
            z_ref[...] = jnp.dot(h, w_ref[...], preferred_element_type=F32)

    pl.when(is_a)(functools.partial(emit, out_a_ref, za_refs))
    pl.when(jnp.logical_not(is_a))(functools.partial(emit, out_b_ref, zb_refs))


def _combine(route, x1_a, x1_b, meta_a, meta_b, yb, norm_w, final_norm, next_in_ws):
    tm = GATHER_TILE
    tiles_a, tiles_b = x1_a.shape[0] // tm, x1_b.shape[0] // tm
    assert tiles_a * tm == x1_a.shape[0] and tiles_b * tm == x1_b.shape[0]
    assert final_norm == (not next_in_ws)
    of_a = lambda n: pl.BlockSpec((tm, n), lambda i, *_: (jnp.minimum(i, tiles_a - 1), 0))
    of_b = lambda n: pl.BlockSpec((tm, n), lambda i, *_: (jnp.maximum(i - tiles_a, 0), 0))
    widths = [w.shape[1] for w in next_in_ws]
    outs = pl.pallas_call(
        functools.partial(_combine_body, tm=tm, tiles_a=tiles_a, final_norm=final_norm, n_proj=len(widths)),
        grid_spec=pltpu.PrefetchScalarGridSpec(
            num_scalar_prefetch=2, grid=(tiles_a + tiles_b,),
            in_specs=[of_a(D_MODEL), of_b(D_MODEL),
                      pl.BlockSpec((SUBLANE, tm), lambda i, *_: (0, jnp.minimum(i, tiles_a - 1))),
                      pl.BlockSpec((SUBLANE, tm), lambda i, *_: (0, jnp.maximum(i - tiles_a, 0))),
                      pl.BlockSpec(memory_space=pl.ANY),
                      pl.BlockSpec((1, D_MODEL), lambda i, *_: (0, 0))] +
                     [pl.BlockSpec(w.shape, lambda i, *_: (0, 0)) for w in next_in_ws],
            out_specs=[of_a(D_MODEL), of_b(D_MODEL)] + [of_a(n) for n in widths] + [of_b(n) for n in widths],
            scratch_shapes=[pltpu.VMEM((2, 2, tm * SUBLANE, LANE), F32), pltpu.SemaphoreType.DMA((2,))]),
        out_shape=[jax.ShapeDtypeStruct(x1_a.shape, F32), jax.ShapeDtypeStruct(x1_b.shape, F32)] +
                  [jax.ShapeDtypeStruct((x.shape[0], n), F32) for x in (x1_a, x1_b) for n in widths],
        compiler_params=_params("arbitrary"),
        name="moe_combine",
    )(*route, x1_a, x1_b, meta_a, meta_b, yb, norm_w, *next_in_ws)
    n = len(widths)
    return (outs[0], outs[1]), (outs[2:2 + n], outs[2 + n:2 + 2 * n])


def _moe(stream_a, stream_b, counts, layer, w1, w3, w2, norm_w, final_norm, next_in_ws):
    (x1_a, h2_a, meta_a), (x1_b, h2_b, meta_b) = stream_a, stream_b
    t = x1_a.shape[0] + x1_b.shape[0]
    n_blocks = (2 * t + N_EXPERTS * (MOE_BLK - 1)) // MOE_BLK
    col = lambda c: jnp.concatenate([meta_a[c], meta_b[c]]).astype(I32)
    e1, e2, r1, r2 = col(0), col(1), col(4), col(5)
    cnt = counts[:N_EXPERTS, 0].astype(I32)
    padded = (cnt + MOE_BLK - 1) // MOE_BLK * MOE_BLK
    pad_end = jnp.cumsum(padded)
    pad_start = pad_end - padded
    slot = lambda e, r: r + jnp.sum(jnp.where(e[:, None] == jnp.arange(N_EXPERTS, dtype=I32)[None, :],
                                              pad_start[None, :], 0), axis=1)
    route = (slot(e1, r1), slot(e2, r2))
    nb_used = (pad_end[-1:] // MOE_BLK).astype(I32)
    blk_start = jnp.arange(n_blocks, dtype=I32) * MOE_BLK
    block_e = jnp.minimum(jnp.sum(blk_start[:, None] >= pad_end[None, :], axis=1), N_EXPERTS - 1).astype(I32)
    n_rows = n_blocks * MOE_BLK
    zero_start = jnp.concatenate([pad_start + cnt, pad_end[-1:]]).astype(I32)
    zero_len = jnp.concatenate([padded - cnt, n_rows - pad_end[-1:]]).astype(I32)
    xb = _scatter(route, zero_start, zero_len, h2_a, h2_b, n_rows)
    yb = _experts(block_e, nb_used, xb, layer, w1, w3, w2, n_blocks)
    return _combine(route, x1_a, x1_b, meta_a, meta_b, yb, norm_w, final_norm, next_in_ws)


def _rope_tables(pos):
    half = RET_DK // 2
    inv = ROPE_BASE ** (-jnp.arange(half, dtype=F32) / half)
    ang = pos[:, None] * inv[None, :]
    cos, sin = jnp.cos(ang), jnp.sin(ang)
    cos_t = jnp.tile(jnp.concatenate([cos, cos], axis=1), (1, RET_HEADS))
    sin_t = jnp.tile(jnp.concatenate([-sin, sin], axis=1), (1, RET_HEADS))
    return cos_t, sin_t


def _block_diag(blocks):
    g, r, c = blocks.shape
    eye = jnp.eye(g, dtype=bool)
    return jnp.where(eye[:, None, :, None], blocks[:, :, None, :], 0).reshape(g * r, g * c)


def _layer_weights(l, w_in, s5, s5_b_bar, s5_c_re, s5_c_im, s5_d, s5_glu_w, s5_glu_b, gdn_conv_w, gdn_a_log,
                   gdn_dt_bias, gdn_norm_w, w_out, router_group_w, router_group_b, router_expert_w,
                   router_expert_b):
    wi = w_in[l]
    g0 = RET_IN + S5_IN
    w_ab = jnp.pad(wi[:, g0 + GDN_QKV + GDN_HD:], ((0, 0), (0, LANE - 2 * GDN_HEADS)))
    in_ws = [wi[:, :RET_IN], wi[:, g0:g0 + GDN_QKV], wi[:, RET_IN:g0], wi[:, g0 + GDN_QKV:g0 + GDN_QKV + GDN_HD],
             w_ab]
    in_ws = [w.astype(BF16) for w in in_ws]
    ab_re, ab_im, bb_re, bb_im = s5
    n = S5_LANES
    sl = slice(l * n, (l + 1) * n)
    ab = jnp.concatenate([ab_re[sl].reshape(1, n), ab_im[sl].reshape(1, n)], axis=0)
    to_bd = lambda m: _block_diag(jnp.swapaxes(m[sl].reshape(S5_GROUPS, S5_STATE, S5_GROUP_CH), 1, 2))
    wb = jnp.concatenate([to_bd(bb_re), to_bd(bb_im)], axis=1).astype(BF16)
    wc_re = _block_diag(jnp.swapaxes(s5_c_re[l], 1, 2)).astype(BF16)
    wc_im = _block_diag(jnp.swapaxes(s5_c_im[l], 1, 2)).astype(BF16)
    s5_ws = (ab, wb, wc_re, wc_im, s5_d[l].reshape(1, S5_CH), s5_glu_w[l].astype(BF16),
             s5_glu_b[l].reshape(1, S5_CH))
    pad4 = lambda v: jnp.pad(v.reshape(1, GDN_HEADS), ((0, 0), (0, LANE - GDN_HEADS)))
    gdn_ws = (gdn_conv_w[l], pad4(gdn_a_log[l]), pad4(gdn_dt_bias[l]),
              jnp.tile(gdn_norm_w[l], GDN_HEADS).reshape(1, GDN_HD))
    wo = w_out[l].astype(BF16)
    out_ws = (wo[:RET_VD], wo[RET_VD:RET_VD + S5_CH], wo[RET_VD + S5_CH:])
    wr = jnp.pad(jnp.concatenate([router_expert_w[l], router_group_w[l]], axis=1).T,
                 ((0, LANE - N_EXPERTS - MOE_GROUPS), (0, 0))).astype(BF16)
    br = jnp.pad(jnp.concatenate([router_expert_b[l], router_group_b[l]]),
                 (0, LANE - N_EXPERTS - MOE_GROUPS)).reshape(LANE, 1)
    return in_ws, s5_ws, gdn_ws, out_ws, (wr, br)


def _mix_and_route(l, x, projected, bsz, seq, rope, states, layer_w, norm_mix, norm_ffn, cnt0):
    t = bsz * seq
    ret_s, s5_re, s5_im, gdn_s, gdn_buf = states
    in_ws, s5_ws, gdn_ws, out_ws, (wr, br) = layer_w
    zr, zq, zs, zg, zab = projected or _inproj(x, norm_mix[l].reshape(1, D_MODEL), in_ws)
    o_ret, ret_fin = _retention(zr, *rope, ret_s.reshape(DEPTH, bsz, RET_QD, RET_DV), l, bsz, seq)
    u_tm = jnp.swapaxes(zs.reshape(bsz, seq, S5_CH), 0, 1)
    o_s5_tm, re_fin, im_fin = _s5(u_tm, s5_re[l].reshape(bsz, S5_LANES), s5_im[l].reshape(bsz, S5_LANES), *s5_ws)
    o_s5 = jnp.swapaxes(o_s5_tm, 0, 1).reshape(t, S5_CH)
    buf0 = jnp.pad(gdn_buf[l], ((0, 0), (SUBLANE - (GDN_CONV - 1), 0), (0, 0)))
    o_gdn, nbuf, gdn_fin = _gdn(zq, zg, zab, *gdn_ws, buf0, gdn_s.reshape(DEPTH, bsz, GDN_HD, GDN_DV), l, bsz, seq)
    x1, h2_tiles, meta, counts = _outproj(x, o_ret, o_s5, o_gdn, *out_ws, norm_ffn[l].reshape(1, D_MODEL), wr, br,
                                          cnt0)
    new_states = (ret_fin.reshape(bsz, RET_HEADS, RET_DK, RET_DV), re_fin.reshape(bsz, S5_GROUPS, S5_STATE),
                  im_fin.reshape(bsz, S5_GROUPS, S5_STATE), gdn_fin.reshape(bsz, GDN_HEADS, GDN_DK, GDN_DV),
                  nbuf[:, SUBLANE - (GDN_CONV - 1):, :])
    return (x1, h2_tiles, meta), counts, new_states


def _trunks(xs, positions, states, layer_ws, norm_mix, norm_ffn, experts, norm_final):
    shapes = [x.shape[:2] for x in xs]
    ropes = [_rope_tables(pos) for pos in positions]
    xs = [x.reshape(b * s, D_MODEL) for x, (b, s) in zip(xs, shapes)]
    outs = [[[] for _ in range(5)] for _ in xs]
    projected = [None for _ in xs]
    for l in range(DEPTH):
        counts = jnp.zeros((LANE, LANE), F32)
        routed = []
        for i, x in enumerate(xs):
            stream, counts, new_states = _mix_and_route(l, x, projected[i], *shapes[i], ropes[i], states[i],
                                                        layer_ws[l], norm_mix, norm_ffn, counts)
            routed.append(stream)
            for lst, s in zip(outs[i], new_states):
                lst.append(s)
        last = l == DEPTH - 1
        norm_w = norm_final if last else norm_mix[l + 1]
        xs, projected = _moe(*routed, counts, l, *experts, norm_w.reshape(1, D_MODEL), last,
                             [] if last else layer_ws[l + 1][0])
    ys = [x.reshape(b, s, D_MODEL) for x, (b, s) in zip(xs, shapes)]
    return ys, [tuple(jnp.stack(o) for o in per_stream) for per_stream in outs]


def kernel(x_prompt, x_sample, state_ret, state_s5_re, state_s5_im, state_gdn, state_gdn_conv, norm_mix, w_in, s5_a_re, s5_a_im, s5_log_dt, s5_b_re, s5_b_im, s5_c_re, s5_c_im, s5_d, s5_glu_w, s5_glu_b, gdn_conv_w, gdn_a_log, gdn_dt_bias, gdn_norm_w, w_out, norm_ffn, router_group_w, router_group_b, router_expert_w, router_expert_b, expert_w1, expert_w3, expert_w2, norm_final):
    n = DEPTH * S5_LANES
    col = lambda a: a.reshape(n, 1)
    log_dt = jnp.broadcast_to(s5_log_dt[:, :, None], (DEPTH, S5_GROUPS, S5_STATE))
    s5 = _s5_prep(col(s5_a_re), col(s5_a_im), col(log_dt), s5_b_re.reshape(n, S5_GROUP_CH),
                  s5_b_im.reshape(n, S5_GROUP_CH))
    layer_ws = [_layer_weights(l, w_in, s5, None, s5_c_re, s5_c_im, s5_d, s5_glu_w, s5_glu_b, gdn_conv_w,
                               gdn_a_log, gdn_dt_bias, gdn_norm_w, w_out, router_group_w, router_group_b,
                               router_expert_w, router_expert_b) for l in range(DEPTH)]
    experts = (expert_w1, expert_w3, expert_w2)
    bp, lp, _ = x_prompt.shape
    zero_states = (jnp.zeros((DEPTH, bp, RET_HEADS, RET_DK, RET_DV), F32),
                   jnp.zeros((DEPTH, bp, S5_GROUPS, S5_STATE), F32),
                   jnp.zeros((DEPTH, bp, S5_GROUPS, S5_STATE), F32),
                   jnp.zeros((DEPTH, bp, GDN_HEADS, GDN_DK, GDN_DV), F32),
                   jnp.zeros((DEPTH, bp, GDN_CONV - 1, GDN_QKV), F32))
    sample_states = (state_ret, state_s5_re, state_s5_im, state_gdn, state_gdn_conv)
    positions = (jnp.arange(lp, dtype=F32), PAST_LEN + jnp.arange(x_sample.shape[1], dtype=F32))
    (y_p, y_s), (p_states, s_states) = _trunks((x_prompt, x_sample), positions, (zero_states, sample_states),
                                               layer_ws, norm_mix, norm_ffn, experts, norm_final)
    return (y_p, y_s) + p_states + s_states
```

```python
import functools
import math

import jax
import jax.numpy as jnp
import numpy as np
from jax import lax
from jax.experimental import pallas as pl
from jax.experimental.pallas import tpu as pltpu

F32 = jnp.float32
BF16 = jnp.bfloat16
I32 = jnp.int32

D_MODEL = 1024
DEPTH = 2
RET_HEADS, RET_DK, RET_DV = 4, 64, 128
S5_GROUPS, S5_GROUP_CH, S5_STATE = 16, 16, 64
S5_CH = S5_GROUPS * S5_GROUP_CH
S5_LANES = S5_GROUPS * S5_STATE
GDN_HEADS, GDN_DK, GDN_DV, GDN_CONV = 4, 64, 64, 4
GDN_HD = GDN_HEADS * GDN_DK
GDN_QKV = 3 * GDN_HD
RET_QD = RET_HEADS * RET_DK
RET_VD = RET_HEADS * RET_DV
RET_IN = 2 * RET_QD + 2 * RET_VD
S5_IN = S5_CH
GDN_IN = GDN_QKV + GDN_HD + 2 * GDN_HEADS
GDN_CHUNK = 64
RET_CHUNK = 64
MOE_GROUPS, EXPERTS_PER_GROUP = 4, 8
N_EXPERTS = MOE_GROUPS * EXPERTS_PER_GROUP
D_EXPERT = 512
ROPE_BASE = 10000.0
EPS = 1e-6
PAST_LEN = 16384

LANE = 128
SUBLANE = 8
ROW_TILE = 512
MOE_BLK = 512
SCATTER_TILE = 512
GATHER_TILE = 256
S5_STEPS = 128
ISSUE_UNROLL = 8
SEQS_PER_STEP = 8
LONG_SEQS_PER_STEP = 2
VMEM_LIMIT = 56 * 1024 * 1024


def _params(*sem):
    return pltpu.CompilerParams(dimension_semantics=sem, vmem_limit_bytes=VMEM_LIMIT)


def _dot(a, b):
    return jnp.dot(a.astype(BF16), b.astype(BF16), preferred_element_type=F32)


def _dot_nt(a, b):
    return lax.dot_general(a.astype(BF16), b.astype(BF16), (((1,), (1,)), ((), ())),
                           preferred_element_type=F32)


def _dot_tn(a, b):
    return lax.dot_general(a.astype(BF16), b.astype(BF16), (((0,), (0,)), ((), ())),
                           preferred_element_type=F32)


def _dot_hi(a, b):
    a1 = a.astype(BF16)
    a2 = (a - a1.astype(F32)).astype(BF16)
    b1 = b.astype(BF16)
    b2 = (b - b1.astype(F32)).astype(BF16)
    d = lambda x, y: jnp.dot(x, y, preferred_element_type=F32)
    return d(a1, b1) + (d(a1, b2) + d(a2, b1))


def _split3(x):
    p1 = x.astype(BF16)
    r1 = x - p1.astype(F32)
    p2 = r1.astype(BF16)
    p3 = (r1 - p2.astype(F32)).astype(BF16)
    return p1, p2, p3


def _dot_sel_l(sel, x):
    p1, p2, p3 = _split3(x)
    d = lambda p: jnp.dot(sel, p, preferred_element_type=F32)
    return d(p1) + d(p2) + d(p3)


def _dot_sel_r(x, sel):
    p1, p2, p3 = _split3(x)
    d = lambda p: jnp.dot(p, sel, preferred_element_type=F32)
    return d(p1) + d(p2) + d(p3)


def _rms(x):
    return x * lax.rsqrt(jnp.mean(x * x, axis=-1, keepdims=True) + EPS)


def _silu(x):
    return x * jax.nn.sigmoid(x)


def _inproj_body(x_ref, nw_ref, wr_ref, wq_ref, ws_ref, wg_ref, wab_ref,
                 zr_ref, zq_ref, zs_ref, zg_ref, zab_ref):
    h = (_rms(x_ref[...]) * nw_ref[...]).astype(BF16)
    for w_ref, z_ref in ((wr_ref, zr_ref), (wq_ref, zq_ref), (ws_ref, zs_ref),
                         (wg_ref, zg_ref), (wab_ref, zab_ref)):
        z_ref[...] = jnp.dot(h, w_ref[...], preferred_element_type=F32)


def _inproj(x, nw, ws):
    t = x.shape[0]
    tm = min(ROW_TILE, t)
    widths = [w.shape[1] for w in ws]
    row = lambda n: pl.BlockSpec((tm, n), lambda i: (i, 0))
    full = lambda a: pl.BlockSpec(a.shape, lambda i: (0, 0))
    return pl.pallas_call(
        _inproj_body,
        grid=(t // tm,),
        in_specs=[row(D_MODEL), full(nw)] + [full(w) for w in ws],
        out_specs=[row(n) for n in widths],
        out_shape=[jax.ShapeDtypeStruct((t, n), F32) for n in widths],
        compiler_params=_params("parallel"),
        name="inproj",
    )(x, nw, *ws)


def _seqs_per_step(bsz, blocks_per_seq):
    want = SEQS_PER_STEP if blocks_per_seq == 1 else LONG_SEQS_PER_STEP
    return want if bsz % want == 0 else 1


def _ret_log_gamma(h):
    return math.log(1.0 - 2.0 ** (-5.0 - h))


def _ret_body(zr_ref, cos_ref, sin_ref, s0_ref, o_ref, sfin_ref, s_scr, *, nseq, rows, chunk):
    j = pl.program_id(1)
    stack = RET_HEADS * chunk

    @pl.when(j == 0)
    def _():
        s_scr[...] = s0_ref[...]

    lane = lax.broadcasted_iota(I32, (rows, RET_QD), 1)
    first_half = (lane % RET_DK) < (RET_DK // 2)

    def rotary(x):
        swapped = jnp.where(first_half, pltpu.roll(x, RET_QD - RET_DK // 2, 1),
                            pltpu.roll(x, RET_DK // 2, 1))
        return x * cos_ref[...] + swapped * sin_ref[...]

    q = [rotary(zr_ref[s, :, 0:RET_QD]) for s in range(nseq)]
    k = [rotary(zr_ref[s, :, RET_QD:2 * RET_QD]) * (RET_DK ** -0.5) for s in range(nseq)]

    def by_head(idx, fn):
        out = jnp.zeros(idx.shape, F32)
        for h in range(RET_HEADS):
            out = jnp.where(idx == h, fn(h), out)
        return out

    lg_lane = by_head(lax.broadcasted_iota(I32, (chunk, RET_QD), 1) // RET_DK, _ret_log_gamma)
    pos = lax.broadcasted_iota(I32, (chunk, RET_QD), 0).astype(F32)
    q_scale = jnp.exp((pos + 1.0) * lg_lane)
    k_scale = jnp.exp((chunk - 1.0 - pos) * lg_lane)
    st_row = lax.broadcasted_iota(I32, (stack, stack), 0)
    st_col = lax.broadcasted_iota(I32, (stack, stack), 1)
    causal = ((st_row // chunk) == (st_col // chunk)) & (st_row >= st_col)
    lg_stack = by_head(st_row // chunk, _ret_log_gamma)
    decay = jnp.where(causal, jnp.exp(jnp.where(causal, (st_row - st_col).astype(F32), 0.0) * lg_stack), 0.0)
    q_rows = (lax.broadcasted_iota(I32, (stack, RET_QD), 0) // chunk) == \
             (lax.broadcasted_iota(I32, (stack, RET_QD), 1) // RET_DK)
    v_rows = (lax.broadcasted_iota(I32, (stack, RET_VD), 0) // chunk) == \
             (lax.broadcasted_iota(I32, (stack, RET_VD), 1) // RET_DV)
    s_row_head = lax.broadcasted_iota(I32, (RET_QD, RET_DV), 0) // RET_DK
    s_decay = by_head(s_row_head, lambda h: math.exp(chunk * _ret_log_gamma(h)))
    kv_diag = (lax.broadcasted_iota(I32, (RET_QD, RET_VD), 0) // RET_DK) == \
              (lax.broadcasted_iota(I32, (RET_QD, RET_VD), 1) // RET_DV)

    def tile4(a):
        return jnp.concatenate([a] * RET_HEADS, axis=0)

    def collapse(a):
        out = a[0:chunk]
        for h in range(1, RET_HEADS):
            out = out + a[h * chunk:(h + 1) * chunk]
        return out

    seqs = range(nseq)
    n_chunks = rows // chunk
    state = [s_scr[s] for s in seqs]
    for ci in range(n_chunks):
        r = slice(ci * chunk, (ci + 1) * chunk)
        v_cs = [zr_ref[s, r, 2 * RET_QD:2 * RET_QD + RET_VD] for s in seqs]
        scores = [_dot_nt(jnp.where(q_rows, tile4(q[s][r]), 0.0), tile4(k[s][r])) * decay for s in seqs]
        o_intra = [collapse(_dot(scores[s], jnp.where(v_rows, tile4(v_cs[s]), 0.0))) for s in seqs]
        s_full = [jnp.concatenate([jnp.where(s_row_head == h, state[s], 0.0) for h in range(RET_HEADS)], axis=1)
                  for s in seqs]
        o_cs = [o_intra[s] + _dot(q[s][r] * q_scale, s_full[s]) for s in seqs]
        for s in seqs:
            gate = zr_ref[s, r, 2 * RET_QD + RET_VD:2 * RET_QD + 2 * RET_VD]
            for h in range(RET_HEADS):
                sl = slice(h * RET_DV, (h + 1) * RET_DV)
                o_ref[s, r, sl] = _rms(o_cs[s][:, sl]) * _silu(gate[:, sl])
        for s in seqs:
            kv = jnp.where(kv_diag, _dot_tn(k[s][r] * k_scale, v_cs[s]), 0.0)
            kv_own = kv[:, 0:RET_DV]
            for h in range(1, RET_HEADS):
                kv_own = kv_own + kv[:, h * RET_DV:(h + 1) * RET_DV]
            state[s] = state[s] * s_decay + kv_own
    for s in seqs:
        s_scr[s] = state[s]

    @pl.when(j == pl.num_programs(1) - 1)
    def _():
        for s in seqs:
            sfin_ref[s] = state[s]


def _retention(zr, cos, sin, s0, layer, bsz, seq):
    chunk = min(RET_CHUNK, seq)
    rows = min(4 * chunk, seq)
    nj = seq // rows
    nseq = _seqs_per_step(bsz, nj)
    o, s_fin = pl.pallas_call(
        functools.partial(_ret_body, nseq=nseq, rows=rows, chunk=chunk),
        grid=(bsz // nseq, nj),
        in_specs=[pl.BlockSpec((nseq, rows, RET_IN), lambda b, j: (b, j, 0)),
                  pl.BlockSpec((rows, RET_QD), lambda b, j: (j, 0)),
                  pl.BlockSpec((rows, RET_QD), lambda b, j: (j, 0)),
                  pl.BlockSpec((None, nseq, RET_QD, RET_DV), lambda b, j: (layer, b, 0, 0))],
        out_specs=[pl.BlockSpec((nseq, rows, RET_VD), lambda b, j: (b, j, 0)),
                   pl.BlockSpec((nseq, RET_QD, RET_DV), lambda b, j: (b, 0, 0))],
        out_shape=[jax.ShapeDtypeStruct((bsz, seq, RET_VD), F32),
                   jax.ShapeDtypeStruct((bsz, RET_QD, RET_DV), F32)],
        scratch_shapes=[pltpu.VMEM((nseq, RET_QD, RET_DV), F32)],
        compiler_params=_params("parallel", "arbitrary"),
        name="retention",
    )(zr.reshape(bsz, seq, RET_IN), cos, sin, s0)
    return o.reshape(bsz * seq, RET_VD), s_fin


def _s5_prep_body(are_ref, aim_ref, ldt_ref, bre_ref, bim_ref, abre_ref, abim_ref, bbre_ref, bbim_ref):
    lam_re, lam_im = are_ref[...], aim_ref[...]
    dt = jnp.exp(ldt_ref[...])
    mag = jnp.exp(lam_re * dt)
    ab_re = mag * jnp.cos(lam_im * dt)
    ab_im = mag * jnp.sin(lam_im * dt)
    den = lam_re * lam_re + lam_im * lam_im
    f_re = ((ab_re - 1.0) * lam_re + ab_im * lam_im) / den
    f_im = (ab_im * lam_re - (ab_re - 1.0) * lam_im) / den
    abre_ref[...] = ab_re
    abim_ref[...] = ab_im
    bbre_ref[...] = f_re * bre_ref[...] - f_im * bim_ref[...]
    bbim_ref[...] = f_re * bim_ref[...] + f_im * bre_ref[...]


def _s5_prep(a_re, a_im, log_dt, b_re, b_im):
    n = a_re.shape[0]
    col = jax.ShapeDtypeStruct((n, 1), F32)
    mat = jax.ShapeDtypeStruct((n, S5_GROUP_CH), F32)
    return pl.pallas_call(_s5_prep_body, out_shape=[col, col, mat, mat], name="s5_prep")(
        a_re, a_im, log_dt, b_re, b_im)


def _gelu_tanh(x):
    return x * (0.5 * (1.0 + jnp.tanh(math.sqrt(2.0 / math.pi) * (x + 0.044715 * (x * x * x)))))


def _s5_body(u_ref, h0re_ref, h0im_ref, ab_ref, wb_ref, wcre_ref, wcim_ref, d_ref, gw_ref, gb_ref,
             o_ref, hre_ref, him_ref, bu_scr, st_scr, *, steps):
    j = pl.program_id(1)

    @pl.when(j == 0)
    def _():
        st_scr[0] = h0re_ref[...]
        st_scr[1] = h0im_ref[...]

    rows = steps * SUBLANE
    u = u_ref[...].reshape(rows, S5_CH)
    bu_scr[...] = _dot(u, wb_ref[...])
    a_re = jnp.broadcast_to(ab_ref[0:1, :], (SUBLANE, S5_LANES))
    a_im = jnp.broadcast_to(ab_ref[1:2, :], (SUBLANE, S5_LANES))

    def step(t, carry):
        h_re, h_im = carry
        r = pl.ds(pl.multiple_of(t * SUBLANE, SUBLANE), SUBLANE)
        n_re = a_re * h_re - a_im * h_im + bu_scr[r, 0:S5_LANES]
        n_im = a_re * h_im + a_im * h_re + bu_scr[r, S5_LANES:2 * S5_LANES]
        bu_scr[r, 0:S5_LANES] = n_re
        bu_scr[r, S5_LANES:2 * S5_LANES] = n_im
        return n_re, n_im

    h_re, h_im = lax.fori_loop(0, steps, step, (st_scr[0], st_scr[1]))
    st_scr[0] = h_re
    st_scr[1] = h_im
    hre_ref[...] = h_re
    him_ref[...] = h_im

    y = _dot(bu_scr[:, 0:S5_LANES], wcre_ref[...]) - _dot(bu_scr[:, S5_LANES:2 * S5_LANES], wcim_ref[...])
    y = _gelu_tanh(y + d_ref[...] * u)
    y = y * jax.nn.sigmoid(_dot(y, gw_ref[...]) + gb_ref[...])
    o_ref[...] = y.reshape(steps, SUBLANE, S5_CH)


def _s5(u_tm, h0_re, h0_im, ab, wb, wc_re, wc_im, d_skip, glu_w, glu_b):
    seq, bsz, _ = u_tm.shape
    steps = min(S5_STEPS, seq)
    full = lambda a: pl.BlockSpec(a.shape, lambda g, j: (0,) * a.ndim)
    st = pl.BlockSpec((SUBLANE, S5_LANES), lambda g, j: (g, 0))
    return pl.pallas_call(
        functools.partial(_s5_body, steps=steps),
        grid=(bsz // SUBLANE, seq // steps),
        in_specs=[pl.BlockSpec((steps, SUBLANE, S5_CH), lambda g, j: (j, g, 0)), st, st,
                  full(ab), full(wb), full(wc_re), full(wc_im), full(d_skip), full(glu_w), full(glu_b)],
        out_specs=[pl.BlockSpec((steps, SUBLANE, S5_CH), lambda g, j: (j, g, 0)), st, st],
        out_shape=[jax.ShapeDtypeStruct((seq, bsz, S5_CH), F32),
                   jax.ShapeDtypeStruct((bsz, S5_LANES), F32),
                   jax.ShapeDtypeStruct((bsz, S5_LANES), F32)],
        scratch_shapes=[pltpu.VMEM((steps * SUBLANE, 2 * S5_LANES), F32),
                        pltpu.VMEM((2, SUBLANE, S5_LANES), F32)],
        compiler_params=_params("parallel", "arbitrary"),
        name="s5",
    )(u_tm, h0_re, h0_im, ab, wb, wc_re, wc_im, d_skip, glu_w, glu_b)


def _gdn_body(zq_ref, zg_ref, zab_ref, cw_ref, alog_ref, dtb_ref, nw_ref, buf0_ref, s0_ref,
              ones_bd_ref, tri_ref, ea_ref, eb_ref, ec_ref, spread_ref, gather_ref,
              o_ref, nbuf_ref, sfin_ref,
              ext_scr, q_scr, k_scr, kb_scr, vb_scr, g_scr, g4_scr, s_scr, *, nseq, rows, chunk):
    j = pl.program_id(1)
    stack = GDN_HEADS * chunk
    bd_state = (lax.broadcasted_iota(I32, (GDN_HD, GDN_HD), 0) // GDN_DK) == \
               (lax.broadcasted_iota(I32, (GDN_HD, GDN_HD), 1) // GDN_DV)

    @pl.when(j == 0)
    def _():
        for s in range(nseq):
            ext_scr[s, 0:SUBLANE, :] = buf0_ref[s]
            s_scr[s] = jnp.where(bd_state, _dot_sel_r(s0_ref[s], spread_ref[...]), 0.0)

    convs = []
    for s in range(nseq):
        ext_scr[s, SUBLANE:SUBLANE + rows, :] = zq_ref[s]
        conv = ext_scr[s, SUBLANE - 3:SUBLANE - 3 + rows, :] * cw_ref[0:1, :]
        for i in range(1, GDN_CONV):
            conv = conv + ext_scr[s, SUBLANE - 3 + i:SUBLANE - 3 + i + rows, :] * cw_ref[i:i + 1, :]
        tail = ext_scr[s, rows:rows + SUBLANE, :]
        nbuf_ref[s] = tail
        ext_scr[s, 0:SUBLANE, :] = tail
        convs.append(conv)
    qkv = _silu(convs[0] if nseq == 1 else jnp.concatenate(convs, axis=0))

    ones_bd = ones_bd_ref[...]
    q_raw = qkv[:, 0:GDN_HD]
    k_raw = qkv[:, GDN_HD:2 * GDN_HD]
    q_scr[...] = q_raw * lax.rsqrt(_dot_sel_r(q_raw * q_raw, ones_bd) + EPS) * (GDN_DK ** -0.5)
    k_n = k_raw * lax.rsqrt(_dot_sel_r(k_raw * k_raw, ones_bd) + EPS)
    k_scr[...] = k_n

    ab = zab_ref[...].reshape(nseq * rows, LANE)
    x = ab + dtb_ref[...]
    softplus = jnp.maximum(x, 0.0) + jnp.log1p(jnp.exp(-jnp.abs(x)))
    g_pad = -jnp.exp(alog_ref[...]) * softplus
    beta = _dot_sel_r(jax.nn.sigmoid(ab), eb_ref[...])
    g_cum = _dot_sel_l(tri_ref[...], g_pad)
    g_scr[...] = _dot_sel_r(g_cum, ea_ref[...])
    g4_scr[...] = _dot_sel_r(g_pad, ec_ref[...])
    kb_scr[...] = k_n * beta
    vb_scr[...] = qkv[:, 2 * GDN_HD:3 * GDN_HD] * beta

    w_row = lax.broadcasted_iota(I32, (chunk, stack), 0)
    w_col = lax.broadcasted_iota(I32, (chunk, stack), 1) % chunk
    strict_w = w_row > w_col
    causal_w = w_row >= w_col
    same_head = (lax.broadcasted_iota(I32, (stack, stack), 0) // chunk) == \
                (lax.broadcasted_iota(I32, (stack, stack), 1) // chunk)
    head_rows = (lax.broadcasted_iota(I32, (stack, GDN_HD), 0) // chunk) == \
                (lax.broadcasted_iota(I32, (stack, GDN_HD), 1) // GDN_DK)
    head_rows2 = jnp.concatenate([head_rows, head_rows], axis=1)
    tri_c = tri_ref[0:chunk, 0:chunk]
    packed_ok = chunk % (2 * SUBLANE) == 0

    def tile4(a):
        return jnp.concatenate([a] * GDN_HEADS, axis=0)

    def split2(a):
        hi = a.astype(BF16)
        return hi, (a - hi.astype(F32)).astype(BF16)

    def stack_masked(parts, mask):
        if packed_ok:
            return [jnp.where(mask, tile4(p), jnp.zeros((), BF16)) for p in parts]
        return [jnp.where(mask, tile4(p.astype(F32)), 0.0).astype(BF16) for p in parts]

    def on_diag(parts):
        return stack_masked(parts, same_head)

    def mm(a, b):
        return jnp.dot(a, b, preferred_element_type=F32)

    def mm_hi(a_parts, b_parts):
        return mm(a_parts[0], b_parts[0]) + (mm(a_parts[0], b_parts[1]) + mm(a_parts[1], b_parts[0]))

    n_chunks = rows // chunk
    every = range(nseq * n_chunks)
    rows_of = lambda u: slice(u * chunk, (u + 1) * chunk)
    solved = []
    for u in every:
        r = rows_of(u)
        q_c, k_c, kb_c, vb_c, g_c = q_scr[r, :], k_scr[r, :], kb_scr[r, :], vb_scr[r, :], g_scr[r, :]
        exp_g = jnp.exp(g_c)
        g_diff = _dot_sel_l(tri_c, jnp.where(strict_w, g4_scr[r, :], 0.0))
        decay = jnp.exp(jnp.where(causal_w, g_diff, 0.0))
        k_heads = jnp.where(head_rows, tile4(k_c), 0.0)
        lmat = jnp.where(strict_w, _dot_nt(kb_c, k_heads) * decay, 0.0)
        attn = jnp.where(causal_w, _dot_nt(q_c, k_heads) * decay, 0.0)
        rhs_c = jnp.concatenate([vb_c, kb_c * exp_g], axis=1)
        solved.append((lmat, attn, exp_g, rhs_c))

    one = lambda a: a.astype(BF16)
    t_acc = [-solved[ci][0] for ci in every]
    p_one = [one(solved[ci][0]) for ci in every]
    p_diag = [on_diag([p])[0] for p in p_one]
    span = 2
    while span < chunk + 1:
        power = [mm(p_one[ci], p_diag[ci]) for ci in every]
        p_one = [one(p) for p in power]
        p_diag = [on_diag([p])[0] for p in p_one]
        t_acc = [t_acc[ci] + power[ci] + mm(one(t_acc[ci]), p_diag[ci]) for ci in every]
        span *= 2
    t_one = [one(t) for t in t_acc]
    apply_inv = lambda ci, v: v + mm(t_one[ci], stack_masked([one(v)], head_rows2)[0])
    sol0 = [apply_inv(ci, solved[ci][3]) for ci in every]
    resid = [solved[ci][3] - (sol0[ci] + mm_hi(split2(solved[ci][0]), stack_masked(split2(sol0[ci]), head_rows2)))
             for ci in every]
    sols = [sol0[ci] + apply_inv(ci, resid[ci]) for ci in every]

    seqs = range(nseq)
    s_cur = [s_scr[s] for s in seqs]
    for ci in range(n_chunks):
        us = [s * n_chunks + ci for s in seqs]
        v_new = [sols[u][:, 0:GDN_HD] - _dot(sols[u][:, GDN_HD:2 * GDN_HD], s_cur[s]) for s, u in enumerate(us)]
        o_cs = [_dot(q_scr[rows_of(u), :] * solved[u][2], s_cur[s])
                + _dot(solved[u][1], jnp.where(head_rows, tile4(v_new[s]), 0.0)) for s, u in enumerate(us)]
        for s, u in enumerate(us):
            g_c = g_scr[rows_of(u), :]
            g_last = g_c[chunk - 1:chunk, :]
            k_dec = k_scr[rows_of(u), :] * jnp.exp(g_last - g_c)
            s_cur[s] = s_cur[s] * jnp.exp(g_last) + jnp.where(bd_state, _dot_tn(k_dec, v_new[s]), 0.0)
        for s, u in enumerate(us):
            r = slice(ci * chunk, (ci + 1) * chunk)
            ms = _dot_sel_r(o_cs[s] * o_cs[s], ones_bd) * (1.0 / GDN_DV)
            o_ref[s, r, :] = o_cs[s] * lax.rsqrt(ms + EPS) * nw_ref[...] * _silu(zg_ref[s, r, :])
    for s in seqs:
        s_scr[s] = s_cur[s]

    @pl.when(j == pl.num_programs(1) - 1)
    def _():
        for s in seqs:
            sfin_ref[s] = _dot_sel_r(s_scr[s], gather_ref[...])


def _gdn_consts(rows, chunk):
    stack = GDN_HEADS * chunk
    blk = lambda n, c: (np.arange(n)[:, None] // c) == (np.arange(n)[None, :] // c)
    ones_bd = blk(GDN_HD, GDN_DK)
    tri = blk(rows, chunk) & (np.arange(rows)[:, None] >= np.arange(rows)[None, :])
    src = np.arange(LANE)[:, None]
    ea = src == np.arange(GDN_HD)[None, :] // GDN_DK
    eb = src == GDN_HEADS + np.arange(GDN_HD)[None, :] // GDN_DK
    ec = src == np.arange(stack)[None, :] // chunk
    spread = np.arange(GDN_DV)[:, None] == np.arange(GDN_HD)[None, :] % GDN_DV
    return tuple(jnp.asarray(m, dtype=BF16) for m in (ones_bd, tri, ea, eb, ec, spread, spread.T))


def _gdn(zq, zg, zab, conv_w, alog_pad, dtb_pad, nw, buf0, s0, layer, bsz, seq):
    chunk = min(GDN_CHUNK, seq)
    rows = min(4 * chunk, seq)
    nj = seq // rows
    nseq = _seqs_per_step(bsz, nj)
    stack = GDN_HEADS * chunk
    consts = _gdn_consts(nseq * rows, chunk)
    row = lambda n: pl.BlockSpec((nseq, rows, n), lambda b, j: (b, j, 0))
    full = lambda a: pl.BlockSpec(a.shape, lambda b, j: (0,) * a.ndim)
    per_b = lambda *s: pl.BlockSpec((nseq,) + s, lambda b, j: (b, 0, 0))
    by_seq = lambda a: a.reshape(bsz, seq, a.shape[-1])
    o, nbuf, s_fin = pl.pallas_call(
        functools.partial(_gdn_body, nseq=nseq, rows=rows, chunk=chunk),
        grid=(bsz // nseq, nj),
        in_specs=[row(GDN_QKV), row(GDN_HD), row(LANE), full(conv_w), full(alog_pad), full(dtb_pad), full(nw),
                  per_b(SUBLANE, GDN_QKV),
                  pl.BlockSpec((None, nseq, GDN_HD, GDN_DV), lambda b, j: (layer, b, 0, 0))] +
                 [full(c) for c in consts],
        out_specs=[row(GDN_HD), per_b(SUBLANE, GDN_QKV), per_b(GDN_HD, GDN_DV)],
        out_shape=[jax.ShapeDtypeStruct((bsz, seq, GDN_HD), F32),
                   jax.ShapeDtypeStruct((bsz, SUBLANE, GDN_QKV), F32),
                   jax.ShapeDtypeStruct((bsz, GDN_HD, GDN_DV), F32)],
        scratch_shapes=[pltpu.VMEM((nseq, rows + SUBLANE, GDN_QKV), F32)] +
                       [pltpu.VMEM((nseq * rows, GDN_HD), F32)] * 5 +
                       [pltpu.VMEM((nseq * rows, stack), F32), pltpu.VMEM((nseq, GDN_HD, GDN_HD), F32)],
        compiler_params=_params("parallel", "arbitrary"),
        name="gdn",
    )(by_seq(zq), by_seq(zg), by_seq(zab), conv_w, alog_pad, dtb_pad, nw, buf0, s0, *consts)
    return o.reshape(bsz * seq, GDN_HD), nbuf, s_fin


def _outproj_body(x_ref, oret_ref, os5_ref, ogdn_ref, w1_ref, w2_ref, w3_ref, nw_ref, wr_ref, br_ref, tri_ref,
                  cnt0_ref, x1_ref, h2_ref, meta_ref, cnt_ref, carry_scr, *, tm):
    i = pl.program_id(0)

    @pl.when(i == 0)
    def _():
        carry_scr[...] = cnt0_ref[...]

    mix = _dot(oret_ref[...], w1_ref[...]) + _dot(os5_ref[...], w2_ref[...]) + _dot(ogdn_ref[...], w3_ref[...])
    x1 = x_ref[...] + mix
    x1_ref[...] = x1
    h2 = _rms(x1) * nw_ref[...]
    for s in range(D_MODEL // LANE):
        h2_ref[pl.ds(s, tm, stride=SUBLANE), :] = h2[:, s * LANE:(s + 1) * LANE]

    logits = _dot_nt(wr_ref[...], h2) + br_ref[...]
    row_i = lax.broadcasted_iota(I32, (LANE, tm), 0)
    row = row_i.astype(F32)
    neg = -jnp.inf
    big = float(LANE)
    g_log = jnp.where((row_i >= N_EXPERTS) & (row_i < N_EXPERTS + MOE_GROUPS), logits, neg)
    g_max = jnp.max(g_log, axis=0, keepdims=True)
    grp = jnp.min(jnp.where(g_log == g_max, row - N_EXPERTS, big), axis=0, keepdims=True)
    p_grp = 1.0 / jnp.sum(jnp.exp(g_log - g_max), axis=0, keepdims=True)
    in_grp = (row >= grp * EXPERTS_PER_GROUP) & (row < (grp + 1.0) * EXPERTS_PER_GROUP)
    e_log = jnp.where(in_grp, logits, neg)
    v1 = jnp.max(e_log, axis=0, keepdims=True)
    i1 = jnp.min(jnp.where(e_log == v1, row, big), axis=0, keepdims=True)
    e_log2 = jnp.where(row == i1, neg, e_log)
    v2 = jnp.max(e_log2, axis=0, keepdims=True)
    i2 = jnp.min(jnp.where(e_log2 == v2, row, big), axis=0, keepdims=True)
    e2 = jnp.exp(v2 - v1)
    gate1 = p_grp / (1.0 + e2)
    gate2 = p_grp * e2 / (1.0 + e2)

    oh1 = row == i1
    oh2 = row == i2
    picked = jnp.where(oh1 | oh2, 1.0, 0.0)
    seen = carry_scr[:, 0:1]
    before = jnp.dot(picked.astype(BF16), tri_ref[...], preferred_element_type=F32) + seen
    rank1 = jnp.sum(jnp.where(oh1, before, 0.0), axis=0, keepdims=True)
    rank2 = jnp.sum(jnp.where(oh2, before, 0.0), axis=0, keepdims=True)
    total = seen + jnp.sum(picked, axis=1, keepdims=True)
    carry_scr[...] = jnp.broadcast_to(total, carry_scr.shape)
    cnt_ref[...] = jnp.broadcast_to(total, cnt_ref.shape)
    meta_ref[...] = jnp.concatenate([i1, i2, gate1, gate2, rank1, rank2, jnp.zeros((2, tm), F32)], axis=0)


def _outproj(x, o_ret, o_s5, o_gdn, w1, w2, w3, nw, wr, br, cnt0):
    t = x.shape[0]
    tm = min(ROW_TILE, t)
    tri = jnp.asarray(np.arange(tm)[:, None] < np.arange(tm)[None, :], dtype=BF16)
    row = lambda n: pl.BlockSpec((tm, n), lambda i: (i, 0))
    full = lambda a: pl.BlockSpec(a.shape, lambda i: (0, 0))
    return pl.pallas_call(
        functools.partial(_outproj_body, tm=tm),
        grid=(t // tm,),
        in_specs=[row(D_MODEL), row(RET_VD), row(S5_CH), row(GDN_HD),
                  full(w1), full(w2), full(w3), full(nw), full(wr), full(br), full(tri), full(cnt0)],
        out_specs=[row(D_MODEL), pl.BlockSpec((tm * SUBLANE, LANE), lambda i: (i, 0)),
                   pl.BlockSpec((SUBLANE, tm), lambda i: (0, i)), pl.BlockSpec((LANE, LANE), lambda i: (0, 0))],
        out_shape=[jax.ShapeDtypeStruct((t, D_MODEL), F32),
                   jax.ShapeDtypeStruct((t * SUBLANE, LANE), F32),
                   jax.ShapeDtypeStruct((SUBLANE, t), F32),
                   jax.ShapeDtypeStruct((LANE, LANE), F32)],
        scratch_shapes=[pltpu.VMEM((LANE, LANE), F32)],
        compiler_params=_params("arbitrary"),
        name="outproj_router",
    )(x, o_ret, o_s5, o_gdn, w1, w2, w3, nw, wr, br, tri, cnt0)


def _token_rows(ref, idx):
    return ref.at[pl.ds(pl.multiple_of(idx * SUBLANE, SUBLANE), SUBLANE)]


ZERO_ROWS = 128


def _zero_segment(zero_scr, xb_ref, sem, start, length, wait):
    def piece(off, n):
        cp = pltpu.make_async_copy(zero_scr.at[pl.ds(0, n * SUBLANE)],
                                   xb_ref.at[pl.ds(pl.multiple_of(off * SUBLANE, SUBLANE), n * SUBLANE)], sem)
        cp.wait() if wait else cp.start()

    n_big = length // ZERO_ROWS

    def big(i, carry):
        piece(start + i * ZERO_ROWS, ZERO_ROWS)
        return carry

    lax.fori_loop(0, n_big, big, 0)
    off = start + n_big * ZERO_ROWS
    rem = length - n_big * ZERO_ROWS
    bit = ZERO_ROWS // 2
    while bit >= 1:
        has = (rem & bit) != 0
        pl.when(has)(functools.partial(piece, off, bit))
        off = off + jnp.where(has, bit, 0)
        bit //= 2


def _scatter_body(d1_ref, d2_ref, zs_ref, zl_ref, src_a_ref, src_b_ref, xb_ref,
                  zero_scr, sem, zsem, *, tm, tiles_a):
    i = pl.program_id(0)
    base = i * tm

    @pl.when(i == 0)
    def _():
        zero_scr[...] = jnp.zeros_like(zero_scr)
        for wait in (False, True):
            lax.fori_loop(0, N_EXPERTS + 1,
                          lambda s, c, wait=wait: (_zero_segment(zero_scr, xb_ref, zsem, zs_ref[s], zl_ref[s], wait),
                                                   c)[1], 0)

    def scatter_tile(src_ref):
        def copy(t, d):
            return pltpu.make_async_copy(_token_rows(src_ref, t), _token_rows(xb_ref, d), sem)

        def issue(t, carry):
            copy(t, d1_ref[base + t]).start(priority=0)
            copy(t, d2_ref[base + t]).start(priority=1)
            return carry

        lax.fori_loop(0, tm, issue, 0, unroll=ISSUE_UNROLL)
        for _ in range(2):
            pltpu.make_async_copy(src_ref, src_ref, sem).wait()

    pl.when(i < tiles_a)(functools.partial(scatter_tile, src_a_ref))
    pl.when(i >= tiles_a)(functools.partial(scatter_tile, src_b_ref))


def _scatter(route, zero_start, zero_len, h2_a, h2_b, n_rows):
    tm = SCATTER_TILE
    tiles_a, tiles_b = h2_a.shape[0] // (tm * SUBLANE), h2_b.shape[0] // (tm * SUBLANE)
    assert tiles_a * tm * SUBLANE == h2_a.shape[0] and tiles_b * tm * SUBLANE == h2_b.shape[0]
    return pl.pallas_call(
        functools.partial(_scatter_body, tm=tm, tiles_a=tiles_a),
        grid_spec=pltpu.PrefetchScalarGridSpec(
            num_scalar_prefetch=4, grid=(tiles_a + tiles_b,),
            in_specs=[pl.BlockSpec((tm * SUBLANE, LANE), lambda i, *_: (jnp.minimum(i, tiles_a - 1), 0)),
                      pl.BlockSpec((tm * SUBLANE, LANE), lambda i, *_: (jnp.maximum(i - tiles_a, 0), 0))],
            out_specs=pl.BlockSpec(memory_space=pl.ANY),
            scratch_shapes=[pltpu.VMEM((ZERO_ROWS * SUBLANE, LANE), F32), pltpu.SemaphoreType.DMA(()),
                            pltpu.SemaphoreType.DMA(())]),
        out_shape=jax.ShapeDtypeStruct((n_rows * SUBLANE, LANE), F32),
        compiler_params=_params("arbitrary"),
        name="moe_scatter",
    )(*route, zero_start, zero_len, h2_a, h2_b)


def _experts_body(be_ref, nb_ref, xb_ref, w1_ref, w3_ref, w2_ref, yb_ref, w1_scr, w3_scr, w2_scr, *, blk):
    i = pl.program_id(0)
    live = i < nb_ref[0]

    @pl.when(live & ((i == 0) | (be_ref[i] != be_ref[jnp.maximum(i - 1, 0)])))
    def _():
        w1_scr[...] = w1_ref[...].astype(BF16)
        w3_scr[...] = w3_ref[...].astype(BF16)
        w2_scr[...] = w2_ref[...].astype(BF16)

    @pl.when(live)
    def _():
        x = jnp.concatenate([xb_ref[pl.ds(s, blk, stride=SUBLANE), :] for s in range(D_MODEL // LANE)],
                            axis=1).astype(BF16)
        hid = _silu(jnp.dot(x, w1_scr[...], preferred_element_type=F32)) * \
            jnp.dot(x, w3_scr[...], preferred_element_type=F32)
        y = _dot(hid, w2_scr[...])
        for s in range(D_MODEL // LANE):
            yb_ref[pl.ds(s, blk, stride=SUBLANE), :] = y[:, s * LANE:(s + 1) * LANE]

    @pl.when(jnp.logical_not(live))
    def _():
        yb_ref[...] = jnp.zeros_like(yb_ref)


def _experts(block_e, nb_used, xb, layer, w1, w3, w2, n_blocks):
    blk = MOE_BLK
    live = lambda i, nb: jnp.minimum(i, nb[0] - 1)
    tile_in = pl.BlockSpec((blk * SUBLANE, LANE), lambda i, be, nb: (live(i, nb), 0))
    tile_out = pl.BlockSpec((blk * SUBLANE, LANE), lambda i, be, nb: (i, 0))
    wspec = lambda a: pl.BlockSpec((None, None) + a.shape[2:],
                                   lambda i, be, nb: (layer, be[live(i, nb)], 0, 0))
    return pl.pallas_call(
        functools.partial(_experts_body, blk=blk),
        grid_spec=pltpu.PrefetchScalarGridSpec(
            num_scalar_prefetch=2, grid=(n_blocks,),
            in_specs=[tile_in, wspec(w1), wspec(w3), wspec(w2)], out_specs=tile_out,
            scratch_shapes=[pltpu.VMEM(w1.shape[2:], BF16), pltpu.VMEM(w3.shape[2:], BF16),
                            pltpu.VMEM(w2.shape[2:], BF16)]),
        out_shape=jax.ShapeDtypeStruct(xb.shape, F32),
        compiler_params=_params("arbitrary"),
        name="moe_experts",
    )(block_e, nb_used, xb, w1, w3, w2)


def _combine_body(*refs, tm, tiles_a, final_norm, n_proj):
    d1_ref, d2_ref, x1_a_ref, x1_b_ref, meta_a_ref, meta_b_ref, yb_ref, nw_ref = refs[:8]
    w_refs = refs[8:8 + n_proj]
    out_a_ref, out_b_ref = refs[8 + n_proj:10 + n_proj]
    za_refs = refs[10 + n_proj:10 + 2 * n_proj]
    zb_refs = refs[10 + 2 * n_proj:10 + 3 * n_proj]
    buf_ref, sem = refs[-2:]
    i = pl.program_id(0)
    phase = i % 2

    def gather(tile, ph):
        base = tile * tm

        def issue(t, carry):
            for k, d_ref in enumerate((d1_ref, d2_ref)):
                pltpu.make_async_copy(_token_rows(yb_ref, d_ref[base + t]),
                                      _token_rows(buf_ref.at[ph, k], t), sem.at[ph]).start(priority=k)
            return carry

        lax.fori_loop(0, tm, issue, 0, unroll=ISSUE_UNROLL)

    @pl.when(i == 0)
    def _():
        gather(0, 0)

    @pl.when(i + 1 < pl.num_programs(0))
    def _():
        gather(i + 1, 1 - phase)

    for k in range(2):
        pltpu.make_async_copy(buf_ref.at[phase, k], buf_ref.at[phase, k], sem.at[phase]).wait()

    def rows_of(slot):
        return jnp.concatenate([buf_ref[phase, slot, pl.ds(s, tm, stride=SUBLANE), :]
                                for s in range(D_MODEL // LANE)], axis=1)

    is_a = i < tiles_a
    meta = jnp.where(is_a, meta_a_ref[...], meta_b_ref[...])
    on_diag = lax.broadcasted_iota(I32, (tm, tm), 0) == lax.broadcasted_iota(I32, (tm, tm), 1)
    as_col = lambda r: jnp.sum(jnp.where(on_diag, r, 0.0), axis=1, keepdims=True)
    x2 = jnp.where(is_a, x1_a_ref[...], x1_b_ref[...]) + (as_col(meta[2:3, :]) * rows_of(0)
                                                         + as_col(meta[3:4, :]) * rows_of(1))
    normed = _rms(x2) * nw_ref[...]

    def emit(out_ref, z_refs):
        out_ref[...] = normed if final_norm else x2
        h = normed.astype(BF16)
        for w_ref, z_ref in zip(w_refs, z_refs):
            z_ref[...] = jnp.dot(h, w_ref[...], preferred_element_type=F32)

    pl.when(is_a)(functools.partial(emit, out_a_ref, za_refs))
    pl.when(jnp.logical_not(is_a))(functools.partial(emit, out_b_ref, zb_refs))


def _combine(route, x1_a, x1_b, meta_a, meta_b, yb, norm_w, final_norm, next_in_ws):
    tm = GATHER_TILE
    tiles_a, tiles_b = x1_a.shape[0] // tm, x1_b.shape[0] // tm
    assert tiles_a * tm == x1_a.shape[0] and tiles_b * tm == x1_b.shape[0]
    assert final_norm == (not next_in_ws)
    of_a = lambda n: pl.BlockSpec((tm, n), lambda i, *_: (jnp.minimum(i, tiles_a - 1), 0))
    of_b = lambda n: pl.BlockSpec((tm, n), lambda i, *_: (jnp.maximum(i - tiles_a, 0), 0))
    widths = [w.shape[1] for w in next_in_ws]
    outs = pl.pallas_call(
        functools.partial(_combine_body, tm=tm, tiles_a=tiles_a, final_norm=final_norm, n_proj=len(widths)),
        grid_spec=pltpu.PrefetchScalarGridSpec(
            num_scalar_prefetch=2, grid=(tiles_a + tiles_b,),
            in_specs=[of_a(D_MODEL), of_b(D_MODEL),
                      pl.BlockSpec((SUBLANE, tm), lambda i, *_: (0, jnp.minimum(i, tiles_a - 1))),
                      pl.BlockSpec((SUBLANE, tm), lambda i, *_: (0, jnp.maximum(i - tiles_a, 0))),
                      pl.BlockSpec(memory_space=pl.ANY),
                      pl.BlockSpec((1, D_MODEL), lambda i, *_: (0, 0))] +
                     [pl.BlockSpec(w.shape, lambda i, *_: (0, 0)) for w in next_in_ws],
            out_specs=[of_a(D_MODEL), of_b(D_MODEL)] + [of_a(n) for n in widths] + [of_b(n) for n in widths],
            scratch_shapes=[pltpu.VMEM((2, 2, tm * SUBLANE, LANE), F32), pltpu.SemaphoreType.DMA((2,))]),
        out_shape=[jax.ShapeDtypeStruct(x1_a.shape, F32), jax.ShapeDtypeStruct(x1_b.shape, F32)] +
                  [jax.ShapeDtypeStruct((x.shape[0], n), F32) for x in (x1_a, x1_b) for n in widths],
        compiler_params=_params("arbitrary"),
        name="moe_combine",
    )(*route, x1_a, x1_b, meta_a, meta_b, yb, norm_w, *next_in_ws)
    n = len(widths)
    return (outs[0], outs[1]), (outs[2:2 + n], outs[2 + n:2 + 2 * n])


def _moe(stream_a, stream_b, counts, layer, w1, w3, w2, norm_w, final_norm, next_in_ws):
    (x1_a, h2_a, meta_a), (x1_b, h2_b, meta_b) = stream_a, stream_b
    t = x1_a.shape[0] + x1_b.shape[0]
    n_blocks = (2 * t + N_EXPERTS * (MOE_BLK - 1)) // MOE_BLK
    col = lambda c: jnp.concatenate([meta_a[c], meta_b[c]]).astype(I32)
    e1, e2, r1, r2 = col(0), col(1), col(4), col(5)
    cnt = counts[:N_EXPERTS, 0].astype(I32)
    padded = (cnt + MOE_BLK - 1) // MOE_BLK * MOE_BLK
    pad_end = jnp.cumsum(padded)
    pad_start = pad_end - padded
    slot = lambda e, r: r + jnp.sum(jnp.where(e[:, None] == jnp.arange(N_EXPERTS, dtype=I32)[None, :],
                                              pad_start[None, :], 0), axis=1)
    route = (slot(e1, r1), slot(e2, r2))
    nb_used = (pad_end[-1:] // MOE_BLK).astype(I32)
    blk_start = jnp.arange(n_blocks, dtype=I32) * MOE_BLK
    block_e = jnp.minimum(jnp.sum(blk_start[:, None] >= pad_end[None, :], axis=1), N_EXPERTS - 1).astype(I32)
    n_rows = n_blocks * MOE_BLK
    zero_start = jnp.concatenate([pad_start + cnt, pad_end[-1:]]).astype(I32)
    zero_len = jnp.concatenate([padded - cnt, n_rows - pad_end[-1:]]).astype(I32)
    xb = _scatter(route, zero_start, zero_len, h2_a, h2_b, n_rows)
    yb = _experts(block_e, nb_used, xb, layer, w1, w3, w2, n_blocks)
    return _combine(route, x1_a, x1_b, meta_a, meta_b, yb, norm_w, final_norm, next_in_ws)


def _rope_tables(pos):
    half = RET_DK // 2
    inv = ROPE_BASE ** (-jnp.arange(half, dtype=F32) / half)
    ang = pos[:, None] * inv[None, :]
    cos, sin = jnp.cos(ang), jnp.sin(ang)
    cos_t = jnp.tile(jnp.concatenate([cos, cos], axis=1), (1, RET_HEADS))
    sin_t = jnp.tile(jnp.concatenate([-sin, sin], axis=1), (1, RET_HEADS))
    return cos_t, sin_t


def _block_diag(blocks):
    g, r, c = blocks.shape
    eye = jnp.eye(g, dtype=bool)
    return jnp.where(eye[:, None, :, None], blocks[:, :, None, :], 0).reshape(g * r, g * c)


def _layer_weights(l, w_in, s5, s5_b_bar, s5_c_re, s5_c_im, s5_d, s5_glu_w, s5_glu_b, gdn_conv_w, gdn_a_log,
                   gdn_dt_bias, gdn_norm_w, w_out, router_group_w, router_group_b, router_expert_w,
                   router_expert_b):
    wi = w_in[l]
    g0 = RET_IN + S5_IN
    w_ab = jnp.pad(wi[:, g0 + GDN_QKV + GDN_HD:], ((0, 0), (0, LANE - 2 * GDN_HEADS)))
    in_ws = [wi[:, :RET_IN], wi[:, g0:g0 + GDN_QKV], wi[:, RET_IN:g0], wi[:, g0 + GDN_QKV:g0 + GDN_QKV + GDN_HD],
             w_ab]
    in_ws = [w.astype(BF16) for w in in_ws]
    ab_re, ab_im, bb_re, bb_im = s5
    n = S5_LANES
    sl = slice(l * n, (l + 1) * n)
    ab = jnp.concatenate([ab_re[sl].reshape(1, n), ab_im[sl].reshape(1, n)], axis=0)
    to_bd = lambda m: _block_diag(jnp.swapaxes(m[sl].reshape(S5_GROUPS, S5_STATE, S5_GROUP_CH), 1, 2))
    wb = jnp.concatenate([to_bd(bb_re), to_bd(bb_im)], axis=1).astype(BF16)
    wc_re = _block_diag(jnp.swapaxes(s5_c_re[l], 1, 2)).astype(BF16)
    wc_im = _block_diag(jnp.swapaxes(s5_c_im[l], 1, 2)).astype(BF16)
    s5_ws = (ab, wb, wc_re, wc_im, s5_d[l].reshape(1, S5_CH), s5_glu_w[l].astype(BF16),
             s5_glu_b[l].reshape(1, S5_CH))
    pad4 = lambda v: jnp.pad(v.reshape(1, GDN_HEADS), ((0, 0), (0, LANE - GDN_HEADS)))
    gdn_ws = (gdn_conv_w[l], pad4(gdn_a_log[l]), pad4(gdn_dt_bias[l]),
              jnp.tile(gdn_norm_w[l], GDN_HEADS).reshape(1, GDN_HD))
    wo = w_out[l].astype(BF16)
    out_ws = (wo[:RET_VD], wo[RET_VD:RET_VD + S5_CH], wo[RET_VD + S5_CH:])
    wr = jnp.pad(jnp.concatenate([router_expert_w[l], router_group_w[l]], axis=1).T,
                 ((0, LANE - N_EXPERTS - MOE_GROUPS), (0, 0))).astype(BF16)
    br = jnp.pad(jnp.concatenate([router_expert_b[l], router_group_b[l]]),
                 (0, LANE - N_EXPERTS - MOE_GROUPS)).reshape(LANE, 1)
    return in_ws, s5_ws, gdn_ws, out_ws, (wr, br)


def _mix_and_route(l, x, projected, bsz, seq, rope, states, layer_w, norm_mix, norm_ffn, cnt0):
    t = bsz * seq
    ret_s, s5_re, s5_im, gdn_s, gdn_buf = states
    in_ws, s5_ws, gdn_ws, out_ws, (wr, br) = layer_w
    zr, zq, zs, zg, zab = projected or _inproj(x, norm_mix[l].reshape(1, D_MODEL), in_ws)
    o_ret, ret_fin = _retention(zr, *rope, ret_s.reshape(DEPTH, bsz, RET_QD, RET_DV), l, bsz, seq)
    u_tm = jnp.swapaxes(zs.reshape(bsz, seq, S5_CH), 0, 1)
    o_s5_tm, re_fin, im_fin = _s5(u_tm, s5_re[l].reshape(bsz, S5_LANES), s5_im[l].reshape(bsz, S5_LANES), *s5_ws)
    o_s5 = jnp.swapaxes(o_s5_tm, 0, 1).reshape(t, S5_CH)
    buf0 = jnp.pad(gdn_buf[l], ((0, 0), (SUBLANE - (GDN_CONV - 1), 0), (0, 0)))
    o_gdn, nbuf, gdn_fin = _gdn(zq, zg, zab, *gdn_ws, buf0, gdn_s.reshape(DEPTH, bsz, GDN_HD, GDN_DV), l, bsz, seq)
    x1, h2_tiles, meta, counts = _outproj(x, o_ret, o_s5, o_gdn, *out_ws, norm_ffn[l].reshape(1, D_MODEL), wr, br,
                                          cnt0)
    new_states = (ret_fin.reshape(bsz, RET_HEADS, RET_DK, RET_DV), re_fin.reshape(bsz, S5_GROUPS, S5_STATE),
                  im_fin.reshape(bsz, S5_GROUPS, S5_STATE), gdn_fin.reshape(bsz, GDN_HEADS, GDN_DK, GDN_DV),
                  nbuf[:, SUBLANE - (GDN_CONV - 1):, :])
    return (x1, h2_tiles, meta), counts, new_states


def _trunks(xs, positions, states, layer_ws, norm_mix, norm_ffn, experts, norm_final):
    shapes = [x.shape[:2] for x in xs]
    ropes = [_rope_tables(pos) for pos in positions]
    xs = [x.reshape(b * s, D_MODEL) for x, (b, s) in zip(xs, shapes)]
    outs = [[[] for _ in range(5)] for _ in xs]
    projected = [None for _ in xs]
    for l in range(DEPTH):
        counts = jnp.zeros((LANE, LANE), F32)
        routed = []
        for i, x in enumerate(xs):
            stream, counts, new_states = _mix_and_route(l, x, projected[i], *shapes[i], ropes[i], states[i],
                                                        layer_ws[l], norm_mix, norm_ffn, counts)
            routed.append(stream)
            for lst, s in zip(outs[i], new_states):
                lst.append(s)
        last = l == DEPTH - 1
        norm_w = norm_final if last else norm_mix[l + 1]
        xs, projected = _moe(*routed, counts, l, *experts, norm_w.reshape(1, D_MODEL), last,
                             [] if last else layer_ws[l + 1][0])
    ys = [x.reshape(b, s, D_MODEL) for x, (b, s) in zip(xs, shapes)]
    return ys, [tuple(jnp.stack(o) for o in per_stream) for per_stream in outs]


def kernel(x_prompt, x_sample, state_ret, state_s5_re, state_s5_im, state_gdn, state_gdn_conv, norm_mix, w_in, s5_a_re, s5_a_im, s5_log_dt, s5_b_re, s5_b_im, s5_c_re, s5_c_im, s5_d, s5_glu_w, s5_glu_b, gdn_conv_w, gdn_a_log, gdn_dt_bias, gdn_norm_w, w_out, norm_ffn, router_group_w, router_group_b, router_expert_w, router_expert_b, expert_w1, expert_w3, expert_w2, norm_final):
    n = DEPTH * S5_LANES
    col = lambda a: a.reshape(n, 1)
    log_dt = jnp.broadcast_to(s5_log_dt[:, :, None], (DEPTH, S5_GROUPS, S5_STATE))
    s5 = _s5_prep(col(s5_a_re), col(s5_a_im), col(log_dt), s5_b_re.reshape(n, S5_GROUP_CH),
                  s5_b_im.reshape(n, S5_GROUP_CH))
    layer_ws = [_layer_weights(l, w_in, s5, None, s5_c_re, s5_c_im, s5_d, s5_glu_w, s5_glu_b, gdn_conv_w,
                               gdn_a_log, gdn_dt_bias, gdn_norm_w, w_out, router_group_w, router_group_b,
                               router_expert_w, router_expert_b) for l in range(DEPTH)]
    experts = (expert_w1, expert_w3, expert_w2)
    bp, lp, _ = x_prompt.shape
    zero_states = (jnp.zeros((DEPTH, bp, RET_HEADS, RET_DK, RET_DV), F32),
                   jnp.zeros((DEPTH, bp, S5_GROUPS, S5_STATE), F32),
                   jnp.zeros((DEPTH, bp, S5_GROUPS, S5_STATE), F32),
                   jnp.zeros((DEPTH, bp, GDN_HEADS, GDN_DK, GDN_DV), F32),
                   jnp.zeros((DEPTH, bp, GDN_CONV - 1, GDN_QKV), F32))
    sample_states = (state_ret, state_s5_re, state_s5_im, state_gdn, state_gdn_conv)
    positions = (jnp.arange(lp, dtype=F32), PAST_LEN + jnp.arange(x_sample.shape[1], dtype=F32))
    (y_p, y_s), (p_states, s_states) = _trunks((x_prompt, x_sample), positions, (zero_states, sample_states),
                                               layer_ws, norm_mix, norm_ffn, experts, norm_final)
    return (y_p, y_s) + p_states + s_states
```

```python
import functools
import math

import jax
import jax.numpy as jnp
import numpy as np
from jax import lax
from jax.experimental import pallas as pl
from jax.experimental.pallas import tpu as pltpu

F32 = jnp.float32
BF16 = jnp.bfloat16
I32 = jnp.int32

D_MODEL = 1024
DEPTH = 2
RET_HEADS, RET_DK, RET_DV = 4, 64, 128
S5_GROUPS, S5_GROUP_CH, S5_STATE = 16, 16, 64
S5_CH = S5_GROUPS * S5_GROUP_CH
S5_LANES = S5_GROUPS * S5_STATE
GDN_HEADS, GDN_DK, GDN_DV, GDN_CONV = 4, 64, 64, 4
GDN_HD = GDN_HEADS * GDN_DK
GDN_QKV = 3 * GDN_HD
RET_QD = RET_HEADS * RET_DK
RET_VD = RET_HEADS * RET_DV
RET_IN = 2 * RET_QD + 2 * RET_VD
S5_IN = S5_CH
GDN_IN = GDN_QKV + GDN_HD + 2 * GDN_HEADS
GDN_CHUNK = 64
RET_CHUNK = 64
MOE_GROUPS, EXPERTS_PER_GROUP = 4, 8
N_EXPERTS = MOE_GROUPS * EXPERTS_PER_GROUP
D_EXPERT = 512
ROPE_BASE = 10000.0
EPS = 1e-6
PAST_LEN = 16384

LANE = 128
SUBLANE = 8
ROW_TILE = 512
MOE_BLK = 512
SCATTER_TILE = 512
GATHER_TILE = 256
S5_STEPS = 128
ISSUE_UNROLL = 8
SEQS_PER_STEP = 16
RET_LONG_SEQS_PER_STEP = 4
GDN_LONG_SEQS_PER_STEP = 2
FINAL_GATHER_TILE = 512
VMEM_LIMIT = 56 * 1024 * 1024


def _params(*sem):
    return pltpu.CompilerParams(dimension_semantics=sem, vmem_limit_bytes=VMEM_LIMIT)


def _dot(a, b):
    return jnp.dot(a.astype(BF16), b.astype(BF16), preferred_element_type=F32)


def _dot_nt(a, b):
    return lax.dot_general(a.astype(BF16), b.astype(BF16), (((1,), (1,)), ((), ())),
                           preferred_element_type=F32)


def _dot_tn(a, b):
    return lax.dot_general(a.astype(BF16), b.astype(BF16), (((0,), (0,)), ((), ())),
                           preferred_element_type=F32)


def _dot_hi(a, b):
    a1 = a.astype(BF16)
    a2 = (a - a1.astype(F32)).astype(BF16)
    b1 = b.astype(BF16)
    b2 = (b - b1.astype(F32)).astype(BF16)
    d = lambda x, y: jnp.dot(x, y, preferred_element_type=F32)
    return d(a1, b1) + (d(a1, b2) + d(a2, b1))


def _split3(x):
    p1 = x.astype(BF16)
    r1 = x - p1.astype(F32)
    p2 = r1.astype(BF16)
    p3 = (r1 - p2.astype(F32)).astype(BF16)
    return p1, p2, p3


def _dot_sel_l(sel, x):
    p1, p2, p3 = _split3(x)
    d = lambda p: jnp.dot(sel, p, preferred_element_type=F32)
    return d(p1) + d(p2) + d(p3)


def _dot_sel_r(x, sel):
    p1, p2, p3 = _split3(x)
    d = lambda p: jnp.dot(p, sel, preferred_element_type=F32)
    return d(p1) + d(p2) + d(p3)


def _rms(x):
    return x * lax.rsqrt(jnp.mean(x * x, axis=-1, keepdims=True) + EPS)


def _silu(x):
    return x * jax.nn.sigmoid(x)


def _inproj_body(x_ref, nw_ref, wr_ref, wq_ref, ws_ref, wg_ref, wab_ref,
                 zr_ref, zq_ref, zs_ref, zg_ref, zab_ref):
    h = (_rms(x_ref[...]) * nw_ref[...]).astype(BF16)
    for w_ref, z_ref in ((wr_ref, zr_ref), (wq_ref, zq_ref), (ws_ref, zs_ref),
                         (wg_ref, zg_ref), (wab_ref, zab_ref)):
        z_ref[...] = jnp.dot(h, w_ref[...], preferred_element_type=F32)


def _inproj(x, nw, ws):
    t = x.shape[0]
    tm = min(ROW_TILE, t)
    widths = [w.shape[1] for w in ws]
    row = lambda n: pl.BlockSpec((tm, n), lambda i: (i, 0))
    full = lambda a: pl.BlockSpec(a.shape, lambda i: (0, 0))
    return pl.pallas_call(
        _inproj_body,
        grid=(t // tm,),
        in_specs=[row(D_MODEL), full(nw)] + [full(w) for w in ws],
        out_specs=[row(n) for n in widths],
        out_shape=[jax.ShapeDtypeStruct((t, n), F32) for n in widths],
        compiler_params=_params("parallel"),
        name="inproj",
    )(x, nw, *ws)


def _seqs_per_step(bsz, blocks_per_seq, long_seqs):
    want = SEQS_PER_STEP if blocks_per_seq == 1 else long_seqs
    return want if bsz % want == 0 else 1


def _ret_log_gamma(h):
    return math.log(1.0 - 2.0 ** (-5.0 - h))


def _ret_body(zr_ref, cos_ref, sin_ref, s0_ref, o_ref, sfin_ref, s_scr, *, nseq, rows, chunk):
    j = pl.program_id(1)
    stack = RET_HEADS * chunk

    @pl.when(j == 0)
    def _():
        s_scr[...] = s0_ref[...]

    lane = lax.broadcasted_iota(I32, (rows, RET_QD), 1)
    first_half = (lane % RET_DK) < (RET_DK // 2)

    def rotary(x):
        swapped = jnp.where(first_half, pltpu.roll(x, RET_QD - RET_DK // 2, 1),
                            pltpu.roll(x, RET_DK // 2, 1))
        return x * cos_ref[...] + swapped * sin_ref[...]

    q = [rotary(zr_ref[s, :, 0:RET_QD]) for s in range(nseq)]
    k = [rotary(zr_ref[s, :, RET_QD:2 * RET_QD]) * (RET_DK ** -0.5) for s in range(nseq)]

    def by_head(idx, fn):
        out = jnp.zeros(idx.shape, F32)
        for h in range(RET_HEADS):
            out = jnp.where(idx == h, fn(h), out)
        return out

    lg_lane = by_head(lax.broadcasted_iota(I32, (chunk, RET_QD), 1) // RET_DK, _ret_log_gamma)
    pos = lax.broadcasted_iota(I32, (chunk, RET_QD), 0).astype(F32)
    q_scale = jnp.exp((pos + 1.0) * lg_lane)
    k_scale = jnp.exp((chunk - 1.0 - pos) * lg_lane)
    st_row = lax.broadcasted_iota(I32, (stack, stack), 0)
    st_col = lax.broadcasted_iota(I32, (stack, stack), 1)
    causal = ((st_row // chunk) == (st_col // chunk)) & (st_row >= st_col)
    lg_stack = by_head(st_row // chunk, _ret_log_gamma)
    decay = jnp.where(causal, jnp.exp(jnp.where(causal, (st_row - st_col).astype(F32), 0.0) * lg_stack), 0.0)
    q_rows = (lax.broadcasted_iota(I32, (stack, RET_QD), 0) // chunk) == \
             (lax.broadcasted_iota(I32, (stack, RET_QD), 1) // RET_DK)
    v_rows = (lax.broadcasted_iota(I32, (stack, RET_VD), 0) // chunk) == \
             (lax.broadcasted_iota(I32, (stack, RET_VD), 1) // RET_DV)
    s_row_head = lax.broadcasted_iota(I32, (RET_QD, RET_DV), 0) // RET_DK
    s_decay = by_head(s_row_head, lambda h: math.exp(chunk * _ret_log_gamma(h)))
    kv_diag = (lax.broadcasted_iota(I32, (RET_QD, RET_VD), 0) // RET_DK) == \
              (lax.broadcasted_iota(I32, (RET_QD, RET_VD), 1) // RET_DV)

    def tile4(a):
        return jnp.concatenate([a] * RET_HEADS, axis=0)

    def collapse(a):
        out = a[0:chunk]
        for h in range(1, RET_HEADS):
            out = out + a[h * chunk:(h + 1) * chunk]
        return out

    seqs = range(nseq)
    n_chunks = rows // chunk
    state = [s_scr[s] for s in seqs]
    for ci in range(n_chunks):
        r = slice(ci * chunk, (ci + 1) * chunk)
        v_cs = [zr_ref[s, r, 2 * RET_QD:2 * RET_QD + RET_VD] for s in seqs]
        scores = [_dot_nt(jnp.where(q_rows, tile4(q[s][r]), 0.0), tile4(k[s][r])) * decay for s in seqs]
        o_intra = [collapse(_dot(scores[s], jnp.where(v_rows, tile4(v_cs[s]), 0.0))) for s in seqs]
        s_full = [jnp.concatenate([jnp.where(s_row_head == h, state[s], 0.0) for h in range(RET_HEADS)], axis=1)
                  for s in seqs]
        o_cs = [o_intra[s] + _dot(q[s][r] * q_scale, s_full[s]) for s in seqs]
        for s in seqs:
            gate = zr_ref[s, r, 2 * RET_QD + RET_VD:2 * RET_QD + 2 * RET_VD]
            for h in range(RET_HEADS):
                sl = slice(h * RET_DV, (h + 1) * RET_DV)
                o_ref[s, r, sl] = _rms(o_cs[s][:, sl]) * _silu(gate[:, sl])
        for s in seqs:
            kv = jnp.where(kv_diag, _dot_tn(k[s][r] * k_scale, v_cs[s]), 0.0)
            kv_own = kv[:, 0:RET_DV]
            for h in range(1, RET_HEADS):
                kv_own = kv_own + kv[:, h * RET_DV:(h + 1) * RET_DV]
            state[s] = state[s] * s_decay + kv_own
    for s in seqs:
        s_scr[s] = state[s]

    @pl.when(j == pl.num_programs(1) - 1)
    def _():
        for s in seqs:
            sfin_ref[s] = state[s]


def _retention(zr, cos, sin, s0, layer, bsz, seq):
    chunk = min(RET_CHUNK, seq)
    rows = min(4 * chunk, seq)
    nj = seq // rows
    nseq = _seqs_per_step(bsz, nj, RET_LONG_SEQS_PER_STEP)
    o, s_fin = pl.pallas_call(
        functools.partial(_ret_body, nseq=nseq, rows=rows, chunk=chunk),
        grid=(bsz // nseq, nj),
        in_specs=[pl.BlockSpec((nseq, rows, RET_IN), lambda b, j: (b, j, 0)),
                  pl.BlockSpec((rows, RET_QD), lambda b, j: (j, 0)),
                  pl.BlockSpec((rows, RET_QD), lambda b, j: (j, 0)),
                  pl.BlockSpec((None, nseq, RET_QD, RET_DV), lambda b, j: (layer, b, 0, 0))],
        out_specs=[pl.BlockSpec((nseq, rows, RET_VD), lambda b, j: (b, j, 0)),
                   pl.BlockSpec((nseq, RET_QD, RET_DV), lambda b, j: (b, 0, 0))],
        out_shape=[jax.ShapeDtypeStruct((bsz, seq, RET_VD), F32),
                   jax.ShapeDtypeStruct((bsz, RET_QD, RET_DV), F32)],
        scratch_shapes=[pltpu.VMEM((nseq, RET_QD, RET_DV), F32)],
        compiler_params=_params("parallel", "arbitrary"),
        name="retention",
    )(zr.reshape(bsz, seq, RET_IN), cos, sin, s0)
    return o.reshape(bsz * seq, RET_VD), s_fin


def _s5_prep_body(are_ref, aim_ref, ldt_ref, bre_ref, bim_ref, abre_ref, abim_ref, bbre_ref, bbim_ref):
    lam_re, lam_im = are_ref[...], aim_ref[...]
    dt = jnp.exp(ldt_ref[...])
    mag = jnp.exp(lam_re * dt)
    ab_re = mag * jnp.cos(lam_im * dt)
    ab_im = mag * jnp.sin(lam_im * dt)
    den = lam_re * lam_re + lam_im * lam_im
    f_re = ((ab_re - 1.0) * lam_re + ab_im * lam_im) / den
    f_im = (ab_im * lam_re - (ab_re - 1.0) * lam_im) / den
    abre_ref[...] = ab_re
    abim_ref[...] = ab_im
    bbre_ref[...] = f_re * bre_ref[...] - f_im * bim_ref[...]
    bbim_ref[...] = f_re * bim_ref[...] + f_im * bre_ref[...]


def _s5_prep(a_re, a_im, log_dt, b_re, b_im):
    n = a_re.shape[0]
    col = jax.ShapeDtypeStruct((n, 1), F32)
    mat = jax.ShapeDtypeStruct((n, S5_GROUP_CH), F32)
    return pl.pallas_call(_s5_prep_body, out_shape=[col, col, mat, mat], name="s5_prep")(
        a_re, a_im, log_dt, b_re, b_im)


def _gelu_tanh(x):
    return x * (0.5 * (1.0 + jnp.tanh(math.sqrt(2.0 / math.pi) * (x + 0.044715 * (x * x * x)))))


def _s5_body(u_ref, h0re_ref, h0im_ref, ab_ref, wb_ref, wcre_ref, wcim_ref, d_ref, gw_ref, gb_ref,
             o_ref, hre_ref, him_ref, bu_scr, st_scr, *, steps):
    j = pl.program_id(1)

    @pl.when(j == 0)
    def _():
        st_scr[0] = h0re_ref[...]
        st_scr[1] = h0im_ref[...]

    rows = steps * SUBLANE
    u = u_ref[...].reshape(rows, S5_CH)
    bu_scr[...] = _dot(u, wb_ref[...])
    a_re = jnp.broadcast_to(ab_ref[0:1, :], (SUBLANE, S5_LANES))
    a_im = jnp.broadcast_to(ab_ref[1:2, :], (SUBLANE, S5_LANES))

    def step(t, carry):
        h_re, h_im = carry
        r = pl.ds(pl.multiple_of(t * SUBLANE, SUBLANE), SUBLANE)
        n_re = a_re * h_re - a_im * h_im + bu_scr[r, 0:S5_LANES]
        n_im = a_re * h_im + a_im * h_re + bu_scr[r, S5_LANES:2 * S5_LANES]
        bu_scr[r, 0:S5_LANES] = n_re
        bu_scr[r, S5_LANES:2 * S5_LANES] = n_im
        return n_re, n_im

    h_re, h_im = lax.fori_loop(0, steps, step, (st_scr[0], st_scr[1]))
    st_scr[0] = h_re
    st_scr[1] = h_im
    hre_ref[...] = h_re
    him_ref[...] = h_im

    y = _dot(bu_scr[:, 0:S5_LANES], wcre_ref[...]) - _dot(bu_scr[:, S5_LANES:2 * S5_LANES], wcim_ref[...])
    y = _gelu_tanh(y + d_ref[...] * u)
    y = y * jax.nn.sigmoid(_dot(y, gw_ref[...]) + gb_ref[...])
    o_ref[...] = y.reshape(steps, SUBLANE, S5_CH)


def _s5(u_tm, h0_re, h0_im, ab, wb, wc_re, wc_im, d_skip, glu_w, glu_b):
    seq, bsz, _ = u_tm.shape
    steps = min(S5_STEPS, seq)
    full = lambda a: pl.BlockSpec(a.shape, lambda g, j: (0,) * a.ndim)
    st = pl.BlockSpec((SUBLANE, S5_LANES), lambda g, j: (g, 0))
    return pl.pallas_call(
        functools.partial(_s5_body, steps=steps),
        grid=(bsz // SUBLANE, seq // steps),
        in_specs=[pl.BlockSpec((steps, SUBLANE, S5_CH), lambda g, j: (j, g, 0)), st, st,
                  full(ab), full(wb), full(wc_re), full(wc_im), full(d_skip), full(glu_w), full(glu_b)],
        out_specs=[pl.BlockSpec((steps, SUBLANE, S5_CH), lambda g, j: (j, g, 0)), st, st],
        out_shape=[jax.ShapeDtypeStruct((seq, bsz, S5_CH), F32),
                   jax.ShapeDtypeStruct((bsz, S5_LANES), F32),
                   jax.ShapeDtypeStruct((bsz, S5_LANES), F32)],
        scratch_shapes=[pltpu.VMEM((steps * SUBLANE, 2 * S5_LANES), F32),
                        pltpu.VMEM((2, SUBLANE, S5_LANES), F32)],
        compiler_params=_params("parallel", "arbitrary"),
        name="s5",
    )(u_tm, h0_re, h0_im, ab, wb, wc_re, wc_im, d_skip, glu_w, glu_b)


def _gdn_body(zq_ref, zg_ref, zab_ref, cw_ref, alog_ref, dtb_ref, nw_ref, buf0_ref, s0_ref,
              ones_bd_ref, tri_ref, ea_ref, eb_ref, ec_ref, spread_ref, gather_ref,
              o_ref, nbuf_ref, sfin_ref,
              ext_scr, q_scr, k_scr, kb_scr, vb_scr, g_scr, g4_scr, s_scr, *, nseq, rows, chunk):
    j = pl.program_id(1)
    stack = GDN_HEADS * chunk
    bd_state = (lax.broadcasted_iota(I32, (GDN_HD, GDN_HD), 0) // GDN_DK) == \
               (lax.broadcasted_iota(I32, (GDN_HD, GDN_HD), 1) // GDN_DV)

    @pl.when(j == 0)
    def _():
        for s in range(nseq):
            ext_scr[s, 0:SUBLANE, :] = buf0_ref[s]
            s_scr[s] = jnp.where(bd_state, _dot_sel_r(s0_ref[s], spread_ref[...]), 0.0)

    convs = []
    for s in range(nseq):
        ext_scr[s, SUBLANE:SUBLANE + rows, :] = zq_ref[s]
        conv = ext_scr[s, SUBLANE - 3:SUBLANE - 3 + rows, :] * cw_ref[0:1, :]
        for i in range(1, GDN_CONV):
            conv = conv + ext_scr[s, SUBLANE - 3 + i:SUBLANE - 3 + i + rows, :] * cw_ref[i:i + 1, :]
        tail = ext_scr[s, rows:rows + SUBLANE, :]
        nbuf_ref[s] = tail
        ext_scr[s, 0:SUBLANE, :] = tail
        convs.append(conv)
    qkv = _silu(convs[0] if nseq == 1 else jnp.concatenate(convs, axis=0))

    ones_bd = ones_bd_ref[...]
    q_raw = qkv[:, 0:GDN_HD]
    k_raw = qkv[:, GDN_HD:2 * GDN_HD]
    q_scr[...] = q_raw * lax.rsqrt(_dot_sel_r(q_raw * q_raw, ones_bd) + EPS) * (GDN_DK ** -0.5)
    k_n = k_raw * lax.rsqrt(_dot_sel_r(k_raw * k_raw, ones_bd) + EPS)
    k_scr[...] = k_n

    ab = zab_ref[...].reshape(nseq * rows, LANE)
    x = ab + dtb_ref[...]
    softplus = jnp.maximum(x, 0.0) + jnp.log1p(jnp.exp(-jnp.abs(x)))
    g_pad = -jnp.exp(alog_ref[...]) * softplus
    beta = _dot_sel_r(jax.nn.sigmoid(ab), eb_ref[...])
    g_cum = _dot_sel_l(tri_ref[...], g_pad)
    g_scr[...] = _dot_sel_r(g_cum, ea_ref[...])
    g4_scr[...] = _dot_sel_r(g_pad, ec_ref[...])
    kb_scr[...] = k_n * beta
    vb_scr[...] = qkv[:, 2 * GDN_HD:3 * GDN_HD] * beta

    w_row = lax.broadcasted_iota(I32, (chunk, stack), 0)
    w_col = lax.broadcasted_iota(I32, (chunk, stack), 1) % chunk
    strict_w = w_row > w_col
    causal_w = w_row >= w_col
    same_head = (lax.broadcasted_iota(I32, (stack, stack), 0) // chunk) == \
                (lax.broadcasted_iota(I32, (stack, stack), 1) // chunk)
    head_rows = (lax.broadcasted_iota(I32, (stack, GDN_HD), 0) // chunk) == \
                (lax.broadcasted_iota(I32, (stack, GDN_HD), 1) // GDN_DK)
    head_rows2 = jnp.concatenate([head_rows, head_rows], axis=1)
    tri_c = tri_ref[0:chunk, 0:chunk]
    packed_ok = chunk % (2 * SUBLANE) == 0

    def tile4(a):
        return jnp.concatenate([a] * GDN_HEADS, axis=0)

    def split2(a):
        hi = a.astype(BF16)
        return hi, (a - hi.astype(F32)).astype(BF16)

    def stack_masked(parts, mask):
        if packed_ok:
            return [jnp.where(mask, tile4(p), jnp.zeros((), BF16)) for p in parts]
        return [jnp.where(mask, tile4(p.astype(F32)), 0.0).astype(BF16) for p in parts]

    def on_diag(parts):
        return stack_masked(parts, same_head)

    def mm(a, b):
        return jnp.dot(a, b, preferred_element_type=F32)

    def mm_hi(a_parts, b_parts):
        return mm(a_parts[0], b_parts[0]) + (mm(a_parts[0], b_parts[1]) + mm(a_parts[1], b_parts[0]))

    n_chunks = rows // chunk
    every = range(nseq * n_chunks)
    rows_of = lambda u: slice(u * chunk, (u + 1) * chunk)
    solved = []
    for u in every:
        r = rows_of(u)
        q_c, k_c, kb_c, vb_c, g_c = q_scr[r, :], k_scr[r, :], kb_scr[r, :], vb_scr[r, :], g_scr[r, :]
        exp_g = jnp.exp(g_c)
        g_diff = _dot_sel_l(tri_c, jnp.where(strict_w, g4_scr[r, :], 0.0))
        decay = jnp.exp(jnp.where(causal_w, g_diff, 0.0))
        k_heads = jnp.where(head_rows, tile4(k_c), 0.0)
        lmat = jnp.where(strict_w, _dot_nt(kb_c, k_heads) * decay, 0.0)
        attn = jnp.where(causal_w, _dot_nt(q_c, k_heads) * decay, 0.0)
        rhs_c = jnp.concatenate([vb_c, kb_c * exp_g], axis=1)
        solved.append((lmat, attn, exp_g, rhs_c))

    one = lambda a: a.astype(BF16)
    t_acc = [-solved[ci][0] for ci in every]
    p_one = [one(solved[ci][0]) for ci in every]
    p_diag = [on_diag([p])[0] for p in p_one]
    span = 2
    while span < chunk + 1:
        power = [mm(p_one[ci], p_diag[ci]) for ci in every]
        p_one = [one(p) for p in power]
        p_diag = [on_diag([p])[0] for p in p_one]
        t_acc = [t_acc[ci] + power[ci] + mm(one(t_acc[ci]), p_diag[ci]) for ci in every]
        span *= 2
    t_one = [one(t) for t in t_acc]
    apply_inv = lambda ci, v: v + mm(t_one[ci], stack_masked([one(v)], head_rows2)[0])
    sol0 = [apply_inv(ci, solved[ci][3]) for ci in every]
    resid = [solved[ci][3] - (sol0[ci] + mm_hi(split2(solved[ci][0]), stack_masked(split2(sol0[ci]), head_rows2)))
             for ci in every]
    sols = [sol0[ci] + apply_inv(ci, resid[ci]) for ci in every]

    seqs = range(nseq)
    s_cur = [s_scr[s] for s in seqs]
    for ci in range(n_chunks):
        us = [s * n_chunks + ci for s in seqs]
        v_new = [sols[u][:, 0:GDN_HD] - _dot(sols[u][:, GDN_HD:2 * GDN_HD], s_cur[s]) for s, u in enumerate(us)]
        o_cs = [_dot(q_scr[rows_of(u), :] * solved[u][2], s_cur[s])
                + _dot(solved[u][1], jnp.where(head_rows, tile4(v_new[s]), 0.0)) for s, u in enumerate(us)]
        for s, u in enumerate(us):
            g_c = g_scr[rows_of(u), :]
            g_last = g_c[chunk - 1:chunk, :]
            k_dec = k_scr[rows_of(u), :] * jnp.exp(g_last - g_c)
            s_cur[s] = s_cur[s] * jnp.exp(g_last) + jnp.where(bd_state, _dot_tn(k_dec, v_new[s]), 0.0)
        for s, u in enumerate(us):
            r = slice(ci * chunk, (ci + 1) * chunk)
            ms = _dot_sel_r(o_cs[s] * o_cs[s], ones_bd) * (1.0 / GDN_DV)
            o_ref[s, r, :] = o_cs[s] * lax.rsqrt(ms + EPS) * nw_ref[...] * _silu(zg_ref[s, r, :])
    for s in seqs:
        s_scr[s] = s_cur[s]

    @pl.when(j == pl.num_programs(1) - 1)
    def _():
        for s in seqs:
            sfin_ref[s] = _dot_sel_r(s_scr[s], gather_ref[...])


def _gdn_consts(rows, chunk):
    stack = GDN_HEADS * chunk
    blk = lambda n, c: (np.arange(n)[:, None] // c) == (np.arange(n)[None, :] // c)
    ones_bd = blk(GDN_HD, GDN_DK)
    tri = blk(rows, chunk) & (np.arange(rows)[:, None] >= np.arange(rows)[None, :])
    src = np.arange(LANE)[:, None]
    ea = src == np.arange(GDN_HD)[None, :] // GDN_DK
    eb = src == GDN_HEADS + np.arange(GDN_HD)[None, :] // GDN_DK
    ec = src == np.arange(stack)[None, :] // chunk
    spread = np.arange(GDN_DV)[:, None] == np.arange(GDN_HD)[None, :] % GDN_DV
    return tuple(jnp.asarray(m, dtype=BF16) for m in (ones_bd, tri, ea, eb, ec, spread, spread.T))


def _gdn(zq, zg, zab, conv_w, alog_pad, dtb_pad, nw, buf0, s0, layer, bsz, seq):
    chunk = min(GDN_CHUNK, seq)
    rows = min(4 * chunk, seq)
    nj = seq // rows
    nseq = _seqs_per_step(bsz, nj, GDN_LONG_SEQS_PER_STEP)
    stack = GDN_HEADS * chunk
    consts = _gdn_consts(nseq * rows, chunk)
    row = lambda n: pl.BlockSpec((nseq, rows, n), lambda b, j: (b, j, 0))
    full = lambda a: pl.BlockSpec(a.shape, lambda b, j: (0,) * a.ndim)
    per_b = lambda *s: pl.BlockSpec((nseq,) + s, lambda b, j: (b, 0, 0))
    by_seq = lambda a: a.reshape(bsz, seq, a.shape[-1])
    o, nbuf, s_fin = pl.pallas_call(
        functools.partial(_gdn_body, nseq=nseq, rows=rows, chunk=chunk),
        grid=(bsz // nseq, nj),
        in_specs=[row(GDN_QKV), row(GDN_HD), row(LANE), full(conv_w), full(alog_pad), full(dtb_pad), full(nw),
                  per_b(SUBLANE, GDN_QKV),
                  pl.BlockSpec((None, nseq, GDN_HD, GDN_DV), lambda b, j: (layer, b, 0, 0))] +
                 [full(c) for c in consts],
        out_specs=[row(GDN_HD), per_b(SUBLANE, GDN_QKV), per_b(GDN_HD, GDN_DV)],
        out_shape=[jax.ShapeDtypeStruct((bsz, seq, GDN_HD), F32),
                   jax.ShapeDtypeStruct((bsz, SUBLANE, GDN_QKV), F32),
                   jax.ShapeDtypeStruct((bsz, GDN_HD, GDN_DV), F32)],
        scratch_shapes=[pltpu.VMEM((nseq, rows + SUBLANE, GDN_QKV), F32)] +
                       [pltpu.VMEM((nseq * rows, GDN_HD), F32)] * 5 +
                       [pltpu.VMEM((nseq * rows, stack), F32), pltpu.VMEM((nseq, GDN_HD, GDN_HD), F32)],
        compiler_params=_params("parallel", "arbitrary"),
        name="gdn",
    )(by_seq(zq), by_seq(zg), by_seq(zab), conv_w, alog_pad, dtb_pad, nw, buf0, s0, *consts)
    return o.reshape(bsz * seq, GDN_HD), nbuf, s_fin


def _outproj_body(x_ref, oret_ref, os5_ref, ogdn_ref, w1_ref, w2_ref, w3_ref, nw_ref, wr_ref, br_ref, tri_ref,
                  cnt0_ref, x1_ref, h2_ref, meta_ref, cnt_ref, carry_scr, *, tm):
    i = pl.program_id(0)

    @pl.when(i == 0)
    def _():
        carry_scr[...] = cnt0_ref[...]

    mix = _dot(oret_ref[...], w1_ref[...]) + _dot(os5_ref[...], w2_ref[...]) + _dot(ogdn_ref[...], w3_ref[...])
    x1 = x_ref[...] + mix
    x1_ref[...] = x1
    h2 = _rms(x1) * nw_ref[...]
    for s in range(D_MODEL // LANE):
        h2_ref[pl.ds(s, tm, stride=SUBLANE), :] = h2[:, s * LANE:(s + 1) * LANE]

    logits = _dot_nt(wr_ref[...], h2) + br_ref[...]
    row_i = lax.broadcasted_iota(I32, (LANE, tm), 0)
    row = row_i.astype(F32)
    neg = -jnp.inf
    big = float(LANE)
    g_log = jnp.where((row_i >= N_EXPERTS) & (row_i < N_EXPERTS + MOE_GROUPS), logits, neg)
    g_max = jnp.max(g_log, axis=0, keepdims=True)
    grp = jnp.min(jnp.where(g_log == g_max, row - N_EXPERTS, big), axis=0, keepdims=True)
    p_grp = 1.0 / jnp.sum(jnp.exp(g_log - g_max), axis=0, keepdims=True)
    in_grp = (row >= grp * EXPERTS_PER_GROUP) & (row < (grp + 1.0) * EXPERTS_PER_GROUP)
    e_log = jnp.where(in_grp, logits, neg)
    v1 = jnp.max(e_log, axis=0, keepdims=True)
    i1 = jnp.min(jnp.where(e_log == v1, row, big), axis=0, keepdims=True)
    e_log2 = jnp.where(row == i1, neg, e_log)
    v2 = jnp.max(e_log2, axis=0, keepdims=True)
    i2 = jnp.min(jnp.where(e_log2 == v2, row, big), axis=0, keepdims=True)
    e2 = jnp.exp(v2 - v1)
    gate1 = p_grp / (1.0 + e2)
    gate2 = p_grp * e2 / (1.0 + e2)

    oh1 = row == i1
    oh2 = row == i2
    picked = jnp.where(oh1 | oh2, 1.0, 0.0)
    seen = carry_scr[:, 0:1]
    before = jnp.dot(picked.astype(BF16), tri_ref[...], preferred_element_type=F32) + seen
    rank1 = jnp.sum(jnp.where(oh1, before, 0.0), axis=0, keepdims=True)
    rank2 = jnp.sum(jnp.where(oh2, before, 0.0), axis=0, keepdims=True)
    total = seen + jnp.sum(picked, axis=1, keepdims=True)
    carry_scr[...] = jnp.broadcast_to(total, carry_scr.shape)
    cnt_ref[...] = jnp.broadcast_to(total, cnt_ref.shape)
    meta_ref[...] = jnp.concatenate([i1, i2, gate1, gate2, rank1, rank2, jnp.zeros((2, tm), F32)], axis=0)


def _outproj(x, o_ret, o_s5, o_gdn, w1, w2, w3, nw, wr, br, cnt0):
    t = x.shape[0]
    tm = min(ROW_TILE, t)
    tri = jnp.asarray(np.arange(tm)[:, None] < np.arange(tm)[None, :], dtype=BF16)
    row = lambda n: pl.BlockSpec((tm, n), lambda i: (i, 0))
    full = lambda a: pl.BlockSpec(a.shape, lambda i: (0, 0))
    return pl.pallas_call(
        functools.partial(_outproj_body, tm=tm),
        grid=(t // tm,),
        in_specs=[row(D_MODEL), row(RET_VD), row(S5_CH), row(GDN_HD),
                  full(w1), full(w2), full(w3), full(nw), full(wr), full(br), full(tri), full(cnt0)],
        out_specs=[row(D_MODEL), pl.BlockSpec((tm * SUBLANE, LANE), lambda i: (i, 0)),
                   pl.BlockSpec((SUBLANE, tm), lambda i: (0, i)), pl.BlockSpec((LANE, LANE), lambda i: (0, 0))],
        out_shape=[jax.ShapeDtypeStruct((t, D_MODEL), F32),
                   jax.ShapeDtypeStruct((t * SUBLANE, LANE), F32),
                   jax.ShapeDtypeStruct((SUBLANE, t), F32),
                   jax.ShapeDtypeStruct((LANE, LANE), F32)],
        scratch_shapes=[pltpu.VMEM((LANE, LANE), F32)],
        compiler_params=_params("arbitrary"),
        name="outproj_router",
    )(x, o_ret, o_s5, o_gdn, w1, w2, w3, nw, wr, br, tri, cnt0)


def _token_rows(ref, idx):
    return ref.at[pl.ds(pl.multiple_of(idx * SUBLANE, SUBLANE), SUBLANE)]


ZERO_ROWS = 128


def _zero_segment(zero_scr, xb_ref, sem, start, length, wait):
    def piece(off, n):
        cp = pltpu.make_async_copy(zero_scr.at[pl.ds(0, n * SUBLANE)],
                                   xb_ref.at[pl.ds(pl.multiple_of(off * SUBLANE, SUBLANE), n * SUBLANE)], sem)
        cp.wait() if wait else cp.start()

    n_big = length // ZERO_ROWS

    def big(i, carry):
        piece(start + i * ZERO_ROWS, ZERO_ROWS)
        return carry

    lax.fori_loop(0, n_big, big, 0)
    off = start + n_big * ZERO_ROWS
    rem = length - n_big * ZERO_ROWS
    bit = ZERO_ROWS // 2
    while bit >= 1:
        has = (rem & bit) != 0
        pl.when(has)(functools.partial(piece, off, bit))
        off = off + jnp.where(has, bit, 0)
        bit //= 2


def _scatter_body(d1_ref, d2_ref, zs_ref, zl_ref, src_a_ref, src_b_ref, xb_ref,
                  zero_scr, sem, zsem, *, tm, tiles_a):
    i = pl.program_id(0)
    base = i * tm

    @pl.when(i == 0)
    def _():
        zero_scr[...] = jnp.zeros_like(zero_scr)
        for wait in (False, True):
            lax.fori_loop(0, N_EXPERTS + 1,
                          lambda s, c, wait=wait: (_zero_segment(zero_scr, xb_ref, zsem, zs_ref[s], zl_ref[s], wait),
                                                   c)[1], 0)

    def scatter_tile(src_ref):
        def copy(t, d):
            return pltpu.make_async_copy(_token_rows(src_ref, t), _token_rows(xb_ref, d), sem)

        def issue(t, carry):
            copy(t, d1_ref[base + t]).start(priority=0)
            copy(t, d2_ref[base + t]).start(priority=1)
            return carry

        lax.fori_loop(0, tm, issue, 0, unroll=ISSUE_UNROLL)
        for _ in range(2):
            pltpu.make_async_copy(src_ref, src_ref, sem).wait()

    pl.when(i < tiles_a)(functools.partial(scatter_tile, src_a_ref))
    pl.when(i >= tiles_a)(functools.partial(scatter_tile, src_b_ref))


def _scatter(route, zero_start, zero_len, h2_a, h2_b, n_rows):
    tm = SCATTER_TILE
    tiles_a, tiles_b = h2_a.shape[0] // (tm * SUBLANE), h2_b.shape[0] // (tm * SUBLANE)
    assert tiles_a * tm * SUBLANE == h2_a.shape[0] and tiles_b * tm * SUBLANE == h2_b.shape[0]
    return pl.pallas_call(
        functools.partial(_scatter_body, tm=tm, tiles_a=tiles_a),
        grid_spec=pltpu.PrefetchScalarGridSpec(
            num_scalar_prefetch=4, grid=(tiles_a + tiles_b,),
            in_specs=[pl.BlockSpec((tm * SUBLANE, LANE), lambda i, *_: (jnp.minimum(i, tiles_a - 1), 0)),
                      pl.BlockSpec((tm * SUBLANE, LANE), lambda i, *_: (jnp.maximum(i - tiles_a, 0), 0))],
            out_specs=pl.BlockSpec(memory_space=pl.ANY),
            scratch_shapes=[pltpu.VMEM((ZERO_ROWS * SUBLANE, LANE), F32), pltpu.SemaphoreType.DMA(()),
                            pltpu.SemaphoreType.DMA(())]),
        out_shape=jax.ShapeDtypeStruct((n_rows * SUBLANE, LANE), F32),
        compiler_params=_params("arbitrary"),
        name="moe_scatter",
    )(*route, zero_start, zero_len, h2_a, h2_b)


def _experts_body(be_ref, nb_ref, xb_ref, w1_ref, w3_ref, w2_ref, yb_ref, w1_scr, w3_scr, w2_scr, *, blk):
    i = pl.program_id(0)
    live = i < nb_ref[0]

    @pl.when(live & ((i == 0) | (be_ref[i] != be_ref[jnp.maximum(i - 1, 0)])))
    def _():
        w1_scr[...] = w1_ref[...].astype(BF16)
        w3_scr[...] = w3_ref[...].astype(BF16)
        w2_scr[...] = w2_ref[...].astype(BF16)

    @pl.when(live)
    def _():
        x = jnp.concatenate([xb_ref[pl.ds(s, blk, stride=SUBLANE), :] for s in range(D_MODEL // LANE)],
                            axis=1).astype(BF16)
        hid = _silu(jnp.dot(x, w1_scr[...], preferred_element_type=F32)) * \
            jnp.dot(x, w3_scr[...], preferred_element_type=F32)
        y = _dot(hid, w2_scr[...])
        for s in range(D_MODEL // LANE):
            yb_ref[pl.ds(s, blk, stride=SUBLANE), :] = y[:, s * LANE:(s + 1) * LANE]

    @pl.when(jnp.logical_not(live))
    def _():
        yb_ref[...] = jnp.zeros_like(yb_ref)


def _experts(block_e, nb_used, xb, layer, w1, w3, w2, n_blocks):
    blk = MOE_BLK
    live = lambda i, nb: jnp.minimum(i, nb[0] - 1)
    tile_in = pl.BlockSpec((blk * SUBLANE, LANE), lambda i, be, nb: (live(i, nb), 0))
    tile_out = pl.BlockSpec((blk * SUBLANE, LANE), lambda i, be, nb: (i, 0))
    wspec = lambda a: pl.BlockSpec((None, None) + a.shape[2:],
                                   lambda i, be, nb: (layer, be[live(i, nb)], 0, 0))
    return pl.pallas_call(
        functools.partial(_experts_body, blk=blk),
        grid_spec=pltpu.PrefetchScalarGridSpec(
            num_scalar_prefetch=2, grid=(n_blocks,),
            in_specs=[tile_in, wspec(w1), wspec(w3), wspec(w2)], out_specs=tile_out,
            scratch_shapes=[pltpu.VMEM(w1.shape[2:], BF16), pltpu.VMEM(w3.shape[2:], BF16),
                            pltpu.VMEM(w2.shape[2:], BF16)]),
        out_shape=jax.ShapeDtypeStruct(xb.shape, F32),
        compiler_params=_params("arbitrary"),
        name="moe_experts",
    )(block_e, nb_used, xb, w1, w3, w2)


def _combine_body(*refs, tm, tiles_a, final_norm, n_proj):
    d1_ref, d2_ref, x1_a_ref, x1_b_ref, meta_a_ref, meta_b_ref, yb_ref, nw_ref = refs[:8]
    w_refs = refs[8:8 + n_proj]
    out_a_ref, out_b_ref = refs[8 + n_proj:10 + n_proj]
    za_refs = refs[10 + n_proj:10 + 2 * n_proj]
    zb_refs = refs[10 + 2 * n_proj:10 + 3 * n_proj]
    buf_ref, sem = refs[-2:]
    i = pl.program_id(0)
    phase = i % 2

    def gather(tile, ph):
        base = tile * tm

        def issue(t, carry):
            for k, d_ref in enumerate((d1_ref, d2_ref)):
                pltpu.make_async_copy(_token_rows(yb_ref, d_ref[base + t]),
                                      _token_rows(buf_ref.at[ph, k], t), sem.at[ph]).start(priority=k)
            return carry

        lax.fori_loop(0, tm, issue, 0, unroll=ISSUE_UNROLL)

    @pl.when(i == 0)
    def _():
        gather(0, 0)

    @pl.when(i + 1 < pl.num_programs(0))
    def _():
        gather(i + 1, 1 - phase)

    for k in range(2):
        pltpu.make_async_copy(buf_ref.at[phase, k], buf_ref.at[phase, k], sem.at[phase]).wait()

    def rows_of(slot):
        return jnp.concatenate([buf_ref[phase, slot, pl.ds(s, tm, stride=SUBLANE), :]
                                for s in range(D_MODEL // LANE)], axis=1)

    is_a = i < tiles_a
    meta = jnp.where(is_a, meta_a_ref[...], meta_b_ref[...])
    on_diag = lax.broadcasted_iota(I32, (tm, tm), 0) == lax.broadcasted_iota(I32, (tm, tm), 1)
    as_col = lambda r: jnp.sum(jnp.where(on_diag, r, 0.0), axis=1, keepdims=True)
    x2 = jnp.where(is_a, x1_a_ref[...], x1_b_ref[...]) + (as_col(meta[2:3, :]) * rows_of(0)
                                                         + as_col(meta[3:4, :]) * rows_of(1))
    normed = _rms(x2) * nw_ref[...]

    def emit(out_ref, z_refs):
        out_ref[...] = normed if final_norm else x2
        h = normed.astype(BF16)
        for w_ref, z_ref in zip(w_refs, z_refs):
            z_ref[...] = jnp.dot(h, w_ref[...], preferred_element_type=F32)

    pl.when(is_a)(functools.partial(emit, out_a_ref, za_refs))
    pl.when(jnp.logical_not(is_a))(functools.partial(emit, out_b_ref, zb_refs))


def _combine(route, x1_a, x1_b, meta_a, meta_b, yb, norm_w, final_norm, next_in_ws):
    tm = GATHER_TILE if next_in_ws else FINAL_GATHER_TILE
    tiles_a, tiles_b = x1_a.shape[0] // tm, x1_b.shape[0] // tm
    assert tiles_a * tm == x1_a.shape[0] and tiles_b * tm == x1_b.shape[0]
    assert final_norm == (not next_in_ws)
    of_a = lambda n: pl.BlockSpec((tm, n), lambda i, *_: (jnp.minimum(i, tiles_a - 1), 0))
    of_b = lambda n: pl.BlockSpec((tm, n), lambda i, *_: (jnp.maximum(i - tiles_a, 0), 0))
    widths = [w.shape[1] for w in next_in_ws]
    outs = pl.pallas_call(
        functools.partial(_combine_body, tm=tm, tiles_a=tiles_a, final_norm=final_norm, n_proj=len(widths)),
        grid_spec=pltpu.PrefetchScalarGridSpec(
            num_scalar_prefetch=2, grid=(tiles_a + tiles_b,),
            in_specs=[of_a(D_MODEL), of_b(D_MODEL),
                      pl.BlockSpec((SUBLANE, tm), lambda i, *_: (0, jnp.minimum(i, tiles_a - 1))),
                      pl.BlockSpec((SUBLANE, tm), lambda i, *_: (0, jnp.maximum(i - tiles_a, 0))),
                      pl.BlockSpec(memory_space=pl.ANY),
                      pl.BlockSpec((1, D_MODEL), lambda i, *_: (0, 0))] +
                     [pl.BlockSpec(w.shape, lambda i, *_: (0, 0)) for w in next_in_ws],
            out_specs=[of_a(D_MODEL), of_b(D_MODEL)] + [of_a(n) for n in widths] + [of_b(n) for n in widths],
            scratch_shapes=[pltpu.VMEM((2, 2, tm * SUBLANE, LANE), F32), pltpu.SemaphoreType.DMA((2,))]),
        out_shape=[jax.ShapeDtypeStruct(x1_a.shape, F32), jax.ShapeDtypeStruct(x1_b.shape, F32)] +
                  [jax.ShapeDtypeStruct((x.shape[0], n), F32) for x in (x1_a, x1_b) for n in widths],
        compiler_params=_params("arbitrary"),
        name="moe_combine",
    )(*route, x1_a, x1_b, meta_a, meta_b, yb, norm_w, *next_in_ws)
    n = len(widths)
    return (outs[0], outs[1]), (outs[2:2 + n], outs[2 + n:2 + 2 * n])


def _moe(stream_a, stream_b, counts, layer, w1, w3, w2, norm_w, final_norm, next_in_ws):
    (x1_a, h2_a, meta_a), (x1_b, h2_b, meta_b) = stream_a, stream_b
    t = x1_a.shape[0] + x1_b.shape[0]
    n_blocks = (2 * t + N_EXPERTS * (MOE_BLK - 1)) // MOE_BLK
    col = lambda c: jnp.concatenate([meta_a[c], meta_b[c]]).astype(I32)
    e1, e2, r1, r2 = col(0), col(1), col(4), col(5)
    cnt = counts[:N_EXPERTS, 0].astype(I32)
    padded = (cnt + MOE_BLK - 1) // MOE_BLK * MOE_BLK
    pad_end = jnp.cumsum(padded)
    pad_start = pad_end - padded
    slot = lambda e, r: r + jnp.sum(jnp.where(e[:, None] == jnp.arange(N_EXPERTS, dtype=I32)[None, :],
                                              pad_start[None, :], 0), axis=1)
    route = (slot(e1, r1), slot(e2, r2))
    nb_used = (pad_end[-1:] // MOE_BLK).astype(I32)
    blk_start = jnp.arange(n_blocks, dtype=I32) * MOE_BLK
    block_e = jnp.minimum(jnp.sum(blk_start[:, None] >= pad_end[None, :], axis=1), N_EXPERTS - 1).astype(I32)
    n_rows = n_blocks * MOE_BLK
    zero_start = jnp.concatenate([pad_start + cnt, pad_end[-1:]]).astype(I32)
    zero_len = jnp.concatenate([padded - cnt, n_rows - pad_end[-1:]]).astype(I32)
    xb = _scatter(route, zero_start, zero_len, h2_a, h2_b, n_rows)
    yb = _experts(block_e, nb_used, xb, layer, w1, w3, w2, n_blocks)
    return _combine(route, x1_a, x1_b, meta_a, meta_b, yb, norm_w, final_norm, next_in_ws)


def _rope_tables(pos):
    half = RET_DK // 2
    inv = ROPE_BASE ** (-jnp.arange(half, dtype=F32) / half)
    ang = pos[:, None] * inv[None, :]
    cos, sin = jnp.cos(ang), jnp.sin(ang)
    cos_t = jnp.tile(jnp.concatenate([cos, cos], axis=1), (1, RET_HEADS))
    sin_t = jnp.tile(jnp.concatenate([-sin, sin], axis=1), (1, RET_HEADS))
    return cos_t, sin_t


def _block_diag(blocks):
    g, r, c = blocks.shape
    eye = jnp.eye(g, dtype=bool)
    return jnp.where(eye[:, None, :, None], blocks[:, :, None, :], 0).reshape(g * r, g * c)


def _layer_weights(l, w_in, s5, s5_b_bar, s5_c_re, s5_c_im, s5_d, s5_glu_w, s5_glu_b, gdn_conv_w, gdn_a_log,
                   gdn_dt_bias, gdn_norm_w, w_out, router_group_w, router_group_b, router_expert_w,
                   router_expert_b):
    wi = w_in[l]
    g0 = RET_IN + S5_IN
    w_ab = jnp.pad(wi[:, g0 + GDN_QKV + GDN_HD:], ((0, 0), (0, LANE - 2 * GDN_HEADS)))
    in_ws = [wi[:, :RET_IN], wi[:, g0:g0 + GDN_QKV], wi[:, RET_IN:g0], wi[:, g0 + GDN_QKV:g0 + GDN_QKV + GDN_HD],
             w_ab]
    in_ws = [w.astype(BF16) for w in in_ws]
    ab_re, ab_im, bb_re, bb_im = s5
    n = S5_LANES
    sl = slice(l * n, (l + 1) * n)
    ab = jnp.concatenate([ab_re[sl].reshape(1, n), ab_im[sl].reshape(1, n)], axis=0)
    to_bd = lambda m: _block_diag(jnp.swapaxes(m[sl].reshape(S5_GROUPS, S5_STATE, S5_GROUP_CH), 1, 2))
    wb = jnp.concatenate([to_bd(bb_re), to_bd(bb_im)], axis=1).astype(BF16)
    wc_re = _block_diag(jnp.swapaxes(s5_c_re[l], 1, 2)).astype(BF16)
    wc_im = _block_diag(jnp.swapaxes(s5_c_im[l], 1, 2)).astype(BF16)
    s5_ws = (ab, wb, wc_re, wc_im, s5_d[l].reshape(1, S5_CH), s5_glu_w[l].astype(BF16),
             s5_glu_b[l].reshape(1, S5_CH))
    pad4 = lambda v: jnp.pad(v.reshape(1, GDN_HEADS), ((0, 0), (0, LANE - GDN_HEADS)))
    gdn_ws = (gdn_conv_w[l], pad4(gdn_a_log[l]), pad4(gdn_dt_bias[l]),
              jnp.tile(gdn_norm_w[l], GDN_HEADS).reshape(1, GDN_HD))
    wo = w_out[l].astype(BF16)
    out_ws = (wo[:RET_VD], wo[RET_VD:RET_VD + S5_CH], wo[RET_VD + S5_CH:])
    wr = jnp.pad(jnp.concatenate([router_expert_w[l], router_group_w[l]], axis=1).T,
                 ((0, LANE - N_EXPERTS - MOE_GROUPS), (0, 0))).astype(BF16)
    br = jnp.pad(jnp.concatenate([router_expert_b[l], router_group_b[l]]),
                 (0, LANE - N_EXPERTS - MOE_GROUPS)).reshape(LANE, 1)
    return in_ws, s5_ws, gdn_ws, out_ws, (wr, br)


def _mix_and_route(l, x, projected, bsz, seq, rope, states, layer_w, norm_mix, norm_ffn, cnt0):
    t = bsz * seq
    ret_s, s5_re, s5_im, gdn_s, gdn_buf = states
    in_ws, s5_ws, gdn_ws, out_ws, (wr, br) = layer_w
    zr, zq, zs, zg, zab = projected or _inproj(x, norm_mix[l].reshape(1, D_MODEL), in_ws)
    o_ret, ret_fin = _retention(zr, *rope, ret_s.reshape(DEPTH, bsz, RET_QD, RET_DV), l, bsz, seq)
    u_tm = jnp.swapaxes(zs.reshape(bsz, seq, S5_CH), 0, 1)
    o_s5_tm, re_fin, im_fin = _s5(u_tm, s5_re[l].reshape(bsz, S5_LANES), s5_im[l].reshape(bsz, S5_LANES), *s5_ws)
    o_s5 = jnp.swapaxes(o_s5_tm, 0, 1).reshape(t, S5_CH)
    buf0 = jnp.pad(gdn_buf[l], ((0, 0), (SUBLANE - (GDN_CONV - 1), 0), (0, 0)))
    o_gdn, nbuf, gdn_fin = _gdn(zq, zg, zab, *gdn_ws, buf0, gdn_s.reshape(DEPTH, bsz, GDN_HD, GDN_DV), l, bsz, seq)
    x1, h2_tiles, meta, counts = _outproj(x, o_ret, o_s5, o_gdn, *out_ws, norm_ffn[l].reshape(1, D_MODEL), wr, br,
                                          cnt0)
    new_states = (ret_fin.reshape(bsz, RET_HEADS, RET_DK, RET_DV), re_fin.reshape(bsz, S5_GROUPS, S5_STATE),
                  im_fin.reshape(bsz, S5_GROUPS, S5_STATE), gdn_fin.reshape(bsz, GDN_HEADS, GDN_DK, GDN_DV),
                  nbuf[:, SUBLANE - (GDN_CONV - 1):, :])
    return (x1, h2_tiles, meta), counts, new_states


def _trunks(xs, positions, states, layer_ws, norm_mix, norm_ffn, experts, norm_final):
    shapes = [x.shape[:2] for x in xs]
    ropes = [_rope_tables(pos) for pos in positions]
    xs = [x.reshape(b * s, D_MODEL) for x, (b, s) in zip(xs, shapes)]
    outs = [[[] for _ in range(5)] for _ in xs]
    projected = [None for _ in xs]
    for l in range(DEPTH):
        counts = jnp.zeros((LANE, LANE), F32)
        routed = []
        for i, x in enumerate(xs):
            stream, counts, new_states = _mix_and_route(l, x, projected[i], *shapes[i], ropes[i], states[i],
                                                        layer_ws[l], norm_mix, norm_ffn, counts)
            routed.append(stream)
            for lst, s in zip(outs[i], new_states):
                lst.append(s)
        last = l == DEPTH - 1
        norm_w = norm_final if last else norm_mix[l + 1]
        xs, projected = _moe(*routed, counts, l, *experts, norm_w.reshape(1, D_MODEL), last,
                             [] if last else layer_ws[l + 1][0])
    ys = [x.reshape(b, s, D_MODEL) for x, (b, s) in zip(xs, shapes)]
    return ys, [tuple(jnp.stack(o) for o in per_stream) for per_stream in outs]


def kernel(x_prompt, x_sample, state_ret, state_s5_re, state_s5_im, state_gdn, state_gdn_conv, norm_mix, w_in, s5_a_re, s5_a_im, s5_log_dt, s5_b_re, s5_b_im, s5_c_re, s5_c_im, s5_d, s5_glu_w, s5_glu_b, gdn_conv_w, gdn_a_log, gdn_dt_bias, gdn_norm_w, w_out, norm_ffn, router_group_w, router_group_b, router_expert_w, router_expert_b, expert_w1, expert_w3, expert_w2, norm_final):
    n = DEPTH * S5_LANES
    col = lambda a: a.reshape(n, 1)
    log_dt = jnp.broadcast_to(s5_log_dt[:, :, None], (DEPTH, S5_GROUPS, S5_STATE))
    s5 = _s5_prep(col(s5_a_re), col(s5_a_im), col(log_dt), s5_b_re.reshape(n, S5_GROUP_CH),
                  s5_b_im.reshape(n, S5_GROUP_CH))
    layer_ws = [_layer_weights(l, w_in, s5, None, s5_c_re, s5_c_im, s5_d, s5_glu_w, s5_glu_b, gdn_conv_w,
                               gdn_a_log, gdn_dt_bias, gdn_norm_w, w_out, router_group_w, router_group_b,
                               router_expert_w, router_expert_b) for l in range(DEPTH)]
    experts = (expert_w1, expert_w3, expert_w2)
    bp, lp, _ = x_prompt.shape
    zero_states = (jnp.zeros((DEPTH, bp, RET_HEADS, RET_DK, RET_DV), F32),
                   jnp.zeros((DEPTH, bp, S5_GROUPS, S5_STATE), F32),
                   jnp.zeros((DEPTH, bp, S5_GROUPS, S5_STATE), F32),
                   jnp.zeros((DEPTH, bp, GDN_HEADS, GDN_DK, GDN_DV), F32),
                   jnp.zeros((DEPTH, bp, GDN_CONV - 1, GDN_QKV), F32))
    sample_states = (state_ret, state_s5_re, state_s5_im, state_gdn, state_gdn_conv)
    positions = (jnp.arange(lp, dtype=F32), PAST_LEN + jnp.arange(x_sample.shape[1], dtype=F32))
    (y_p, y_s), (p_states, s_states) = _trunks((x_prompt, x_sample), positions, (zero_states, sample_states),
                                               layer_ws, norm_mix, norm_ffn, experts, norm_final)
    return (y_p, y_s) + p_states + s_states
```

```python
import functools
import math

import jax
import jax.numpy as jnp
import numpy as np
from jax import lax
from jax.experimental import pallas as pl
from jax.experimental.pallas import tpu as pltpu

F32 = jnp.float32
BF16 = jnp.bfloat16
I32 = jnp.int32

D_MODEL = 1024
DEPTH = 2
RET_HEADS, RET_DK, RET_DV = 4, 64, 128
S5_GROUPS, S5_GROUP_CH, S5_STATE = 16, 16, 64
S5_CH = S5_GROUPS * S5_GROUP_CH
S5_LANES = S5_GROUPS * S5_STATE
GDN_HEADS, GDN_DK, GDN_DV, GDN_CONV = 4, 64, 64, 4
GDN_HD = GDN_HEADS * GDN_DK
GDN_QKV = 3 * GDN_HD
RET_QD = RET_HEADS * RET_DK
RET_VD = RET_HEADS * RET_DV
RET_IN = 2 * RET_QD + 2 * RET_VD
S5_IN = S5_CH
GDN_IN = GDN_QKV + GDN_HD + 2 * GDN_HEADS
GDN_CHUNK = 64
RET_CHUNK = 64
MOE_GROUPS, EXPERTS_PER_GROUP = 4, 8
N_EXPERTS = MOE_GROUPS * EXPERTS_PER_GROUP
D_EXPERT = 512
ROPE_BASE = 10000.0
EPS = 1e-6
PAST_LEN = 16384

LANE = 128
SUBLANE = 8
ROW_TILE = 512
MOE_BLK = 512
SCATTER_TILE = 512
GATHER_TILE = 256
S5_STEPS = 128
ISSUE_UNROLL = 8
SEQS_PER_STEP = 16
RET_LONG_SEQS_PER_STEP = 4
GDN_LONG_SEQS_PER_STEP = 2
FINAL_GATHER_TILE = 512
VMEM_LIMIT = 56 * 1024 * 1024


def _params(*sem):
    return pltpu.CompilerParams(dimension_semantics=sem, vmem_limit_bytes=VMEM_LIMIT)


def _dot(a, b):
    return jnp.dot(a.astype(BF16), b.astype(BF16), preferred_element_type=F32)


def _dot_nt(a, b):
    return lax.dot_general(a.astype(BF16), b.astype(BF16), (((1,), (1,)), ((), ())),
                           preferred_element_type=F32)


def _dot_tn(a, b):
    return lax.dot_general(a.astype(BF16), b.astype(BF16), (((0,), (0,)), ((), ())),
                           preferred_element_type=F32)


def _dot_hi(a, b):
    a1 = a.astype(BF16)
    a2 = (a - a1.astype(F32)).astype(BF16)
    b1 = b.astype(BF16)
    b2 = (b - b1.astype(F32)).astype(BF16)
    d = lambda x, y: jnp.dot(x, y, preferred_element_type=F32)
    return d(a1, b1) + (d(a1, b2) + d(a2, b1))


def _split3(x):
    p1 = x.astype(BF16)
    r1 = x - p1.astype(F32)
    p2 = r1.astype(BF16)
    p3 = (r1 - p2.astype(F32)).astype(BF16)
    return p1, p2, p3


def _dot_sel_l(sel, x):
    p1, p2, p3 = _split3(x)
    d = lambda p: jnp.dot(sel, p, preferred_element_type=F32)
    return d(p1) + d(p2) + d(p3)


def _dot_sel_r(x, sel):
    p1, p2, p3 = _split3(x)
    d = lambda p: jnp.dot(p, sel, preferred_element_type=F32)
    return d(p1) + d(p2) + d(p3)


def _rms(x):
    return x * lax.rsqrt(jnp.mean(x * x, axis=-1, keepdims=True) + EPS)


def _silu(x):
    return x * jax.nn.sigmoid(x)


def _inproj_body(x_ref, nw_ref, wr_ref, wq_ref, ws_ref, wg_ref, wab_ref,
                 zr_ref, zq_ref, zs_ref, zg_ref, zab_ref):
    h = (_rms(x_ref[...]) * nw_ref[...]).astype(BF16)
    for w_ref, z_ref in ((wr_ref, zr_ref), (wq_ref, zq_ref), (ws_ref, zs_ref),
                         (wg_ref, zg_ref), (wab_ref, zab_ref)):
        z_ref[...] = jnp.dot(h, w_ref[...], preferred_element_type=F32)


def _inproj(x, nw, ws):
    t = x.shape[0]
    tm = min(ROW_TILE, t)
    widths = [w.shape[1] for w in ws]
    row = lambda n: pl.BlockSpec((tm, n), lambda i: (i, 0))
    full = lambda a: pl.BlockSpec(a.shape, lambda i: (0, 0))
    return pl.pallas_call(
        _inproj_body,
        grid=(t // tm,),
        in_specs=[row(D_MODEL), full(nw)] + [full(w) for w in ws],
        out_specs=[row(n) for n in widths],
        out_shape=[jax.ShapeDtypeStruct((t, n), F32) for n in widths],
        compiler_params=_params("parallel"),
        name="inproj",
    )(x, nw, *ws)


def _mix_dtype(chunk):
    return BF16 if chunk % (2 * SUBLANE) == 0 else F32


def _seqs_per_step(bsz, blocks_per_seq, long_seqs):
    want = SEQS_PER_STEP if blocks_per_seq == 1 else long_seqs
    return want if bsz % want == 0 else 1


def _ret_log_gamma(h):
    return math.log(1.0 - 2.0 ** (-5.0 - h))


def _ret_body(zr_ref, cos_ref, sin_ref, s0_ref, o_ref, sfin_ref, s_scr, *, nseq, rows, chunk):
    j = pl.program_id(1)
    stack = RET_HEADS * chunk

    @pl.when(j == 0)
    def _():
        s_scr[...] = s0_ref[...]

    lane = lax.broadcasted_iota(I32, (rows, RET_QD), 1)
    first_half = (lane % RET_DK) < (RET_DK // 2)

    def rotary(x):
        swapped = jnp.where(first_half, pltpu.roll(x, RET_QD - RET_DK // 2, 1),
                            pltpu.roll(x, RET_DK // 2, 1))
        return x * cos_ref[...] + swapped * sin_ref[...]

    q = [rotary(zr_ref[s, :, 0:RET_QD]) for s in range(nseq)]
    k = [rotary(zr_ref[s, :, RET_QD:2 * RET_QD]) * (RET_DK ** -0.5) for s in range(nseq)]

    def by_head(idx, fn):
        out = jnp.zeros(idx.shape, F32)
        for h in range(RET_HEADS):
            out = jnp.where(idx == h, fn(h), out)
        return out

    lg_lane = by_head(lax.broadcasted_iota(I32, (chunk, RET_QD), 1) // RET_DK, _ret_log_gamma)
    pos = lax.broadcasted_iota(I32, (chunk, RET_QD), 0).astype(F32)
    q_scale = jnp.exp((pos + 1.0) * lg_lane)
    k_scale = jnp.exp((chunk - 1.0 - pos) * lg_lane)
    st_row = lax.broadcasted_iota(I32, (stack, stack), 0)
    st_col = lax.broadcasted_iota(I32, (stack, stack), 1)
    causal = ((st_row // chunk) == (st_col // chunk)) & (st_row >= st_col)
    lg_stack = by_head(st_row // chunk, _ret_log_gamma)
    decay = jnp.where(causal, jnp.exp(jnp.where(causal, (st_row - st_col).astype(F32), 0.0) * lg_stack), 0.0)
    q_rows = (lax.broadcasted_iota(I32, (stack, RET_QD), 0) // chunk) == \
             (lax.broadcasted_iota(I32, (stack, RET_QD), 1) // RET_DK)
    v_rows = (lax.broadcasted_iota(I32, (stack, RET_VD), 0) // chunk) == \
             (lax.broadcasted_iota(I32, (stack, RET_VD), 1) // RET_DV)
    s_row_head = lax.broadcasted_iota(I32, (RET_QD, RET_DV), 0) // RET_DK
    s_decay = by_head(s_row_head, lambda h: math.exp(chunk * _ret_log_gamma(h)))
    kv_diag = (lax.broadcasted_iota(I32, (RET_QD, RET_VD), 0) // RET_DK) == \
              (lax.broadcasted_iota(I32, (RET_QD, RET_VD), 1) // RET_DV)

    def tile4(a):
        return jnp.concatenate([a] * RET_HEADS, axis=0)

    def collapse(a):
        out = a[0:chunk]
        for h in range(1, RET_HEADS):
            out = out + a[h * chunk:(h + 1) * chunk]
        return out

    seqs = range(nseq)
    n_chunks = rows // chunk
    state = [s_scr[s] for s in seqs]
    for ci in range(n_chunks):
        r = slice(ci * chunk, (ci + 1) * chunk)
        v_cs = [zr_ref[s, r, 2 * RET_QD:2 * RET_QD + RET_VD] for s in seqs]
        scores = [_dot_nt(jnp.where(q_rows, tile4(q[s][r]), 0.0), tile4(k[s][r])) * decay for s in seqs]
        o_intra = [collapse(_dot(scores[s], jnp.where(v_rows, tile4(v_cs[s]), 0.0))) for s in seqs]
        s_full = [jnp.concatenate([jnp.where(s_row_head == h, state[s], 0.0) for h in range(RET_HEADS)], axis=1)
                  for s in seqs]
        o_cs = [o_intra[s] + _dot(q[s][r] * q_scale, s_full[s]) for s in seqs]
        for s in seqs:
            gate = zr_ref[s, r, 2 * RET_QD + RET_VD:2 * RET_QD + 2 * RET_VD]
            for h in range(RET_HEADS):
                sl = slice(h * RET_DV, (h + 1) * RET_DV)
                o_ref[s, r, sl] = (_rms(o_cs[s][:, sl]) * _silu(gate[:, sl])).astype(o_ref.dtype)
        for s in seqs:
            kv = jnp.where(kv_diag, _dot_tn(k[s][r] * k_scale, v_cs[s]), 0.0)
            kv_own = kv[:, 0:RET_DV]
            for h in range(1, RET_HEADS):
                kv_own = kv_own + kv[:, h * RET_DV:(h + 1) * RET_DV]
            state[s] = state[s] * s_decay + kv_own
    for s in seqs:
        s_scr[s] = state[s]

    @pl.when(j == pl.num_programs(1) - 1)
    def _():
        for s in seqs:
            sfin_ref[s] = state[s]


def _retention(zr, cos, sin, s0, layer, bsz, seq):
    chunk = min(RET_CHUNK, seq)
    rows = min(4 * chunk, seq)
    nj = seq // rows
    nseq = _seqs_per_step(bsz, nj, RET_LONG_SEQS_PER_STEP)
    o, s_fin = pl.pallas_call(
        functools.partial(_ret_body, nseq=nseq, rows=rows, chunk=chunk),
        grid=(bsz // nseq, nj),
        in_specs=[pl.BlockSpec((nseq, rows, RET_IN), lambda b, j: (b, j, 0)),
                  pl.BlockSpec((rows, RET_QD), lambda b, j: (j, 0)),
                  pl.BlockSpec((rows, RET_QD), lambda b, j: (j, 0)),
                  pl.BlockSpec((None, nseq, RET_QD, RET_DV), lambda b, j: (layer, b, 0, 0))],
        out_specs=[pl.BlockSpec((nseq, rows, RET_VD), lambda b, j: (b, j, 0)),
                   pl.BlockSpec((nseq, RET_QD, RET_DV), lambda b, j: (b, 0, 0))],
        out_shape=[jax.ShapeDtypeStruct((bsz, seq, RET_VD), _mix_dtype(chunk)),
                   jax.ShapeDtypeStruct((bsz, RET_QD, RET_DV), F32)],
        scratch_shapes=[pltpu.VMEM((nseq, RET_QD, RET_DV), F32)],
        compiler_params=_params("parallel", "arbitrary"),
        name="retention",
    )(zr.reshape(bsz, seq, RET_IN), cos, sin, s0)
    return o.reshape(bsz * seq, RET_VD), s_fin


def _s5_prep_body(are_ref, aim_ref, ldt_ref, bre_ref, bim_ref, abre_ref, abim_ref, bbre_ref, bbim_ref):
    lam_re, lam_im = are_ref[...], aim_ref[...]
    dt = jnp.exp(ldt_ref[...])
    mag = jnp.exp(lam_re * dt)
    ab_re = mag * jnp.cos(lam_im * dt)
    ab_im = mag * jnp.sin(lam_im * dt)
    den = lam_re * lam_re + lam_im * lam_im
    f_re = ((ab_re - 1.0) * lam_re + ab_im * lam_im) / den
    f_im = (ab_im * lam_re - (ab_re - 1.0) * lam_im) / den
    abre_ref[...] = ab_re
    abim_ref[...] = ab_im
    bbre_ref[...] = f_re * bre_ref[...] - f_im * bim_ref[...]
    bbim_ref[...] = f_re * bim_ref[...] + f_im * bre_ref[...]


def _s5_prep(a_re, a_im, log_dt, b_re, b_im):
    n = a_re.shape[0]
    col = jax.ShapeDtypeStruct((n, 1), F32)
    mat = jax.ShapeDtypeStruct((n, S5_GROUP_CH), F32)
    return pl.pallas_call(_s5_prep_body, out_shape=[col, col, mat, mat], name="s5_prep")(
        a_re, a_im, log_dt, b_re, b_im)


def _gelu_tanh(x):
    return x * (0.5 * (1.0 + jnp.tanh(math.sqrt(2.0 / math.pi) * (x + 0.044715 * (x * x * x)))))


def _s5_body(u_ref, h0re_ref, h0im_ref, ab_ref, wb_ref, wcre_ref, wcim_ref, d_ref, gw_ref, gb_ref,
             o_ref, hre_ref, him_ref, bu_scr, st_scr, *, steps):
    j = pl.program_id(1)

    @pl.when(j == 0)
    def _():
        st_scr[0] = h0re_ref[...]
        st_scr[1] = h0im_ref[...]

    rows = steps * SUBLANE
    u = u_ref[...].reshape(rows, S5_CH)
    bu_scr[...] = _dot(u, wb_ref[...])
    a_re = jnp.broadcast_to(ab_ref[0:1, :], (SUBLANE, S5_LANES))
    a_im = jnp.broadcast_to(ab_ref[1:2, :], (SUBLANE, S5_LANES))

    def step(t, carry):
        h_re, h_im = carry
        r = pl.ds(pl.multiple_of(t * SUBLANE, SUBLANE), SUBLANE)
        n_re = a_re * h_re - a_im * h_im + bu_scr[r, 0:S5_LANES]
        n_im = a_re * h_im + a_im * h_re + bu_scr[r, S5_LANES:2 * S5_LANES]
        bu_scr[r, 0:S5_LANES] = n_re
        bu_scr[r, S5_LANES:2 * S5_LANES] = n_im
        return n_re, n_im

    h_re, h_im = lax.fori_loop(0, steps, step, (st_scr[0], st_scr[1]))
    st_scr[0] = h_re
    st_scr[1] = h_im
    hre_ref[...] = h_re
    him_ref[...] = h_im

    y = _dot(bu_scr[:, 0:S5_LANES], wcre_ref[...]) - _dot(bu_scr[:, S5_LANES:2 * S5_LANES], wcim_ref[...])
    y = _gelu_tanh(y + d_ref[...] * u)
    y = y * jax.nn.sigmoid(_dot(y, gw_ref[...]) + gb_ref[...])
    o_ref[...] = y.reshape(steps, SUBLANE, S5_CH)


def _s5(u_tm, h0_re, h0_im, ab, wb, wc_re, wc_im, d_skip, glu_w, glu_b):
    seq, bsz, _ = u_tm.shape
    steps = min(S5_STEPS, seq)
    full = lambda a: pl.BlockSpec(a.shape, lambda g, j: (0,) * a.ndim)
    st = pl.BlockSpec((SUBLANE, S5_LANES), lambda g, j: (g, 0))
    return pl.pallas_call(
        functools.partial(_s5_body, steps=steps),
        grid=(bsz // SUBLANE, seq // steps),
        in_specs=[pl.BlockSpec((steps, SUBLANE, S5_CH), lambda g, j: (j, g, 0)), st, st,
                  full(ab), full(wb), full(wc_re), full(wc_im), full(d_skip), full(glu_w), full(glu_b)],
        out_specs=[pl.BlockSpec((steps, SUBLANE, S5_CH), lambda g, j: (j, g, 0)), st, st],
        out_shape=[jax.ShapeDtypeStruct((seq, bsz, S5_CH), F32),
                   jax.ShapeDtypeStruct((bsz, S5_LANES), F32),
                   jax.ShapeDtypeStruct((bsz, S5_LANES), F32)],
        scratch_shapes=[pltpu.VMEM((steps * SUBLANE, 2 * S5_LANES), F32),
                        pltpu.VMEM((2, SUBLANE, S5_LANES), F32)],
        compiler_params=_params("parallel", "arbitrary"),
        name="s5",
    )(u_tm, h0_re, h0_im, ab, wb, wc_re, wc_im, d_skip, glu_w, glu_b)


def _gdn_body(zq_ref, zg_ref, zab_ref, cw_ref, alog_ref, dtb_ref, nw_ref, buf0_ref, s0_ref,
              ones_bd_ref, tri_ref, ea_ref, eb_ref, ec_ref, spread_ref, gather_ref,
              o_ref, nbuf_ref, sfin_ref,
              ext_scr, q_scr, k_scr, kb_scr, vb_scr, g_scr, g4_scr, s_scr, *, nseq, rows, chunk):
    j = pl.program_id(1)
    stack = GDN_HEADS * chunk
    bd_state = (lax.broadcasted_iota(I32, (GDN_HD, GDN_HD), 0) // GDN_DK) == \
               (lax.broadcasted_iota(I32, (GDN_HD, GDN_HD), 1) // GDN_DV)

    @pl.when(j == 0)
    def _():
        for s in range(nseq):
            ext_scr[s, 0:SUBLANE, :] = buf0_ref[s]
            s_scr[s] = jnp.where(bd_state, _dot_sel_r(s0_ref[s], spread_ref[...]), 0.0)

    convs = []
    for s in range(nseq):
        ext_scr[s, SUBLANE:SUBLANE + rows, :] = zq_ref[s]
        conv = ext_scr[s, SUBLANE - 3:SUBLANE - 3 + rows, :] * cw_ref[0:1, :]
        for i in range(1, GDN_CONV):
            conv = conv + ext_scr[s, SUBLANE - 3 + i:SUBLANE - 3 + i + rows, :] * cw_ref[i:i + 1, :]
        tail = ext_scr[s, rows:rows + SUBLANE, :]
        nbuf_ref[s] = tail
        ext_scr[s, 0:SUBLANE, :] = tail
        convs.append(conv)
    qkv = _silu(convs[0] if nseq == 1 else jnp.concatenate(convs, axis=0))

    ones_bd = ones_bd_ref[...]
    q_raw = qkv[:, 0:GDN_HD]
    k_raw = qkv[:, GDN_HD:2 * GDN_HD]
    q_scr[...] = q_raw * lax.rsqrt(_dot_sel_r(q_raw * q_raw, ones_bd) + EPS) * (GDN_DK ** -0.5)
    k_n = k_raw * lax.rsqrt(_dot_sel_r(k_raw * k_raw, ones_bd) + EPS)
    k_scr[...] = k_n

    ab = zab_ref[...].reshape(nseq * rows, LANE)
    x = ab + dtb_ref[...]
    softplus = jnp.maximum(x, 0.0) + jnp.log1p(jnp.exp(-jnp.abs(x)))
    g_pad = -jnp.exp(alog_ref[...]) * softplus
    beta = _dot_sel_r(jax.nn.sigmoid(ab), eb_ref[...])
    g_cum = _dot_sel_l(tri_ref[...], g_pad)
    g_scr[...] = _dot_sel_r(g_cum, ea_ref[...])
    g4_scr[...] = _dot_sel_r(g_pad, ec_ref[...])
    kb_scr[...] = k_n * beta
    vb_scr[...] = qkv[:, 2 * GDN_HD:3 * GDN_HD] * beta

    w_row = lax.broadcasted_iota(I32, (chunk, stack), 0)
    w_col = lax.broadcasted_iota(I32, (chunk, stack), 1) % chunk
    strict_w = w_row > w_col
    causal_w = w_row >= w_col
    same_head = (lax.broadcasted_iota(I32, (stack, stack), 0) // chunk) == \
                (lax.broadcasted_iota(I32, (stack, stack), 1) // chunk)
    head_rows = (lax.broadcasted_iota(I32, (stack, GDN_HD), 0) // chunk) == \
                (lax.broadcasted_iota(I32, (stack, GDN_HD), 1) // GDN_DK)
    head_rows2 = jnp.concatenate([head_rows, head_rows], axis=1)
    tri_c = tri_ref[0:chunk, 0:chunk]
    packed_ok = chunk % (2 * SUBLANE) == 0

    def tile4(a):
        return jnp.concatenate([a] * GDN_HEADS, axis=0)

    def split2(a):
        hi = a.astype(BF16)
        return hi, (a - hi.astype(F32)).astype(BF16)

    def stack_masked(parts, mask):
        if packed_ok:
            return [jnp.where(mask, tile4(p), jnp.zeros((), BF16)) for p in parts]
        return [jnp.where(mask, tile4(p.astype(F32)), 0.0).astype(BF16) for p in parts]

    def on_diag(parts):
        return stack_masked(parts, same_head)

    def mm(a, b):
        return jnp.dot(a, b, preferred_element_type=F32)

    def mm_hi(a_parts, b_parts):
        return mm(a_parts[0], b_parts[0]) + (mm(a_parts[0], b_parts[1]) + mm(a_parts[1], b_parts[0]))

    n_chunks = rows // chunk
    every = range(nseq * n_chunks)
    rows_of = lambda u: slice(u * chunk, (u + 1) * chunk)
    solved = []
    for u in every:
        r = rows_of(u)
        q_c, k_c, kb_c, vb_c, g_c = q_scr[r, :], k_scr[r, :], kb_scr[r, :], vb_scr[r, :], g_scr[r, :]
        exp_g = jnp.exp(g_c)
        g_diff = _dot_sel_l(tri_c, jnp.where(strict_w, g4_scr[r, :], 0.0))
        decay = jnp.exp(jnp.where(causal_w, g_diff, 0.0))
        k_heads = jnp.where(head_rows, tile4(k_c), 0.0)
        lmat = jnp.where(strict_w, _dot_nt(kb_c, k_heads) * decay, 0.0)
        attn = jnp.where(causal_w, _dot_nt(q_c, k_heads) * decay, 0.0)
        rhs_c = jnp.concatenate([vb_c, kb_c * exp_g], axis=1)
        solved.append((lmat, attn, exp_g, rhs_c))

    one = lambda a: a.astype(BF16)
    t_acc = [-solved[ci][0] for ci in every]
    p_one = [one(solved[ci][0]) for ci in every]
    p_diag = [on_diag([p])[0] for p in p_one]
    span = 2
    while span < chunk + 1:
        power = [mm(p_one[ci], p_diag[ci]) for ci in every]
        p_one = [one(p) for p in power]
        p_diag = [on_diag([p])[0] for p in p_one]
        t_acc = [t_acc[ci] + power[ci] + mm(one(t_acc[ci]), p_diag[ci]) for ci in every]
        span *= 2
    t_one = [one(t) for t in t_acc]
    apply_inv = lambda ci, v: v + mm(t_one[ci], stack_masked([one(v)], head_rows2)[0])
    sol0 = [apply_inv(ci, solved[ci][3]) for ci in every]
    resid = [solved[ci][3] - (sol0[ci] + mm_hi(split2(solved[ci][0]), stack_masked(split2(sol0[ci]), head_rows2)))
             for ci in every]
    sols = [sol0[ci] + apply_inv(ci, resid[ci]) for ci in every]

    seqs = range(nseq)
    s_cur = [s_scr[s] for s in seqs]
    for ci in range(n_chunks):
        us = [s * n_chunks + ci for s in seqs]
        v_new = [sols[u][:, 0:GDN_HD] - _dot(sols[u][:, GDN_HD:2 * GDN_HD], s_cur[s]) for s, u in enumerate(us)]
        o_cs = [_dot(q_scr[rows_of(u), :] * solved[u][2], s_cur[s])
                + _dot(solved[u][1], jnp.where(head_rows, tile4(v_new[s]), 0.0)) for s, u in enumerate(us)]
        for s, u in enumerate(us):
            g_c = g_scr[rows_of(u), :]
            g_last = g_c[chunk - 1:chunk, :]
            k_dec = k_scr[rows_of(u), :] * jnp.exp(g_last - g_c)
            s_cur[s] = s_cur[s] * jnp.exp(g_last) + jnp.where(bd_state, _dot_tn(k_dec, v_new[s]), 0.0)
        for s, u in enumerate(us):
            r = slice(ci * chunk, (ci + 1) * chunk)
            ms = _dot_sel_r(o_cs[s] * o_cs[s], ones_bd) * (1.0 / GDN_DV)
            o_ref[s, r, :] = (o_cs[s] * lax.rsqrt(ms + EPS) * nw_ref[...] * _silu(zg_ref[s, r, :])).astype(o_ref.dtype)
    for s in seqs:
        s_scr[s] = s_cur[s]

    @pl.when(j == pl.num_programs(1) - 1)
    def _():
        for s in seqs:
            sfin_ref[s] = _dot_sel_r(s_scr[s], gather_ref[...])


def _gdn_consts(rows, chunk):
    stack = GDN_HEADS * chunk
    blk = lambda n, c: (np.arange(n)[:, None] // c) == (np.arange(n)[None, :] // c)
    ones_bd = blk(GDN_HD, GDN_DK)
    tri = blk(rows, chunk) & (np.arange(rows)[:, None] >= np.arange(rows)[None, :])
    src = np.arange(LANE)[:, None]
    ea = src == np.arange(GDN_HD)[None, :] // GDN_DK
    eb = src == GDN_HEADS + np.arange(GDN_HD)[None, :] // GDN_DK
    ec = src == np.arange(stack)[None, :] // chunk
    spread = np.arange(GDN_DV)[:, None] == np.arange(GDN_HD)[None, :] % GDN_DV
    return tuple(jnp.asarray(m, dtype=BF16) for m in (ones_bd, tri, ea, eb, ec, spread, spread.T))


def _gdn(zq, zg, zab, conv_w, alog_pad, dtb_pad, nw, buf0, s0, layer, bsz, seq):
    chunk = min(GDN_CHUNK, seq)
    rows = min(4 * chunk, seq)
    nj = seq // rows
    nseq = _seqs_per_step(bsz, nj, GDN_LONG_SEQS_PER_STEP)
    stack = GDN_HEADS * chunk
    consts = _gdn_consts(nseq * rows, chunk)
    row = lambda n: pl.BlockSpec((nseq, rows, n), lambda b, j: (b, j, 0))
    full = lambda a: pl.BlockSpec(a.shape, lambda b, j: (0,) * a.ndim)
    per_b = lambda *s: pl.BlockSpec((nseq,) + s, lambda b, j: (b, 0, 0))
    by_seq = lambda a: a.reshape(bsz, seq, a.shape[-1])
    o, nbuf, s_fin = pl.pallas_call(
        functools.partial(_gdn_body, nseq=nseq, rows=rows, chunk=chunk),
        grid=(bsz // nseq, nj),
        in_specs=[row(GDN_QKV), row(GDN_HD), row(LANE), full(conv_w), full(alog_pad), full(dtb_pad), full(nw),
                  per_b(SUBLANE, GDN_QKV),
                  pl.BlockSpec((None, nseq, GDN_HD, GDN_DV), lambda b, j: (layer, b, 0, 0))] +
                 [full(c) for c in consts],
        out_specs=[row(GDN_HD), per_b(SUBLANE, GDN_QKV), per_b(GDN_HD, GDN_DV)],
        out_shape=[jax.ShapeDtypeStruct((bsz, seq, GDN_HD), _mix_dtype(chunk)),
                   jax.ShapeDtypeStruct((bsz, SUBLANE, GDN_QKV), F32),
                   jax.ShapeDtypeStruct((bsz, GDN_HD, GDN_DV), F32)],
        scratch_shapes=[pltpu.VMEM((nseq, rows + SUBLANE, GDN_QKV), F32)] +
                       [pltpu.VMEM((nseq * rows, GDN_HD), F32)] * 5 +
                       [pltpu.VMEM((nseq * rows, stack), F32), pltpu.VMEM((nseq, GDN_HD, GDN_HD), F32)],
        compiler_params=_params("parallel", "arbitrary"),
        name="gdn",
    )(by_seq(zq), by_seq(zg), by_seq(zab), conv_w, alog_pad, dtb_pad, nw, buf0, s0, *consts)
    return o.reshape(bsz * seq, GDN_HD), nbuf, s_fin


def _outproj_body(x_ref, oret_ref, os5_ref, ogdn_ref, w1_ref, w2_ref, w3_ref, nw_ref, wr_ref, br_ref, tri_ref,
                  cnt0_ref, x1_ref, h2_ref, meta_ref, cnt_ref, carry_scr, *, tm):
    i = pl.program_id(0)

    @pl.when(i == 0)
    def _():
        carry_scr[...] = cnt0_ref[...]

    mix = _dot(oret_ref[...], w1_ref[...]) + _dot(os5_ref[...], w2_ref[...]) + _dot(ogdn_ref[...], w3_ref[...])
    x1 = x_ref[...] + mix
    x1_ref[...] = x1
    h2 = _rms(x1) * nw_ref[...]
    for s in range(D_MODEL // LANE):
        h2_ref[pl.ds(s, tm, stride=SUBLANE), :] = h2[:, s * LANE:(s + 1) * LANE]

    logits = _dot_nt(wr_ref[...], h2) + br_ref[...]
    row_i = lax.broadcasted_iota(I32, (LANE, tm), 0)
    row = row_i.astype(F32)
    neg = -jnp.inf
    big = float(LANE)
    g_log = jnp.where((row_i >= N_EXPERTS) & (row_i < N_EXPERTS + MOE_GROUPS), logits, neg)
    g_max = jnp.max(g_log, axis=0, keepdims=True)
    grp = jnp.min(jnp.where(g_log == g_max, row - N_EXPERTS, big), axis=0, keepdims=True)
    p_grp = 1.0 / jnp.sum(jnp.exp(g_log - g_max), axis=0, keepdims=True)
    in_grp = (row >= grp * EXPERTS_PER_GROUP) & (row < (grp + 1.0) * EXPERTS_PER_GROUP)
    e_log = jnp.where(in_grp, logits, neg)
    v1 = jnp.max(e_log, axis=0, keepdims=True)
    i1 = jnp.min(jnp.where(e_log == v1, row, big), axis=0, keepdims=True)
    e_log2 = jnp.where(row == i1, neg, e_log)
    v2 = jnp.max(e_log2, axis=0, keepdims=True)
    i2 = jnp.min(jnp.where(e_log2 == v2, row, big), axis=0, keepdims=True)
    e2 = jnp.exp(v2 - v1)
    gate1 = p_grp / (1.0 + e2)
    gate2 = p_grp * e2 / (1.0 + e2)

    oh1 = row == i1
    oh2 = row == i2
    picked = jnp.where(oh1 | oh2, 1.0, 0.0)
    seen = carry_scr[:, 0:1]
    before = jnp.dot(picked.astype(BF16), tri_ref[...], preferred_element_type=F32) + seen
    rank1 = jnp.sum(jnp.where(oh1, before, 0.0), axis=0, keepdims=True)
    rank2 = jnp.sum(jnp.where(oh2, before, 0.0), axis=0, keepdims=True)
    total = seen + jnp.sum(picked, axis=1, keepdims=True)
    carry_scr[...] = jnp.broadcast_to(total, carry_scr.shape)
    cnt_ref[...] = jnp.broadcast_to(total, cnt_ref.shape)
    meta_ref[...] = jnp.concatenate([i1, i2, gate1, gate2, rank1, rank2, jnp.zeros((2, tm), F32)], axis=0)


def _outproj(x, o_ret, o_s5, o_gdn, w1, w2, w3, nw, wr, br, cnt0):
    t = x.shape[0]
    tm = min(ROW_TILE, t)
    tri = jnp.asarray(np.arange(tm)[:, None] < np.arange(tm)[None, :], dtype=BF16)
    row = lambda n: pl.BlockSpec((tm, n), lambda i: (i, 0))
    full = lambda a: pl.BlockSpec(a.shape, lambda i: (0, 0))
    return pl.pallas_call(
        functools.partial(_outproj_body, tm=tm),
        grid=(t // tm,),
        in_specs=[row(D_MODEL), row(RET_VD), row(S5_CH), row(GDN_HD),
                  full(w1), full(w2), full(w3), full(nw), full(wr), full(br), full(tri), full(cnt0)],
        out_specs=[row(D_MODEL), pl.BlockSpec((tm * SUBLANE, LANE), lambda i: (i, 0)),
                   pl.BlockSpec((SUBLANE, tm), lambda i: (0, i)), pl.BlockSpec((LANE, LANE), lambda i: (0, 0))],
        out_shape=[jax.ShapeDtypeStruct((t, D_MODEL), F32),
                   jax.ShapeDtypeStruct((t * SUBLANE, LANE), F32),
                   jax.ShapeDtypeStruct((SUBLANE, t), F32),
                   jax.ShapeDtypeStruct((LANE, LANE), F32)],
        scratch_shapes=[pltpu.VMEM((LANE, LANE), F32)],
        compiler_params=_params("arbitrary"),
        name="outproj_router",
    )(x, o_ret, o_s5, o_gdn, w1, w2, w3, nw, wr, br, tri, cnt0)


def _token_rows(ref, idx):
    return ref.at[pl.ds(pl.multiple_of(idx * SUBLANE, SUBLANE), SUBLANE)]


ZERO_ROWS = 128


def _zero_segment(zero_scr, xb_ref, sem, start, length, wait):
    def piece(off, n):
        cp = pltpu.make_async_copy(zero_scr.at[pl.ds(0, n * SUBLANE)],
                                   xb_ref.at[pl.ds(pl.multiple_of(off * SUBLANE, SUBLANE), n * SUBLANE)], sem)
        cp.wait() if wait else cp.start()

    n_big = length // ZERO_ROWS

    def big(i, carry):
        piece(start + i * ZERO_ROWS, ZERO_ROWS)
        return carry

    lax.fori_loop(0, n_big, big, 0)
    off = start + n_big * ZERO_ROWS
    rem = length - n_big * ZERO_ROWS
    bit = ZERO_ROWS // 2
    while bit >= 1:
        has = (rem & bit) != 0
        pl.when(has)(functools.partial(piece, off, bit))
        off = off + jnp.where(has, bit, 0)
        bit //= 2


def _scatter_body(d1_ref, d2_ref, zs_ref, zl_ref, src_a_ref, src_b_ref, xb_ref,
                  zero_scr, sem, zsem, *, tm, tiles_a):
    i = pl.program_id(0)
    base = i * tm

    @pl.when(i == 0)
    def _():
        zero_scr[...] = jnp.zeros_like(zero_scr)
        for wait in (False, True):
            lax.fori_loop(0, N_EXPERTS + 1,
                          lambda s, c, wait=wait: (_zero_segment(zero_scr, xb_ref, zsem, zs_ref[s], zl_ref[s], wait),
                                                   c)[1], 0)

    def scatter_tile(src_ref):
        def copy(t, d):
            return pltpu.make_async_copy(_token_rows(src_ref, t), _token_rows(xb_ref, d), sem)

        def issue(t, carry):
            copy(t, d1_ref[base + t]).start(priority=0)
            copy(t, d2_ref[base + t]).start(priority=1)
            return carry

        lax.fori_loop(0, tm, issue, 0, unroll=ISSUE_UNROLL)
        for _ in range(2):
            pltpu.make_async_copy(src_ref, src_ref, sem).wait()

    pl.when(i < tiles_a)(functools.partial(scatter_tile, src_a_ref))
    pl.when(i >= tiles_a)(functools.partial(scatter_tile, src_b_ref))


def _scatter(route, zero_start, zero_len, h2_a, h2_b, n_rows):
    tm = SCATTER_TILE
    tiles_a, tiles_b = h2_a.shape[0] // (tm * SUBLANE), h2_b.shape[0] // (tm * SUBLANE)
    assert tiles_a * tm * SUBLANE == h2_a.shape[0] and tiles_b * tm * SUBLANE == h2_b.shape[0]
    return pl.pallas_call(
        functools.partial(_scatter_body, tm=tm, tiles_a=tiles_a),
        grid_spec=pltpu.PrefetchScalarGridSpec(
            num_scalar_prefetch=4, grid=(tiles_a + tiles_b,),
            in_specs=[pl.BlockSpec((tm * SUBLANE, LANE), lambda i, *_: (jnp.minimum(i, tiles_a - 1), 0)),
                      pl.BlockSpec((tm * SUBLANE, LANE), lambda i, *_: (jnp.maximum(i - tiles_a, 0), 0))],
            out_specs=pl.BlockSpec(memory_space=pl.ANY),
            scratch_shapes=[pltpu.VMEM((ZERO_ROWS * SUBLANE, LANE), F32), pltpu.SemaphoreType.DMA(()),
                            pltpu.SemaphoreType.DMA(())]),
        out_shape=jax.ShapeDtypeStruct((n_rows * SUBLANE, LANE), F32),
        compiler_params=_params("arbitrary"),
        name="moe_scatter",
    )(*route, zero_start, zero_len, h2_a, h2_b)


def _experts_body(be_ref, nb_ref, xb_ref, w1_ref, w3_ref, w2_ref, yb_ref, w1_scr, w3_scr, w2_scr, *, blk):
    i = pl.program_id(0)
    live = i < nb_ref[0]

    @pl.when(live & ((i == 0) | (be_ref[i] != be_ref[jnp.maximum(i - 1, 0)])))
    def _():
        w1_scr[...] = w1_ref[...].astype(BF16)
        w3_scr[...] = w3_ref[...].astype(BF16)
        w2_scr[...] = w2_ref[...].astype(BF16)

    @pl.when(live)
    def _():
        x = jnp.concatenate([xb_ref[pl.ds(s, blk, stride=SUBLANE), :] for s in range(D_MODEL // LANE)],
                            axis=1).astype(BF16)
        hid = _silu(jnp.dot(x, w1_scr[...], preferred_element_type=F32)) * \
            jnp.dot(x, w3_scr[...], preferred_element_type=F32)
        y = _dot(hid, w2_scr[...])
        for s in range(D_MODEL // LANE):
            yb_ref[pl.ds(s, blk, stride=SUBLANE), :] = y[:, s * LANE:(s + 1) * LANE]

    @pl.when(jnp.logical_not(live))
    def _():
        yb_ref[...] = jnp.zeros_like(yb_ref)


def _experts(block_e, nb_used, xb, layer, w1, w3, w2, n_blocks):
    blk = MOE_BLK
    live = lambda i, nb: jnp.minimum(i, nb[0] - 1)
    tile_in = pl.BlockSpec((blk * SUBLANE, LANE), lambda i, be, nb: (live(i, nb), 0))
    tile_out = pl.BlockSpec((blk * SUBLANE, LANE), lambda i, be, nb: (i, 0))
    wspec = lambda a: pl.BlockSpec((None, None) + a.shape[2:],
                                   lambda i, be, nb: (layer, be[live(i, nb)], 0, 0))
    return pl.pallas_call(
        functools.partial(_experts_body, blk=blk),
        grid_spec=pltpu.PrefetchScalarGridSpec(
            num_scalar_prefetch=2, grid=(n_blocks,),
            in_specs=[tile_in, wspec(w1), wspec(w3), wspec(w2)], out_specs=tile_out,
            scratch_shapes=[pltpu.VMEM(w1.shape[2:], BF16), pltpu.VMEM(w3.shape[2:], BF16),
                            pltpu.VMEM(w2.shape[2:], BF16)]),
        out_shape=jax.ShapeDtypeStruct(xb.shape, F32),
        compiler_params=_params("arbitrary"),
        name="moe_experts",
    )(block_e, nb_used, xb, w1, w3, w2)


def _combine_body(*refs, tm, tiles_a, final_norm, n_proj):
    d1_ref, d2_ref, x1_a_ref, x1_b_ref, meta_a_ref, meta_b_ref, yb_ref, nw_ref = refs[:8]
    w_refs = refs[8:8 + n_proj]
    out_a_ref, out_b_ref = refs[8 + n_proj:10 + n_proj]
    za_refs = refs[10 + n_proj:10 + 2 * n_proj]
    zb_refs = refs[10 + 2 * n_proj:10 + 3 * n_proj]
    buf_ref, sem = refs[-2:]
    i = pl.program_id(0)
    phase = i % 2

    def gather(tile, ph):
        base = tile * tm

        def issue(t, carry):
            for k, d_ref in enumerate((d1_ref, d2_ref)):
                pltpu.make_async_copy(_token_rows(yb_ref, d_ref[base + t]),
                                      _token_rows(buf_ref.at[ph, k], t), sem.at[ph]).start(priority=k)
            return carry

        lax.fori_loop(0, tm, issue, 0, unroll=ISSUE_UNROLL)

    @pl.when(i == 0)
    def _():
        gather(0, 0)

    @pl.when(i + 1 < pl.num_programs(0))
    def _():
        gather(i + 1, 1 - phase)

    for k in range(2):
        pltpu.make_async_copy(buf_ref.at[phase, k], buf_ref.at[phase, k], sem.at[phase]).wait()

    def rows_of(slot):
        return jnp.concatenate([buf_ref[phase, slot, pl.ds(s, tm, stride=SUBLANE), :]
                                for s in range(D_MODEL // LANE)], axis=1)

    is_a = i < tiles_a
    meta = jnp.where(is_a, meta_a_ref[...], meta_b_ref[...])
    on_diag = lax.broadcasted_iota(I32, (tm, tm), 0) == lax.broadcasted_iota(I32, (tm, tm), 1)
    as_col = lambda r: jnp.sum(jnp.where(on_diag, r, 0.0), axis=1, keepdims=True)
    x2 = jnp.where(is_a, x1_a_ref[...], x1_b_ref[...]) + (as_col(meta[2:3, :]) * rows_of(0)
                                                         + as_col(meta[3:4, :]) * rows_of(1))
    normed = _rms(x2) * nw_ref[...]

    def emit(out_ref, z_refs):
        out_ref[...] = normed if final_norm else x2
        h = normed.astype(BF16)
        for w_ref, z_ref in zip(w_refs, z_refs):
            z_ref[...] = jnp.dot(h, w_ref[...], preferred_element_type=F32)

    pl.when(is_a)(functools.partial(emit, out_a_ref, za_refs))
    pl.when(jnp.logical_not(is_a))(functools.partial(emit, out_b_ref, zb_refs))


def _combine(route, x1_a, x1_b, meta_a, meta_b, yb, norm_w, final_norm, next_in_ws):
    tm = GATHER_TILE if next_in_ws else FINAL_GATHER_TILE
    tiles_a, tiles_b = x1_a.shape[0] // tm, x1_b.shape[0] // tm
    assert tiles_a * tm == x1_a.shape[0] and tiles_b * tm == x1_b.shape[0]
    assert final_norm == (not next_in_ws)
    of_a = lambda n: pl.BlockSpec((tm, n), lambda i, *_: (jnp.minimum(i, tiles_a - 1), 0))
    of_b = lambda n: pl.BlockSpec((tm, n), lambda i, *_: (jnp.maximum(i - tiles_a, 0), 0))
    widths = [w.shape[1] for w in next_in_ws]
    outs = pl.pallas_call(
        functools.partial(_combine_body, tm=tm, tiles_a=tiles_a, final_norm=final_norm, n_proj=len(widths)),
        grid_spec=pltpu.PrefetchScalarGridSpec(
            num_scalar_prefetch=2, grid=(tiles_a + tiles_b,),
            in_specs=[of_a(D_MODEL), of_b(D_MODEL),
                      pl.BlockSpec((SUBLANE, tm), lambda i, *_: (0, jnp.minimum(i, tiles_a - 1))),
                      pl.BlockSpec((SUBLANE, tm), lambda i, *_: (0, jnp.maximum(i - tiles_a, 0))),
                      pl.BlockSpec(memory_space=pl.ANY),
                      pl.BlockSpec((1, D_MODEL), lambda i, *_: (0, 0))] +
                     [pl.BlockSpec(w.shape, lambda i, *_: (0, 0)) for w in next_in_ws],
            out_specs=[of_a(D_MODEL), of_b(D_MODEL)] + [of_a(n) for n in widths] + [of_b(n) for n in widths],
            scratch_shapes=[pltpu.VMEM((2, 2, tm * SUBLANE, LANE), F32), pltpu.SemaphoreType.DMA((2,))]),
        out_shape=[jax.ShapeDtypeStruct(x1_a.shape, F32), jax.ShapeDtypeStruct(x1_b.shape, F32)] +
                  [jax.ShapeDtypeStruct((x.shape[0], n), F32) for x in (x1_a, x1_b) for n in widths],
        compiler_params=_params("arbitrary"),
        name="moe_combine",
    )(*route, x1_a, x1_b, meta_a, meta_b, yb, norm_w, *next_in_ws)
    n = len(widths)
    return (outs[0], outs[1]), (outs[2:2 + n], outs[2 + n:2 + 2 * n])


def _moe(stream_a, stream_b, counts, layer, w1, w3, w2, norm_w, final_norm, next_in_ws):
    (x1_a, h2_a, meta_a), (x1_b, h2_b, meta_b) = stream_a, stream_b
    t = x1_a.shape[0] + x1_b.shape[0]
    n_blocks = (2 * t + N_EXPERTS * (MOE_BLK - 1)) // MOE_BLK
    col = lambda c: jnp.concatenate([meta_a[c], meta_b[c]]).astype(I32)
    e1, e2, r1, r2 = col(0), col(1), col(4), col(5)
    cnt = counts[:N_EXPERTS, 0].astype(I32)
    padded = (cnt + MOE_BLK - 1) // MOE_BLK * MOE_BLK
    pad_end = jnp.cumsum(padded)
    pad_start = pad_end - padded
    slot = lambda e, r: r + jnp.sum(jnp.where(e[:, None] == jnp.arange(N_EXPERTS, dtype=I32)[None, :],
                                              pad_start[None, :], 0), axis=1)
    route = (slot(e1, r1), slot(e2, r2))
    nb_used = (pad_end[-1:] // MOE_BLK).astype(I32)
    blk_start = jnp.arange(n_blocks, dtype=I32) * MOE_BLK
    block_e = jnp.minimum(jnp.sum(blk_start[:, None] >= pad_end[None, :], axis=1), N_EXPERTS - 1).astype(I32)
    n_rows = n_blocks * MOE_BLK
    zero_start = jnp.concatenate([pad_start + cnt, pad_end[-1:]]).astype(I32)
    zero_len = jnp.concatenate([padded - cnt, n_rows - pad_end[-1:]]).astype(I32)
    xb = _scatter(route, zero_start, zero_len, h2_a, h2_b, n_rows)
    yb = _experts(block_e, nb_used, xb, layer, w1, w3, w2, n_blocks)
    return _combine(route, x1_a, x1_b, meta_a, meta_b, yb, norm_w, final_norm, next_in_ws)


def _rope_tables(pos):
    half = RET_DK // 2
    inv = ROPE_BASE ** (-jnp.arange(half, dtype=F32) / half)
    ang = pos[:, None] * inv[None, :]
    cos, sin = jnp.cos(ang), jnp.sin(ang)
    cos_t = jnp.tile(jnp.concatenate([cos, cos], axis=1), (1, RET_HEADS))
    sin_t = jnp.tile(jnp.concatenate([-sin, sin], axis=1), (1, RET_HEADS))
    return cos_t, sin_t


def _block_diag(blocks):
    g, r, c = blocks.shape
    eye = jnp.eye(g, dtype=bool)
    return jnp.where(eye[:, None, :, None], blocks[:, :, None, :], 0).reshape(g * r, g * c)


def _layer_weights(l, w_in, s5, s5_b_bar, s5_c_re, s5_c_im, s5_d, s5_glu_w, s5_glu_b, gdn_conv_w, gdn_a_log,
                   gdn_dt_bias, gdn_norm_w, w_out, router_group_w, router_group_b, router_expert_w,
                   router_expert_b):
    wi = w_in[l]
    g0 = RET_IN + S5_IN
    w_ab = jnp.pad(wi[:, g0 + GDN_QKV + GDN_HD:], ((0, 0), (0, LANE - 2 * GDN_HEADS)))
    in_ws = [wi[:, :RET_IN], wi[:, g0:g0 + GDN_QKV], wi[:, RET_IN:g0], wi[:, g0 + GDN_QKV:g0 + GDN_QKV + GDN_HD],
             w_ab]
    in_ws = [w.astype(BF16) for w in in_ws]
    ab_re, ab_im, bb_re, bb_im = s5
    n = S5_LANES
    sl = slice(l * n, (l + 1) * n)
    ab = jnp.concatenate([ab_re[sl].reshape(1, n), ab_im[sl].reshape(1, n)], axis=0)
    to_bd = lambda m: _block_diag(jnp.swapaxes(m[sl].reshape(S5_GROUPS, S5_STATE, S5_GROUP_CH), 1, 2))
    wb = jnp.concatenate([to_bd(bb_re), to_bd(bb_im)], axis=1).astype(BF16)
    wc_re = _block_diag(jnp.swapaxes(s5_c_re[l], 1, 2)).astype(BF16)
    wc_im = _block_diag(jnp.swapaxes(s5_c_im[l], 1, 2)).astype(BF16)
    s5_ws = (ab, wb, wc_re, wc_im, s5_d[l].reshape(1, S5_CH), s5_glu_w[l].astype(BF16),
             s5_glu_b[l].reshape(1, S5_CH))
    pad4 = lambda v: jnp.pad(v.reshape(1, GDN_HEADS), ((0, 0), (0, LANE - GDN_HEADS)))
    gdn_ws = (gdn_conv_w[l], pad4(gdn_a_log[l]), pad4(gdn_dt_bias[l]),
              jnp.tile(gdn_norm_w[l], GDN_HEADS).reshape(1, GDN_HD))
    wo = w_out[l].astype(BF16)
    out_ws = (wo[:RET_VD], wo[RET_VD:RET_VD + S5_CH], wo[RET_VD + S5_CH:])
    wr = jnp.pad(jnp.concatenate([router_expert_w[l], router_group_w[l]], axis=1).T,
                 ((0, LANE - N_EXPERTS - MOE_GROUPS), (0, 0))).astype(BF16)
    br = jnp.pad(jnp.concatenate([router_expert_b[l], router_group_b[l]]),
                 (0, LANE - N_EXPERTS - MOE_GROUPS)).reshape(LANE, 1)
    return in_ws, s5_ws, gdn_ws, out_ws, (wr, br)


def _mix_and_route(l, x, projected, bsz, seq, rope, states, layer_w, norm_mix, norm_ffn, cnt0):
    t = bsz * seq
    ret_s, s5_re, s5_im, gdn_s, gdn_buf = states
    in_ws, s5_ws, gdn_ws, out_ws, (wr, br) = layer_w
    zr, zq, zs, zg, zab = projected or _inproj(x, norm_mix[l].reshape(1, D_MODEL), in_ws)
    o_ret, ret_fin = _retention(zr, *rope, ret_s.reshape(DEPTH, bsz, RET_QD, RET_DV), l, bsz, seq)
    u_tm = jnp.swapaxes(zs.reshape(bsz, seq, S5_CH), 0, 1)
    o_s5_tm, re_fin, im_fin = _s5(u_tm, s5_re[l].reshape(bsz, S5_LANES), s5_im[l].reshape(bsz, S5_LANES), *s5_ws)
    o_s5 = jnp.swapaxes(o_s5_tm, 0, 1).reshape(t, S5_CH).astype(BF16)
    buf0 = jnp.pad(gdn_buf[l], ((0, 0), (SUBLANE - (GDN_CONV - 1), 0), (0, 0)))
    o_gdn, nbuf, gdn_fin = _gdn(zq, zg, zab, *gdn_ws, buf0, gdn_s.reshape(DEPTH, bsz, GDN_HD, GDN_DV), l, bsz, seq)
    x1, h2_tiles, meta, counts = _outproj(x, o_ret, o_s5, o_gdn, *out_ws, norm_ffn[l].reshape(1, D_MODEL), wr, br,
                                          cnt0)
    new_states = (ret_fin.reshape(bsz, RET_HEADS, RET_DK, RET_DV), re_fin.reshape(bsz, S5_GROUPS, S5_STATE),
                  im_fin.reshape(bsz, S5_GROUPS, S5_STATE), gdn_fin.reshape(bsz, GDN_HEADS, GDN_DK, GDN_DV),
                  nbuf[:, SUBLANE - (GDN_CONV - 1):, :])
    return (x1, h2_tiles, meta), counts, new_states


def _trunks(xs, positions, states, layer_ws, norm_mix, norm_ffn, experts, norm_final):
    shapes = [x.shape[:2] for x in xs]
    ropes = [_rope_tables(pos) for pos in positions]
    xs = [x.reshape(b * s, D_MODEL) for x, (b, s) in zip(xs, shapes)]
    outs = [[[] for _ in range(5)] for _ in xs]
    projected = [None for _ in xs]
    for l in range(DEPTH):
        counts = jnp.zeros((LANE, LANE), F32)
        routed = []
        for i, x in enumerate(xs):
            stream, counts, new_states = _mix_and_route(l, x, projected[i], *shapes[i], ropes[i], states[i],
                                                        layer_ws[l], norm_mix, norm_ffn, counts)
            routed.append(stream)
            for lst, s in zip(outs[i], new_states):
                lst.append(s)
        last = l == DEPTH - 1
        norm_w = norm_final if last else norm_mix[l + 1]
        xs, projected = _moe(*routed, counts, l, *experts, norm_w.reshape(1, D_MODEL), last,
                             [] if last else layer_ws[l + 1][0])
    ys = [x.reshape(b, s, D_MODEL) for x, (b, s) in zip(xs, shapes)]
    return ys, [tuple(jnp.stack(o) for o in per_stream) for per_stream in outs]


def kernel(x_prompt, x_sample, state_ret, state_s5_re, state_s5_im, state_gdn, state_gdn_conv, norm_mix, w_in, s5_a_re, s5_a_im, s5_log_dt, s5_b_re, s5_b_im, s5_c_re, s5_c_im, s5_d, s5_glu_w, s5_glu_b, gdn_conv_w, gdn_a_log, gdn_dt_bias, gdn_norm_w, w_out, norm_ffn, router_group_w, router_group_b, router_expert_w, router_expert_b, expert_w1, expert_w3, expert_w2, norm_final):
    n = DEPTH * S5_LANES
    col = lambda a: a.reshape(n, 1)
    log_dt = jnp.broadcast_to(s5_log_dt[:, :, None], (DEPTH, S5_GROUPS, S5_STATE))
    s5 = _s5_prep(col(s5_a_re), col(s5_a_im), col(log_dt), s5_b_re.reshape(n, S5_GROUP_CH),
                  s5_b_im.reshape(n, S5_GROUP_CH))
    layer_ws = [_layer_weights(l, w_in, s5, None, s5_c_re, s5_c_im, s5_d, s5_glu_w, s5_glu_b, gdn_conv_w,
                               gdn_a_log, gdn_dt_bias, gdn_norm_w, w_out, router_group_w, router_group_b,
                               router_expert_w, router_expert_b) for l in range(DEPTH)]
    experts = (expert_w1, expert_w3, expert_w2)
    bp, lp, _ = x_prompt.shape
    zero_states = (jnp.zeros((DEPTH, bp, RET_HEADS, RET_DK, RET_DV), F32),
                   jnp.zeros((DEPTH, bp, S5_GROUPS, S5_STATE), F32),
                   jnp.zeros((DEPTH, bp, S5_GROUPS, S5_STATE), F32),
                   jnp.zeros((DEPTH, bp, GDN_HEADS, GDN_DK, GDN_DV), F32),
                   jnp.zeros((DEPTH, bp, GDN_CONV - 1, GDN_QKV), F32))
    sample_states = (state_ret, state_s5_re, state_s5_im, state_gdn, state_gdn_conv)
    positions = (jnp.arange(lp, dtype=F32), PAST_LEN + jnp.arange(x_sample.shape[1], dtype=F32))
    (y_p, y_s), (p_states, s_states) = _trunks((x_prompt, x_sample), positions, (zero_states, sample_states),
                                               layer_ws, norm_mix, norm_ffn, experts, norm_final)
    return (y_p, y_s) + p_states + s_states
```

```python
import functools
import math

import jax
import jax.numpy as jnp
import numpy as np
from jax import lax
from jax.experimental import pallas as pl
from jax.experimental.pallas import tpu as pltpu

F32 = jnp.float32
BF16 = jnp.bfloat16
I32 = jnp.int32

D_MODEL = 1024
DEPTH = 2
RET_HEADS, RET_DK, RET_DV = 4, 64, 128
S5_GROUPS, S5_GROUP_CH, S5_STATE = 16, 16, 64
S5_CH = S5_GROUPS * S5_GROUP_CH
S5_LANES = S5_GROUPS * S5_STATE
GDN_HEADS, GDN_DK, GDN_DV, GDN_CONV = 4, 64, 64, 4
GDN_HD = GDN_HEADS * GDN_DK
GDN_QKV = 3 * GDN_HD
RET_QD = RET_HEADS * RET_DK
RET_VD = RET_HEADS * RET_DV
RET_IN = 2 * RET_QD + 2 * RET_VD
S5_IN = S5_CH
GDN_CHUNK = 64
RET_CHUNK = 64
MOE_GROUPS, EXPERTS_PER_GROUP = 4, 8
N_EXPERTS = MOE_GROUPS * EXPERTS_PER_GROUP
ROPE_BASE = 10000.0
EPS = 1e-6
PAST_LEN = 16384

LANE = 128
SUBLANE = 8
ROW_TILE = 512
MOE_BLK = 512
SCATTER_TILE = 512
GATHER_TILE = 256
S5_STEPS = 128
ISSUE_UNROLL = 8
SEQS_PER_STEP = 16
RET_LONG_SEQS_PER_STEP = 4
GDN_LONG_SEQS_PER_STEP = 2
FINAL_GATHER_TILE = 512
VMEM_LIMIT = 56 * 1024 * 1024


def _params(*sem):
    return pltpu.CompilerParams(dimension_semantics=sem, vmem_limit_bytes=VMEM_LIMIT)


def _dot(a, b):
    return jnp.dot(a.astype(BF16), b.astype(BF16), preferred_element_type=F32)


def _dot_nt(a, b):
    return lax.dot_general(a.astype(BF16), b.astype(BF16), (((1,), (1,)), ((), ())),
                           preferred_element_type=F32)


def _dot_tn(a, b):
    return lax.dot_general(a.astype(BF16), b.astype(BF16), (((0,), (0,)), ((), ())),
                           preferred_element_type=F32)


def _split3(x):
    p1 = x.astype(BF16)
    r1 = x - p1.astype(F32)
    p2 = r1.astype(BF16)
    p3 = (r1 - p2.astype(F32)).astype(BF16)
    return p1, p2, p3


def _dot_sel_l(sel, x):
    p1, p2, p3 = _split3(x)
    d = lambda p: jnp.dot(sel, p, preferred_element_type=F32)
    return d(p1) + d(p2) + d(p3)


def _dot_sel_r(x, sel):
    p1, p2, p3 = _split3(x)
    d = lambda p: jnp.dot(p, sel, preferred_element_type=F32)
    return d(p1) + d(p2) + d(p3)


def _rms(x):
    return x * lax.rsqrt(jnp.mean(x * x, axis=-1, keepdims=True) + EPS)


def _silu(x):
    return x * jax.nn.sigmoid(x)


def _inproj_body(x_ref, nw_ref, wr_ref, wq_ref, ws_ref, wg_ref, wab_ref,
                 zr_ref, zq_ref, zs_ref, zg_ref, zab_ref):
    h = (_rms(x_ref[...]) * nw_ref[...]).astype(BF16)
    for w_ref, z_ref in ((wr_ref, zr_ref), (wq_ref, zq_ref), (ws_ref, zs_ref),
                         (wg_ref, zg_ref), (wab_ref, zab_ref)):
        z_ref[...] = jnp.dot(h, w_ref[...], preferred_element_type=F32)


def _inproj(x, nw, ws):
    t = x.shape[0]
    tm = min(ROW_TILE, t)
    widths = [w.shape[1] for w in ws]
    row = lambda n: pl.BlockSpec((tm, n), lambda i: (i, 0))
    full = lambda a: pl.BlockSpec(a.shape, lambda i: (0, 0))
    return pl.pallas_call(
        _inproj_body,
        grid=(t // tm,),
        in_specs=[row(D_MODEL), full(nw)] + [full(w) for w in ws],
        out_specs=[row(n) for n in widths],
        out_shape=[jax.ShapeDtypeStruct((t, n), F32) for n in widths],
        compiler_params=_params("parallel"),
        name="inproj",
    )(x, nw, *ws)


def _seqs_per_step(bsz, blocks_per_seq, long_seqs):
    want = SEQS_PER_STEP if blocks_per_seq == 1 else long_seqs
    return want if bsz % want == 0 else 1


def _ret_log_gamma(h):
    return math.log(1.0 - 2.0 ** (-5.0 - h))


def _ret_body(zr_ref, cos_ref, sin_ref, s0_ref, o_ref, sfin_ref, s_scr, *, nseq, rows, chunk):
    j = pl.program_id(1)
    stack = RET_HEADS * chunk

    @pl.when(j == 0)
    def _():
        s_scr[...] = s0_ref[...]

    lane = lax.broadcasted_iota(I32, (rows, RET_QD), 1)
    first_half = (lane % RET_DK) < (RET_DK // 2)

    def rotary(x):
        swapped = jnp.where(first_half, pltpu.roll(x, RET_QD - RET_DK // 2, 1),
                            pltpu.roll(x, RET_DK // 2, 1))
        return x * cos_ref[...] + swapped * sin_ref[...]

    q = [rotary(zr_ref[s, :, 0:RET_QD]) for s in range(nseq)]
    k = [rotary(zr_ref[s, :, RET_QD:2 * RET_QD]) * (RET_DK ** -0.5) for s in range(nseq)]

    def by_head(idx, fn):
        out = jnp.zeros(idx.shape, F32)
        for h in range(RET_HEADS):
            out = jnp.where(idx == h, fn(h), out)
        return out

    lg_lane = by_head(lax.broadcasted_iota(I32, (chunk, RET_QD), 1) // RET_DK, _ret_log_gamma)
    pos = lax.broadcasted_iota(I32, (chunk, RET_QD), 0).astype(F32)
    q_scale = jnp.exp((pos + 1.0) * lg_lane)
    k_scale = jnp.exp((chunk - 1.0 - pos) * lg_lane)
    st_row = lax.broadcasted_iota(I32, (stack, stack), 0)
    st_col = lax.broadcasted_iota(I32, (stack, stack), 1)
    causal = ((st_row // chunk) == (st_col // chunk)) & (st_row >= st_col)
    lg_stack = by_head(st_row // chunk, _ret_log_gamma)
    decay = jnp.where(causal, jnp.exp(jnp.where(causal, (st_row - st_col).astype(F32), 0.0) * lg_stack), 0.0)
    q_rows = (lax.broadcasted_iota(I32, (stack, RET_QD), 0) // chunk) == \
             (lax.broadcasted_iota(I32, (stack, RET_QD), 1) // RET_DK)
    v_rows = (lax.broadcasted_iota(I32, (stack, RET_VD), 0) // chunk) == \
             (lax.broadcasted_iota(I32, (stack, RET_VD), 1) // RET_DV)
    s_row_head = lax.broadcasted_iota(I32, (RET_QD, RET_DV), 0) // RET_DK
    s_decay = by_head(s_row_head, lambda h: math.exp(chunk * _ret_log_gamma(h)))
    kv_diag = (lax.broadcasted_iota(I32, (RET_QD, RET_VD), 0) // RET_DK) == \
              (lax.broadcasted_iota(I32, (RET_QD, RET_VD), 1) // RET_DV)

    def tile4(a):
        return jnp.concatenate([a] * RET_HEADS, axis=0)

    def collapse(a):
        out = a[0:chunk]
        for h in range(1, RET_HEADS):
            out = out + a[h * chunk:(h + 1) * chunk]
        return out

    seqs = range(nseq)
    n_chunks = rows // chunk
    state = [s_scr[s] for s in seqs]
    for ci in range(n_chunks):
        r = slice(ci * chunk, (ci + 1) * chunk)
        v_cs = [zr_ref[s, r, 2 * RET_QD:2 * RET_QD + RET_VD] for s in seqs]
        scores = [_dot_nt(jnp.where(q_rows, tile4(q[s][r]), 0.0), tile4(k[s][r])) * decay for s in seqs]
        o_intra = [collapse(_dot(scores[s], jnp.where(v_rows, tile4(v_cs[s]), 0.0))) for s in seqs]
        s_full = [jnp.concatenate([jnp.where(s_row_head == h, state[s], 0.0) for h in range(RET_HEADS)], axis=1)
                  for s in seqs]
        o_cs = [o_intra[s] + _dot(q[s][r] * q_scale, s_full[s]) for s in seqs]
        for s in seqs:
            gate = zr_ref[s, r, 2 * RET_QD + RET_VD:2 * RET_QD + 2 * RET_VD]
            for h in range(RET_HEADS):
                sl = slice(h * RET_DV, (h + 1) * RET_DV)
                o_ref[s, r, sl] = _rms(o_cs[s][:, sl]) * _silu(gate[:, sl])
        for s in seqs:
            kv = jnp.where(kv_diag, _dot_tn(k[s][r] * k_scale, v_cs[s]), 0.0)
            kv_own = kv[:, 0:RET_DV]
            for h in range(1, RET_HEADS):
                kv_own = kv_own + kv[:, h * RET_DV:(h + 1) * RET_DV]
            state[s] = state[s] * s_decay + kv_own
    for s in seqs:
        s_scr[s] = state[s]

    @pl.when(j == pl.num_programs(1) - 1)
    def _():
        for s in seqs:
            sfin_ref[s] = state[s]


def _retention(zr, cos, sin, s0, layer, bsz, seq):
    chunk = min(RET_CHUNK, seq)
    rows = min(4 * chunk, seq)
    nj = seq // rows
    nseq = _seqs_per_step(bsz, nj, RET_LONG_SEQS_PER_STEP)
    o, s_fin = pl.pallas_call(
        functools.partial(_ret_body, nseq=nseq, rows=rows, chunk=chunk),
        grid=(bsz // nseq, nj),
        in_specs=[pl.BlockSpec((nseq, rows, RET_IN), lambda b, j: (b, j, 0)),
                  pl.BlockSpec((rows, RET_QD), lambda b, j: (j, 0)),
                  pl.BlockSpec((rows, RET_QD), lambda b, j: (j, 0)),
                  pl.BlockSpec((None, nseq, RET_QD, RET_DV), lambda b, j: (layer, b, 0, 0))],
        out_specs=[pl.BlockSpec((nseq, rows, RET_VD), lambda b, j: (b, j, 0)),
                   pl.BlockSpec((nseq, RET_QD, RET_DV), lambda b, j: (b, 0, 0))],
        out_shape=[jax.ShapeDtypeStruct((bsz, seq, RET_VD), F32),
                   jax.ShapeDtypeStruct((bsz, RET_QD, RET_DV), F32)],
        scratch_shapes=[pltpu.VMEM((nseq, RET_QD, RET_DV), F32)],
        compiler_params=_params("parallel", "arbitrary"),
        name="retention",
    )(zr.reshape(bsz, seq, RET_IN), cos, sin, s0)
    return o.reshape(bsz * seq, RET_VD), s_fin


def _s5_prep_body(are_ref, aim_ref, ldt_ref, bre_ref, bim_ref, abre_ref, abim_ref, bbre_ref, bbim_ref):
    lam_re, lam_im = are_ref[...], aim_ref[...]
    dt = jnp.exp(ldt_ref[...])
    mag = jnp.exp(lam_re * dt)
    ab_re = mag * jnp.cos(lam_im * dt)
    ab_im = mag * jnp.sin(lam_im * dt)
    den = lam_re * lam_re + lam_im * lam_im
    f_re = ((ab_re - 1.0) * lam_re + ab_im * lam_im) / den
    f_im = (ab_im * lam_re - (ab_re - 1.0) * lam_im) / den
    abre_ref[...] = ab_re
    abim_ref[...] = ab_im
    bbre_ref[...] = f_re * bre_ref[...] - f_im * bim_ref[...]
    bbim_ref[...] = f_re * bim_ref[...] + f_im * bre_ref[...]


def _s5_prep(a_re, a_im, log_dt, b_re, b_im):
    n = a_re.shape[0]
    col = jax.ShapeDtypeStruct((n, 1), F32)
    mat = jax.ShapeDtypeStruct((n, S5_GROUP_CH), F32)
    return pl.pallas_call(_s5_prep_body, out_shape=[col, col, mat, mat], name="s5_prep")(
        a_re, a_im, log_dt, b_re, b_im)


def _gelu_tanh(x):
    return x * (0.5 * (1.0 + jnp.tanh(math.sqrt(2.0 / math.pi) * (x + 0.044715 * (x * x * x)))))


def _s5_body(u_ref, h0re_ref, h0im_ref, ab_ref, wb_ref, wcre_ref, wcim_ref, d_ref, gw_ref, gb_ref,
             o_ref, hre_ref, him_ref, bu_scr, st_scr, *, steps):
    j = pl.program_id(1)

    @pl.when(j == 0)
    def _():
        st_scr[0] = h0re_ref[...]
        st_scr[1] = h0im_ref[...]

    rows = steps * SUBLANE
    u = u_ref[...].reshape(rows, S5_CH)
    bu_scr[...] = _dot(u, wb_ref[...])
    a_re = jnp.broadcast_to(ab_ref[0:1, :], (SUBLANE, S5_LANES))
    a_im = jnp.broadcast_to(ab_ref[1:2, :], (SUBLANE, S5_LANES))

    def step(t, carry):
        h_re, h_im = carry
        r = pl.ds(pl.multiple_of(t * SUBLANE, SUBLANE), SUBLANE)
        n_re = a_re * h_re - a_im * h_im + bu_scr[r, 0:S5_LANES]
        n_im = a_re * h_im + a_im * h_re + bu_scr[r, S5_LANES:2 * S5_LANES]
        bu_scr[r, 0:S5_LANES] = n_re
        bu_scr[r, S5_LANES:2 * S5_LANES] = n_im
        return n_re, n_im

    h_re, h_im = lax.fori_loop(0, steps, step, (st_scr[0], st_scr[1]))
    st_scr[0] = h_re
    st_scr[1] = h_im
    hre_ref[...] = h_re
    him_ref[...] = h_im

    y = _dot(bu_scr[:, 0:S5_LANES], wcre_ref[...]) - _dot(bu_scr[:, S5_LANES:2 * S5_LANES], wcim_ref[...])
    y = _gelu_tanh(y + d_ref[...] * u)
    y = y * jax.nn.sigmoid(_dot(y, gw_ref[...]) + gb_ref[...])
    o_ref[...] = y.reshape(steps, SUBLANE, S5_CH)


def _s5(u_tm, h0_re, h0_im, ab, wb, wc_re, wc_im, d_skip, glu_w, glu_b):
    seq, bsz, _ = u_tm.shape
    steps = min(S5_STEPS, seq)
    full = lambda a: pl.BlockSpec(a.shape, lambda g, j: (0,) * a.ndim)
    st = pl.BlockSpec((SUBLANE, S5_LANES), lambda g, j: (g, 0))
    return pl.pallas_call(
        functools.partial(_s5_body, steps=steps),
        grid=(bsz // SUBLANE, seq // steps),
        in_specs=[pl.BlockSpec((steps, SUBLANE, S5_CH), lambda g, j: (j, g, 0)), st, st,
                  full(ab), full(wb), full(wc_re), full(wc_im), full(d_skip), full(glu_w), full(glu_b)],
        out_specs=[pl.BlockSpec((steps, SUBLANE, S5_CH), lambda g, j: (j, g, 0)), st, st],
        out_shape=[jax.ShapeDtypeStruct((seq, bsz, S5_CH), F32),
                   jax.ShapeDtypeStruct((bsz, S5_LANES), F32),
                   jax.ShapeDtypeStruct((bsz, S5_LANES), F32)],
        scratch_shapes=[pltpu.VMEM((steps * SUBLANE, 2 * S5_LANES), F32),
                        pltpu.VMEM((2, SUBLANE, S5_LANES), F32)],
        compiler_params=_params("parallel", "arbitrary"),
        name="s5",
    )(u_tm, h0_re, h0_im, ab, wb, wc_re, wc_im, d_skip, glu_w, glu_b)


def _gdn_body(zq_ref, zg_ref, zab_ref, cw_ref, alog_ref, dtb_ref, nw_ref, buf0_ref, s0_ref,
              ones_bd_ref, tri_ref, ea_ref, eb_ref, ec_ref, spread_ref, gather_ref,
              o_ref, nbuf_ref, sfin_ref,
              ext_scr, q_scr, k_scr, kb_scr, vb_scr, g_scr, g4_scr, s_scr, *, nseq, rows, chunk):
    j = pl.program_id(1)
    stack = GDN_HEADS * chunk
    bd_state = (lax.broadcasted_iota(I32, (GDN_HD, GDN_HD), 0) // GDN_DK) == \
               (lax.broadcasted_iota(I32, (GDN_HD, GDN_HD), 1) // GDN_DV)

    @pl.when(j == 0)
    def _():
        for s in range(nseq):
            ext_scr[s, 0:SUBLANE, :] = buf0_ref[s]
            s_scr[s] = jnp.where(bd_state, _dot_sel_r(s0_ref[s], spread_ref[...]), 0.0)

    convs = []
    for s in range(nseq):
        ext_scr[s, SUBLANE:SUBLANE + rows, :] = zq_ref[s]
        conv = ext_scr[s, SUBLANE - 3:SUBLANE - 3 + rows, :] * cw_ref[0:1, :]
        for i in range(1, GDN_CONV):
            conv = conv + ext_scr[s, SUBLANE - 3 + i:SUBLANE - 3 + i + rows, :] * cw_ref[i:i + 1, :]
        tail = ext_scr[s, rows:rows + SUBLANE, :]
        nbuf_ref[s] = tail
        ext_scr[s, 0:SUBLANE, :] = tail
        convs.append(conv)
    qkv = _silu(convs[0] if nseq == 1 else jnp.concatenate(convs, axis=0))

    ones_bd = ones_bd_ref[...]
    q_raw = qkv[:, 0:GDN_HD]
    k_raw = qkv[:, GDN_HD:2 * GDN_HD]
    q_scr[...] = q_raw * lax.rsqrt(_dot_sel_r(q_raw * q_raw, ones_bd) + EPS) * (GDN_DK ** -0.5)
    k_n = k_raw * lax.rsqrt(_dot_sel_r(k_raw * k_raw, ones_bd) + EPS)
    k_scr[...] = k_n

    ab = zab_ref[...].reshape(nseq * rows, LANE)
    x = ab + dtb_ref[...]
    softplus = jnp.maximum(x, 0.0) + jnp.log1p(jnp.exp(-jnp.abs(x)))
    g_pad = -jnp.exp(alog_ref[...]) * softplus
    beta = _dot_sel_r(jax.nn.sigmoid(ab), eb_ref[...])
    g_cum = _dot_sel_l(tri_ref[...], g_pad)
    g_scr[...] = _dot_sel_r(g_cum, ea_ref[...])
    g4_scr[...] = _dot_sel_r(g_pad, ec_ref[...])
    kb_scr[...] = k_n * beta
    vb_scr[...] = qkv[:, 2 * GDN_HD:3 * GDN_HD] * beta

    w_row = lax.broadcasted_iota(I32, (chunk, stack), 0)
    w_col = lax.broadcasted_iota(I32, (chunk, stack), 1) % chunk
    strict_w = w_row > w_col
    causal_w = w_row >= w_col
    same_head = (lax.broadcasted_iota(I32, (stack, stack), 0) // chunk) == \
                (lax.broadcasted_iota(I32, (stack, stack), 1) // chunk)
    head_rows = (lax.broadcasted_iota(I32, (stack, GDN_HD), 0) // chunk) == \
                (lax.broadcasted_iota(I32, (stack, GDN_HD), 1) // GDN_DK)
    head_rows2 = jnp.concatenate([head_rows, head_rows], axis=1)
    tri_c = tri_ref[0:chunk, 0:chunk]
    packed_ok = chunk % (2 * SUBLANE) == 0

    def tile4(a):
        return jnp.concatenate([a] * GDN_HEADS, axis=0)

    def split2(a):
        hi = a.astype(BF16)
        return hi, (a - hi.astype(F32)).astype(BF16)

    def stack_masked(parts, mask):
        if packed_ok:
            return [jnp.where(mask, tile4(p), jnp.zeros((), BF16)) for p in parts]
        return [jnp.where(mask, tile4(p.astype(F32)), 0.0).astype(BF16) for p in parts]

    def on_diag(parts):
        return stack_masked(parts, same_head)

    def mm(a, b):
        return jnp.dot(a, b, preferred_element_type=F32)

    def mm_hi(a_parts, b_parts):
        return mm(a_parts[0], b_parts[0]) + (mm(a_parts[0], b_parts[1]) + mm(a_parts[1], b_parts[0]))

    n_chunks = rows // chunk
    every = range(nseq * n_chunks)
    rows_of = lambda u: slice(u * chunk, (u + 1) * chunk)
    solved = []
    for u in every:
        r = rows_of(u)
        q_c, k_c, kb_c, vb_c, g_c = q_scr[r, :], k_scr[r, :], kb_scr[r, :], vb_scr[r, :], g_scr[r, :]
        exp_g = jnp.exp(g_c)
        g_diff = _dot_sel_l(tri_c, jnp.where(strict_w, g4_scr[r, :], 0.0))
        decay = jnp.exp(jnp.where(causal_w, g_diff, 0.0))
        k_heads = jnp.where(head_rows, tile4(k_c), 0.0)
        lmat = jnp.where(strict_w, _dot_nt(kb_c, k_heads) * decay, 0.0)
        attn = jnp.where(causal_w, _dot_nt(q_c, k_heads) * decay, 0.0)
        rhs_c = jnp.concatenate([vb_c, kb_c * exp_g], axis=1)
        solved.append((lmat, attn, exp_g, rhs_c))

    one = lambda a: a.astype(BF16)
    t_acc = [-solved[ci][0] for ci in every]
    p_one = [one(solved[ci][0]) for ci in every]
    p_diag = [on_diag([p])[0] for p in p_one]
    span = 2
    while span < chunk + 1:
        power = [mm(p_one[ci], p_diag[ci]) for ci in every]
        p_one = [one(p) for p in power]
        p_diag = [on_diag([p])[0] for p in p_one]
        t_acc = [t_acc[ci] + power[ci] + mm(one(t_acc[ci]), p_diag[ci]) for ci in every]
        span *= 2
    t_one = [one(t) for t in t_acc]
    apply_inv = lambda ci, v: v + mm(t_one[ci], stack_masked([one(v)], head_rows2)[0])
    sol0 = [apply_inv(ci, solved[ci][3]) for ci in every]
    resid = [solved[ci][3] - (sol0[ci] + mm_hi(split2(solved[ci][0]), stack_masked(split2(sol0[ci]), head_rows2)))
             for ci in every]
    sols = [sol0[ci] + apply_inv(ci, resid[ci]) for ci in every]

    seqs = range(nseq)
    s_cur = [s_scr[s] for s in seqs]
    for ci in range(n_chunks):
        us = [s * n_chunks + ci for s in seqs]
        v_new = [sols[u][:, 0:GDN_HD] - _dot(sols[u][:, GDN_HD:2 * GDN_HD], s_cur[s]) for s, u in enumerate(us)]
        o_cs = [_dot(q_scr[rows_of(u), :] * solved[u][2], s_cur[s])
                + _dot(solved[u][1], jnp.where(head_rows, tile4(v_new[s]), 0.0)) for s, u in enumerate(us)]
        for s, u in enumerate(us):
            g_c = g_scr[rows_of(u), :]
            g_last = g_c[chunk - 1:chunk, :]
            k_dec = k_scr[rows_of(u), :] * jnp.exp(g_last - g_c)
            s_cur[s] = s_cur[s] * jnp.exp(g_last) + jnp.where(bd_state, _dot_tn(k_dec, v_new[s]), 0.0)
        for s, u in enumerate(us):
            r = slice(ci * chunk, (ci + 1) * chunk)
            ms = _dot_sel_r(o_cs[s] * o_cs[s], ones_bd) * (1.0 / GDN_DV)
            o_ref[s, r, :] = o_cs[s] * lax.rsqrt(ms + EPS) * nw_ref[...] * _silu(zg_ref[s, r, :])
    for s in seqs:
        s_scr[s] = s_cur[s]

    @pl.when(j == pl.num_programs(1) - 1)
    def _():
        for s in seqs:
            sfin_ref[s] = _dot_sel_r(s_scr[s], gather_ref[...])


def _gdn_consts(rows, chunk):
    stack = GDN_HEADS * chunk
    blk = lambda n, c: (np.arange(n)[:, None] // c) == (np.arange(n)[None, :] // c)
    ones_bd = blk(GDN_HD, GDN_DK)
    tri = blk(rows, chunk) & (np.arange(rows)[:, None] >= np.arange(rows)[None, :])
    src = np.arange(LANE)[:, None]
    ea = src == np.arange(GDN_HD)[None, :] // GDN_DK
    eb = src == GDN_HEADS + np.arange(GDN_HD)[None, :] // GDN_DK
    ec = src == np.arange(stack)[None, :] // chunk
    spread = np.arange(GDN_DV)[:, None] == np.arange(GDN_HD)[None, :] % GDN_DV
    return tuple(jnp.asarray(m, dtype=BF16) for m in (ones_bd, tri, ea, eb, ec, spread, spread.T))


def _gdn(zq, zg, zab, conv_w, alog_pad, dtb_pad, nw, buf0, s0, layer, bsz, seq):
    chunk = min(GDN_CHUNK, seq)
    rows = min(4 * chunk, seq)
    nj = seq // rows
    nseq = _seqs_per_step(bsz, nj, GDN_LONG_SEQS_PER_STEP)
    stack = GDN_HEADS * chunk
    consts = _gdn_consts(nseq * rows, chunk)
    row = lambda n: pl.BlockSpec((nseq, rows, n), lambda b, j: (b, j, 0))
    full = lambda a: pl.BlockSpec(a.shape, lambda b, j: (0,) * a.ndim)
    per_b = lambda *s: pl.BlockSpec((nseq,) + s, lambda b, j: (b, 0, 0))
    by_seq = lambda a: a.reshape(bsz, seq, a.shape[-1])
    o, nbuf, s_fin = pl.pallas_call(
        functools.partial(_gdn_body, nseq=nseq, rows=rows, chunk=chunk),
        grid=(bsz // nseq, nj),
        in_specs=[row(GDN_QKV), row(GDN_HD), row(LANE), full(conv_w), full(alog_pad), full(dtb_pad), full(nw),
                  per_b(SUBLANE, GDN_QKV),
                  pl.BlockSpec((None, nseq, GDN_HD, GDN_DV), lambda b, j: (layer, b, 0, 0))] +
                 [full(c) for c in consts],
        out_specs=[row(GDN_HD), per_b(SUBLANE, GDN_QKV), per_b(GDN_HD, GDN_DV)],
        out_shape=[jax.ShapeDtypeStruct((bsz, seq, GDN_HD), F32),
                   jax.ShapeDtypeStruct((bsz, SUBLANE, GDN_QKV), F32),
                   jax.ShapeDtypeStruct((bsz, GDN_HD, GDN_DV), F32)],
        scratch_shapes=[pltpu.VMEM((nseq, rows + SUBLANE, GDN_QKV), F32)] +
                       [pltpu.VMEM((nseq * rows, GDN_HD), F32)] * 5 +
                       [pltpu.VMEM((nseq * rows, stack), F32), pltpu.VMEM((nseq, GDN_HD, GDN_HD), F32)],
        compiler_params=_params("parallel", "arbitrary"),
        name="gdn",
    )(by_seq(zq), by_seq(zg), by_seq(zab), conv_w, alog_pad, dtb_pad, nw, buf0, s0, *consts)
    return o.reshape(bsz * seq, GDN_HD), nbuf, s_fin


def _outproj_body(x_ref, oret_ref, os5_ref, ogdn_ref, w1_ref, w2_ref, w3_ref, nw_ref, wr_ref, br_ref, tri_ref,
                  cnt0_ref, x1_ref, h2_ref, meta_ref, cnt_ref, carry_scr, *, tm):
    i = pl.program_id(0)

    @pl.when(i == 0)
    def _():
        carry_scr[...] = cnt0_ref[...]

    mix = _dot(oret_ref[...], w1_ref[...]) + _dot(os5_ref[...], w2_ref[...]) + _dot(ogdn_ref[...], w3_ref[...])
    x1 = x_ref[...] + mix
    x1_ref[...] = x1
    h2 = _rms(x1) * nw_ref[...]
    for s in range(D_MODEL // LANE):
        h2_ref[pl.ds(s, tm, stride=SUBLANE), :] = h2[:, s * LANE:(s + 1) * LANE]

    logits = _dot_nt(wr_ref[...], h2) + br_ref[...]
    row_i = lax.broadcasted_iota(I32, (LANE, tm), 0)
    row = row_i.astype(F32)
    neg = -jnp.inf
    big = float(LANE)
    g_log = jnp.where((row_i >= N_EXPERTS) & (row_i < N_EXPERTS + MOE_GROUPS), logits, neg)
    g_max = jnp.max(g_log, axis=0, keepdims=True)
    grp = jnp.min(jnp.where(g_log == g_max, row - N_EXPERTS, big), axis=0, keepdims=True)
    p_grp = 1.0 / jnp.sum(jnp.exp(g_log - g_max), axis=0, keepdims=True)
    in_grp = (row >= grp * EXPERTS_PER_GROUP) & (row < (grp + 1.0) * EXPERTS_PER_GROUP)
    e_log = jnp.where(in_grp, logits, neg)
    v1 = jnp.max(e_log, axis=0, keepdims=True)
    i1 = jnp.min(jnp.where(e_log == v1, row, big), axis=0, keepdims=True)
    e_log2 = jnp.where(row == i1, neg, e_log)
    v2 = jnp.max(e_log2, axis=0, keepdims=True)
    i2 = jnp.min(jnp.where(e_log2 == v2, row, big), axis=0, keepdims=True)
    e2 = jnp.exp(v2 - v1)
    gate1 = p_grp / (1.0 + e2)
    gate2 = p_grp * e2 / (1.0 + e2)

    oh1 = row == i1
    oh2 = row == i2
    picked = jnp.where(oh1 | oh2, 1.0, 0.0)
    seen = carry_scr[:, 0:1]
    before = jnp.dot(picked.astype(BF16), tri_ref[...], preferred_element_type=F32) + seen
    rank1 = jnp.sum(jnp.where(oh1, before, 0.0), axis=0, keepdims=True)
    rank2 = jnp.sum(jnp.where(oh2, before, 0.0), axis=0, keepdims=True)
    total = seen + jnp.sum(picked, axis=1, keepdims=True)
    carry_scr[...] = jnp.broadcast_to(total, carry_scr.shape)
    cnt_ref[...] = jnp.broadcast_to(total, cnt_ref.shape)
    meta_ref[...] = jnp.concatenate([i1, i2, gate1, gate2, rank1, rank2, jnp.zeros((2, tm), F32)], axis=0)


def _outproj(x, o_ret, o_s5, o_gdn, w1, w2, w3, nw, wr, br, cnt0):
    t = x.shape[0]
    tm = min(ROW_TILE, t)
    tri = jnp.asarray(np.arange(tm)[:, None] < np.arange(tm)[None, :], dtype=BF16)
    row = lambda n: pl.BlockSpec((tm, n), lambda i: (i, 0))
    full = lambda a: pl.BlockSpec(a.shape, lambda i: (0, 0))
    return pl.pallas_call(
        functools.partial(_outproj_body, tm=tm),
        grid=(t // tm,),
        in_specs=[row(D_MODEL), row(RET_VD), row(S5_CH), row(GDN_HD),
                  full(w1), full(w2), full(w3), full(nw), full(wr), full(br), full(tri), full(cnt0)],
        out_specs=[row(D_MODEL), pl.BlockSpec((tm * SUBLANE, LANE), lambda i: (i, 0)),
                   pl.BlockSpec((SUBLANE, tm), lambda i: (0, i)), pl.BlockSpec((LANE, LANE), lambda i: (0, 0))],
        out_shape=[jax.ShapeDtypeStruct((t, D_MODEL), F32),
                   jax.ShapeDtypeStruct((t * SUBLANE, LANE), F32),
                   jax.ShapeDtypeStruct((SUBLANE, t), F32),
                   jax.ShapeDtypeStruct((LANE, LANE), F32)],
        scratch_shapes=[pltpu.VMEM((LANE, LANE), F32)],
        compiler_params=_params("arbitrary"),
        name="outproj_router",
    )(x, o_ret, o_s5, o_gdn, w1, w2, w3, nw, wr, br, tri, cnt0)


def _token_rows(ref, idx):
    return ref.at[pl.ds(pl.multiple_of(idx * SUBLANE, SUBLANE), SUBLANE)]


ZERO_ROWS = 128


def _zero_segment(zero_scr, xb_ref, sem, start, length, wait):
    def piece(off, n):
        cp = pltpu.make_async_copy(zero_scr.at[pl.ds(0, n * SUBLANE)],
                                   xb_ref.at[pl.ds(pl.multiple_of(off * SUBLANE, SUBLANE), n * SUBLANE)], sem)
        cp.wait() if wait else cp.start()

    n_big = length // ZERO_ROWS

    def big(i, carry):
        piece(start + i * ZERO_ROWS, ZERO_ROWS)
        return carry

    lax.fori_loop(0, n_big, big, 0)
    off = start + n_big * ZERO_ROWS
    rem = length - n_big * ZERO_ROWS
    bit = ZERO_ROWS // 2
    while bit >= 1:
        has = (rem & bit) != 0
        pl.when(has)(functools.partial(piece, off, bit))
        off = off + jnp.where(has, bit, 0)
        bit //= 2


def _scatter_body(d1_ref, d2_ref, zs_ref, zl_ref, src_a_ref, src_b_ref, xb_ref,
                  zero_scr, sem, zsem, *, tm, tiles_a):
    i = pl.program_id(0)
    base = i * tm

    @pl.when(i == 0)
    def _():
        zero_scr[...] = jnp.zeros_like(zero_scr)
        for wait in (False, True):
            lax.fori_loop(0, N_EXPERTS + 1,
                          lambda s, c, wait=wait: (_zero_segment(zero_scr, xb_ref, zsem, zs_ref[s], zl_ref[s], wait),
                                                   c)[1], 0)

    def scatter_tile(src_ref):
        def copy(t, d):
            return pltpu.make_async_copy(_token_rows(src_ref, t), _token_rows(xb_ref, d), sem)

        def issue(t, carry):
            copy(t, d1_ref[base + t]).start(priority=0)
            copy(t, d2_ref[base + t]).start(priority=1)
            return carry

        lax.fori_loop(0, tm, issue, 0, unroll=ISSUE_UNROLL)
        for _ in range(2):
            pltpu.make_async_copy(src_ref, src_ref, sem).wait()

    pl.when(i < tiles_a)(functools.partial(scatter_tile, src_a_ref))
    pl.when(i >= tiles_a)(functools.partial(scatter_tile, src_b_ref))


def _scatter(route, zero_start, zero_len, h2_a, h2_b, n_rows):
    tm = SCATTER_TILE
    tiles_a, tiles_b = h2_a.shape[0] // (tm * SUBLANE), h2_b.shape[0] // (tm * SUBLANE)
    assert tiles_a * tm * SUBLANE == h2_a.shape[0] and tiles_b * tm * SUBLANE == h2_b.shape[0]
    return pl.pallas_call(
        functools.partial(_scatter_body, tm=tm, tiles_a=tiles_a),
        grid_spec=pltpu.PrefetchScalarGridSpec(
            num_scalar_prefetch=4, grid=(tiles_a + tiles_b,),
            in_specs=[pl.BlockSpec((tm * SUBLANE, LANE), lambda i, *_: (jnp.minimum(i, tiles_a - 1), 0)),
                      pl.BlockSpec((tm * SUBLANE, LANE), lambda i, *_: (jnp.maximum(i - tiles_a, 0), 0))],
            out_specs=pl.BlockSpec(memory_space=pl.ANY),
            scratch_shapes=[pltpu.VMEM((ZERO_ROWS * SUBLANE, LANE), F32), pltpu.SemaphoreType.DMA(()),
                            pltpu.SemaphoreType.DMA(())]),
        out_shape=jax.ShapeDtypeStruct((n_rows * SUBLANE, LANE), F32),
        compiler_params=_params("arbitrary"),
        name="moe_scatter",
    )(*route, zero_start, zero_len, h2_a, h2_b)


def _experts_body(be_ref, nb_ref, xb_ref, w1_ref, w3_ref, w2_ref, yb_ref, w1_scr, w3_scr, w2_scr, *, blk):
    i = pl.program_id(0)
    live = i < nb_ref[0]

    @pl.when(live & ((i == 0) | (be_ref[i] != be_ref[jnp.maximum(i - 1, 0)])))
    def _():
        w1_scr[...] = w1_ref[...].astype(BF16)
        w3_scr[...] = w3_ref[...].astype(BF16)
        w2_scr[...] = w2_ref[...].astype(BF16)

    @pl.when(live)
    def _():
        x = jnp.concatenate([xb_ref[pl.ds(s, blk, stride=SUBLANE), :] for s in range(D_MODEL // LANE)],
                            axis=1).astype(BF16)
        hid = _silu(jnp.dot(x, w1_scr[...], preferred_element_type=F32)) * \
            jnp.dot(x, w3_scr[...], preferred_element_type=F32)
        y = _dot(hid, w2_scr[...])
        for s in range(D_MODEL // LANE):
            yb_ref[pl.ds(s, blk, stride=SUBLANE), :] = y[:, s * LANE:(s + 1) * LANE]

    @pl.when(jnp.logical_not(live))
    def _():
        yb_ref[...] = jnp.zeros_like(yb_ref)


def _experts(block_e, nb_used, xb, layer, w1, w3, w2, n_blocks):
    blk = MOE_BLK
    live = lambda i, nb: jnp.minimum(i, nb[0] - 1)
    tile_in = pl.BlockSpec((blk * SUBLANE, LANE), lambda i, be, nb: (live(i, nb), 0))
    tile_out = pl.BlockSpec((blk * SUBLANE, LANE), lambda i, be, nb: (i, 0))
    wspec = lambda a: pl.BlockSpec((None, None) + a.shape[2:],
                                   lambda i, be, nb: (layer, be[live(i, nb)], 0, 0))
    return pl.pallas_call(
        functools.partial(_experts_body, blk=blk),
        grid_spec=pltpu.PrefetchScalarGridSpec(
            num_scalar_prefetch=2, grid=(n_blocks,),
            in_specs=[tile_in, wspec(w1), wspec(w3), wspec(w2)], out_specs=tile_out,
            scratch_shapes=[pltpu.VMEM(w1.shape[2:], BF16), pltpu.VMEM(w3.shape[2:], BF16),
                            pltpu.VMEM(w2.shape[2:], BF16)]),
        out_shape=jax.ShapeDtypeStruct(xb.shape, F32),
        compiler_params=_params("arbitrary"),
        name="moe_experts",
    )(block_e, nb_used, xb, w1, w3, w2)


def _combine_body(*refs, tm, tiles_a, final_norm, n_proj):
    d1_ref, d2_ref, x1_a_ref, x1_b_ref, meta_a_ref, meta_b_ref, yb_ref, nw_ref = refs[:8]
    w_refs = refs[8:8 + n_proj]
    out_a_ref, out_b_ref = refs[8 + n_proj:10 + n_proj]
    za_refs = refs[10 + n_proj:10 + 2 * n_proj]
    zb_refs = refs[10 + 2 * n_proj:10 + 3 * n_proj]
    buf_ref, sem = refs[-2:]
    i = pl.program_id(0)
    phase = i % 2

    def gather(tile, ph):
        base = tile * tm

        def issue(t, carry):
            for k, d_ref in enumerate((d1_ref, d2_ref)):
                pltpu.make_async_copy(_token_rows(yb_ref, d_ref[base + t]),
                                      _token_rows(buf_ref.at[ph, k], t), sem.at[ph]).start(priority=k)
            return carry

        lax.fori_loop(0, tm, issue, 0, unroll=ISSUE_UNROLL)

    @pl.when(i == 0)
    def _():
        gather(0, 0)

    @pl.when(i + 1 < pl.num_programs(0))
    def _():
        gather(i + 1, 1 - phase)

    for k in range(2):
        pltpu.make_async_copy(buf_ref.at[phase, k], buf_ref.at[phase, k], sem.at[phase]).wait()

    def rows_of(slot):
        return jnp.concatenate([buf_ref[phase, slot, pl.ds(s, tm, stride=SUBLANE), :]
                                for s in range(D_MODEL // LANE)], axis=1)

    is_a = i < tiles_a
    meta = jnp.where(is_a, meta_a_ref[...], meta_b_ref[...])
    on_diag = lax.broadcasted_iota(I32, (tm, tm), 0) == lax.broadcasted_iota(I32, (tm, tm), 1)
    as_col = lambda r: jnp.sum(jnp.where(on_diag, r, 0.0), axis=1, keepdims=True)
    x2 = jnp.where(is_a, x1_a_ref[...], x1_b_ref[...]) + (as_col(meta[2:3, :]) * rows_of(0)
                                                         + as_col(meta[3:4, :]) * rows_of(1))
    normed = _rms(x2) * nw_ref[...]

    def emit(out_ref, z_refs):
        out_ref[...] = normed if final_norm else x2
        h = normed.astype(BF16)
        for w_ref, z_ref in zip(w_refs, z_refs):
            z_ref[...] = jnp.dot(h, w_ref[...], preferred_element_type=F32)

    pl.when(is_a)(functools.partial(emit, out_a_ref, za_refs))
    pl.when(jnp.logical_not(is_a))(functools.partial(emit, out_b_ref, zb_refs))


def _combine(route, x1_a, x1_b, meta_a, meta_b, yb, norm_w, final_norm, next_in_ws):
    tm = GATHER_TILE if next_in_ws else FINAL_GATHER_TILE
    tiles_a, tiles_b = x1_a.shape[0] // tm, x1_b.shape[0] // tm
    assert tiles_a * tm == x1_a.shape[0] and tiles_b * tm == x1_b.shape[0]
    assert final_norm == (not next_in_ws)
    of_a = lambda n: pl.BlockSpec((tm, n), lambda i, *_: (jnp.minimum(i, tiles_a - 1), 0))
    of_b = lambda n: pl.BlockSpec((tm, n), lambda i, *_: (jnp.maximum(i - tiles_a, 0), 0))
    widths = [w.shape[1] for w in next_in_ws]
    outs = pl.pallas_call(
        functools.partial(_combine_body, tm=tm, tiles_a=tiles_a, final_norm=final_norm, n_proj=len(widths)),
        grid_spec=pltpu.PrefetchScalarGridSpec(
            num_scalar_prefetch=2, grid=(tiles_a + tiles_b,),
            in_specs=[of_a(D_MODEL), of_b(D_MODEL),
                      pl.BlockSpec((SUBLANE, tm), lambda i, *_: (0, jnp.minimum(i, tiles_a - 1))),
                      pl.BlockSpec((SUBLANE, tm), lambda i, *_: (0, jnp.maximum(i - tiles_a, 0))),
                      pl.BlockSpec(memory_space=pl.ANY),
                      pl.BlockSpec((1, D_MODEL), lambda i, *_: (0, 0))] +
                     [pl.BlockSpec(w.shape, lambda i, *_: (0, 0)) for w in next_in_ws],
            out_specs=[of_a(D_MODEL), of_b(D_MODEL)] + [of_a(n) for n in widths] + [of_b(n) for n in widths],
            scratch_shapes=[pltpu.VMEM((2, 2, tm * SUBLANE, LANE), F32), pltpu.SemaphoreType.DMA((2,))]),
        out_shape=[jax.ShapeDtypeStruct(x1_a.shape, F32), jax.ShapeDtypeStruct(x1_b.shape, F32)] +
                  [jax.ShapeDtypeStruct((x.shape[0], n), F32) for x in (x1_a, x1_b) for n in widths],
        compiler_params=_params("arbitrary"),
        name="moe_combine",
    )(*route, x1_a, x1_b, meta_a, meta_b, yb, norm_w, *next_in_ws)
    n = len(widths)
    return (outs[0], outs[1]), (outs[2:2 + n], outs[2 + n:2 + 2 * n])


def _moe(stream_a, stream_b, counts, layer, w1, w3, w2, norm_w, final_norm, next_in_ws):
    (x1_a, h2_a, meta_a), (x1_b, h2_b, meta_b) = stream_a, stream_b
    t = x1_a.shape[0] + x1_b.shape[0]
    n_blocks = (2 * t + N_EXPERTS * (MOE_BLK - 1)) // MOE_BLK
    col = lambda c: jnp.concatenate([meta_a[c], meta_b[c]]).astype(I32)
    e1, e2, r1, r2 = col(0), col(1), col(4), col(5)
    cnt = counts[:N_EXPERTS, 0].astype(I32)
    padded = (cnt + MOE_BLK - 1) // MOE_BLK * MOE_BLK
    pad_end = jnp.cumsum(padded)
    pad_start = pad_end - padded
    slot = lambda e, r: r + jnp.sum(jnp.where(e[:, None] == jnp.arange(N_EXPERTS, dtype=I32)[None, :],
                                              pad_start[None, :], 0), axis=1)
    route = (slot(e1, r1), slot(e2, r2))
    nb_used = (pad_end[-1:] // MOE_BLK).astype(I32)
    blk_start = jnp.arange(n_blocks, dtype=I32) * MOE_BLK
    block_e = jnp.minimum(jnp.sum(blk_start[:, None] >= pad_end[None, :], axis=1), N_EXPERTS - 1).astype(I32)
    n_rows = n_blocks * MOE_BLK
    zero_start = jnp.concatenate([pad_start + cnt, pad_end[-1:]]).astype(I32)
    zero_len = jnp.concatenate([padded - cnt, n_rows - pad_end[-1:]]).astype(I32)
    xb = _scatter(route, zero_start, zero_len, h2_a, h2_b, n_rows)
    yb = _experts(block_e, nb_used, xb, layer, w1, w3, w2, n_blocks)
    return _combine(route, x1_a, x1_b, meta_a, meta_b, yb, norm_w, final_norm, next_in_ws)


def _rope_tables(pos):
    half = RET_DK // 2
    inv = ROPE_BASE ** (-jnp.arange(half, dtype=F32) / half)
    ang = pos[:, None] * inv[None, :]
    cos, sin = jnp.cos(ang), jnp.sin(ang)
    cos_t = jnp.tile(jnp.concatenate([cos, cos], axis=1), (1, RET_HEADS))
    sin_t = jnp.tile(jnp.concatenate([-sin, sin], axis=1), (1, RET_HEADS))
    return cos_t, sin_t


def _block_diag(blocks):
    g, r, c = blocks.shape
    eye = jnp.eye(g, dtype=bool)
    return jnp.where(eye[:, None, :, None], blocks[:, :, None, :], 0).reshape(g * r, g * c)


def _layer_weights(l, w_in, s5, s5_c_re, s5_c_im, s5_d, s5_glu_w, s5_glu_b, gdn_conv_w, gdn_a_log,
                   gdn_dt_bias, gdn_norm_w, w_out, router_group_w, router_group_b, router_expert_w,
                   router_expert_b):
    wi = w_in[l]
    g0 = RET_IN + S5_IN
    w_ab = jnp.pad(wi[:, g0 + GDN_QKV + GDN_HD:], ((0, 0), (0, LANE - 2 * GDN_HEADS)))
    in_ws = [wi[:, :RET_IN], wi[:, g0:g0 + GDN_QKV], wi[:, RET_IN:g0], wi[:, g0 + GDN_QKV:g0 + GDN_QKV + GDN_HD],
             w_ab]
    in_ws = [w.astype(BF16) for w in in_ws]
    ab_re, ab_im, bb_re, bb_im = s5
    n = S5_LANES
    sl = slice(l * n, (l + 1) * n)
    ab = jnp.concatenate([ab_re[sl].reshape(1, n), ab_im[sl].reshape(1, n)], axis=0)
    to_bd = lambda m: _block_diag(jnp.swapaxes(m[sl].reshape(S5_GROUPS, S5_STATE, S5_GROUP_CH), 1, 2))
    wb = jnp.concatenate([to_bd(bb_re), to_bd(bb_im)], axis=1).astype(BF16)
    wc_re = _block_diag(jnp.swapaxes(s5_c_re[l], 1, 2)).astype(BF16)
    wc_im = _block_diag(jnp.swapaxes(s5_c_im[l], 1, 2)).astype(BF16)
    s5_ws = (ab, wb, wc_re, wc_im, s5_d[l].reshape(1, S5_CH), s5_glu_w[l].astype(BF16),
             s5_glu_b[l].reshape(1, S5_CH))
    pad4 = lambda v: jnp.pad(v.reshape(1, GDN_HEADS), ((0, 0), (0, LANE - GDN_HEADS)))
    gdn_ws = (gdn_conv_w[l], pad4(gdn_a_log[l]), pad4(gdn_dt_bias[l]),
              jnp.tile(gdn_norm_w[l], GDN_HEADS).reshape(1, GDN_HD))
    wo = w_out[l].astype(BF16)
    out_ws = (wo[:RET_VD], wo[RET_VD:RET_VD + S5_CH], wo[RET_VD + S5_CH:])
    wr = jnp.pad(jnp.concatenate([router_expert_w[l], router_group_w[l]], axis=1).T,
                 ((0, LANE - N_EXPERTS - MOE_GROUPS), (0, 0))).astype(BF16)
    br = jnp.pad(jnp.concatenate([router_expert_b[l], router_group_b[l]]),
                 (0, LANE - N_EXPERTS - MOE_GROUPS)).reshape(LANE, 1)
    return in_ws, s5_ws, gdn_ws, out_ws, (wr, br)


def _mix_and_route(l, x, projected, bsz, seq, rope, states, layer_w, norm_mix, norm_ffn, cnt0):
    t = bsz * seq
    ret_s, s5_re, s5_im, gdn_s, gdn_buf = states
    in_ws, s5_ws, gdn_ws, out_ws, (wr, br) = layer_w
    zr, zq, zs, zg, zab = projected or _inproj(x, norm_mix[l].reshape(1, D_MODEL), in_ws)
    o_ret, ret_fin = _retention(zr, *rope, ret_s.reshape(DEPTH, bsz, RET_QD, RET_DV), l, bsz, seq)
    u_tm = jnp.swapaxes(zs.reshape(bsz, seq, S5_CH), 0, 1)
    o_s5_tm, re_fin, im_fin = _s5(u_tm, s5_re[l].reshape(bsz, S5_LANES), s5_im[l].reshape(bsz, S5_LANES), *s5_ws)
    o_s5 = jnp.swapaxes(o_s5_tm, 0, 1).reshape(t, S5_CH)
    buf0 = jnp.pad(gdn_buf[l], ((0, 0), (SUBLANE - (GDN_CONV - 1), 0), (0, 0)))
    o_gdn, nbuf, gdn_fin = _gdn(zq, zg, zab, *gdn_ws, buf0, gdn_s.reshape(DEPTH, bsz, GDN_HD, GDN_DV), l, bsz, seq)
    x1, h2_tiles, meta, counts = _outproj(x, o_ret, o_s5, o_gdn, *out_ws, norm_ffn[l].reshape(1, D_MODEL), wr, br,
                                          cnt0)
    new_states = (ret_fin.reshape(bsz, RET_HEADS, RET_DK, RET_DV), re_fin.reshape(bsz, S5_GROUPS, S5_STATE),
                  im_fin.reshape(bsz, S5_GROUPS, S5_STATE), gdn_fin.reshape(bsz, GDN_HEADS, GDN_DK, GDN_DV),
                  nbuf[:, SUBLANE - (GDN_CONV - 1):, :])
    return (x1, h2_tiles, meta), counts, new_states


def _trunks(xs, positions, states, layer_ws, norm_mix, norm_ffn, experts, norm_final):
    shapes = [x.shape[:2] for x in xs]
    ropes = [_rope_tables(pos) for pos in positions]
    xs = [x.reshape(b * s, D_MODEL) for x, (b, s) in zip(xs, shapes)]
    outs = [[[] for _ in range(5)] for _ in xs]
    projected = [None for _ in xs]
    for l in range(DEPTH):
        counts = jnp.zeros((LANE, LANE), F32)
        routed = []
        for i, x in enumerate(xs):
            stream, counts, new_states = _mix_and_route(l, x, projected[i], *shapes[i], ropes[i], states[i],
                                                        layer_ws[l], norm_mix, norm_ffn, counts)
            routed.append(stream)
            for lst, s in zip(outs[i], new_states):
                lst.append(s)
        last = l == DEPTH - 1
        norm_w = norm_final if last else norm_mix[l + 1]
        xs, projected = _moe(*routed, counts, l, *experts, norm_w.reshape(1, D_MODEL), last,
                             [] if last else layer_ws[l + 1][0])
    ys = [x.reshape(b, s, D_MODEL) for x, (b, s) in zip(xs, shapes)]
    return ys, [tuple(jnp.stack(o) for o in per_stream) for per_stream in outs]


def kernel(x_prompt, x_sample, state_ret, state_s5_re, state_s5_im, state_gdn, state_gdn_conv, norm_mix, w_in, s5_a_re, s5_a_im, s5_log_dt, s5_b_re, s5_b_im, s5_c_re, s5_c_im, s5_d, s5_glu_w, s5_glu_b, gdn_conv_w, gdn_a_log, gdn_dt_bias, gdn_norm_w, w_out, norm_ffn, router_group_w, router_group_b, router_expert_w, router_expert_b, expert_w1, expert_w3, expert_w2, norm_final):
    n = DEPTH * S5_LANES
    col = lambda a: a.reshape(n, 1)
    log_dt = jnp.broadcast_to(s5_log_dt[:, :, None], (DEPTH, S5_GROUPS, S5_STATE))
    s5 = _s5_prep(col(s5_a_re), col(s5_a_im), col(log_dt), s5_b_re.reshape(n, S5_GROUP_CH),
                  s5_b_im.reshape(n, S5_GROUP_CH))
    layer_ws = [_layer_weights(l, w_in, s5, s5_c_re, s5_c_im, s5_d, s5_glu_w, s5_glu_b, gdn_conv_w,
                               gdn_a_log, gdn_dt_bias, gdn_norm_w, w_out, router_group_w, router_group_b,
                               router_expert_w, router_expert_b) for l in range(DEPTH)]
    experts = (expert_w1, expert_w3, expert_w2)
    bp, lp, _ = x_prompt.shape
    zero_states = (jnp.zeros((DEPTH, bp, RET_HEADS, RET_DK, RET_DV), F32),
                   jnp.zeros((DEPTH, bp, S5_GROUPS, S5_STATE), F32),
                   jnp.zeros((DEPTH, bp, S5_GROUPS, S5_STATE), F32),
                   jnp.zeros((DEPTH, bp, GDN_HEADS, GDN_DK, GDN_DV), F32),
                   jnp.zeros((DEPTH, bp, GDN_CONV - 1, GDN_QKV), F32))
    sample_states = (state_ret, state_s5_re, state_s5_im, state_gdn, state_gdn_conv)
    positions = (jnp.arange(lp, dtype=F32), PAST_LEN + jnp.arange(x_sample.shape[1], dtype=F32))
    (y_p, y_s), (p_states, s_states) = _trunks((x_prompt, x_sample), positions, (zero_states, sample_states),
                                               layer_ws, norm_mix, norm_ffn, experts, norm_final)
    return (y_p, y_s) + p_states + s_states
```
